```python
import math
import jax, jax.numpy as jnp
from jax import lax
import numpy as np

D_MODEL = 1024
BATCH = 4
SEQ = 8192
DEPTH = 1

ATT_HEADS = 4
ATT_HEAD_DIM = 64
ATT_Q_WIDTH = ATT_HEADS * 2 * ATT_HEAD_DIM
ATT_V_DIM = 2 * ATT_HEAD_DIM
ATT_V_WIDTH = ATT_HEADS * ATT_V_DIM
ROPE_THETA = 10000.0
Q_BLOCK = 128
SUBLN_EPS = 1e-5
CONV_WIDTH = D_MODEL // 2
CONV_K = 3
N_BRANCHES = 2
IN_SIZES = (ATT_Q_WIDTH, ATT_Q_WIDTH, ATT_V_WIDTH, CONV_WIDTH, CONV_WIDTH, CONV_WIDTH, N_BRANCHES * D_MODEL)
IN_COLS = sum(IN_SIZES)
N_GROUPS = 4
EXPERTS_PER_GROUP = 8
N_EXPERTS = N_GROUPS * EXPERTS_PER_GROUP
TOP_K = 2
EXPERT_FF = D_MODEL // 2
MOE_BLOCK = 128
LN_EPS = 1e-5
DEEPNORM_ALPHA = (2.0 * DEPTH) ** 0.25
DEEPNORM_BETA = (8.0 * DEPTH) ** -0.25

kernel_name = "hybrid_diffattn_shortconv_hmoe_deepnorm"


def layer_norm(x, g, b):
    xf = x.astype(jnp.float32)
    mu = jnp.mean(xf, -1, keepdims=True)
    var = jnp.mean(jnp.square(xf - mu), -1, keepdims=True)
    return ((xf - mu) * lax.rsqrt(var + LN_EPS) * g.astype(jnp.float32) + b.astype(jnp.float32)).astype(x.dtype)


def rope_tables(positions):
    inv_freq = ROPE_THETA ** (-jnp.arange(0, ATT_HEAD_DIM, 2, dtype=jnp.float32) / ATT_HEAD_DIM)
    ang = positions.astype(jnp.float32)[..., None] * inv_freq
    ang = jnp.concatenate([ang, ang], -1)
    return jnp.cos(ang), jnp.sin(ang)


def apply_rope(t, cos, sin):
    c = cos[:, :, None, None, :]
    s = sin[:, :, None, None, :]
    t1, t2 = jnp.split(t, 2, axis=-1)
    rot = jnp.concatenate([-t2, t1], -1)
    return (t.astype(jnp.float32) * c + rot.astype(jnp.float32) * s).astype(t.dtype)


def diff_attention(q, k, v, lam):
    B, S = q.shape[0], q.shape[1]
    nqb = S // Q_BLOCK
    scale = ATT_HEAD_DIM ** -0.5
    kt = k.transpose(0, 2, 3, 1, 4)
    vt = v.transpose(0, 2, 1, 3)
    qb = q.reshape(B, nqb, Q_BLOCK, ATT_HEADS, 2, ATT_HEAD_DIM).transpose(1, 0, 3, 4, 2, 5)
    kpos = jnp.arange(S)
    neg = jnp.finfo(jnp.float32).min

    def block(args):
        qi, blk = args
        s = jnp.einsum('bhmqd,bhmkd->bhmqk', qi, kt, preferred_element_type=jnp.float32) * scale
        qpos = blk * Q_BLOCK + jnp.arange(Q_BLOCK)
        mask = kpos[None, :] <= qpos[:, None]
        p = jax.nn.softmax(jnp.where(mask, s, neg), axis=-1)
        a = p[:, :, 0] - lam * p[:, :, 1]
        return jnp.einsum('bhqk,bhkv->bhqv', a.astype(vt.dtype), vt)

    o = lax.map(block, (qb, jnp.arange(nqb)))
    return o.transpose(1, 0, 3, 2, 4).reshape(B, S, ATT_HEADS, ATT_V_DIM)


def short_conv(u, w):
    S = u.shape[1]
    up = jnp.pad(u, ((0, 0), (CONV_K - 1, 0), (0, 0)))
    y = w[0] * up[:, 0:S]
    for j in range(1, CONV_K):
        y = y + w[j] * up[:, j:j + S]
    return y


def hier_moe(x, w_rg, b_rg, w_re, b_re, w_gate, w_up, w_down):
    B, S, D = x.shape
    T = B * S
    xf = x.reshape(T, D)
    g_logits = (xf @ w_rg).astype(jnp.float32) + b_rg.astype(jnp.float32)
    g_prob = jax.nn.softmax(g_logits, -1)
    g_sel = jnp.argmax(g_logits, -1)
    g_w = jnp.take_along_axis(g_prob, g_sel[:, None], -1)[:, 0]
    e_logits = ((xf @ w_re).astype(jnp.float32) + b_re.astype(jnp.float32)).reshape(T, N_GROUPS, EXPERTS_PER_GROUP)
    e_logits = jnp.take_along_axis(e_logits, g_sel[:, None, None], 1)[:, 0]
    top_v, top_i = lax.top_k(e_logits, TOP_K)
    top_p = jax.nn.softmax(top_v, -1) * g_w[:, None]
    expert_id = g_sel[:, None].astype(jnp.int32) * EXPERTS_PER_GROUP + top_i.astype(jnp.int32)

    A = T * TOP_K
    e_flat = expert_id.reshape(A)
    tok_flat = jnp.repeat(jnp.arange(T, dtype=jnp.int32), TOP_K)
    w_flat = top_p.reshape(A)
    order = jnp.argsort(e_flat)
    e_sorted = e_flat[order]
    tok_sorted = tok_flat[order]
    w_sorted = w_flat[order]
    counts = jnp.bincount(e_flat, length=N_EXPERTS).astype(jnp.int32)
    padded = (counts + MOE_BLOCK - 1) // MOE_BLOCK * MOE_BLOCK
    start = jnp.cumsum(counts) - counts
    pend = jnp.cumsum(padded)
    pstart = pend - padded
    dest = pstart[e_sorted] + jnp.arange(A, dtype=jnp.int32) - start[e_sorted]
    P = A + N_EXPERTS * MOE_BLOCK
    NB = P // MOE_BLOCK
    slot_tok = jnp.full((P,), T, jnp.int32).at[dest].set(tok_sorted)
    slot_w = jnp.zeros((P,), jnp.float32).at[dest].set(w_sorted)
    block_expert = jnp.minimum(
        jnp.searchsorted(pend, jnp.arange(NB, dtype=jnp.int32) * MOE_BLOCK, side='right'), N_EXPERTS - 1)
    x_pad = jnp.concatenate([xf, jnp.zeros((1, D), xf.dtype)], 0)
    xs = x_pad[slot_tok].reshape(NB, MOE_BLOCK, D)

    def run(args):
        xb, e = args
        h = jax.nn.silu(xb @ w_gate[e]) * (xb @ w_up[e])
        return h @ w_down[e]

    ys = lax.map(run, (xs, block_expert)).reshape(P, D)
    out = jnp.zeros((T + 1, D), jnp.float32).at[slot_tok].add(ys.astype(jnp.float32) * slot_w[:, None])
    return out[:T].reshape(B, S, D).astype(x.dtype)


def setup_inputs(seed: int = 0) -> dict:
    key = jax.random.key(seed)
    ks = jax.random.split(key, 24)
    f32 = jnp.float32
    nrm = lambda k, shape, s: jax.random.normal(k, shape, f32) * s
    x = jax.random.normal(ks[0], (BATCH, SEQ, D_MODEL), f32)
    positions = jnp.broadcast_to(jnp.arange(SEQ, dtype=jnp.int32), (BATCH, SEQ))
    col_scale = jnp.concatenate([
        jnp.ones((2 * ATT_Q_WIDTH,), f32),
        jnp.full((ATT_V_WIDTH,), DEEPNORM_BETA, f32),
        jnp.ones((3 * CONV_WIDTH + N_BRANCHES * D_MODEL,), f32)])
    w_in = nrm(ks[1], (DEPTH, D_MODEL, IN_COLS), D_MODEL ** -0.5) * col_scale
    b_gate = nrm(ks[2], (DEPTH, N_BRANCHES * D_MODEL), 0.02)
    lambda_q1 = nrm(ks[3], (DEPTH, ATT_HEAD_DIM), 0.1)
    lambda_k1 = nrm(ks[4], (DEPTH, ATT_HEAD_DIM), 0.1)
    lambda_q2 = nrm(ks[5], (DEPTH, ATT_HEAD_DIM), 0.1)
    lambda_k2 = nrm(ks[6], (DEPTH, ATT_HEAD_DIM), 0.1)
    subln_g = 1.0 + nrm(ks[7], (DEPTH, ATT_V_DIM), 0.02)
    w_o_att = nrm(ks[8], (DEPTH, ATT_V_WIDTH, D_MODEL), ATT_V_WIDTH ** -0.5 * DEEPNORM_BETA)
    conv_w = nrm(ks[9], (DEPTH, CONV_K, CONV_WIDTH), CONV_K ** -0.5)
    w_o_conv = nrm(ks[10], (DEPTH, CONV_WIDTH, D_MODEL), CONV_WIDTH ** -0.5 * DEEPNORM_BETA)
    w_mix_out = nrm(ks[11], (DEPTH, D_MODEL, D_MODEL), D_MODEL ** -0.5 * DEEPNORM_BETA)
    ln1_g = 1.0 + nrm(ks[12], (DEPTH, D_MODEL), 0.02)
    ln1_b = nrm(ks[13], (DEPTH, D_MODEL), 0.02)
    w_router_group = nrm(ks[14], (DEPTH, D_MODEL, N_GROUPS), D_MODEL ** -0.5)
    b_router_group = nrm(ks[15], (DEPTH, N_GROUPS), 0.01)
    w_router_expert = nrm(ks[16], (DEPTH, D_MODEL, N_EXPERTS), D_MODEL ** -0.5)
    b_router_expert = nrm(ks[17], (DEPTH, N_EXPERTS), 0.01)
    w_exp_gate = nrm(ks[18], (DEPTH, N_EXPERTS, D_MODEL, EXPERT_FF), D_MODEL ** -0.5)
    w_exp_up = nrm(ks[19], (DEPTH, N_EXPERTS, D_MODEL, EXPERT_FF), D_MODEL ** -0.5 * DEEPNORM_BETA)
    w_exp_down = nrm(ks[20], (DEPTH, N_EXPERTS, EXPERT_FF, D_MODEL), EXPERT_FF ** -0.5 * DEEPNORM_BETA)
    ln2_g = 1.0 + nrm(ks[21], (DEPTH, D_MODEL), 0.02)
    ln2_b = nrm(ks[22], (DEPTH, D_MODEL), 0.02)
    return {"x": x, "positions": positions, "w_in": w_in, "b_gate": b_gate,
            "lambda_q1": lambda_q1, "lambda_k1": lambda_k1, "lambda_q2": lambda_q2, "lambda_k2": lambda_k2,
            "subln_g": subln_g, "w_o_att": w_o_att, "conv_w": conv_w, "w_o_conv": w_o_conv,
            "w_mix_out": w_mix_out, "ln1_g": ln1_g, "ln1_b": ln1_b,
            "w_router_group": w_router_group, "b_router_group": b_router_group,
            "w_router_expert": w_router_expert, "b_router_expert": b_router_expert,
            "w_exp_gate": w_exp_gate, "w_exp_up": w_exp_up, "w_exp_down": w_exp_down,
            "ln2_g": ln2_g, "ln2_b": ln2_b}


def reference(x, positions, w_in, b_gate, lambda_q1, lambda_k1, lambda_q2, lambda_k2, subln_g, w_o_att,
              conv_w, w_o_conv, w_mix_out, ln1_g, ln1_b, w_router_group, b_router_group, w_router_expert,
              b_router_expert, w_exp_gate, w_exp_up, w_exp_down, ln2_g, ln2_b):
    B, S, D = x.shape
    cos, sin = rope_tables(positions)
    split_pts = [int(v) for v in np.cumsum(IN_SIZES)[:-1]]
    h = x
    for l in range(DEPTH):
        lambda_init = 0.8 - 0.6 * math.exp(-0.3 * l)
        proj = h @ w_in[l]
        q, k, v, cb, cc, cx, gates = jnp.split(proj, split_pts, axis=-1)
        q = apply_rope(q.reshape(B, S, ATT_HEADS, 2, ATT_HEAD_DIM), cos, sin)
        k = apply_rope(k.reshape(B, S, ATT_HEADS, 2, ATT_HEAD_DIM), cos, sin)
        v = v.reshape(B, S, ATT_HEADS, ATT_V_DIM)
        lam = (jnp.exp(jnp.sum(lambda_q1[l].astype(jnp.float32) * lambda_k1[l].astype(jnp.float32)))
               - jnp.exp(jnp.sum(lambda_q2[l].astype(jnp.float32) * lambda_k2[l].astype(jnp.float32)))
               + lambda_init)
        o = diff_attention(q, k, v, lam).astype(jnp.float32)
        o = o * lax.rsqrt(jnp.mean(jnp.square(o), -1, keepdims=True) + SUBLN_EPS)
        o = (o * subln_g[l].astype(jnp.float32) * (1.0 - lambda_init)).astype(h.dtype)
        y_att = o.reshape(B, S, ATT_V_WIDTH) @ w_o_att[l]
        y_conv = (cb * short_conv(cc * cx, conv_w[l])) @ w_o_conv[l]
        g = jax.nn.sigmoid(gates.astype(jnp.float32) + b_gate[l].astype(jnp.float32)).reshape(B, S, N_BRANCHES, D)
        merged = (g[:, :, 0] * y_att.astype(jnp.float32) + g[:, :, 1] * y_conv.astype(jnp.float32)).astype(h.dtype)
        h = layer_norm(DEEPNORM_ALPHA * h + merged @ w_mix_out[l], ln1_g[l], ln1_b[l])
        ffn = hier_moe(h, w_router_group[l], b_router_group[l], w_router_expert[l], b_router_expert[l],
                       w_exp_gate[l], w_exp_up[l], w_exp_down[l])
        h = layer_norm(DEEPNORM_ALPHA * h + ffn, ln2_g[l], ln2_b[l])
    return h
```

```python
import functools
import math

import jax
import jax.numpy as jnp
from jax import lax
from jax.experimental import pallas as pl
from jax.experimental.pallas import tpu as pltpu

D_MODEL = 1024
ATT_HEADS = 4
HEAD_DIM = 64
HEAD_W = 2 * HEAD_DIM
QK_W = ATT_HEADS * HEAD_W
CONV_W = D_MODEL // 2
CONV_K = 3
N_GROUPS = 4
EPG = 8
N_EXPERTS = N_GROUPS * EPG
EXPERT_FF = D_MODEL // 2
ROPE_THETA = 10000.0
SUBLN_EPS = 1e-5
LN_EPS = 1e-5
DEPTH = 1
DEEPNORM_ALPHA = (2.0 * DEPTH) ** 0.25

LANES = 128
SUBLANES = 8
VMEM_LIMIT = 48 * 1024 * 1024

PROJ_TM = 512
ATT_TQ = 256
ATT_TK = 256
MIX_TM = 256
MOE_BM = 128
DISP_TM = 1024
COMB_TM = 256
NEG_BIG = -1e30

F32 = jnp.float32
BF16 = jnp.bfloat16


def _dot(a, b):
    return jnp.dot(a, b, preferred_element_type=F32)


def _dot_nt(a, b):
    return lax.dot_general(a, b, (((1,), (1,)), ((), ())), preferred_element_type=F32)


def _proj_kernel(x_ref, w_ref, wvt_ref, cos_ref, sin_ref, cw_ref,
                 q_ref, k_ref, vt_ref, u_ref, pbuf):
    tm = x_ref.shape[0]
    i = pl.program_id(1)
    xb = x_ref[...].astype(BF16)
    cos = cos_ref[...]
    sin = sin_ref[...]
    lane = lax.broadcasted_iota(jnp.int32, (tm, LANES), 1)
    low_half = (lane % HEAD_DIM) < (HEAD_DIM // 2)

    def rope(t):
        rot = jnp.where(low_half, pltpu.roll(t, LANES - HEAD_DIM // 2, 1),
                        pltpu.roll(t, HEAD_DIM // 2, 1))
        return t * cos + rot * sin

    q = _dot(xb, w_ref[:, 0:QK_W])
    for h in range(ATT_HEADS):
        sl = slice(h * HEAD_W, (h + 1) * HEAD_W)
        q_ref[:, sl] = (rope(q[:, sl]) * (HEAD_DIM ** -0.5)).astype(BF16)
    k = _dot(xb, w_ref[:, QK_W:2 * QK_W])
    for h in range(ATT_HEADS):
        sl = slice(h * HEAD_W, (h + 1) * HEAD_W)
        k_ref[:, sl] = rope(k[:, sl]).astype(BF16)
    vt_ref[...] = _dot_nt(wvt_ref[...], xb).astype(BF16)

    c0 = 3 * QK_W
    cb = _dot(xb, w_ref[:, c0:c0 + CONV_W])
    cc = _dot(xb, w_ref[:, c0 + CONV_W:c0 + 2 * CONV_W])
    cx = _dot(xb, w_ref[:, c0 + 2 * CONV_W:c0 + 3 * CONV_W])
    p = cc * cx

    @pl.when(i == 0)
    def _():
        pbuf[0:SUBLANES, :] = jnp.zeros((SUBLANES, CONV_W), F32)

    pbuf[SUBLANES:SUBLANES + tm, :] = p
    y = (cw_ref[0:1, :] * pbuf[SUBLANES - 2:SUBLANES - 2 + tm, :]
         + cw_ref[1:2, :] * pbuf[SUBLANES - 1:SUBLANES - 1 + tm, :]
         + cw_ref[2:3, :] * p)
    u_ref[...] = (cb * y).astype(BF16)
    pbuf[0:SUBLANES, :] = pbuf[tm:tm + SUBLANES, :]


def _proj_call(x, w_main, wv_t, cos_t, sin_t, conv_w):
    B, S, D = x.shape
    tm = PROJ_TM
    n_main = w_main.shape[1]
    return pl.pallas_call(
        _proj_kernel,
        grid=(B, S // tm),
        in_specs=[
            pl.BlockSpec((None, tm, D), lambda b, i: (b, i, 0)),
            pl.BlockSpec((D, n_main), lambda b, i: (0, 0)),
            pl.BlockSpec((QK_W, D), lambda b, i: (0, 0)),
            pl.BlockSpec((None, tm, LANES), lambda b, i: (b, i, 0)),
            pl.BlockSpec((None, tm, LANES), lambda b, i: (b, i, 0)),
            pl.BlockSpec((CONV_K, CONV_W), lambda b, i: (0, 0)),
        ],
        out_specs=[
            pl.BlockSpec((None, tm, QK_W), lambda b, i: (b, i, 0)),
            pl.BlockSpec((None, tm, QK_W), lambda b, i: (b, i, 0)),
            pl.BlockSpec((None, QK_W, tm), lambda b, i: (b, 0, i)),
            pl.BlockSpec((None, tm, CONV_W), lambda b, i: (b, i, 0)),
        ],
        out_shape=[
            jax.ShapeDtypeStruct((B, S, QK_W), BF16),
            jax.ShapeDtypeStruct((B, S, QK_W), BF16),
            jax.ShapeDtypeStruct((B, QK_W, S), BF16),
            jax.ShapeDtypeStruct((B, S, CONV_W), BF16),
        ],
        scratch_shapes=[pltpu.VMEM((tm + 2 * SUBLANES, CONV_W), F32)],
        compiler_params=pltpu.CompilerParams(
            dimension_semantics=("arbitrary", "arbitrary"),
            vmem_limit_bytes=VMEM_LIMIT),
        name="proj_rope_conv",
    )(x, w_main, wv_t, cos_t, sin_t, conv_w)


def _attn_kernel(lam_ref, q_ref, k_ref, vt_ref, g_ref, o_ref, qzt_ref, m_ref, l_ref, acc_ref,
                 *, out_scale):
    tq = q_ref.shape[0]
    tk = ATT_TK
    i = pl.program_id(2)

    qf = q_ref[...].astype(F32)
    lane = lax.broadcasted_iota(jnp.int32, (tq, HEAD_W), 1)
    q1 = jnp.where(lane < HEAD_DIM, qf, 0.0)
    q2 = jnp.where(lane >= HEAD_DIM, qf, 0.0)
    qzt_ref[:, 0:tq] = q1.T.astype(BF16)
    qzt_ref[:, tq:2 * tq] = q2.T.astype(BF16)
    m_ref[...] = jnp.full(m_ref.shape, NEG_BIG, F32)
    l_ref[...] = jnp.zeros(l_ref.shape, F32)
    acc_ref[...] = jnp.zeros(acc_ref.shape, F32)

    def step(j, masked):
        k0 = pl.multiple_of(j * tk, tk)
        kj = k_ref[pl.ds(k0, tk), :]
        st = _dot(kj, qzt_ref[...])
        if masked:
            key = lax.broadcasted_iota(jnp.int32, (tk, 2 * tq), 0)
            col = lax.broadcasted_iota(jnp.int32, (tk, 2 * tq), 1)
            qry = jnp.where(col >= tq, col - tq, col)
            st = jnp.where(key <= qry, st, NEG_BIG)
        m_old = m_ref[...]
        m_new = jnp.maximum(m_old, jnp.max(st, axis=0, keepdims=True))
        alpha = jnp.exp(m_old - m_new)
        p = jnp.exp(st - m_new)
        l_ref[...] = alpha * l_ref[...] + jnp.sum(p, axis=0, keepdims=True)
        vj = vt_ref[:, pl.ds(k0, tk)]
        acc_ref[...] = alpha * acc_ref[...] + _dot(vj, p.astype(BF16))
        m_ref[...] = m_new

    def body(j, carry):
        step(j, False)
        return carry

    lax.fori_loop(0, i, body, 0)
    step(i, True)

    acc = acc_ref[...]
    l = l_ref[...]
    lam = lam_ref[0]
    o = acc[:, 0:tq] / l[:, 0:tq] - lam * (acc[:, tq:2 * tq] / l[:, tq:2 * tq])
    ms = jnp.mean(o * o, axis=0, keepdims=True)
    o = o * lax.rsqrt(ms + SUBLN_EPS) * g_ref[...] * out_scale
    o_ref[...] = o.T.astype(BF16)


def _attn_call(lam, q, k, vt, g_col, out_scale):
    B, S, _ = q.shape
    tq = ATT_TQ
    assert ATT_TQ == ATT_TK
    kernel = functools.partial(_attn_kernel, out_scale=out_scale)
    return pl.pallas_call(
        kernel,
        grid=(B, ATT_HEADS, S // tq),
        in_specs=[
            pl.BlockSpec(memory_space=pltpu.SMEM),
            pl.BlockSpec((None, tq, HEAD_W), lambda b, h, i: (b, i, h)),
            pl.BlockSpec((None, S, HEAD_W), lambda b, h, i: (b, 0, h)),
            pl.BlockSpec((None, HEAD_W, S), lambda b, h, i: (b, h, 0)),
            pl.BlockSpec((HEAD_W, 1), lambda b, h, i: (0, 0)),
        ],
        out_specs=pl.BlockSpec((None, tq, HEAD_W), lambda b, h, i: (b, i, h)),
        out_shape=jax.ShapeDtypeStruct((B, S, QK_W), BF16),
        scratch_shapes=[
            pltpu.VMEM((HEAD_W, 2 * tq), BF16),
            pltpu.VMEM((1, 2 * tq), F32),
            pltpu.VMEM((1, 2 * tq), F32),
            pltpu.VMEM((HEAD_W, 2 * tq), F32),
        ],
        compiler_params=pltpu.CompilerParams(
            dimension_semantics=("arbitrary", "arbitrary", "arbitrary"),
            vmem_limit_bytes=VMEM_LIMIT),
        name="diff_flash_attn",
    )(lam, q, k, vt, g_col)


def _pack_bf16_pairs(h):
    half = h.shape[1] // 2
    hb = h.astype(BF16).astype(F32)
    bits = lax.bitcast_convert_type(hb, jnp.uint32)
    return (bits[:, 0:half] >> 16) | (bits[:, half:] & jnp.uint32(0xFFFF0000))


def _unpack_bf16_pairs(w):
    lo = lax.bitcast_convert_type(w << 16, F32).astype(BF16)
    hi = lax.bitcast_convert_type(w & jnp.uint32(0xFFFF0000), F32).astype(BF16)
    return jnp.concatenate([lo, hi], axis=1)


def _first_index_of(mask, row_f, big):
    return jnp.min(jnp.where(mask, row_f, big), axis=0, keepdims=True)


def _mix_kernel(x_ref, o_ref, u_ref, wg_ref, bg_ref, woa_ref, woc_ref, wmo_ref, g1_ref, b1_ref,
                wr_ref, br_ref,
                h1_ref, h1p_ref, ri_ref, rf_ref, cnt_ref, carry_ref):
    tm = x_ref.shape[0]
    step = pl.program_id(0)

    @pl.when(step == 0)
    def _():
        carry_ref[...] = jnp.zeros(carry_ref.shape, F32)

    x = x_ref[...]
    xb = x.astype(BF16)
    gates = jax.nn.sigmoid(_dot(xb, wg_ref[...]) + bg_ref[...])
    ya = _dot(o_ref[...], woa_ref[...])
    yc = _dot(u_ref[...], woc_ref[...])
    merged = (gates[:, 0:D_MODEL] * ya + gates[:, D_MODEL:] * yc).astype(BF16)
    z = DEEPNORM_ALPHA * x + _dot(merged, wmo_ref[...])
    mu = jnp.mean(z, axis=-1, keepdims=True)
    zc = z - mu
    var = jnp.mean(zc * zc, axis=-1, keepdims=True)
    h1 = zc * lax.rsqrt(var + LN_EPS) * g1_ref[...] + b1_ref[...]
    h1_ref[...] = h1
    h1p_ref[...] = _pack_bf16_pairs(h1)

    logits = _dot(h1.astype(BF16), wr_ref[...]) + br_ref[...]
    lt = logits.T
    row8 = lax.broadcasted_iota(jnp.int32, (SUBLANES, tm), 0).astype(F32)
    gl = jnp.where(row8 < N_GROUPS, lt[0:SUBLANES, :], NEG_BIG)
    gmax = jnp.max(gl, axis=0, keepdims=True)
    gsel = _first_index_of(gl == gmax, row8, float(SUBLANES))
    gw = 1.0 / jnp.sum(jnp.exp(gl - gmax), axis=0, keepdims=True)
    el = lt[SUBLANES:2 * SUBLANES, :]
    for g in range(1, N_GROUPS):
        el = jnp.where(gsel == float(g), lt[(g + 1) * SUBLANES:(g + 2) * SUBLANES, :], el)
    v1 = jnp.max(el, axis=0, keepdims=True)
    i1 = _first_index_of(el == v1, row8, float(EPG))
    el2 = jnp.where(row8 == i1, -jnp.inf, el)
    v2 = jnp.max(el2, axis=0, keepdims=True)
    i2 = _first_index_of(el2 == v2, row8, float(EPG))
    t = jnp.exp(v2 - v1)
    p1 = gw / (1.0 + t)
    p2 = gw * t / (1.0 + t)
    e1 = gsel * float(EPG) + i1
    e2 = gsel * float(EPG) + i2

    rowe = lax.broadcasted_iota(jnp.int32, (N_EXPERTS, tm), 0).astype(F32)
    oh1 = (rowe == e1).astype(F32)
    oh2 = (rowe == e2).astype(F32)
    r_i = lax.broadcasted_iota(jnp.int32, (tm, tm), 0)
    c_i = lax.broadcasted_iota(jnp.int32, (tm, tm), 1)
    upper = jnp.where(r_i < c_i, 1.0, 0.0).astype(BF16)
    cum1 = _dot(oh1.astype(BF16), upper)
    cum2 = _dot(oh2.astype(BF16), upper)
    tot1 = jnp.sum(oh1, axis=1, keepdims=True)
    tot2 = jnp.sum(oh2, axis=1, keepdims=True)
    carry = carry_ref[:, 0:1]
    rank1 = jnp.sum(oh1 * (carry + cum1), axis=0, keepdims=True)
    rank2 = jnp.sum(oh2 * (carry + tot1 + cum2), axis=0, keepdims=True)
    new_carry = carry_ref[...] + (tot1 + tot2)
    carry_ref[...] = new_carry
    cnt_ref[...] = new_carry.astype(jnp.int32)

    ri_ref[...] = jnp.zeros(ri_ref.shape, jnp.int32)
    ri_ref[0:1, :] = e1.astype(jnp.int32)
    ri_ref[1:2, :] = e2.astype(jnp.int32)
    ri_ref[2:3, :] = rank1.astype(jnp.int32)
    ri_ref[3:4, :] = rank2.astype(jnp.int32)

    row128 = lax.broadcasted_iota(jnp.int32, (LANES, tm), 0)
    pw = jnp.where(row128 == 0, p1, jnp.where(row128 == 1, p2, 0.0))
    rf_ref[...] = pw.T


def _mix_call(x2, o_n, u, wg, bg, woa, woc, wmo, g1, b1, wr, br):
    T, D = x2.shape
    tm = MIX_TM
    const = lambda i: (0, 0)
    row = lambda i: (i, 0)
    return pl.pallas_call(
        _mix_kernel,
        grid=(T // tm,),
        in_specs=[
            pl.BlockSpec((tm, D), row),
            pl.BlockSpec((tm, QK_W), row),
            pl.BlockSpec((tm, CONV_W), row),
            pl.BlockSpec(wg.shape, const),
            pl.BlockSpec(bg.shape, const),
            pl.BlockSpec(woa.shape, const),
            pl.BlockSpec(woc.shape, const),
            pl.BlockSpec(wmo.shape, const),
            pl.BlockSpec(g1.shape, const),
            pl.BlockSpec(b1.shape, const),
            pl.BlockSpec(wr.shape, const),
            pl.BlockSpec(br.shape, const),
        ],
        out_specs=[
            pl.BlockSpec((tm, D), row),
            pl.BlockSpec((tm, D // 2), row),
            pl.BlockSpec((SUBLANES, tm), lambda i: (0, i)),
            pl.BlockSpec((tm, LANES), row),
            pl.BlockSpec((N_EXPERTS, LANES), const),
        ],
        out_shape=[
            jax.ShapeDtypeStruct((T, D), F32),
            jax.ShapeDtypeStruct((T, D // 2), jnp.uint32),
            jax.ShapeDtypeStruct((SUBLANES, T), jnp.int32),
            jax.ShapeDtypeStruct((T, LANES), F32),
            jax.ShapeDtypeStruct((N_EXPERTS, LANES), jnp.int32),
        ],
        scratch_shapes=[pltpu.VMEM((N_EXPERTS, LANES), F32)],
        compiler_params=pltpu.CompilerParams(
            dimension_semantics=("arbitrary",),
            vmem_limit_bytes=VMEM_LIMIT),
        name="mix_ln_router",
    )(x2, o_n, u, wg, bg, woa, woc, wmo, g1, b1, wr, br)


def _dispatch_kernel(dest_ref, h1p_ref, xs_in_ref, xs_ref, sem):
    del xs_in_ref
    n = dest_ref.shape[2]
    base = pl.program_id(0) * (n // 2)

    def row_copy(a):
        return pltpu.make_async_copy(
            h1p_ref.at[pl.ds(base + a // 2, 1), :],
            xs_ref.at[pl.ds(dest_ref[0, 0, a], 1), :],
            sem)

    def issue(a, carry):
        row_copy(a).start()
        return carry

    lax.fori_loop(0, n, issue, 0, unroll=8)
    pltpu.make_async_copy(h1p_ref.at[pl.ds(0, n), :], xs_ref.at[pl.ds(0, n), :], sem).wait()


def _dispatch_call(dest3, h1p, xs_zero):
    n_steps, _, n = dest3.shape
    return pl.pallas_call(
        _dispatch_kernel,
        grid=(n_steps,),
        in_specs=[
            pl.BlockSpec((1, 1, n), lambda i: (i, 0, 0), memory_space=pltpu.SMEM),
            pl.BlockSpec(memory_space=pl.ANY),
            pl.BlockSpec(memory_space=pl.ANY),
        ],
        out_specs=pl.BlockSpec(memory_space=pl.ANY),
        out_shape=jax.ShapeDtypeStruct(xs_zero.shape, xs_zero.dtype),
        scratch_shapes=[pltpu.SemaphoreType.DMA],
        input_output_aliases={2: 0},
        compiler_params=pltpu.CompilerParams(
            dimension_semantics=("arbitrary",), has_side_effects=True),
        name="moe_dispatch",
    )(dest3, h1p, xs_zero)


def _expert_kernel(be_ref, nu_ref, xs_ref, wg_ref, wu_ref, wd_ref, ys_ref):
    i = pl.program_id(0)

    @pl.when(i < nu_ref[0])
    def _():
        xb = _unpack_bf16_pairs(xs_ref[...])
        g = _dot(xb, wg_ref[...])
        u = _dot(xb, wu_ref[...])
        h = (g * jax.nn.sigmoid(g) * u).astype(BF16)
        ys_ref[...] = _dot(h, wd_ref[...])

    @pl.when(i >= nu_ref[0])
    def _():
        ys_ref[...] = jnp.zeros(ys_ref.shape, F32)


def _expert_call(block_expert, n_used, xs, wg, wu, wd):
    P = xs.shape[0]
    bm = MOE_BM
    nb = P // bm
    xs_map = lambda i, be, nu: (jnp.minimum(i, nu[0] - 1), 0)
    w_map = lambda i, be, nu: (be[i], 0, 0)
    return pl.pallas_call(
        _expert_kernel,
        grid_spec=pltpu.PrefetchScalarGridSpec(
            num_scalar_prefetch=2,
            grid=(nb,),
            in_specs=[
                pl.BlockSpec((bm, D_MODEL // 2), xs_map),
                pl.BlockSpec((None, D_MODEL, EXPERT_FF), w_map),
                pl.BlockSpec((None, D_MODEL, EXPERT_FF), w_map),
                pl.BlockSpec((None, EXPERT_FF, D_MODEL), w_map),
            ],
            out_specs=pl.BlockSpec((bm, D_MODEL), lambda i, be, nu: (i, 0)),
        ),
        out_shape=jax.ShapeDtypeStruct((P, D_MODEL), F32),
        compiler_params=pltpu.CompilerParams(
            dimension_semantics=("arbitrary",),
            vmem_limit_bytes=VMEM_LIMIT),
        name="moe_experts",
    )(block_expert, n_used, xs, wg, wu, wd)


def _combine_kernel(dest_ref, ys_ref, h1_ref, rf_ref, g2_ref, b2_ref, out_ref, buf_a, buf_b, sem):
    tm = h1_ref.shape[0]

    def row_copy(r, k, buf):
        return pltpu.make_async_copy(
            ys_ref.at[pl.ds(dest_ref[0, 0, 2 * r + k], 1), :],
            buf.at[pl.ds(r, 1), :],
            sem)

    def issue(r, carry):
        row_copy(r, 0, buf_a).start()
        row_copy(r, 1, buf_b).start()
        return carry

    lax.fori_loop(0, tm, issue, 0, unroll=8)
    pltpu.make_async_copy(ys_ref.at[pl.ds(0, tm), :], buf_a, sem).wait()
    pltpu.make_async_copy(ys_ref.at[pl.ds(0, tm), :], buf_b, sem).wait()

    rf = rf_ref[...]
    ffn = rf[:, 0:1] * buf_a[...] + rf[:, 1:2] * buf_b[...]
    z = DEEPNORM_ALPHA * h1_ref[...] + ffn
    mu = jnp.mean(z, axis=-1, keepdims=True)
    zc = z - mu
    var = jnp.mean(zc * zc, axis=-1, keepdims=True)
    out_ref[...] = zc * lax.rsqrt(var + LN_EPS) * g2_ref[...] + b2_ref[...]


def _combine_call(dest3, ys, h1, rf, g2, b2):
    T, D = h1.shape
    tm = COMB_TM
    const = lambda i: (0, 0)
    row = lambda i: (i, 0)
    return pl.pallas_call(
        _combine_kernel,
        grid=(T // tm,),
        in_specs=[
            pl.BlockSpec((1, 1, 2 * tm), lambda i: (i, 0, 0), memory_space=pltpu.SMEM),
            pl.BlockSpec(memory_space=pl.ANY),
            pl.BlockSpec((tm, D), row),
            pl.BlockSpec((tm, LANES), row),
            pl.BlockSpec(g2.shape, const),
            pl.BlockSpec(b2.shape, const),
        ],
        out_specs=pl.BlockSpec((tm, D), row),
        out_shape=jax.ShapeDtypeStruct((T, D), F32),
        scratch_shapes=[
            pltpu.VMEM((tm, D), F32),
            pltpu.VMEM((tm, D), F32),
            pltpu.SemaphoreType.DMA,
        ],
        compiler_params=pltpu.CompilerParams(
            dimension_semantics=("arbitrary",),
            vmem_limit_bytes=VMEM_LIMIT),
        name="moe_combine_ln",
    )(dest3, ys, h1, rf, g2, b2)


def kernel(x, positions, w_in, b_gate, lambda_q1, lambda_k1, lambda_q2, lambda_k2, subln_g, w_o_att, conv_w, w_o_conv, w_mix_out, ln1_g, ln1_b, w_router_group, b_router_group, w_router_expert, b_router_expert, w_exp_gate, w_exp_up, w_exp_down, ln2_g, ln2_b):
    B, S, D = x.shape
    T = B * S
    l = 0
    lambda_init = 0.8 - 0.6 * math.exp(-0.3 * l)

    inv_freq = ROPE_THETA ** (-jnp.arange(0, HEAD_DIM, 2, dtype=F32) / HEAD_DIM)
    ang = positions.astype(F32)[..., None] * inv_freq
    cos_t = jnp.tile(jnp.cos(ang), (1, 1, 4))
    sin_h = jnp.sin(ang)
    sin_t = jnp.tile(jnp.concatenate([-sin_h, sin_h], -1), (1, 1, 2))

    n_main = 2 * QK_W + QK_W + 3 * CONV_W
    w0 = w_in[l]
    w_main = w0[:, :n_main].astype(BF16)
    wv_t = w0[:, 2 * QK_W:3 * QK_W].T.astype(BF16)
    w_gates = w0[:, n_main:].astype(BF16)

    q, k, vt, u = _proj_call(x, w_main, wv_t, cos_t, sin_t, conv_w[l])

    lam = (jnp.exp(jnp.sum(lambda_q1[l].astype(F32) * lambda_k1[l].astype(F32)))
           - jnp.exp(jnp.sum(lambda_q2[l].astype(F32) * lambda_k2[l].astype(F32)))
           + lambda_init).reshape(1)
    o_n = _attn_call(lam, q, k, vt, subln_g[l].astype(F32).reshape(HEAD_W, 1), 1.0 - lambda_init)

    wr = jnp.zeros((D, LANES), F32)
    wr = wr.at[:, 0:N_GROUPS].set(w_router_group[l]).at[:, SUBLANES:SUBLANES + N_EXPERTS].set(w_router_expert[l])
    br = jnp.zeros((1, LANES), F32)
    br = br.at[0, 0:N_GROUPS].set(b_router_group[l]).at[0, SUBLANES:SUBLANES + N_EXPERTS].set(b_router_expert[l])

    h1, h1p, ri, rf, cnt = _mix_call(
        x.reshape(T, D), o_n.reshape(T, QK_W), u.reshape(T, CONV_W),
        w_gates, b_gate[l].reshape(1, -1).astype(F32),
        w_o_att[l].astype(BF16), w_o_conv[l].astype(BF16), w_mix_out[l].astype(BF16),
        ln1_g[l].reshape(1, D).astype(F32), ln1_b[l].reshape(1, D).astype(F32),
        wr.astype(BF16), br)

    bm = MOE_BM
    counts = cnt[:, 0]
    padded = (counts + bm - 1) // bm * bm
    pend = jnp.cumsum(padded)
    pstart = pend - padded
    dest = jnp.stack([pstart[ri[0]] + ri[2], pstart[ri[1]] + ri[3]], axis=-1)
    A = 2 * T
    P = A + N_EXPERTS * bm
    nb = P // bm
    block_expert = jnp.minimum(
        jnp.searchsorted(pend, jnp.arange(nb, dtype=jnp.int32) * bm, side='right'),
        N_EXPERTS - 1).astype(jnp.int32)
    n_used = (pend[-1] // bm).astype(jnp.int32).reshape(1)

    xs = _dispatch_call(dest.reshape(T // DISP_TM, 1, 2 * DISP_TM), h1p,
                        jnp.zeros((P, D // 2), jnp.uint32))
    ys = _expert_call(block_expert, n_used, xs,
                      w_exp_gate[l].astype(BF16), w_exp_up[l].astype(BF16), w_exp_down[l].astype(BF16))
    out = _combine_call(dest.reshape(T // COMB_TM, 1, 2 * COMB_TM), ys, h1, rf,
                        ln2_g[l].reshape(1, D).astype(F32), ln2_b[l].reshape(1, D).astype(F32))
    return out.reshape(B, S, D)
```

```python
import functools
import math

import jax
import jax.numpy as jnp
from jax import lax
from jax.experimental import pallas as pl
from jax.experimental.pallas import tpu as pltpu

D_MODEL = 1024
ATT_HEADS = 4
HEAD_DIM = 64
HEAD_W = 2 * HEAD_DIM
QK_W = ATT_HEADS * HEAD_W
CONV_W = D_MODEL // 2
CONV_K = 3
N_GROUPS = 4
EPG = 8
N_EXPERTS = N_GROUPS * EPG
EXPERT_FF = D_MODEL // 2
ROPE_THETA = 10000.0
SUBLN_EPS = 1e-5
LN_EPS = 1e-5
DEPTH = 1
DEEPNORM_ALPHA = (2.0 * DEPTH) ** 0.25

LANES = 128
SUBLANES = 8
VMEM_LIMIT = 48 * 1024 * 1024

PROJ_TM = 512
ATT_TQ = 256
ATT_TK = 256
MIX_TM = 256
MOE_BM = 128
DISP_TM = 1024
COMB_TM = 256
NEG_BIG = -1e30

F32 = jnp.float32
BF16 = jnp.bfloat16


def _dot(a, b):
    return jnp.dot(a, b, preferred_element_type=F32)


def _dot_nt(a, b):
    return lax.dot_general(a, b, (((1,), (1,)), ((), ())), preferred_element_type=F32)


def _proj_kernel(x_ref, w_ref, wvt_ref, cos_ref, sin_ref, cw_ref,
                 q_ref, k_ref, vt_ref, u_ref, pbuf):
    tm = x_ref.shape[0]
    i = pl.program_id(1)
    xb = x_ref[...].astype(BF16)
    cos = cos_ref[...]
    sin = sin_ref[...]
    lane = lax.broadcasted_iota(jnp.int32, (tm, LANES), 1)
    low_half = (lane % HEAD_DIM) < (HEAD_DIM // 2)

    def rope(t):
        rot = jnp.where(low_half, pltpu.roll(t, LANES - HEAD_DIM // 2, 1),
                        pltpu.roll(t, HEAD_DIM // 2, 1))
        return t * cos + rot * sin

    q = _dot(xb, w_ref[:, 0:QK_W])
    for h in range(ATT_HEADS):
        sl = slice(h * HEAD_W, (h + 1) * HEAD_W)
        q_ref[:, sl] = (rope(q[:, sl]) * (HEAD_DIM ** -0.5)).astype(BF16)
    k = _dot(xb, w_ref[:, QK_W:2 * QK_W])
    for h in range(ATT_HEADS):
        sl = slice(h * HEAD_W, (h + 1) * HEAD_W)
        k_ref[:, sl] = rope(k[:, sl]).astype(BF16)
    vt_ref[...] = _dot_nt(wvt_ref[...], xb).astype(BF16)

    c0 = 3 * QK_W
    cb = _dot(xb, w_ref[:, c0:c0 + CONV_W])
    cc = _dot(xb, w_ref[:, c0 + CONV_W:c0 + 2 * CONV_W])
    cx = _dot(xb, w_ref[:, c0 + 2 * CONV_W:c0 + 3 * CONV_W])
    p = cc * cx

    @pl.when(i == 0)
    def _():
        pbuf[0:SUBLANES, :] = jnp.zeros((SUBLANES, CONV_W), F32)

    pbuf[SUBLANES:SUBLANES + tm, :] = p
    y = (cw_ref[0:1, :] * pbuf[SUBLANES - 2:SUBLANES - 2 + tm, :]
         + cw_ref[1:2, :] * pbuf[SUBLANES - 1:SUBLANES - 1 + tm, :]
         + cw_ref[2:3, :] * p)
    u_ref[...] = (cb * y).astype(BF16)
    pbuf[0:SUBLANES, :] = pbuf[tm:tm + SUBLANES, :]


def _proj_call(x, w_main, wv_t, cos_t, sin_t, conv_w):
    B, S, D = x.shape
    tm = PROJ_TM
    n_main = w_main.shape[1]
    return pl.pallas_call(
        _proj_kernel,
        grid=(B, S // tm),
        in_specs=[
            pl.BlockSpec((None, tm, D), lambda b, i: (b, i, 0)),
            pl.BlockSpec((D, n_main), lambda b, i: (0, 0)),
            pl.BlockSpec((QK_W, D), lambda b, i: (0, 0)),
            pl.BlockSpec((None, tm, LANES), lambda b, i: (b, i, 0)),
            pl.BlockSpec((None, tm, LANES), lambda b, i: (b, i, 0)),
            pl.BlockSpec((CONV_K, CONV_W), lambda b, i: (0, 0)),
        ],
        out_specs=[
            pl.BlockSpec((None, tm, QK_W), lambda b, i: (b, i, 0)),
            pl.BlockSpec((None, tm, QK_W), lambda b, i: (b, i, 0)),
            pl.BlockSpec((None, QK_W, tm), lambda b, i: (b, 0, i)),
            pl.BlockSpec((None, tm, CONV_W), lambda b, i: (b, i, 0)),
        ],
        out_shape=[
            jax.ShapeDtypeStruct((B, S, QK_W), BF16),
            jax.ShapeDtypeStruct((B, S, QK_W), BF16),
            jax.ShapeDtypeStruct((B, QK_W, S), BF16),
            jax.ShapeDtypeStruct((B, S, CONV_W), BF16),
        ],
        scratch_shapes=[pltpu.VMEM((tm + 2 * SUBLANES, CONV_W), F32)],
        compiler_params=pltpu.CompilerParams(
            dimension_semantics=("arbitrary", "arbitrary"),
            vmem_limit_bytes=VMEM_LIMIT),
        name="proj_rope_conv",
    )(x, w_main, wv_t, cos_t, sin_t, conv_w)


def _attn_kernel(lam_ref, q_ref, k_ref, vt_ref, g_ref, o_ref, qzt_ref, m_ref, l_ref, acc_ref,
                 *, out_scale):
    tq = q_ref.shape[0]
    tk = ATT_TK
    i = pl.program_id(1)
    lane = lax.broadcasted_iota(jnp.int32, (tq, HEAD_W), 1)

    for h in range(ATT_HEADS):
        qf = q_ref[:, h * HEAD_W:(h + 1) * HEAD_W].astype(F32)
        q1 = jnp.where(lane < HEAD_DIM, qf, 0.0)
        q2 = jnp.where(lane >= HEAD_DIM, qf, 0.0)
        qzt_ref[h, :, 0:tq] = q1.T.astype(BF16)
        qzt_ref[h, :, tq:2 * tq] = q2.T.astype(BF16)
    m_ref[...] = jnp.full(m_ref.shape, NEG_BIG, F32)
    l_ref[...] = jnp.zeros(l_ref.shape, F32)
    acc_ref[...] = jnp.zeros(acc_ref.shape, F32)

    def step(j, masked):
        k0 = pl.multiple_of(j * tk, tk)
        for h in range(ATT_HEADS):
            kj = k_ref[pl.ds(k0, tk), h * HEAD_W:(h + 1) * HEAD_W]
            st = _dot(kj, qzt_ref[h])
            if masked:
                key = lax.broadcasted_iota(jnp.int32, (tk, 2 * tq), 0)
                col = lax.broadcasted_iota(jnp.int32, (tk, 2 * tq), 1)
                qry = jnp.where(col >= tq, col - tq, col)
                st = jnp.where(key <= qry, st, NEG_BIG)
            m_old = m_ref[h]
            m_new = jnp.maximum(m_old, jnp.max(st, axis=0, keepdims=True))
            alpha = jnp.exp(m_old - m_new)
            p = jnp.exp(st - m_new)
            l_ref[h] = alpha * l_ref[h] + jnp.sum(p, axis=0, keepdims=True)
            vj = vt_ref[h * HEAD_W:(h + 1) * HEAD_W, pl.ds(k0, tk)]
            acc_ref[h] = alpha * acc_ref[h] + _dot(vj, p.astype(BF16))
            m_ref[h] = m_new

    def body(j, carry):
        step(j, False)
        return carry

    lax.fori_loop(0, i, body, 0)
    step(i, True)

    lam = lam_ref[0]
    for h in range(ATT_HEADS):
        acc = acc_ref[h]
        l = l_ref[h]
        o = acc[:, 0:tq] / l[:, 0:tq] - lam * (acc[:, tq:2 * tq] / l[:, tq:2 * tq])
        ms = jnp.mean(o * o, axis=0, keepdims=True)
        o = o * lax.rsqrt(ms + SUBLN_EPS) * g_ref[...] * out_scale
        o_ref[:, h * HEAD_W:(h + 1) * HEAD_W] = o.T.astype(BF16)


def _attn_call(lam, q, k, vt, g_col, out_scale):
    B, S, _ = q.shape
    tq = ATT_TQ
    assert ATT_TQ == ATT_TK
    kernel = functools.partial(_attn_kernel, out_scale=out_scale)
    return pl.pallas_call(
        kernel,
        grid=(B, S // tq),
        in_specs=[
            pl.BlockSpec(memory_space=pltpu.SMEM),
            pl.BlockSpec((None, tq, QK_W), lambda b, i: (b, i, 0)),
            pl.BlockSpec((None, S, QK_W), lambda b, i: (b, 0, 0), pipeline_mode=pl.Buffered(1)),
            pl.BlockSpec((None, QK_W, S), lambda b, i: (b, 0, 0), pipeline_mode=pl.Buffered(1)),
            pl.BlockSpec((HEAD_W, 1), lambda b, i: (0, 0)),
        ],
        out_specs=pl.BlockSpec((None, tq, QK_W), lambda b, i: (b, i, 0)),
        out_shape=jax.ShapeDtypeStruct((B, S, QK_W), BF16),
        scratch_shapes=[
            pltpu.VMEM((ATT_HEADS, HEAD_W, 2 * tq), BF16),
            pltpu.VMEM((ATT_HEADS, 1, 2 * tq), F32),
            pltpu.VMEM((ATT_HEADS, 1, 2 * tq), F32),
            pltpu.VMEM((ATT_HEADS, HEAD_W, 2 * tq), F32),
        ],
        compiler_params=pltpu.CompilerParams(
            dimension_semantics=("arbitrary", "arbitrary"),
            vmem_limit_bytes=VMEM_LIMIT),
        name="diff_flash_attn",
    )(lam, q, k, vt, g_col)


def _pack_bf16_pairs(h):
    half = h.shape[1] // 2
    hb = h.astype(BF16).astype(F32)
    bits = lax.bitcast_convert_type(hb, jnp.uint32)
    return (bits[:, 0:half] >> 16) | (bits[:, half:] & jnp.uint32(0xFFFF0000))


def _unpack_bf16_pairs(w):
    lo = lax.bitcast_convert_type(w << 16, F32).astype(BF16)
    hi = lax.bitcast_convert_type(w & jnp.uint32(0xFFFF0000), F32).astype(BF16)
    return jnp.concatenate([lo, hi], axis=1)


def _first_index_of(mask, row_f, big):
    return jnp.min(jnp.where(mask, row_f, big), axis=0, keepdims=True)


def _mix_kernel(x_ref, o_ref, u_ref, wg_ref, bg_ref, woa_ref, woc_ref, wmo_ref, g1_ref, b1_ref,
                wr_ref, br_ref,
                h1_ref, h1p_ref, ri_ref, rf_ref, cnt_ref, carry_ref):
    tm = x_ref.shape[0]
    step = pl.program_id(0)

    @pl.when(step == 0)
    def _():
        carry_ref[...] = jnp.zeros(carry_ref.shape, F32)

    x = x_ref[...]
    xb = x.astype(BF16)
    gates = jax.nn.sigmoid(_dot(xb, wg_ref[...]) + bg_ref[...])
    ya = _dot(o_ref[...], woa_ref[...])
    yc = _dot(u_ref[...], woc_ref[...])
    merged = (gates[:, 0:D_MODEL] * ya + gates[:, D_MODEL:] * yc).astype(BF16)
    z = DEEPNORM_ALPHA * x + _dot(merged, wmo_ref[...])
    mu = jnp.mean(z, axis=-1, keepdims=True)
    zc = z - mu
    var = jnp.mean(zc * zc, axis=-1, keepdims=True)
    h1 = zc * lax.rsqrt(var + LN_EPS) * g1_ref[...] + b1_ref[...]
    h1_ref[...] = h1
    h1p_ref[...] = _pack_bf16_pairs(h1)

    logits = _dot(h1.astype(BF16), wr_ref[...]) + br_ref[...]
    lt = logits.T
    row8 = lax.broadcasted_iota(jnp.int32, (SUBLANES, tm), 0).astype(F32)
    gl = jnp.where(row8 < N_GROUPS, lt[0:SUBLANES, :], NEG_BIG)
    gmax = jnp.max(gl, axis=0, keepdims=True)
    gsel = _first_index_of(gl == gmax, row8, float(SUBLANES))
    gw = 1.0 / jnp.sum(jnp.exp(gl - gmax), axis=0, keepdims=True)
    el = lt[SUBLANES:2 * SUBLANES, :]
    for g in range(1, N_GROUPS):
        el = jnp.where(gsel == float(g), lt[(g + 1) * SUBLANES:(g + 2) * SUBLANES, :], el)
    v1 = jnp.max(el, axis=0, keepdims=True)
    i1 = _first_index_of(el == v1, row8, float(EPG))
    el2 = jnp.where(row8 == i1, -jnp.inf, el)
    v2 = jnp.max(el2, axis=0, keepdims=True)
    i2 = _first_index_of(el2 == v2, row8, float(EPG))
    t = jnp.exp(v2 - v1)
    p1 = gw / (1.0 + t)
    p2 = gw * t / (1.0 + t)
    e1 = gsel * float(EPG) + i1
    e2 = gsel * float(EPG) + i2

    rowe = lax.broadcasted_iota(jnp.int32, (N_EXPERTS, tm), 0).astype(F32)
    oh1 = (rowe == e1).astype(F32)
    oh2 = (rowe == e2).astype(F32)
    r_i = lax.broadcasted_iota(jnp.int32, (tm, tm), 0)
    c_i = lax.broadcasted_iota(jnp.int32, (tm, tm), 1)
    upper = jnp.where(r_i < c_i, 1.0, 0.0).astype(BF16)
    cum1 = _dot(oh1.astype(BF16), upper)
    cum2 = _dot(oh2.astype(BF16), upper)
    tot1 = jnp.sum(oh1, axis=1, keepdims=True)
    tot2 = jnp.sum(oh2, axis=1, keepdims=True)
    carry = carry_ref[:, 0:1]
    rank1 = jnp.sum(oh1 * (carry + cum1), axis=0, keepdims=True)
    rank2 = jnp.sum(oh2 * (carry + tot1 + cum2), axis=0, keepdims=True)
    new_carry = carry_ref[...] + (tot1 + tot2)
    carry_ref[...] = new_carry
    cnt_ref[...] = new_carry.astype(jnp.int32)

    ri_ref[...] = jnp.zeros(ri_ref.shape, jnp.int32)
    ri_ref[0:1, :] = e1.astype(jnp.int32)
    ri_ref[1:2, :] = e2.astype(jnp.int32)
    ri_ref[2:3, :] = rank1.astype(jnp.int32)
    ri_ref[3:4, :] = rank2.astype(jnp.int32)

    row128 = lax.broadcasted_iota(jnp.int32, (LANES, tm), 0)
    pw = jnp.where(row128 == 0, p1, jnp.where(row128 == 1, p2, 0.0))
    rf_ref[...] = pw.T


def _mix_call(x2, o_n, u, wg, bg, woa, woc, wmo, g1, b1, wr, br):
    T, D = x2.shape
    tm = MIX_TM
    const = lambda i: (0, 0)
    row = lambda i: (i, 0)
    return pl.pallas_call(
        _mix_kernel,
        grid=(T // tm,),
        in_specs=[
            pl.BlockSpec((tm, D), row),
            pl.BlockSpec((tm, QK_W), row),
            pl.BlockSpec((tm, CONV_W), row),
            pl.BlockSpec(wg.shape, const),
            pl.BlockSpec(bg.shape, const),
            pl.BlockSpec(woa.shape, const),
            pl.BlockSpec(woc.shape, const),
            pl.BlockSpec(wmo.shape, const),
            pl.BlockSpec(g1.shape, const),
            pl.BlockSpec(b1.shape, const),
            pl.BlockSpec(wr.shape, const),
            pl.BlockSpec(br.shape, const),
        ],
        out_specs=[
            pl.BlockSpec((tm, D), row),
            pl.BlockSpec((tm, D // 2), row),
            pl.BlockSpec((SUBLANES, tm), lambda i: (0, i)),
            pl.BlockSpec((tm, LANES), row),
            pl.BlockSpec((N_EXPERTS, LANES), const),
        ],
        out_shape=[
            jax.ShapeDtypeStruct((T, D), F32),
            jax.ShapeDtypeStruct((T, D // 2), jnp.uint32),
            jax.ShapeDtypeStruct((SUBLANES, T), jnp.int32),
            jax.ShapeDtypeStruct((T, LANES), F32),
            jax.ShapeDtypeStruct((N_EXPERTS, LANES), jnp.int32),
        ],
        scratch_shapes=[pltpu.VMEM((N_EXPERTS, LANES), F32)],
        compiler_params=pltpu.CompilerParams(
            dimension_semantics=("arbitrary",),
            vmem_limit_bytes=VMEM_LIMIT),
        name="mix_ln_router",
    )(x2, o_n, u, wg, bg, woa, woc, wmo, g1, b1, wr, br)


def _dispatch_kernel(dest_ref, h1p_ref, xs_in_ref, xs_ref, sem):
    del xs_in_ref
    tm = h1p_ref.shape[0]

    def row_copy(r, k):
        return pltpu.make_async_copy(
            h1p_ref.at[pl.ds(r, 1), :],
            xs_ref.at[pl.ds(dest_ref[0, 0, 2 * r + k], 1), :],
            sem)

    def issue(r, carry):
        row_copy(r, 0).start()
        row_copy(r, 1).start()
        return carry

    lax.fori_loop(0, tm, issue, 0, unroll=8)
    for _ in range(2):
        pltpu.make_async_copy(h1p_ref, xs_ref.at[pl.ds(0, tm), :], sem).wait()


def _dispatch_call(dest3, h1p, xs_zero):
    n_steps, _, n = dest3.shape
    return pl.pallas_call(
        _dispatch_kernel,
        grid=(n_steps,),
        in_specs=[
            pl.BlockSpec((1, 1, n), lambda i: (i, 0, 0), memory_space=pltpu.SMEM),
            pl.BlockSpec((n // 2, h1p.shape[1]), lambda i: (i, 0)),
            pl.BlockSpec(memory_space=pl.ANY),
        ],
        out_specs=pl.BlockSpec(memory_space=pl.ANY),
        out_shape=jax.ShapeDtypeStruct(xs_zero.shape, xs_zero.dtype),
        scratch_shapes=[pltpu.SemaphoreType.DMA],
        input_output_aliases={2: 0},
        compiler_params=pltpu.CompilerParams(
            dimension_semantics=("arbitrary",), has_side_effects=True),
        name="moe_dispatch",
    )(dest3, h1p, xs_zero)


def _expert_kernel(be_ref, nu_ref, xs_ref, wg_ref, wu_ref, wd_ref, ys_ref):
    i = pl.program_id(0)

    @pl.when(i < nu_ref[0])
    def _():
        xb = _unpack_bf16_pairs(xs_ref[...])
        g = _dot(xb, wg_ref[...])
        u = _dot(xb, wu_ref[...])
        h = (g * jax.nn.sigmoid(g) * u).astype(BF16)
        ys_ref[...] = _dot(h, wd_ref[...])

    @pl.when(i >= nu_ref[0])
    def _():
        ys_ref[...] = jnp.zeros(ys_ref.shape, F32)


def _expert_call(block_expert, n_used, xs, wg, wu, wd):
    P = xs.shape[0]
    bm = MOE_BM
    nb = P // bm
    xs_map = lambda i, be, nu: (jnp.minimum(i, nu[0] - 1), 0)
    w_map = lambda i, be, nu: (be[i], 0, 0)
    return pl.pallas_call(
        _expert_kernel,
        grid_spec=pltpu.PrefetchScalarGridSpec(
            num_scalar_prefetch=2,
            grid=(nb,),
            in_specs=[
                pl.BlockSpec((bm, D_MODEL // 2), xs_map),
                pl.BlockSpec((None, D_MODEL, EXPERT_FF), w_map),
                pl.BlockSpec((None, D_MODEL, EXPERT_FF), w_map),
                pl.BlockSpec((None, EXPERT_FF, D_MODEL), w_map),
            ],
            out_specs=pl.BlockSpec((bm, D_MODEL), lambda i, be, nu: (i, 0)),
        ),
        out_shape=jax.ShapeDtypeStruct((P, D_MODEL), F32),
        compiler_params=pltpu.CompilerParams(
            dimension_semantics=("arbitrary",),
            vmem_limit_bytes=VMEM_LIMIT),
        name="moe_experts",
    )(block_expert, n_used, xs, wg, wu, wd)


def _combine_kernel(dest_ref, ys_ref, h1_ref, rf_ref, g2_ref, b2_ref, out_ref, buf_a, buf_b, sem):
    tm = h1_ref.shape[0]

    def row_copy(r, k, buf):
        return pltpu.make_async_copy(
            ys_ref.at[pl.ds(dest_ref[0, 0, 2 * r + k], 1), :],
            buf.at[pl.ds(r, 1), :],
            sem)

    def issue(r, carry):
        row_copy(r, 0, buf_a).start()
        row_copy(r, 1, buf_b).start()
        return carry

    lax.fori_loop(0, tm, issue, 0, unroll=8)
    pltpu.make_async_copy(ys_ref.at[pl.ds(0, tm), :], buf_a, sem).wait()
    pltpu.make_async_copy(ys_ref.at[pl.ds(0, tm), :], buf_b, sem).wait()

    rf = rf_ref[...]
    ffn = rf[:, 0:1] * buf_a[...] + rf[:, 1:2] * buf_b[...]
    z = DEEPNORM_ALPHA * h1_ref[...] + ffn
    mu = jnp.mean(z, axis=-1, keepdims=True)
    zc = z - mu
    var = jnp.mean(zc * zc, axis=-1, keepdims=True)
    out_ref[...] = zc * lax.rsqrt(var + LN_EPS) * g2_ref[...] + b2_ref[...]


def _combine_call(dest3, ys, h1, rf, g2, b2):
    T, D = h1.shape
    tm = COMB_TM
    const = lambda i: (0, 0)
    row = lambda i: (i, 0)
    return pl.pallas_call(
        _combine_kernel,
        grid=(T // tm,),
        in_specs=[
            pl.BlockSpec((1, 1, 2 * tm), lambda i: (i, 0, 0), memory_space=pltpu.SMEM),
            pl.BlockSpec(memory_space=pl.ANY),
            pl.BlockSpec((tm, D), row),
            pl.BlockSpec((tm, LANES), row),
            pl.BlockSpec(g2.shape, const),
            pl.BlockSpec(b2.shape, const),
        ],
        out_specs=pl.BlockSpec((tm, D), row),
        out_shape=jax.ShapeDtypeStruct((T, D), F32),
        scratch_shapes=[
            pltpu.VMEM((tm, D), F32),
            pltpu.VMEM((tm, D), F32),
            pltpu.SemaphoreType.DMA,
        ],
        compiler_params=pltpu.CompilerParams(
            dimension_semantics=("arbitrary",),
            vmem_limit_bytes=VMEM_LIMIT),
        name="moe_combine_ln",
    )(dest3, ys, h1, rf, g2, b2)


def kernel(x, positions, w_in, b_gate, lambda_q1, lambda_k1, lambda_q2, lambda_k2, subln_g, w_o_att, conv_w, w_o_conv, w_mix_out, ln1_g, ln1_b, w_router_group, b_router_group, w_router_expert, b_router_expert, w_exp_gate, w_exp_up, w_exp_down, ln2_g, ln2_b):
    B, S, D = x.shape
    T = B * S
    l = 0
    lambda_init = 0.8 - 0.6 * math.exp(-0.3 * l)

    inv_freq = ROPE_THETA ** (-jnp.arange(0, HEAD_DIM, 2, dtype=F32) / HEAD_DIM)
    ang = positions.astype(F32)[..., None] * inv_freq
    cos_t = jnp.tile(jnp.cos(ang), (1, 1, 4))
    sin_h = jnp.sin(ang)
    sin_t = jnp.tile(jnp.concatenate([-sin_h, sin_h], -1), (1, 1, 2))

    n_main = 2 * QK_W + QK_W + 3 * CONV_W
    w0 = w_in[l]
    w_main = w0[:, :n_main].astype(BF16)
    wv_t = w0[:, 2 * QK_W:3 * QK_W].T.astype(BF16)
    w_gates = w0[:, n_main:].astype(BF16)

    q, k, vt, u = _proj_call(x, w_main, wv_t, cos_t, sin_t, conv_w[l])

    lam = (jnp.exp(jnp.sum(lambda_q1[l].astype(F32) * lambda_k1[l].astype(F32)))
           - jnp.exp(jnp.sum(lambda_q2[l].astype(F32) * lambda_k2[l].astype(F32)))
           + lambda_init).reshape(1)
    o_n = _attn_call(lam, q, k, vt, subln_g[l].astype(F32).reshape(HEAD_W, 1), 1.0 - lambda_init)

    wr = jnp.zeros((D, LANES), F32)
    wr = wr.at[:, 0:N_GROUPS].set(w_router_group[l]).at[:, SUBLANES:SUBLANES + N_EXPERTS].set(w_router_expert[l])
    br = jnp.zeros((1, LANES), F32)
    br = br.at[0, 0:N_GROUPS].set(b_router_group[l]).at[0, SUBLANES:SUBLANES + N_EXPERTS].set(b_router_expert[l])

    h1, h1p, ri, rf, cnt = _mix_call(
        x.reshape(T, D), o_n.reshape(T, QK_W), u.reshape(T, CONV_W),
        w_gates, b_gate[l].reshape(1, -1).astype(F32),
        w_o_att[l].astype(BF16), w_o_conv[l].astype(BF16), w_mix_out[l].astype(BF16),
        ln1_g[l].reshape(1, D).astype(F32), ln1_b[l].reshape(1, D).astype(F32),
        wr.astype(BF16), br)

    bm = MOE_BM
    counts = cnt[:, 0]
    padded = (counts + bm - 1) // bm * bm
    pend = jnp.cumsum(padded)
    pstart = pend - padded
    dest = jnp.stack([pstart[ri[0]] + ri[2], pstart[ri[1]] + ri[3]], axis=-1)
    A = 2 * T
    P = A + N_EXPERTS * bm
    nb = P // bm
    blk_start = jnp.arange(nb, dtype=jnp.int32) * bm
    block_expert = jnp.minimum(
        jnp.sum((pend[None, :] <= blk_start[:, None]).astype(jnp.int32), axis=1), N_EXPERTS - 1)
    n_used = (pend[-1] // bm).astype(jnp.int32).reshape(1)

    xs = _dispatch_call(dest.reshape(T // DISP_TM, 1, 2 * DISP_TM), h1p,
                        jnp.zeros((P, D // 2), jnp.uint32))
    ys = _expert_call(block_expert, n_used, xs,
                      w_exp_gate[l].astype(BF16), w_exp_up[l].astype(BF16), w_exp_down[l].astype(BF16))
    out = _combine_call(dest.reshape(T // COMB_TM, 1, 2 * COMB_TM), ys, h1, rf,
                        ln2_g[l].reshape(1, D).astype(F32), ln2_b[l].reshape(1, D).astype(F32))
    return out.reshape(B, S, D)
```

```python
import functools
import math

import jax
import jax.numpy as jnp
from jax import lax
from jax.experimental import pallas as pl
from jax.experimental.pallas import tpu as pltpu

D_MODEL = 1024
ATT_HEADS = 4
HEAD_DIM = 64
HEAD_W = 2 * HEAD_DIM
QK_W = ATT_HEADS * HEAD_W
CONV_W = D_MODEL // 2
CONV_K = 3
N_GROUPS = 4
EPG = 8
N_EXPERTS = N_GROUPS * EPG
EXPERT_FF = D_MODEL // 2
ROPE_THETA = 10000.0
SUBLN_EPS = 1e-5
LN_EPS = 1e-5
DEPTH = 1
DEEPNORM_ALPHA = (2.0 * DEPTH) ** 0.25

LANES = 128
SUBLANES = 8
VMEM_LIMIT = 48 * 1024 * 1024

PROJ_TM = 512
ATT_TQ = 256
ATT_TK = 256
MIX_TM = 256
MOE_BM = 128
DISP_TM = 1024
COMB_TM = 256
NEG_BIG = -1e30

F32 = jnp.float32
BF16 = jnp.bfloat16


def _dot(a, b):
    return jnp.dot(a, b, preferred_element_type=F32)


def _dot_nt(a, b):
    return lax.dot_general(a, b, (((1,), (1,)), ((), ())), preferred_element_type=F32)


def _proj_kernel(x_ref, w_ref, wvt_ref, cos_ref, sin_ref, cw_ref,
                 q_ref, k_ref, vt_ref, u_ref, pbuf):
    tm = x_ref.shape[0]
    i = pl.program_id(1)
    xb = x_ref[...].astype(BF16)
    cos = cos_ref[...]
    sin = sin_ref[...]
    lane = lax.broadcasted_iota(jnp.int32, (tm, LANES), 1)
    low_half = (lane % HEAD_DIM) < (HEAD_DIM // 2)

    def rope(t):
        rot = jnp.where(low_half, pltpu.roll(t, LANES - HEAD_DIM // 2, 1),
                        pltpu.roll(t, HEAD_DIM // 2, 1))
        return t * cos + rot * sin

    q = _dot(xb, w_ref[:, 0:QK_W])
    for h in range(ATT_HEADS):
        sl = slice(h * HEAD_W, (h + 1) * HEAD_W)
        q_ref[:, sl] = (rope(q[:, sl]) * (HEAD_DIM ** -0.5 * math.log2(math.e))).astype(BF16)
    k = _dot(xb, w_ref[:, QK_W:2 * QK_W])
    for h in range(ATT_HEADS):
        sl = slice(h * HEAD_W, (h + 1) * HEAD_W)
        k_ref[:, sl] = rope(k[:, sl]).astype(BF16)
    vt_ref[...] = _dot_nt(wvt_ref[...], xb).astype(BF16)

    c0 = 3 * QK_W
    cb = _dot(xb, w_ref[:, c0:c0 + CONV_W])
    cc = _dot(xb, w_ref[:, c0 + CONV_W:c0 + 2 * CONV_W])
    cx = _dot(xb, w_ref[:, c0 + 2 * CONV_W:c0 + 3 * CONV_W])
    p = cc * cx

    @pl.when(i == 0)
    def _():
        pbuf[0:SUBLANES, :] = jnp.zeros((SUBLANES, CONV_W), F32)

    pbuf[SUBLANES:SUBLANES + tm, :] = p
    y = (cw_ref[0:1, :] * pbuf[SUBLANES - 2:SUBLANES - 2 + tm, :]
         + cw_ref[1:2, :] * pbuf[SUBLANES - 1:SUBLANES - 1 + tm, :]
         + cw_ref[2:3, :] * p)
    u_ref[...] = (cb * y).astype(BF16)
    pbuf[0:SUBLANES, :] = pbuf[tm:tm + SUBLANES, :]


def _proj_call(x, w_main, wv_t, cos_t, sin_t, conv_w):
    B, S, D = x.shape
    tm = PROJ_TM
    n_main = w_main.shape[1]
    return pl.pallas_call(
        _proj_kernel,
        grid=(B, S // tm),
        in_specs=[
            pl.BlockSpec((None, tm, D), lambda b, i: (b, i, 0)),
            pl.BlockSpec((D, n_main), lambda b, i: (0, 0)),
            pl.BlockSpec((QK_W, D), lambda b, i: (0, 0)),
            pl.BlockSpec((None, tm, LANES), lambda b, i: (b, i, 0)),
            pl.BlockSpec((None, tm, LANES), lambda b, i: (b, i, 0)),
            pl.BlockSpec((CONV_K, CONV_W), lambda b, i: (0, 0)),
        ],
        out_specs=[
            pl.BlockSpec((None, tm, QK_W), lambda b, i: (b, i, 0)),
            pl.BlockSpec((None, tm, QK_W), lambda b, i: (b, i, 0)),
            pl.BlockSpec((None, QK_W, tm), lambda b, i: (b, 0, i)),
            pl.BlockSpec((None, tm, CONV_W), lambda b, i: (b, i, 0)),
        ],
        out_shape=[
            jax.ShapeDtypeStruct((B, S, QK_W), BF16),
            jax.ShapeDtypeStruct((B, S, QK_W), BF16),
            jax.ShapeDtypeStruct((B, QK_W, S), BF16),
            jax.ShapeDtypeStruct((B, S, CONV_W), BF16),
        ],
        scratch_shapes=[pltpu.VMEM((tm + 2 * SUBLANES, CONV_W), F32)],
        compiler_params=pltpu.CompilerParams(
            dimension_semantics=("arbitrary", "arbitrary"),
            vmem_limit_bytes=VMEM_LIMIT),
        name="proj_rope_conv",
    )(x, w_main, wv_t, cos_t, sin_t, conv_w)


ACC_ROWS = HEAD_W + 16


def _attn_kernel(lam_ref, q_ref, k_ref, vt_ref, g_ref, o_ref, qzt_ref, *scratch, out_scale):
    nh = ATT_HEADS
    s_refs, p_refs = scratch[0:nh], scratch[nh:2 * nh]
    m_refs, a_refs, acc_refs = scratch[2 * nh:3 * nh], scratch[3 * nh:4 * nh], scratch[4 * nh:5 * nh]
    tq = q_ref.shape[0]
    tk = ATT_TK
    nblk = 2 * tq // LANES
    i = pl.program_id(1)
    lane = lax.broadcasted_iota(jnp.int32, (tq, HEAD_W), 1)
    ones_rows = jnp.ones((ACC_ROWS - HEAD_W, tk), BF16)

    for h in range(nh):
        qf = q_ref[:, h * HEAD_W:(h + 1) * HEAD_W].astype(F32)
        q1 = jnp.where(lane < HEAD_DIM, qf, 0.0)
        q2 = jnp.where(lane >= HEAD_DIM, qf, 0.0)
        qzt_ref[h, :, 0:tq] = q1.T.astype(BF16)
        qzt_ref[h, :, tq:2 * tq] = q2.T.astype(BF16)
        m_refs[h][...] = jnp.full(m_refs[h].shape, NEG_BIG, F32)
        acc_refs[h][...] = jnp.zeros(acc_refs[h].shape, F32)

    def scores(h, k0):
        kj = k_ref[pl.ds(k0, tk), h * HEAD_W:(h + 1) * HEAD_W]
        s_refs[h][...] = _dot(kj, qzt_ref[h])

    def softmax(h, masked):
        for c in range(nblk):
            cs = slice(c * LANES, (c + 1) * LANES)
            s = s_refs[h][:, cs]
            if masked:
                key = lax.broadcasted_iota(jnp.int32, (tk, LANES), 0)
                qry = lax.broadcasted_iota(jnp.int32, (tk, LANES), 1) + (c * LANES) % tq
                s = jnp.where(key <= qry, s, NEG_BIG)
            m_old = m_refs[h][:, cs]
            m_new = jnp.maximum(m_old, jnp.max(s, axis=0, keepdims=True))
            a_refs[h][:, cs] = jnp.exp2(m_old - m_new)
            m_refs[h][:, cs] = m_new
            p_refs[h][:, cs] = jnp.exp2(s - m_new).astype(BF16)

    def accumulate(h, k0):
        vj = jnp.concatenate([vt_ref[h * HEAD_W:(h + 1) * HEAD_W, pl.ds(k0, tk)], ones_rows], axis=0)
        acc_refs[h][...] = a_refs[h][...] * acc_refs[h][...] + _dot(vj, p_refs[h][...])

    for h in range(nh):
        scores(h, 0)

    def body(j, carry):
        k0 = pl.multiple_of(j * tk, tk)
        k1 = pl.multiple_of(j * tk + tk, tk)
        for h in range(nh):
            softmax(h, False)
            scores(h, k1)
            accumulate(h, k0)
        return carry

    lax.fori_loop(0, i, body, 0)
    kd = pl.multiple_of(i * tk, tk)
    for h in range(nh):
        softmax(h, True)
        accumulate(h, kd)

    lam = lam_ref[0]
    for h in range(nh):
        acc = acc_refs[h][0:HEAD_W, :]
        l = acc_refs[h][HEAD_W:HEAD_W + 1, :]
        o = acc[:, 0:tq] / l[:, 0:tq] - lam * (acc[:, tq:2 * tq] / l[:, tq:2 * tq])
        ms = jnp.mean(o * o, axis=0, keepdims=True)
        o = o * lax.rsqrt(ms + SUBLN_EPS) * g_ref[...] * out_scale
        o_ref[:, h * HEAD_W:(h + 1) * HEAD_W] = o.T.astype(BF16)


def _attn_call(lam, q, k, vt, g_col, out_scale):
    B, S, _ = q.shape
    tq = ATT_TQ
    assert ATT_TQ == ATT_TK
    kernel = functools.partial(_attn_kernel, out_scale=out_scale)
    return pl.pallas_call(
        kernel,
        grid=(B, S // tq),
        in_specs=[
            pl.BlockSpec(memory_space=pltpu.SMEM),
            pl.BlockSpec((None, tq, QK_W), lambda b, i: (b, i, 0)),
            pl.BlockSpec((None, S, QK_W), lambda b, i: (b, 0, 0), pipeline_mode=pl.Buffered(1)),
            pl.BlockSpec((None, QK_W, S), lambda b, i: (b, 0, 0), pipeline_mode=pl.Buffered(1)),
            pl.BlockSpec((HEAD_W, 1), lambda b, i: (0, 0)),
        ],
        out_specs=pl.BlockSpec((None, tq, QK_W), lambda b, i: (b, i, 0)),
        out_shape=jax.ShapeDtypeStruct((B, S, QK_W), BF16),
        scratch_shapes=(
            [pltpu.VMEM((ATT_HEADS, HEAD_W, 2 * tq), BF16)]
            + [pltpu.VMEM((ATT_TK, 2 * tq), F32) for _ in range(ATT_HEADS)]
            + [pltpu.VMEM((ATT_TK, 2 * tq), BF16) for _ in range(ATT_HEADS)]
            + [pltpu.VMEM((1, 2 * tq), F32) for _ in range(ATT_HEADS)]
            + [pltpu.VMEM((1, 2 * tq), F32) for _ in range(ATT_HEADS)]
            + [pltpu.VMEM((ACC_ROWS, 2 * tq), F32) for _ in range(ATT_HEADS)]
        ),
        compiler_params=pltpu.CompilerParams(
            dimension_semantics=("arbitrary", "arbitrary"),
            vmem_limit_bytes=VMEM_LIMIT),
        name="diff_flash_attn",
    )(lam, q, k, vt, g_col)


def _pack_bf16_pairs(h):
    half = h.shape[1] // 2
    hb = h.astype(BF16).astype(F32)
    bits = lax.bitcast_convert_type(hb, jnp.uint32)
    return (bits[:, 0:half] >> 16) | (bits[:, half:] & jnp.uint32(0xFFFF0000))


def _unpack_bf16_pairs(w):
    lo = lax.bitcast_convert_type(w << 16, F32).astype(BF16)
    hi = lax.bitcast_convert_type(w & jnp.uint32(0xFFFF0000), F32).astype(BF16)
    return jnp.concatenate([lo, hi], axis=1)


def _first_index_of(mask, row_f, big):
    return jnp.min(jnp.where(mask, row_f, big), axis=0, keepdims=True)


def _mix_kernel(x_ref, o_ref, u_ref, wg_ref, bg_ref, woa_ref, woc_ref, wmo_ref, g1_ref, b1_ref,
                wr_ref, br_ref,
                h1_ref, h1p_ref, ri_ref, rf_ref, cnt_ref, carry_ref):
    tm = x_ref.shape[0]
    step = pl.program_id(0)

    @pl.when(step == 0)
    def _():
        carry_ref[...] = jnp.zeros(carry_ref.shape, F32)

    x = x_ref[...]
    xb = x.astype(BF16)
    gates = jax.nn.sigmoid(_dot(xb, wg_ref[...]) + bg_ref[...])
    ya = _dot(o_ref[...], woa_ref[...])
    yc = _dot(u_ref[...], woc_ref[...])
    merged = (gates[:, 0:D_MODEL] * ya + gates[:, D_MODEL:] * yc).astype(BF16)
    z = DEEPNORM_ALPHA * x + _dot(merged, wmo_ref[...])
    mu = jnp.mean(z, axis=-1, keepdims=True)
    zc = z - mu
    var = jnp.mean(zc * zc, axis=-1, keepdims=True)
    h1 = zc * lax.rsqrt(var + LN_EPS) * g1_ref[...] + b1_ref[...]
    h1_ref[...] = h1
    h1p_ref[...] = _pack_bf16_pairs(h1)

    logits = _dot(h1.astype(BF16), wr_ref[...]) + br_ref[...]
    lt = logits.T
    row8 = lax.broadcasted_iota(jnp.int32, (SUBLANES, tm), 0).astype(F32)
    gl = jnp.where(row8 < N_GROUPS, lt[0:SUBLANES, :], NEG_BIG)
    gmax = jnp.max(gl, axis=0, keepdims=True)
    gsel = _first_index_of(gl == gmax, row8, float(SUBLANES))
    gw = 1.0 / jnp.sum(jnp.exp(gl - gmax), axis=0, keepdims=True)
    el = lt[SUBLANES:2 * SUBLANES, :]
    for g in range(1, N_GROUPS):
        el = jnp.where(gsel == float(g), lt[(g + 1) * SUBLANES:(g + 2) * SUBLANES, :], el)
    v1 = jnp.max(el, axis=0, keepdims=True)
    i1 = _first_index_of(el == v1, row8, float(EPG))
    el2 = jnp.where(row8 == i1, -jnp.inf, el)
    v2 = jnp.max(el2, axis=0, keepdims=True)
    i2 = _first_index_of(el2 == v2, row8, float(EPG))
    t = jnp.exp(v2 - v1)
    p1 = gw / (1.0 + t)
    p2 = gw * t / (1.0 + t)
    e1 = gsel * float(EPG) + i1
    e2 = gsel * float(EPG) + i2

    rowe = lax.broadcasted_iota(jnp.int32, (N_EXPERTS, tm), 0).astype(F32)
    oh1 = (rowe == e1).astype(F32)
    oh2 = (rowe == e2).astype(F32)
    r_i = lax.broadcasted_iota(jnp.int32, (tm, tm), 0)
    c_i = lax.broadcasted_iota(jnp.int32, (tm, tm), 1)
    upper = jnp.where(r_i < c_i, 1.0, 0.0).astype(BF16)
    cum1 = _dot(oh1.astype(BF16), upper)
    cum2 = _dot(oh2.astype(BF16), upper)
    tot1 = jnp.sum(oh1, axis=1, keepdims=True)
    tot2 = jnp.sum(oh2, axis=1, keepdims=True)
    carry = carry_ref[:, 0:1]
    rank1 = jnp.sum(oh1 * (carry + cum1), axis=0, keepdims=True)
    rank2 = jnp.sum(oh2 * (carry + tot1 + cum2), axis=0, keepdims=True)
    new_carry = carry_ref[...] + (tot1 + tot2)
    carry_ref[...] = new_carry
    cnt_ref[...] = new_carry.astype(jnp.int32)

    ri_ref[...] = jnp.zeros(ri_ref.shape, jnp.int32)
    ri_ref[0:1, :] = e1.astype(jnp.int32)
    ri_ref[1:2, :] = e2.astype(jnp.int32)
    ri_ref[2:3, :] = rank1.astype(jnp.int32)
    ri_ref[3:4, :] = rank2.astype(jnp.int32)

    row128 = lax.broadcasted_iota(jnp.int32, (LANES, tm), 0)
    pw = jnp.where(row128 == 0, p1, jnp.where(row128 == 1, p2, 0.0))
    rf_ref[...] = pw.T


def _mix_call(x2, o_n, u, wg, bg, woa, woc, wmo, g1, b1, wr, br):
    T, D = x2.shape
    tm = MIX_TM
    const = lambda i: (0, 0)
    row = lambda i: (i, 0)
    return pl.pallas_call(
        _mix_kernel,
        grid=(T // tm,),
        in_specs=[
            pl.BlockSpec((tm, D), row),
            pl.BlockSpec((tm, QK_W), row),
            pl.BlockSpec((tm, CONV_W), row),
            pl.BlockSpec(wg.shape, const),
            pl.BlockSpec(bg.shape, const),
            pl.BlockSpec(woa.shape, const),
            pl.BlockSpec(woc.shape, const),
            pl.BlockSpec(wmo.shape, const),
            pl.BlockSpec(g1.shape, const),
            pl.BlockSpec(b1.shape, const),
            pl.BlockSpec(wr.shape, const),
            pl.BlockSpec(br.shape, const),
        ],
        out_specs=[
            pl.BlockSpec((tm, D), row),
            pl.BlockSpec((tm, D // 2), row),
            pl.BlockSpec((SUBLANES, tm), lambda i: (0, i)),
            pl.BlockSpec((tm, LANES), row),
            pl.BlockSpec((N_EXPERTS, LANES), const),
        ],
        out_shape=[
            jax.ShapeDtypeStruct((T, D), F32),
            jax.ShapeDtypeStruct((T, D // 2), jnp.uint32),
            jax.ShapeDtypeStruct((SUBLANES, T), jnp.int32),
            jax.ShapeDtypeStruct((T, LANES), F32),
            jax.ShapeDtypeStruct((N_EXPERTS, LANES), jnp.int32),
        ],
        scratch_shapes=[pltpu.VMEM((N_EXPERTS, LANES), F32)],
        compiler_params=pltpu.CompilerParams(
            dimension_semantics=("arbitrary",),
            vmem_limit_bytes=VMEM_LIMIT),
        name="mix_ln_router",
    )(x2, o_n, u, wg, bg, woa, woc, wmo, g1, b1, wr, br)


def _dispatch_kernel(dest_ref, h1p_ref, xs_in_ref, xs_ref, sem):
    del xs_in_ref
    tm = h1p_ref.shape[0]

    def row_copy(r, k):
        return pltpu.make_async_copy(
            h1p_ref.at[pl.ds(r, 1), :],
            xs_ref.at[pl.ds(dest_ref[0, 0, 2 * r + k], 1), :],
            sem)

    def issue(r, carry):
        row_copy(r, 0).start()
        row_copy(r, 1).start()
        return carry

    lax.fori_loop(0, tm, issue, 0, unroll=8)
    for _ in range(2):
        pltpu.make_async_copy(h1p_ref, xs_ref.at[pl.ds(0, tm), :], sem).wait()


def _dispatch_call(dest3, h1p, xs_zero):
    n_steps, _, n = dest3.shape
    return pl.pallas_call(
        _dispatch_kernel,
        grid=(n_steps,),
        in_specs=[
            pl.BlockSpec((1, 1, n), lambda i: (i, 0, 0), memory_space=pltpu.SMEM),
            pl.BlockSpec((n // 2, h1p.shape[1]), lambda i: (i, 0)),
            pl.BlockSpec(memory_space=pl.ANY),
        ],
        out_specs=pl.BlockSpec(memory_space=pl.ANY),
        out_shape=jax.ShapeDtypeStruct(xs_zero.shape, xs_zero.dtype),
        scratch_shapes=[pltpu.SemaphoreType.DMA],
        input_output_aliases={2: 0},
        compiler_params=pltpu.CompilerParams(
            dimension_semantics=("arbitrary",), has_side_effects=True),
        name="moe_dispatch",
    )(dest3, h1p, xs_zero)


def _expert_kernel(be_ref, nu_ref, xs_ref, wg_ref, wu_ref, wd_ref, ys_ref):
    i = pl.program_id(0)

    @pl.when(i < nu_ref[0])
    def _():
        xb = _unpack_bf16_pairs(xs_ref[...])
        g = _dot(xb, wg_ref[...])
        u = _dot(xb, wu_ref[...])
        h = (g * jax.nn.sigmoid(g) * u).astype(BF16)
        ys_ref[...] = _dot(h, wd_ref[...])

    @pl.when(i >= nu_ref[0])
    def _():
        ys_ref[...] = jnp.zeros(ys_ref.shape, F32)


def _expert_call(block_expert, n_used, xs, wg, wu, wd):
    P = xs.shape[0]
    bm = MOE_BM
    nb = P // bm
    xs_map = lambda i, be, nu: (jnp.minimum(i, nu[0] - 1), 0)
    w_map = lambda i, be, nu: (be[i], 0, 0)
    return pl.pallas_call(
        _expert_kernel,
        grid_spec=pltpu.PrefetchScalarGridSpec(
            num_scalar_prefetch=2,
            grid=(nb,),
            in_specs=[
                pl.BlockSpec((bm, D_MODEL // 2), xs_map),
                pl.BlockSpec((None, D_MODEL, EXPERT_FF), w_map),
                pl.BlockSpec((None, D_MODEL, EXPERT_FF), w_map),
                pl.BlockSpec((None, EXPERT_FF, D_MODEL), w_map),
            ],
            out_specs=pl.BlockSpec((bm, D_MODEL), lambda i, be, nu: (i, 0)),
        ),
        out_shape=jax.ShapeDtypeStruct((P, D_MODEL), F32),
        compiler_params=pltpu.CompilerParams(
            dimension_semantics=("arbitrary",),
            vmem_limit_bytes=VMEM_LIMIT),
        name="moe_experts",
    )(block_expert, n_used, xs, wg, wu, wd)


def _combine_kernel(dest_ref, ys_ref, h1_ref, rf_ref, g2_ref, b2_ref, out_ref, buf_a, buf_b, sem):
    tm = h1_ref.shape[0]

    def row_copy(r, k, buf):
        return pltpu.make_async_copy(
            ys_ref.at[pl.ds(dest_ref[0, 0, 2 * r + k], 1), :],
            buf.at[pl.ds(r, 1), :],
            sem)

    def issue(r, carry):
        row_copy(r, 0, buf_a).start()
        row_copy(r, 1, buf_b).start()
        return carry

    lax.fori_loop(0, tm, issue, 0, unroll=8)
    pltpu.make_async_copy(ys_ref.at[pl.ds(0, tm), :], buf_a, sem).wait()
    pltpu.make_async_copy(ys_ref.at[pl.ds(0, tm), :], buf_b, sem).wait()

    rf = rf_ref[...]
    ffn = rf[:, 0:1] * buf_a[...] + rf[:, 1:2] * buf_b[...]
    z = DEEPNORM_ALPHA * h1_ref[...] + ffn
    mu = jnp.mean(z, axis=-1, keepdims=True)
    zc = z - mu
    var = jnp.mean(zc * zc, axis=-1, keepdims=True)
    out_ref[...] = zc * lax.rsqrt(var + LN_EPS) * g2_ref[...] + b2_ref[...]


def _combine_call(dest3, ys, h1, rf, g2, b2):
    T, D = h1.shape
    tm = COMB_TM
    const = lambda i: (0, 0)
    row = lambda i: (i, 0)
    return pl.pallas_call(
        _combine_kernel,
        grid=(T // tm,),
        in_specs=[
            pl.BlockSpec((1, 1, 2 * tm), lambda i: (i, 0, 0), memory_space=pltpu.SMEM),
            pl.BlockSpec(memory_space=pl.ANY),
            pl.BlockSpec((tm, D), row),
            pl.BlockSpec((tm, LANES), row),
            pl.BlockSpec(g2.shape, const),
            pl.BlockSpec(b2.shape, const),
        ],
        out_specs=pl.BlockSpec((tm, D), row),
        out_shape=jax.ShapeDtypeStruct((T, D), F32),
        scratch_shapes=[
            pltpu.VMEM((tm, D), F32),
            pltpu.VMEM((tm, D), F32),
            pltpu.SemaphoreType.DMA,
        ],
        compiler_params=pltpu.CompilerParams(
            dimension_semantics=("arbitrary",),
            vmem_limit_bytes=VMEM_LIMIT),
        name="moe_combine_ln",
    )(dest3, ys, h1, rf, g2, b2)


def kernel(x, positions, w_in, b_gate, lambda_q1, lambda_k1, lambda_q2, lambda_k2, subln_g, w_o_att, conv_w, w_o_conv, w_mix_out, ln1_g, ln1_b, w_router_group, b_router_group, w_router_expert, b_router_expert, w_exp_gate, w_exp_up, w_exp_down, ln2_g, ln2_b):
    B, S, D = x.shape
    T = B * S
    l = 0
    lambda_init = 0.8 - 0.6 * math.exp(-0.3 * l)

    inv_freq = ROPE_THETA ** (-jnp.arange(0, HEAD_DIM, 2, dtype=F32) / HEAD_DIM)
    ang = positions.astype(F32)[..., None] * inv_freq
    cos_t = jnp.tile(jnp.cos(ang), (1, 1, 4))
    sin_h = jnp.sin(ang)
    sin_t = jnp.tile(jnp.concatenate([-sin_h, sin_h], -1), (1, 1, 2))

    n_main = 2 * QK_W + QK_W + 3 * CONV_W
    w0 = w_in[l]
    w_main = w0[:, :n_main].astype(BF16)
    wv_t = w0[:, 2 * QK_W:3 * QK_W].T.astype(BF16)
    w_gates = w0[:, n_main:].astype(BF16)

    q, k, vt, u = _proj_call(x, w_main, wv_t, cos_t, sin_t, conv_w[l])

    lam = (jnp.exp(jnp.sum(lambda_q1[l].astype(F32) * lambda_k1[l].astype(F32)))
           - jnp.exp(jnp.sum(lambda_q2[l].astype(F32) * lambda_k2[l].astype(F32)))
           + lambda_init).reshape(1)
    o_n = _attn_call(lam, q, k, vt, subln_g[l].astype(F32).reshape(HEAD_W, 1), 1.0 - lambda_init)

    wr = jnp.zeros((D, LANES), F32)
    wr = wr.at[:, 0:N_GROUPS].set(w_router_group[l]).at[:, SUBLANES:SUBLANES + N_EXPERTS].set(w_router_expert[l])
    br = jnp.zeros((1, LANES), F32)
    br = br.at[0, 0:N_GROUPS].set(b_router_group[l]).at[0, SUBLANES:SUBLANES + N_EXPERTS].set(b_router_expert[l])

    h1, h1p, ri, rf, cnt = _mix_call(
        x.reshape(T, D), o_n.reshape(T, QK_W), u.reshape(T, CONV_W),
        w_gates, b_gate[l].reshape(1, -1).astype(F32),
        w_o_att[l].astype(BF16), w_o_conv[l].astype(BF16), w_mix_out[l].astype(BF16),
        ln1_g[l].reshape(1, D).astype(F32), ln1_b[l].reshape(1, D).astype(F32),
        wr.astype(BF16), br)

    bm = MOE_BM
    counts = cnt[:, 0]
    padded = (counts + bm - 1) // bm * bm
    pend = jnp.cumsum(padded)
    pstart = pend - padded
    dest = jnp.stack([pstart[ri[0]] + ri[2], pstart[ri[1]] + ri[3]], axis=-1)
    A = 2 * T
    P = A + N_EXPERTS * bm
    nb = P // bm
    blk_start = jnp.arange(nb, dtype=jnp.int32) * bm
    block_expert = jnp.minimum(
        jnp.sum((pend[None, :] <= blk_start[:, None]).astype(jnp.int32), axis=1), N_EXPERTS - 1)
    n_used = (pend[-1] // bm).astype(jnp.int32).reshape(1)

    xs = _dispatch_call(dest.reshape(T // DISP_TM, 1, 2 * DISP_TM), h1p,
                        jnp.zeros((P, D // 2), jnp.uint32))
    ys = _expert_call(block_expert, n_used, xs,
                      w_exp_gate[l].astype(BF16), w_exp_up[l].astype(BF16), w_exp_down[l].astype(BF16))
    out = _combine_call(dest.reshape(T // COMB_TM, 1, 2 * COMB_TM), ys, h1, rf,
                        ln2_g[l].reshape(1, D).astype(F32), ln2_b[l].reshape(1, D).astype(F32))
    return out.reshape(B, S, D)
```

```python
import functools
import math

import jax
import jax.numpy as jnp
from jax import lax
from jax.experimental import pallas as pl
from jax.experimental.pallas import tpu as pltpu

D_MODEL = 1024
ATT_HEADS = 4
HEAD_DIM = 64
HEAD_W = 2 * HEAD_DIM
QK_W = ATT_HEADS * HEAD_W
CONV_W = D_MODEL // 2
CONV_K = 3
N_GROUPS = 4
EPG = 8
N_EXPERTS = N_GROUPS * EPG
EXPERT_FF = D_MODEL // 2
ROPE_THETA = 10000.0
SUBLN_EPS = 1e-5
LN_EPS = 1e-5
DEPTH = 1
DEEPNORM_ALPHA = (2.0 * DEPTH) ** 0.25

LANES = 128
SUBLANES = 8
VMEM_LIMIT = 48 * 1024 * 1024

PROJ_TM = 512
ATT_TQ = 256
ATT_TK = 256
MIX_TM = 256
MOE_BM = 256
DISP_TM = 1024
COMB_TM = 256
NEG_BIG = -1e30

F32 = jnp.float32
BF16 = jnp.bfloat16


def _dot(a, b):
    return jnp.dot(a, b, preferred_element_type=F32)


def _dot_nt(a, b):
    return lax.dot_general(a, b, (((1,), (1,)), ((), ())), preferred_element_type=F32)


def _proj_kernel(x_ref, w_ref, wvt_ref, cos_ref, sin_ref, cw_ref,
                 q_ref, k_ref, vt_ref, u_ref, pbuf):
    tm = x_ref.shape[0]
    i = pl.program_id(1)
    xb = x_ref[...].astype(BF16)
    cos = cos_ref[...]
    sin = sin_ref[...]
    lane = lax.broadcasted_iota(jnp.int32, (tm, LANES), 1)
    low_half = (lane % HEAD_DIM) < (HEAD_DIM // 2)

    def rope(t):
        rot = jnp.where(low_half, pltpu.roll(t, LANES - HEAD_DIM // 2, 1),
                        pltpu.roll(t, HEAD_DIM // 2, 1))
        return t * cos + rot * sin

    q = _dot(xb, w_ref[:, 0:QK_W])
    for h in range(ATT_HEADS):
        sl = slice(h * HEAD_W, (h + 1) * HEAD_W)
        q_ref[:, sl] = (rope(q[:, sl]) * (HEAD_DIM ** -0.5 * math.log2(math.e))).astype(BF16)
    k = _dot(xb, w_ref[:, QK_W:2 * QK_W])
    for h in range(ATT_HEADS):
        sl = slice(h * HEAD_W, (h + 1) * HEAD_W)
        k_ref[:, sl] = rope(k[:, sl]).astype(BF16)
    vt_ref[...] = _dot_nt(wvt_ref[...], xb).astype(BF16)

    c0 = 3 * QK_W
    cb = _dot(xb, w_ref[:, c0:c0 + CONV_W])
    cc = _dot(xb, w_ref[:, c0 + CONV_W:c0 + 2 * CONV_W])
    cx = _dot(xb, w_ref[:, c0 + 2 * CONV_W:c0 + 3 * CONV_W])
    p = cc * cx

    @pl.when(i == 0)
    def _():
        pbuf[0:SUBLANES, :] = jnp.zeros((SUBLANES, CONV_W), F32)

    pbuf[SUBLANES:SUBLANES + tm, :] = p
    y = (cw_ref[0:1, :] * pbuf[SUBLANES - 2:SUBLANES - 2 + tm, :]
         + cw_ref[1:2, :] * pbuf[SUBLANES - 1:SUBLANES - 1 + tm, :]
         + cw_ref[2:3, :] * p)
    u_ref[...] = (cb * y).astype(BF16)
    pbuf[0:SUBLANES, :] = pbuf[tm:tm + SUBLANES, :]


def _proj_call(x, w_main, wv_t, cos_t, sin_t, conv_w):
    B, S, D = x.shape
    tm = PROJ_TM
    n_main = w_main.shape[1]
    return pl.pallas_call(
        _proj_kernel,
        grid=(B, S // tm),
        in_specs=[
            pl.BlockSpec((None, tm, D), lambda b, i: (b, i, 0)),
            pl.BlockSpec((D, n_main), lambda b, i: (0, 0)),
            pl.BlockSpec((QK_W, D), lambda b, i: (0, 0)),
            pl.BlockSpec((None, tm, LANES), lambda b, i: (b, i, 0)),
            pl.BlockSpec((None, tm, LANES), lambda b, i: (b, i, 0)),
            pl.BlockSpec((CONV_K, CONV_W), lambda b, i: (0, 0)),
        ],
        out_specs=[
            pl.BlockSpec((None, tm, QK_W), lambda b, i: (b, i, 0)),
            pl.BlockSpec((None, tm, QK_W), lambda b, i: (b, i, 0)),
            pl.BlockSpec((None, QK_W, tm), lambda b, i: (b, 0, i)),
            pl.BlockSpec((None, tm, CONV_W), lambda b, i: (b, i, 0)),
        ],
        out_shape=[
            jax.ShapeDtypeStruct((B, S, QK_W), BF16),
            jax.ShapeDtypeStruct((B, S, QK_W), BF16),
            jax.ShapeDtypeStruct((B, QK_W, S), BF16),
            jax.ShapeDtypeStruct((B, S, CONV_W), BF16),
        ],
        scratch_shapes=[pltpu.VMEM((tm + 2 * SUBLANES, CONV_W), F32)],
        compiler_params=pltpu.CompilerParams(
            dimension_semantics=("arbitrary", "arbitrary"),
            vmem_limit_bytes=VMEM_LIMIT),
        name="proj_rope_conv",
    )(x, w_main, wv_t, cos_t, sin_t, conv_w)


ACC_ROWS = HEAD_W + 16


def _attn_kernel(lam_ref, q_ref, k_ref, vt_ref, g_ref, o_ref, qzt_ref, *scratch, out_scale):
    nh = ATT_HEADS
    s_refs, p_refs = scratch[0:nh], scratch[nh:2 * nh]
    m_refs, a_refs, acc_refs = scratch[2 * nh:3 * nh], scratch[3 * nh:4 * nh], scratch[4 * nh:5 * nh]
    tq = q_ref.shape[0]
    tk = ATT_TK
    nblk = 2 * tq // LANES
    i = pl.program_id(1)
    lane = lax.broadcasted_iota(jnp.int32, (tq, HEAD_W), 1)
    ones_rows = jnp.ones((ACC_ROWS - HEAD_W, tk), BF16)

    for h in range(nh):
        qf = q_ref[:, h * HEAD_W:(h + 1) * HEAD_W].astype(F32)
        q1 = jnp.where(lane < HEAD_DIM, qf, 0.0)
        q2 = jnp.where(lane >= HEAD_DIM, qf, 0.0)
        qzt_ref[h, :, 0:tq] = q1.T.astype(BF16)
        qzt_ref[h, :, tq:2 * tq] = q2.T.astype(BF16)
        m_refs[h][...] = jnp.full(m_refs[h].shape, NEG_BIG, F32)
        acc_refs[h][...] = jnp.zeros(acc_refs[h].shape, F32)

    def scores(h, k0):
        kj = k_ref[pl.ds(k0, tk), h * HEAD_W:(h + 1) * HEAD_W]
        s_refs[h][...] = _dot(kj, qzt_ref[h])

    def softmax(h, masked):
        for c in range(nblk):
            cs = slice(c * LANES, (c + 1) * LANES)
            s = s_refs[h][:, cs]
            if masked:
                key = lax.broadcasted_iota(jnp.int32, (tk, LANES), 0)
                qry = lax.broadcasted_iota(jnp.int32, (tk, LANES), 1) + (c * LANES) % tq
                s = jnp.where(key <= qry, s, NEG_BIG)
            m_old = m_refs[h][:, cs]
            m_new = jnp.maximum(m_old, jnp.max(s, axis=0, keepdims=True))
            a_refs[h][:, cs] = jnp.exp2(m_old - m_new)
            m_refs[h][:, cs] = m_new
            p_refs[h][:, cs] = jnp.exp2(s - m_new).astype(BF16)

    def accumulate(h, k0):
        vj = jnp.concatenate([vt_ref[h * HEAD_W:(h + 1) * HEAD_W, pl.ds(k0, tk)], ones_rows], axis=0)
        acc_refs[h][...] = a_refs[h][...] * acc_refs[h][...] + _dot(vj, p_refs[h][...])

    for h in range(nh):
        scores(h, 0)

    def body(j, carry):
        k0 = pl.multiple_of(j * tk, tk)
        k1 = pl.multiple_of(j * tk + tk, tk)
        for h in range(nh):
            softmax(h, False)
            scores(h, k1)
            accumulate(h, k0)
        return carry

    lax.fori_loop(0, i, body, 0)
    kd = pl.multiple_of(i * tk, tk)
    for h in range(nh):
        softmax(h, True)
        accumulate(h, kd)

    lam = lam_ref[0]
    for h in range(nh):
        acc = acc_refs[h][0:HEAD_W, :]
        l = acc_refs[h][HEAD_W:HEAD_W + 1, :]
        o = acc[:, 0:tq] / l[:, 0:tq] - lam * (acc[:, tq:2 * tq] / l[:, tq:2 * tq])
        ms = jnp.mean(o * o, axis=0, keepdims=True)
        o = o * lax.rsqrt(ms + SUBLN_EPS) * g_ref[...] * out_scale
        o_ref[:, h * HEAD_W:(h + 1) * HEAD_W] = o.T.astype(BF16)


def _attn_call(lam, q, k, vt, g_col, out_scale):
    B, S, _ = q.shape
    tq = ATT_TQ
    assert ATT_TQ == ATT_TK
    kernel = functools.partial(_attn_kernel, out_scale=out_scale)
    return pl.pallas_call(
        kernel,
        grid=(B, S // tq),
        in_specs=[
            pl.BlockSpec(memory_space=pltpu.SMEM),
            pl.BlockSpec((None, tq, QK_W), lambda b, i: (b, i, 0)),
            pl.BlockSpec((None, S, QK_W), lambda b, i: (b, 0, 0), pipeline_mode=pl.Buffered(1)),
            pl.BlockSpec((None, QK_W, S), lambda b, i: (b, 0, 0), pipeline_mode=pl.Buffered(1)),
            pl.BlockSpec((HEAD_W, 1), lambda b, i: (0, 0)),
        ],
        out_specs=pl.BlockSpec((None, tq, QK_W), lambda b, i: (b, i, 0)),
        out_shape=jax.ShapeDtypeStruct((B, S, QK_W), BF16),
        scratch_shapes=(
            [pltpu.VMEM((ATT_HEADS, HEAD_W, 2 * tq), BF16)]
            + [pltpu.VMEM((ATT_TK, 2 * tq), F32) for _ in range(ATT_HEADS)]
            + [pltpu.VMEM((ATT_TK, 2 * tq), BF16) for _ in range(ATT_HEADS)]
            + [pltpu.VMEM((1, 2 * tq), F32) for _ in range(ATT_HEADS)]
            + [pltpu.VMEM((1, 2 * tq), F32) for _ in range(ATT_HEADS)]
            + [pltpu.VMEM((ACC_ROWS, 2 * tq), F32) for _ in range(ATT_HEADS)]
        ),
        compiler_params=pltpu.CompilerParams(
            dimension_semantics=("arbitrary", "arbitrary"),
            vmem_limit_bytes=VMEM_LIMIT),
        name="diff_flash_attn",
    )(lam, q, k, vt, g_col)


def _pack_bf16_pairs(h):
    half = h.shape[1] // 2
    hb = h.astype(BF16).astype(F32)
    bits = lax.bitcast_convert_type(hb, jnp.uint32)
    return (bits[:, 0:half] >> 16) | (bits[:, half:] & jnp.uint32(0xFFFF0000))


def _unpack_bf16_pairs(w):
    lo = lax.bitcast_convert_type(w << 16, F32).astype(BF16)
    hi = lax.bitcast_convert_type(w & jnp.uint32(0xFFFF0000), F32).astype(BF16)
    return jnp.concatenate([lo, hi], axis=1)


def _first_index_of(mask, row_f, big):
    return jnp.min(jnp.where(mask, row_f, big), axis=0, keepdims=True)


def _mix_kernel(x_ref, o_ref, u_ref, wg_ref, bg_ref, woa_ref, woc_ref, wmo_ref, g1_ref, b1_ref,
                wr_ref, br_ref,
                h1_ref, h1p_ref, ri_ref, rf_ref, cnt_ref, carry_ref):
    tm = x_ref.shape[0]
    step = pl.program_id(0)

    @pl.when(step == 0)
    def _():
        carry_ref[...] = jnp.zeros(carry_ref.shape, F32)

    x = x_ref[...]
    xb = x.astype(BF16)
    gates = jax.nn.sigmoid(_dot(xb, wg_ref[...]) + bg_ref[...])
    ya = _dot(o_ref[...], woa_ref[...])
    yc = _dot(u_ref[...], woc_ref[...])
    merged = (gates[:, 0:D_MODEL] * ya + gates[:, D_MODEL:] * yc).astype(BF16)
    z = DEEPNORM_ALPHA * x + _dot(merged, wmo_ref[...])
    mu = jnp.mean(z, axis=-1, keepdims=True)
    zc = z - mu
    var = jnp.mean(zc * zc, axis=-1, keepdims=True)
    h1 = zc * lax.rsqrt(var + LN_EPS) * g1_ref[...] + b1_ref[...]
    h1_ref[...] = h1
    h1p_ref[...] = _pack_bf16_pairs(h1)

    logits = _dot(h1.astype(BF16), wr_ref[...]) + br_ref[...]
    lt = logits.T
    row8 = lax.broadcasted_iota(jnp.int32, (SUBLANES, tm), 0).astype(F32)
    gl = jnp.where(row8 < N_GROUPS, lt[0:SUBLANES, :], NEG_BIG)
    gmax = jnp.max(gl, axis=0, keepdims=True)
    gsel = _first_index_of(gl == gmax, row8, float(SUBLANES))
    gw = 1.0 / jnp.sum(jnp.exp(gl - gmax), axis=0, keepdims=True)
    el = lt[SUBLANES:2 * SUBLANES, :]
    for g in range(1, N_GROUPS):
        el = jnp.where(gsel == float(g), lt[(g + 1) * SUBLANES:(g + 2) * SUBLANES, :], el)
    v1 = jnp.max(el, axis=0, keepdims=True)
    i1 = _first_index_of(el == v1, row8, float(EPG))
    el2 = jnp.where(row8 == i1, -jnp.inf, el)
    v2 = jnp.max(el2, axis=0, keepdims=True)
    i2 = _first_index_of(el2 == v2, row8, float(EPG))
    t = jnp.exp(v2 - v1)
    p1 = gw / (1.0 + t)
    p2 = gw * t / (1.0 + t)
    e1 = gsel * float(EPG) + i1
    e2 = gsel * float(EPG) + i2

    rowe = lax.broadcasted_iota(jnp.int32, (N_EXPERTS, tm), 0).astype(F32)
    oh1 = (rowe == e1).astype(F32)
    oh2 = (rowe == e2).astype(F32)
    r_i = lax.broadcasted_iota(jnp.int32, (tm, tm), 0)
    c_i = lax.broadcasted_iota(jnp.int32, (tm, tm), 1)
    upper = jnp.where(r_i < c_i, 1.0, 0.0).astype(BF16)
    cum1 = _dot(oh1.astype(BF16), upper)
    cum2 = _dot(oh2.astype(BF16), upper)
    tot1 = jnp.sum(oh1, axis=1, keepdims=True)
    tot2 = jnp.sum(oh2, axis=1, keepdims=True)
    carry = carry_ref[:, 0:1]
    rank1 = jnp.sum(oh1 * (carry + cum1), axis=0, keepdims=True)
    rank2 = jnp.sum(oh2 * (carry + tot1 + cum2), axis=0, keepdims=True)
    new_carry = carry_ref[...] + (tot1 + tot2)
    carry_ref[...] = new_carry
    cnt_ref[...] = new_carry.astype(jnp.int32)

    ri_ref[...] = jnp.zeros(ri_ref.shape, jnp.int32)
    ri_ref[0:1, :] = e1.astype(jnp.int32)
    ri_ref[1:2, :] = e2.astype(jnp.int32)
    ri_ref[2:3, :] = rank1.astype(jnp.int32)
    ri_ref[3:4, :] = rank2.astype(jnp.int32)

    row128 = lax.broadcasted_iota(jnp.int32, (LANES, tm), 0)
    pw = jnp.where(row128 == 0, p1, jnp.where(row128 == 1, p2, 0.0))
    rf_ref[...] = pw.T


def _mix_call(x2, o_n, u, wg, bg, woa, woc, wmo, g1, b1, wr, br):
    T, D = x2.shape
    tm = MIX_TM
    const = lambda i: (0, 0)
    row = lambda i: (i, 0)
    return pl.pallas_call(
        _mix_kernel,
        grid=(T // tm,),
        in_specs=[
            pl.BlockSpec((tm, D), row),
            pl.BlockSpec((tm, QK_W), row),
            pl.BlockSpec((tm, CONV_W), row),
            pl.BlockSpec(wg.shape, const),
            pl.BlockSpec(bg.shape, const),
            pl.BlockSpec(woa.shape, const),
            pl.BlockSpec(woc.shape, const),
            pl.BlockSpec(wmo.shape, const),
            pl.BlockSpec(g1.shape, const),
            pl.BlockSpec(b1.shape, const),
            pl.BlockSpec(wr.shape, const),
            pl.BlockSpec(br.shape, const),
        ],
        out_specs=[
            pl.BlockSpec((tm, D), row),
            pl.BlockSpec((tm, D // 2), row),
            pl.BlockSpec((SUBLANES, tm), lambda i: (0, i)),
            pl.BlockSpec((tm, LANES), row),
            pl.BlockSpec((N_EXPERTS, LANES), const),
        ],
        out_shape=[
            jax.ShapeDtypeStruct((T, D), F32),
            jax.ShapeDtypeStruct((T, D // 2), jnp.uint32),
            jax.ShapeDtypeStruct((SUBLANES, T), jnp.int32),
            jax.ShapeDtypeStruct((T, LANES), F32),
            jax.ShapeDtypeStruct((N_EXPERTS, LANES), jnp.int32),
        ],
        scratch_shapes=[pltpu.VMEM((N_EXPERTS, LANES), F32)],
        compiler_params=pltpu.CompilerParams(
            dimension_semantics=("arbitrary",),
            vmem_limit_bytes=VMEM_LIMIT),
        name="mix_ln_router",
    )(x2, o_n, u, wg, bg, woa, woc, wmo, g1, b1, wr, br)


def _dispatch_kernel(pend_ref, dest_ref, h1p_ref, xs_ref, zero_ref, sem):
    tm = h1p_ref.shape[0]
    bm = zero_ref.shape[0]

    @pl.when(pl.program_id(0) == 0)
    def _():
        zero_ref[...] = jnp.zeros(zero_ref.shape, zero_ref.dtype)
        n_rows = xs_ref.shape[0]

        def zero_copy(e):
            if e < N_EXPERTS:
                prev_end = pend_ref[e - 1] if e > 0 else 0
                start = pl.multiple_of(jnp.maximum(pend_ref[e] - bm, 0), bm)
                needed = pend_ref[e] > prev_end
            else:
                start = n_rows - (e - N_EXPERTS + 1) * bm
                needed = start >= pend_ref[N_EXPERTS - 1]
            return pltpu.make_async_copy(zero_ref, xs_ref.at[pl.ds(start, bm), :], sem), needed

        for e in range(2 * N_EXPERTS):
            copy, needed = zero_copy(e)
            pl.when(needed)(copy.start)
        for e in range(2 * N_EXPERTS):
            copy, needed = zero_copy(e)
            pl.when(needed)(copy.wait)

    def row_copy(r, k):
        return pltpu.make_async_copy(
            h1p_ref.at[pl.ds(r, 1), :],
            xs_ref.at[pl.ds(dest_ref[k, r], 1), :],
            sem)

    def issue(r, carry):
        row_copy(r, 0).start()
        row_copy(r, 1).start()
        return carry

    lax.fori_loop(0, tm, issue, 0, unroll=8)
    for _ in range(2):
        pltpu.make_async_copy(h1p_ref, xs_ref.at[pl.ds(0, tm), :], sem).wait()


def _dispatch_call(pend, dest, h1p, n_rows):
    T, W = h1p.shape
    tm = DISP_TM
    return pl.pallas_call(
        _dispatch_kernel,
        grid_spec=pltpu.PrefetchScalarGridSpec(
            num_scalar_prefetch=1,
            grid=(T // tm,),
            in_specs=[
                pl.BlockSpec((2, tm), lambda i, pend: (0, i), memory_space=pltpu.SMEM),
                pl.BlockSpec((tm, W), lambda i, pend: (i, 0)),
            ],
            out_specs=pl.BlockSpec(memory_space=pl.ANY),
            scratch_shapes=[pltpu.VMEM((MOE_BM, W), h1p.dtype), pltpu.SemaphoreType.DMA],
        ),
        out_shape=jax.ShapeDtypeStruct((n_rows, W), h1p.dtype),
        compiler_params=pltpu.CompilerParams(
            dimension_semantics=("arbitrary",), has_side_effects=True),
        name="moe_dispatch",
    )(pend, dest, h1p)


def _expert_kernel(be_ref, nu_ref, xs_ref, wg_ref, wu_ref, wd_ref, ys_ref, wgb_ref, wub_ref, wdb_ref):
    i = pl.program_id(0)
    used = i < nu_ref[0]

    @pl.when(used & ((i == 0) | (be_ref[i] != be_ref[jnp.maximum(i - 1, 0)])))
    def _():
        wgb_ref[...] = wg_ref[...].astype(BF16)
        wub_ref[...] = wu_ref[...].astype(BF16)
        wdb_ref[...] = wd_ref[...].astype(BF16)

    @pl.when(used)
    def _():
        xb = _unpack_bf16_pairs(xs_ref[...])
        g = _dot(xb, wgb_ref[...])
        u = _dot(xb, wub_ref[...])
        h = (g * jax.nn.sigmoid(g) * u).astype(BF16)
        ys_ref[...] = _dot(h, wdb_ref[...])

    @pl.when(i >= nu_ref[0])
    def _():
        ys_ref[...] = jnp.zeros(ys_ref.shape, F32)


def _expert_call(block_expert, n_used, xs, wg, wu, wd):
    P = xs.shape[0]
    bm = MOE_BM
    nb = P // bm
    xs_map = lambda i, be, nu: (jnp.minimum(i, nu[0] - 1), 0)
    w_map = lambda i, be, nu: (be[i], 0, 0)
    return pl.pallas_call(
        _expert_kernel,
        grid_spec=pltpu.PrefetchScalarGridSpec(
            num_scalar_prefetch=2,
            grid=(nb,),
            in_specs=[
                pl.BlockSpec((bm, D_MODEL // 2), xs_map),
                pl.BlockSpec((None, D_MODEL, EXPERT_FF), w_map),
                pl.BlockSpec((None, D_MODEL, EXPERT_FF), w_map),
                pl.BlockSpec((None, EXPERT_FF, D_MODEL), w_map),
            ],
            out_specs=pl.BlockSpec((bm, D_MODEL), lambda i, be, nu: (i, 0)),
            scratch_shapes=[
                pltpu.VMEM((D_MODEL, EXPERT_FF), BF16),
                pltpu.VMEM((D_MODEL, EXPERT_FF), BF16),
                pltpu.VMEM((EXPERT_FF, D_MODEL), BF16),
            ],
        ),
        out_shape=jax.ShapeDtypeStruct((P, D_MODEL), F32),
        compiler_params=pltpu.CompilerParams(
            dimension_semantics=("arbitrary",),
            vmem_limit_bytes=VMEM_LIMIT),
        name="moe_experts",
    )(block_expert, n_used, xs, wg, wu, wd)


def _combine_kernel(dest_ref, ys_ref, h1_ref, rf_ref, g2_ref, b2_ref, out_ref, buf_a, buf_b, sem):
    tm = h1_ref.shape[0]

    def row_copy(r, k, buf):
        return pltpu.make_async_copy(
            ys_ref.at[pl.ds(dest_ref[k, r], 1), :],
            buf.at[pl.ds(r, 1), :],
            sem)

    def issue(r, carry):
        row_copy(r, 0, buf_a).start()
        row_copy(r, 1, buf_b).start()
        return carry

    lax.fori_loop(0, tm, issue, 0, unroll=8)
    pltpu.make_async_copy(ys_ref.at[pl.ds(0, tm), :], buf_a, sem).wait()
    pltpu.make_async_copy(ys_ref.at[pl.ds(0, tm), :], buf_b, sem).wait()

    rf = rf_ref[...]
    ffn = rf[:, 0:1] * buf_a[...] + rf[:, 1:2] * buf_b[...]
    z = DEEPNORM_ALPHA * h1_ref[...] + ffn
    mu = jnp.mean(z, axis=-1, keepdims=True)
    zc = z - mu
    var = jnp.mean(zc * zc, axis=-1, keepdims=True)
    out_ref[...] = zc * lax.rsqrt(var + LN_EPS) * g2_ref[...] + b2_ref[...]


def _combine_call(dest, ys, h1, rf, g2, b2):
    T, D = h1.shape
    tm = COMB_TM
    const = lambda i: (0, 0)
    row = lambda i: (i, 0)
    return pl.pallas_call(
        _combine_kernel,
        grid=(T // tm,),
        in_specs=[
            pl.BlockSpec((2, tm), lambda i: (0, i), memory_space=pltpu.SMEM),
            pl.BlockSpec(memory_space=pl.ANY),
            pl.BlockSpec((tm, D), row),
            pl.BlockSpec((tm, LANES), row),
            pl.BlockSpec(g2.shape, const),
            pl.BlockSpec(b2.shape, const),
        ],
        out_specs=pl.BlockSpec((tm, D), row),
        out_shape=jax.ShapeDtypeStruct((T, D), F32),
        scratch_shapes=[
            pltpu.VMEM((tm, D), F32),
            pltpu.VMEM((tm, D), F32),
            pltpu.SemaphoreType.DMA,
        ],
        compiler_params=pltpu.CompilerParams(
            dimension_semantics=("arbitrary",),
            vmem_limit_bytes=VMEM_LIMIT),
        name="moe_combine_ln",
    )(dest, ys, h1, rf, g2, b2)


def kernel(x, positions, w_in, b_gate, lambda_q1, lambda_k1, lambda_q2, lambda_k2, subln_g, w_o_att, conv_w, w_o_conv, w_mix_out, ln1_g, ln1_b, w_router_group, b_router_group, w_router_expert, b_router_expert, w_exp_gate, w_exp_up, w_exp_down, ln2_g, ln2_b):
    B, S, D = x.shape
    T = B * S
    l = 0
    lambda_init = 0.8 - 0.6 * math.exp(-0.3 * l)

    inv_freq = ROPE_THETA ** (-jnp.arange(0, HEAD_DIM, 2, dtype=F32) / HEAD_DIM)
    ang = positions.astype(F32)[..., None] * inv_freq
    cos_t = jnp.tile(jnp.cos(ang), (1, 1, 4))
    sin_h = jnp.sin(ang)
    sin_t = jnp.tile(jnp.concatenate([-sin_h, sin_h], -1), (1, 1, 2))

    n_main = 2 * QK_W + QK_W + 3 * CONV_W
    w0 = w_in[l]
    w_main = w0[:, :n_main].astype(BF16)
    wv_t = w0[:, 2 * QK_W:3 * QK_W].T.astype(BF16)
    w_gates = w0[:, n_main:].astype(BF16)

    q, k, vt, u = _proj_call(x, w_main, wv_t, cos_t, sin_t, conv_w[l])

    lam = (jnp.exp(jnp.sum(lambda_q1[l].astype(F32) * lambda_k1[l].astype(F32)))
           - jnp.exp(jnp.sum(lambda_q2[l].astype(F32) * lambda_k2[l].astype(F32)))
           + lambda_init).reshape(1)
    o_n = _attn_call(lam, q, k, vt, subln_g[l].astype(F32).reshape(HEAD_W, 1), 1.0 - lambda_init)

    wr = jnp.zeros((D, LANES), F32)
    wr = wr.at[:, 0:N_GROUPS].set(w_router_group[l]).at[:, SUBLANES:SUBLANES + N_EXPERTS].set(w_router_expert[l])
    br = jnp.zeros((1, LANES), F32)
    br = br.at[0, 0:N_GROUPS].set(b_router_group[l]).at[0, SUBLANES:SUBLANES + N_EXPERTS].set(b_router_expert[l])

    h1, h1p, ri, rf, cnt = _mix_call(
        x.reshape(T, D), o_n.reshape(T, QK_W), u.reshape(T, CONV_W),
        w_gates, b_gate[l].reshape(1, -1).astype(F32),
        w_o_att[l].astype(BF16), w_o_conv[l].astype(BF16), w_mix_out[l].astype(BF16),
        ln1_g[l].reshape(1, D).astype(F32), ln1_b[l].reshape(1, D).astype(F32),
        wr.astype(BF16), br)

    bm = MOE_BM
    counts = cnt[:, 0]
    padded = (counts + bm - 1) // bm * bm
    pend = jnp.cumsum(padded)
    pstart = pend - padded
    onehot = (ri[0:2, :, None] == jnp.arange(N_EXPERTS, dtype=jnp.int32)).astype(jnp.int32)
    dest = jnp.sum(onehot * pstart, axis=-1) + ri[2:4]
    A = 2 * T
    P = A + N_EXPERTS * bm
    nb = P // bm
    blk_start = jnp.arange(nb, dtype=jnp.int32) * bm
    block_expert = jnp.minimum(
        jnp.sum((pend[None, :] <= blk_start[:, None]).astype(jnp.int32), axis=1), N_EXPERTS - 1)
    n_used = (pend[-1] // bm).astype(jnp.int32).reshape(1)

    xs = _dispatch_call(pend.astype(jnp.int32), dest, h1p, P)
    ys = _expert_call(block_expert, n_used, xs, w_exp_gate[l], w_exp_up[l], w_exp_down[l])
    out = _combine_call(dest, ys, h1, rf,
                        ln2_g[l].reshape(1, D).astype(F32), ln2_b[l].reshape(1, D).astype(F32))
    return out.reshape(B, S, D)
```

```python
import functools
import math

import jax
import jax.numpy as jnp
from jax import lax
from jax.experimental import pallas as pl
from jax.experimental.pallas import tpu as pltpu

D_MODEL = 1024
ATT_HEADS = 4
HEAD_DIM = 64
HEAD_W = 2 * HEAD_DIM
QK_W = ATT_HEADS * HEAD_W
CONV_W = D_MODEL // 2
CONV_K = 3
N_GROUPS = 4
EPG = 8
N_EXPERTS = N_GROUPS * EPG
EXPERT_FF = D_MODEL // 2
ROPE_THETA = 10000.0
SUBLN_EPS = 1e-5
LN_EPS = 1e-5
DEPTH = 1
DEEPNORM_ALPHA = (2.0 * DEPTH) ** 0.25

LANES = 128
SUBLANES = 8
VMEM_LIMIT = 48 * 1024 * 1024

PROJ_TM = 512
ATT_TQ = 256
ATT_TK = 256
MIX_TM = 256
MOE_BM = 256
DISP_TM = 1024
COMB_TM = 256
NEG_BIG = -1e30

F32 = jnp.float32
BF16 = jnp.bfloat16


def _dot(a, b):
    return jnp.dot(a, b, preferred_element_type=F32)


def _dot_nt(a, b):
    return lax.dot_general(a, b, (((1,), (1,)), ((), ())), preferred_element_type=F32)


def _proj_kernel(x_ref, w_ref, wvt_ref, cos_ref, sin_ref, cw_ref,
                 q_ref, k_ref, vt_ref, u_ref, pbuf):
    tm = x_ref.shape[0]
    i = pl.program_id(1)
    xb = x_ref[...].astype(BF16)
    cos = cos_ref[...]
    sin = sin_ref[...]
    lane = lax.broadcasted_iota(jnp.int32, (tm, LANES), 1)
    low_half = (lane % HEAD_DIM) < (HEAD_DIM // 2)

    def rope(t):
        rot = jnp.where(low_half, pltpu.roll(t, LANES - HEAD_DIM // 2, 1),
                        pltpu.roll(t, HEAD_DIM // 2, 1))
        return t * cos + rot * sin

    q = _dot(xb, w_ref[:, 0:QK_W])
    for h in range(ATT_HEADS):
        sl = slice(h * HEAD_W, (h + 1) * HEAD_W)
        q_ref[:, sl] = (rope(q[:, sl]) * (HEAD_DIM ** -0.5 * math.log2(math.e))).astype(BF16)
    k = _dot(xb, w_ref[:, QK_W:2 * QK_W])
    for h in range(ATT_HEADS):
        sl = slice(h * HEAD_W, (h + 1) * HEAD_W)
        k_ref[:, sl] = rope(k[:, sl]).astype(BF16)
    vt_ref[...] = _dot_nt(wvt_ref[...], xb).astype(BF16)

    c0 = 3 * QK_W
    cb = _dot(xb, w_ref[:, c0:c0 + CONV_W])
    cc = _dot(xb, w_ref[:, c0 + CONV_W:c0 + 2 * CONV_W])
    cx = _dot(xb, w_ref[:, c0 + 2 * CONV_W:c0 + 3 * CONV_W])
    p = cc * cx

    @pl.when(i == 0)
    def _():
        pbuf[0:SUBLANES, :] = jnp.zeros((SUBLANES, CONV_W), F32)

    pbuf[SUBLANES:SUBLANES + tm, :] = p
    y = (cw_ref[0:1, :] * pbuf[SUBLANES - 2:SUBLANES - 2 + tm, :]
         + cw_ref[1:2, :] * pbuf[SUBLANES - 1:SUBLANES - 1 + tm, :]
         + cw_ref[2:3, :] * p)
    u_ref[...] = (cb * y).astype(BF16)
    pbuf[0:SUBLANES, :] = pbuf[tm:tm + SUBLANES, :]


def _proj_call(x, w_main, wv_t, cos_t, sin_t, conv_w):
    B, S, D = x.shape
    tm = PROJ_TM
    n_main = w_main.shape[1]
    return pl.pallas_call(
        _proj_kernel,
        grid=(B, S // tm),
        in_specs=[
            pl.BlockSpec((None, tm, D), lambda b, i: (b, i, 0)),
            pl.BlockSpec((D, n_main), lambda b, i: (0, 0)),
            pl.BlockSpec((QK_W, D), lambda b, i: (0, 0)),
            pl.BlockSpec((None, tm, LANES), lambda b, i: (b, i, 0)),
            pl.BlockSpec((None, tm, LANES), lambda b, i: (b, i, 0)),
            pl.BlockSpec((CONV_K, CONV_W), lambda b, i: (0, 0)),
        ],
        out_specs=[
            pl.BlockSpec((None, tm, QK_W), lambda b, i: (b, i, 0)),
            pl.BlockSpec((None, tm, QK_W), lambda b, i: (b, i, 0)),
            pl.BlockSpec((None, QK_W, tm), lambda b, i: (b, 0, i)),
            pl.BlockSpec((None, tm, CONV_W), lambda b, i: (b, i, 0)),
        ],
        out_shape=[
            jax.ShapeDtypeStruct((B, S, QK_W), BF16),
            jax.ShapeDtypeStruct((B, S, QK_W), BF16),
            jax.ShapeDtypeStruct((B, QK_W, S), BF16),
            jax.ShapeDtypeStruct((B, S, CONV_W), BF16),
        ],
        scratch_shapes=[pltpu.VMEM((tm + 2 * SUBLANES, CONV_W), F32)],
        compiler_params=pltpu.CompilerParams(
            dimension_semantics=("arbitrary", "arbitrary"),
            vmem_limit_bytes=VMEM_LIMIT),
        name="proj_rope_conv",
    )(x, w_main, wv_t, cos_t, sin_t, conv_w)


ACC_ROWS = HEAD_W + 16


def _attn_kernel(lam_ref, q_ref, k_ref, vt_ref, g_ref, o_ref, qzt_ref, *scratch, out_scale):
    nh = ATT_HEADS
    s_refs, p_refs = scratch[0:nh], scratch[nh:2 * nh]
    m_refs, a_refs, acc_refs = scratch[2 * nh:3 * nh], scratch[3 * nh:4 * nh], scratch[4 * nh:5 * nh]
    tq = q_ref.shape[0]
    tk = ATT_TK
    nblk = 2 * tq // LANES
    i = pl.program_id(1)
    lane = lax.broadcasted_iota(jnp.int32, (tq, HEAD_W), 1)
    ones_rows = jnp.ones((ACC_ROWS - HEAD_W, tk), BF16)

    for h in range(nh):
        qf = q_ref[:, h * HEAD_W:(h + 1) * HEAD_W].astype(F32)
        q1 = jnp.where(lane < HEAD_DIM, qf, 0.0)
        q2 = jnp.where(lane >= HEAD_DIM, qf, 0.0)
        qzt_ref[h, :, 0:tq] = q1.T.astype(BF16)
        qzt_ref[h, :, tq:2 * tq] = q2.T.astype(BF16)
        m_refs[h][...] = jnp.full(m_refs[h].shape, NEG_BIG, F32)
        acc_refs[h][...] = jnp.zeros(acc_refs[h].shape, F32)

    def scores(h, k0):
        kj = k_ref[pl.ds(k0, tk), h * HEAD_W:(h + 1) * HEAD_W]
        s_refs[h][:, 0:2 * tq] = _dot(kj, qzt_ref[h])

    def softmax(h, masked):
        for c in range(nblk):
            cs = slice(c * LANES, (c + 1) * LANES)
            s = s_refs[h][:, cs]
            if masked:
                key = lax.broadcasted_iota(jnp.int32, (tk, LANES), 0)
                qry = lax.broadcasted_iota(jnp.int32, (tk, LANES), 1) + (c * LANES) % tq
                s = jnp.where(key <= qry, s, NEG_BIG)
            m_old = m_refs[h][:, cs]
            m_new = jnp.maximum(m_old, jnp.max(s, axis=0, keepdims=True))
            a_refs[h][:, cs] = jnp.exp2(m_old - m_new)
            m_refs[h][:, cs] = m_new
            p_refs[h][:, cs] = jnp.exp2(s - m_new).astype(BF16)

    def accumulate(h, k0):
        vj = jnp.concatenate([vt_ref[h * HEAD_W:(h + 1) * HEAD_W, pl.ds(k0, tk)], ones_rows], axis=0)
        acc_refs[h][...] = a_refs[h][...] * acc_refs[h][...] + _dot(vj, p_refs[h][:, 0:2 * tq])

    for h in range(nh):
        scores(h, 0)

    def body(j, carry):
        k0 = pl.multiple_of(j * tk, tk)
        k1 = pl.multiple_of(j * tk + tk, tk)
        for h in range(nh):
            softmax(h, False)
            scores(h, k1)
            accumulate(h, k0)
        return carry

    lax.fori_loop(0, i, body, 0)
    kd = pl.multiple_of(i * tk, tk)
    for h in range(nh):
        softmax(h, True)
        accumulate(h, kd)

    lam = lam_ref[0]
    for h in range(nh):
        acc = acc_refs[h][0:HEAD_W, :]
        l = acc_refs[h][HEAD_W:HEAD_W + 1, :]
        o = acc[:, 0:tq] / l[:, 0:tq] - lam * (acc[:, tq:2 * tq] / l[:, tq:2 * tq])
        ms = jnp.mean(o * o, axis=0, keepdims=True)
        o = o * lax.rsqrt(ms + SUBLN_EPS) * g_ref[...] * out_scale
        o_ref[:, h * HEAD_W:(h + 1) * HEAD_W] = o.T.astype(BF16)


def _attn_call(lam, q, k, vt, g_col, out_scale):
    B, S, _ = q.shape
    tq = ATT_TQ
    assert ATT_TQ == ATT_TK
    kernel = functools.partial(_attn_kernel, out_scale=out_scale)
    return pl.pallas_call(
        kernel,
        grid=(B, S // tq),
        in_specs=[
            pl.BlockSpec(memory_space=pltpu.SMEM),
            pl.BlockSpec((None, tq, QK_W), lambda b, i: (b, i, 0)),
            pl.BlockSpec((None, S, QK_W), lambda b, i: (b, 0, 0), pipeline_mode=pl.Buffered(1)),
            pl.BlockSpec((None, QK_W, S), lambda b, i: (b, 0, 0), pipeline_mode=pl.Buffered(1)),
            pl.BlockSpec((HEAD_W, 1), lambda b, i: (0, 0)),
        ],
        out_specs=pl.BlockSpec((None, tq, QK_W), lambda b, i: (b, i, 0)),
        out_shape=jax.ShapeDtypeStruct((B, S, QK_W), BF16),
        scratch_shapes=(
            [pltpu.VMEM((ATT_HEADS, HEAD_W, 2 * tq), BF16)]
            + [pltpu.VMEM((ATT_TK, 2 * tq + LANES), F32) for _ in range(ATT_HEADS)]
            + [pltpu.VMEM((ATT_TK, 2 * tq + LANES), BF16) for _ in range(ATT_HEADS)]
            + [pltpu.VMEM((1, 2 * tq), F32) for _ in range(ATT_HEADS)]
            + [pltpu.VMEM((1, 2 * tq), F32) for _ in range(ATT_HEADS)]
            + [pltpu.VMEM((ACC_ROWS, 2 * tq), F32) for _ in range(ATT_HEADS)]
        ),
        compiler_params=pltpu.CompilerParams(
            dimension_semantics=("arbitrary", "arbitrary"),
            vmem_limit_bytes=VMEM_LIMIT),
        name="diff_flash_attn",
    )(lam, q, k, vt, g_col)


def _pack_bf16_pairs(h):
    half = h.shape[1] // 2
    hb = h.astype(BF16).astype(F32)
    bits = lax.bitcast_convert_type(hb, jnp.uint32)
    return (bits[:, 0:half] >> 16) | (bits[:, half:] & jnp.uint32(0xFFFF0000))


def _unpack_bf16_pairs(w):
    lo = lax.bitcast_convert_type(w << 16, F32).astype(BF16)
    hi = lax.bitcast_convert_type(w & jnp.uint32(0xFFFF0000), F32).astype(BF16)
    return jnp.concatenate([lo, hi], axis=1)


def _first_index_of(mask, row_f, big):
    return jnp.min(jnp.where(mask, row_f, big), axis=0, keepdims=True)


def _mix_kernel(x_ref, o_ref, u_ref, wg_ref, bg_ref, woa_ref, woc_ref, wmo_ref, g1_ref, b1_ref,
                wr_ref, br_ref,
                h1_ref, h1p_ref, ri_ref, rf_ref, cnt_ref, carry_ref):
    tm = x_ref.shape[0]
    step = pl.program_id(0)

    @pl.when(step == 0)
    def _():
        carry_ref[...] = jnp.zeros(carry_ref.shape, F32)

    x = x_ref[...]
    xb = x.astype(BF16)
    gates = jax.nn.sigmoid(_dot(xb, wg_ref[...]) + bg_ref[...])
    ya = _dot(o_ref[...], woa_ref[...])
    yc = _dot(u_ref[...], woc_ref[...])
    merged = (gates[:, 0:D_MODEL] * ya + gates[:, D_MODEL:] * yc).astype(BF16)
    z = DEEPNORM_ALPHA * x + _dot(merged, wmo_ref[...])
    mu = jnp.mean(z, axis=-1, keepdims=True)
    zc = z - mu
    var = jnp.mean(zc * zc, axis=-1, keepdims=True)
    h1 = zc * lax.rsqrt(var + LN_EPS) * g1_ref[...] + b1_ref[...]
    h1_ref[...] = h1
    h1p_ref[...] = _pack_bf16_pairs(h1)

    logits = _dot(h1.astype(BF16), wr_ref[...]) + br_ref[...]
    lt = logits.T
    row8 = lax.broadcasted_iota(jnp.int32, (SUBLANES, tm), 0).astype(F32)
    gl = jnp.where(row8 < N_GROUPS, lt[0:SUBLANES, :], NEG_BIG)
    gmax = jnp.max(gl, axis=0, keepdims=True)
    gsel = _first_index_of(gl == gmax, row8, float(SUBLANES))
    gw = 1.0 / jnp.sum(jnp.exp(gl - gmax), axis=0, keepdims=True)
    el = lt[SUBLANES:2 * SUBLANES, :]
    for g in range(1, N_GROUPS):
        el = jnp.where(gsel == float(g), lt[(g + 1) * SUBLANES:(g + 2) * SUBLANES, :], el)
    v1 = jnp.max(el, axis=0, keepdims=True)
    i1 = _first_index_of(el == v1, row8, float(EPG))
    el2 = jnp.where(row8 == i1, -jnp.inf, el)
    v2 = jnp.max(el2, axis=0, keepdims=True)
    i2 = _first_index_of(el2 == v2, row8, float(EPG))
    t = jnp.exp(v2 - v1)
    p1 = gw / (1.0 + t)
    p2 = gw * t / (1.0 + t)
    e1 = gsel * float(EPG) + i1
    e2 = gsel * float(EPG) + i2

    rowe = lax.broadcasted_iota(jnp.int32, (N_EXPERTS, tm), 0).astype(F32)
    oh1 = (rowe == e1).astype(F32)
    oh2 = (rowe == e2).astype(F32)
    r_i = lax.broadcasted_iota(jnp.int32, (tm, tm), 0)
    c_i = lax.broadcasted_iota(jnp.int32, (tm, tm), 1)
    upper = jnp.where(r_i < c_i, 1.0, 0.0).astype(BF16)
    cum1 = _dot(oh1.astype(BF16), upper)
    cum2 = _dot(oh2.astype(BF16), upper)
    tot1 = jnp.sum(oh1, axis=1, keepdims=True)
    tot2 = jnp.sum(oh2, axis=1, keepdims=True)
    carry = carry_ref[:, 0:1]
    rank1 = jnp.sum(oh1 * (carry + cum1), axis=0, keepdims=True)
    rank2 = jnp.sum(oh2 * (carry + tot1 + cum2), axis=0, keepdims=True)
    new_carry = carry_ref[...] + (tot1 + tot2)
    carry_ref[...] = new_carry
    cnt_ref[...] = new_carry.astype(jnp.int32)

    ri_ref[...] = jnp.zeros(ri_ref.shape, jnp.int32)
    ri_ref[0:1, :] = e1.astype(jnp.int32)
    ri_ref[1:2, :] = e2.astype(jnp.int32)
    ri_ref[2:3, :] = rank1.astype(jnp.int32)
    ri_ref[3:4, :] = rank2.astype(jnp.int32)

    row128 = lax.broadcasted_iota(jnp.int32, (LANES, tm), 0)
    pw = jnp.where(row128 == 0, p1, jnp.where(row128 == 1, p2, 0.0))
    rf_ref[...] = pw.T


def _mix_call(x2, o_n, u, wg, bg, woa, woc, wmo, g1, b1, wr, br):
    T, D = x2.shape
    tm = MIX_TM
    const = lambda i: (0, 0)
    row = lambda i: (i, 0)
    return pl.pallas_call(
        _mix_kernel,
        grid=(T // tm,),
        in_specs=[
            pl.BlockSpec((tm, D), row),
            pl.BlockSpec((tm, QK_W), row),
            pl.BlockSpec((tm, CONV_W), row),
            pl.BlockSpec(wg.shape, const),
            pl.BlockSpec(bg.shape, const),
            pl.BlockSpec(woa.shape, const),
            pl.BlockSpec(woc.shape, const),
            pl.BlockSpec(wmo.shape, const),
            pl.BlockSpec(g1.shape, const),
            pl.BlockSpec(b1.shape, const),
            pl.BlockSpec(wr.shape, const),
            pl.BlockSpec(br.shape, const),
        ],
        out_specs=[
            pl.BlockSpec((tm, D), row),
            pl.BlockSpec((tm, D // 2), row),
            pl.BlockSpec((SUBLANES, tm), lambda i: (0, i)),
            pl.BlockSpec((tm, LANES), row),
            pl.BlockSpec((N_EXPERTS, LANES), const),
        ],
        out_shape=[
            jax.ShapeDtypeStruct((T, D), F32),
            jax.ShapeDtypeStruct((T, D // 2), jnp.uint32),
            jax.ShapeDtypeStruct((SUBLANES, T), jnp.int32),
            jax.ShapeDtypeStruct((T, LANES), F32),
            jax.ShapeDtypeStruct((N_EXPERTS, LANES), jnp.int32),
        ],
        scratch_shapes=[pltpu.VMEM((N_EXPERTS, LANES), F32)],
        compiler_params=pltpu.CompilerParams(
            dimension_semantics=("arbitrary",),
            vmem_limit_bytes=VMEM_LIMIT),
        name="mix_ln_router",
    )(x2, o_n, u, wg, bg, woa, woc, wmo, g1, b1, wr, br)


def _dispatch_kernel(pend_ref, dest_ref, h1p_ref, xs_ref, zero_ref, sem):
    tm = h1p_ref.shape[0]
    bm = zero_ref.shape[0]

    @pl.when(pl.program_id(0) == 0)
    def _():
        zero_ref[...] = jnp.zeros(zero_ref.shape, zero_ref.dtype)
        n_rows = xs_ref.shape[0]

        def zero_copy(e):
            if e < N_EXPERTS:
                prev_end = pend_ref[e - 1] if e > 0 else 0
                start = pl.multiple_of(jnp.maximum(pend_ref[e] - bm, 0), bm)
                needed = pend_ref[e] > prev_end
            else:
                start = n_rows - (e - N_EXPERTS + 1) * bm
                needed = start >= pend_ref[N_EXPERTS - 1]
            return pltpu.make_async_copy(zero_ref, xs_ref.at[pl.ds(start, bm), :], sem), needed

        for e in range(2 * N_EXPERTS):
            copy, needed = zero_copy(e)
            pl.when(needed)(copy.start)
        for e in range(2 * N_EXPERTS):
            copy, needed = zero_copy(e)
            pl.when(needed)(copy.wait)

    def row_copy(r, k):
        return pltpu.make_async_copy(
            h1p_ref.at[pl.ds(r, 1), :],
            xs_ref.at[pl.ds(dest_ref[k, r], 1), :],
            sem)

    def issue(r, carry):
        row_copy(r, 0).start()
        row_copy(r, 1).start()
        return carry

    lax.fori_loop(0, tm, issue, 0, unroll=8)
    for _ in range(2):
        pltpu.make_async_copy(h1p_ref, xs_ref.at[pl.ds(0, tm), :], sem).wait()


def _dispatch_call(pend, dest, h1p, n_rows):
    T, W = h1p.shape
    tm = DISP_TM
    return pl.pallas_call(
        _dispatch_kernel,
        grid_spec=pltpu.PrefetchScalarGridSpec(
            num_scalar_prefetch=1,
            grid=(T // tm,),
            in_specs=[
                pl.BlockSpec((2, tm), lambda i, pend: (0, i), memory_space=pltpu.SMEM),
                pl.BlockSpec((tm, W), lambda i, pend: (i, 0)),
            ],
            out_specs=pl.BlockSpec(memory_space=pl.ANY),
            scratch_shapes=[pltpu.VMEM((MOE_BM, W), h1p.dtype), pltpu.SemaphoreType.DMA],
        ),
        out_shape=jax.ShapeDtypeStruct((n_rows, W), h1p.dtype),
        compiler_params=pltpu.CompilerParams(
            dimension_semantics=("arbitrary",), has_side_effects=True),
        name="moe_dispatch",
    )(pend, dest, h1p)


def _expert_kernel(be_ref, nu_ref, xs_ref, wg_ref, wu_ref, wd_ref, ys_ref, wgb_ref, wub_ref, wdb_ref):
    i = pl.program_id(0)
    used = i < nu_ref[0]

    @pl.when(used & ((i == 0) | (be_ref[i] != be_ref[jnp.maximum(i - 1, 0)])))
    def _():
        wgb_ref[...] = wg_ref[...].astype(BF16)
        wub_ref[...] = wu_ref[...].astype(BF16)
        wdb_ref[...] = wd_ref[...].astype(BF16)

    @pl.when(used)
    def _():
        xb = _unpack_bf16_pairs(xs_ref[...])
        g = _dot(xb, wgb_ref[...])
        u = _dot(xb, wub_ref[...])
        h = (g * jax.nn.sigmoid(g) * u).astype(BF16)
        ys_ref[...] = _dot(h, wdb_ref[...])

    @pl.when(i >= nu_ref[0])
    def _():
        ys_ref[...] = jnp.zeros(ys_ref.shape, F32)


def _expert_call(block_expert, n_used, xs, wg, wu, wd):
    P = xs.shape[0]
    bm = MOE_BM
    nb = P // bm
    xs_map = lambda i, be, nu: (jnp.minimum(i, nu[0] - 1), 0)
    w_map = lambda i, be, nu: (be[i], 0, 0)
    return pl.pallas_call(
        _expert_kernel,
        grid_spec=pltpu.PrefetchScalarGridSpec(
            num_scalar_prefetch=2,
            grid=(nb,),
            in_specs=[
                pl.BlockSpec((bm, D_MODEL // 2), xs_map),
                pl.BlockSpec((None, D_MODEL, EXPERT_FF), w_map),
                pl.BlockSpec((None, D_MODEL, EXPERT_FF), w_map),
                pl.BlockSpec((None, EXPERT_FF, D_MODEL), w_map),
            ],
            out_specs=pl.BlockSpec((bm, D_MODEL), lambda i, be, nu: (i, 0)),
            scratch_shapes=[
                pltpu.VMEM((D_MODEL, EXPERT_FF), BF16),
                pltpu.VMEM((D_MODEL, EXPERT_FF), BF16),
                pltpu.VMEM((EXPERT_FF, D_MODEL), BF16),
            ],
        ),
        out_shape=jax.ShapeDtypeStruct((P, D_MODEL), F32),
        compiler_params=pltpu.CompilerParams(
            dimension_semantics=("arbitrary",),
            vmem_limit_bytes=VMEM_LIMIT),
        name="moe_experts",
    )(block_expert, n_used, xs, wg, wu, wd)


def _combine_kernel(dest_ref, dest_next_ref, ys_ref, h1_ref, rf_ref, g2_ref, b2_ref, out_ref, buf, sem):
    tm = h1_ref.shape[0]
    i = pl.program_id(0)
    n = pl.num_programs(0)
    slot = i % 2

    def issue_tile(d_ref, s):
        def issue(r, carry):
            for k in range(2):
                pltpu.make_async_copy(ys_ref.at[pl.ds(d_ref[k, r], 1), :],
                                      buf.at[s, k, pl.ds(r, 1), :], sem.at[s]).start()
            return carry
        lax.fori_loop(0, tm, issue, 0, unroll=8)

    @pl.when(i == 0)
    def _():
        issue_tile(dest_ref, 0)

    @pl.when(i + 1 < n)
    def _():
        issue_tile(dest_next_ref, 1 - slot)

    for k in range(2):
        pltpu.make_async_copy(ys_ref.at[pl.ds(0, tm), :], buf.at[slot, k], sem.at[slot]).wait()

    rf = rf_ref[...]
    ffn = rf[:, 0:1] * buf[slot, 0] + rf[:, 1:2] * buf[slot, 1]
    z = DEEPNORM_ALPHA * h1_ref[...] + ffn
    mu = jnp.mean(z, axis=-1, keepdims=True)
    zc = z - mu
    var = jnp.mean(zc * zc, axis=-1, keepdims=True)
    out_ref[...] = zc * lax.rsqrt(var + LN_EPS) * g2_ref[...] + b2_ref[...]


def _combine_call(dest, ys, h1, rf, g2, b2):
    T, D = h1.shape
    tm = COMB_TM
    const = lambda i: (0, 0)
    row = lambda i: (i, 0)
    return pl.pallas_call(
        _combine_kernel,
        grid=(T // tm,),
        in_specs=[
            pl.BlockSpec((2, tm), lambda i: (0, i), memory_space=pltpu.SMEM),
            pl.BlockSpec((2, tm), lambda i: (0, jnp.minimum(i + 1, T // tm - 1)), memory_space=pltpu.SMEM),
            pl.BlockSpec(memory_space=pl.ANY),
            pl.BlockSpec((tm, D), row),
            pl.BlockSpec((tm, LANES), row),
            pl.BlockSpec(g2.shape, const),
            pl.BlockSpec(b2.shape, const),
        ],
        out_specs=pl.BlockSpec((tm, D), row),
        out_shape=jax.ShapeDtypeStruct((T, D), F32),
        scratch_shapes=[
            pltpu.VMEM((2, 2, tm, D), F32),
            pltpu.SemaphoreType.DMA((2,)),
        ],
        compiler_params=pltpu.CompilerParams(
            dimension_semantics=("arbitrary",),
            vmem_limit_bytes=VMEM_LIMIT),
        name="moe_combine_ln",
    )(dest, dest, ys, h1, rf, g2, b2)


def kernel(x, positions, w_in, b_gate, lambda_q1, lambda_k1, lambda_q2, lambda_k2, subln_g, w_o_att, conv_w, w_o_conv, w_mix_out, ln1_g, ln1_b, w_router_group, b_router_group, w_router_expert, b_router_expert, w_exp_gate, w_exp_up, w_exp_down, ln2_g, ln2_b):
    B, S, D = x.shape
    T = B * S
    l = 0
    lambda_init = 0.8 - 0.6 * math.exp(-0.3 * l)

    inv_freq = ROPE_THETA ** (-jnp.arange(0, HEAD_DIM, 2, dtype=F32) / HEAD_DIM)
    ang = positions.astype(F32)[..., None] * inv_freq
    cos_t = jnp.tile(jnp.cos(ang), (1, 1, 4))
    sin_h = jnp.sin(ang)
    sin_t = jnp.tile(jnp.concatenate([-sin_h, sin_h], -1), (1, 1, 2))

    n_main = 2 * QK_W + QK_W + 3 * CONV_W
    w0 = w_in[l]
    w_main = w0[:, :n_main].astype(BF16)
    wv_t = w0[:, 2 * QK_W:3 * QK_W].T.astype(BF16)
    w_gates = w0[:, n_main:].astype(BF16)

    q, k, vt, u = _proj_call(x, w_main, wv_t, cos_t, sin_t, conv_w[l])

    lam = (jnp.exp(jnp.sum(lambda_q1[l].astype(F32) * lambda_k1[l].astype(F32)))
           - jnp.exp(jnp.sum(lambda_q2[l].astype(F32) * lambda_k2[l].astype(F32)))
           + lambda_init).reshape(1)
    o_n = _attn_call(lam, q, k, vt, subln_g[l].astype(F32).reshape(HEAD_W, 1), 1.0 - lambda_init)

    wr = jnp.zeros((D, LANES), F32)
    wr = wr.at[:, 0:N_GROUPS].set(w_router_group[l]).at[:, SUBLANES:SUBLANES + N_EXPERTS].set(w_router_expert[l])
    br = jnp.zeros((1, LANES), F32)
    br = br.at[0, 0:N_GROUPS].set(b_router_group[l]).at[0, SUBLANES:SUBLANES + N_EXPERTS].set(b_router_expert[l])

    h1, h1p, ri, rf, cnt = _mix_call(
        x.reshape(T, D), o_n.reshape(T, QK_W), u.reshape(T, CONV_W),
        w_gates, b_gate[l].reshape(1, -1).astype(F32),
        w_o_att[l].astype(BF16), w_o_conv[l].astype(BF16), w_mix_out[l].astype(BF16),
        ln1_g[l].reshape(1, D).astype(F32), ln1_b[l].reshape(1, D).astype(F32),
        wr.astype(BF16), br)

    bm = MOE_BM
    counts = cnt[:, 0]
    padded = (counts + bm - 1) // bm * bm
    pend = jnp.cumsum(padded)
    pstart = pend - padded
    onehot = (ri[0:2, :, None] == jnp.arange(N_EXPERTS, dtype=jnp.int32)).astype(jnp.int32)
    dest = jnp.sum(onehot * pstart, axis=-1) + ri[2:4]
    A = 2 * T
    P = A + N_EXPERTS * bm
    nb = P // bm
    blk_start = jnp.arange(nb, dtype=jnp.int32) * bm
    block_expert = jnp.minimum(
        jnp.sum((pend[None, :] <= blk_start[:, None]).astype(jnp.int32), axis=1), N_EXPERTS - 1)
    n_used = (pend[-1] // bm).astype(jnp.int32).reshape(1)

    xs = _dispatch_call(pend.astype(jnp.int32), dest, h1p, P)
    ys = _expert_call(block_expert, n_used, xs, w_exp_gate[l], w_exp_up[l], w_exp_down[l])
    out = _combine_call(dest, ys, h1, rf,
                        ln2_g[l].reshape(1, D).astype(F32), ln2_b[l].reshape(1, D).astype(F32))
    return out.reshape(B, S, D)
```

```python
import functools
import math

import jax
import jax.numpy as jnp
from jax import lax
from jax.experimental import pallas as pl
from jax.experimental.pallas import tpu as pltpu

D_MODEL = 1024
ATT_HEADS = 4
HEAD_DIM = 64
HEAD_W = 2 * HEAD_DIM
QK_W = ATT_HEADS * HEAD_W
CONV_W = D_MODEL // 2
CONV_K = 3
N_GROUPS = 4
EPG = 8
N_EXPERTS = N_GROUPS * EPG
EXPERT_FF = D_MODEL // 2
ROPE_THETA = 10000.0
SUBLN_EPS = 1e-5
LN_EPS = 1e-5
DEPTH = 1
DEEPNORM_ALPHA = (2.0 * DEPTH) ** 0.25

LANES = 128
SUBLANES = 8
VMEM_LIMIT = 48 * 1024 * 1024

PROJ_TM = 512
ATT_TQ = 256
ATT_TK = 256
MIX_TM = 512
MOE_BM = 256
DISP_TM = 1024
COMB_TM = 256
NEG_BIG = -1e30

F32 = jnp.float32
BF16 = jnp.bfloat16


def _dot(a, b):
    return jnp.dot(a, b, preferred_element_type=F32)


def _dot_nt(a, b):
    return lax.dot_general(a, b, (((1,), (1,)), ((), ())), preferred_element_type=F32)


def _proj_kernel(x_ref, w_ref, wvt_ref, cos_ref, sin_ref, cw_ref,
                 q_ref, k_ref, vt_ref, u_ref, pbuf):
    tm = x_ref.shape[0]
    i = pl.program_id(1)
    xb = x_ref[...].astype(BF16)
    cos = cos_ref[...]
    sin = sin_ref[...]
    lane = lax.broadcasted_iota(jnp.int32, (tm, LANES), 1)
    low_half = (lane % HEAD_DIM) < (HEAD_DIM // 2)

    def rope(t):
        rot = jnp.where(low_half, pltpu.roll(t, LANES - HEAD_DIM // 2, 1),
                        pltpu.roll(t, HEAD_DIM // 2, 1))
        return t * cos + rot * sin

    q = _dot(xb, w_ref[:, 0:QK_W])
    for h in range(ATT_HEADS):
        sl = slice(h * HEAD_W, (h + 1) * HEAD_W)
        q_ref[:, sl] = (rope(q[:, sl]) * (HEAD_DIM ** -0.5 * math.log2(math.e))).astype(BF16)
    k = _dot(xb, w_ref[:, QK_W:2 * QK_W])
    for h in range(ATT_HEADS):
        sl = slice(h * HEAD_W, (h + 1) * HEAD_W)
        k_ref[:, sl] = rope(k[:, sl]).astype(BF16)
    vt_ref[...] = _dot_nt(wvt_ref[...], xb).astype(BF16)

    c0 = 3 * QK_W
    cb = _dot(xb, w_ref[:, c0:c0 + CONV_W])
    cc = _dot(xb, w_ref[:, c0 + CONV_W:c0 + 2 * CONV_W])
    cx = _dot(xb, w_ref[:, c0 + 2 * CONV_W:c0 + 3 * CONV_W])
    p = cc * cx

    @pl.when(i == 0)
    def _():
        pbuf[0:SUBLANES, :] = jnp.zeros((SUBLANES, CONV_W), F32)

    pbuf[SUBLANES:SUBLANES + tm, :] = p
    y = (cw_ref[0:1, :] * pbuf[SUBLANES - 2:SUBLANES - 2 + tm, :]
         + cw_ref[1:2, :] * pbuf[SUBLANES - 1:SUBLANES - 1 + tm, :]
         + cw_ref[2:3, :] * p)
    u_ref[...] = (cb * y).astype(BF16)
    pbuf[0:SUBLANES, :] = pbuf[tm:tm + SUBLANES, :]


def _proj_call(x, w_main, wv_t, cos_t, sin_t, conv_w):
    B, S, D = x.shape
    tm = PROJ_TM
    n_main = w_main.shape[1]
    return pl.pallas_call(
        _proj_kernel,
        grid=(B, S // tm),
        in_specs=[
            pl.BlockSpec((None, tm, D), lambda b, i: (b, i, 0)),
            pl.BlockSpec((D, n_main), lambda b, i: (0, 0)),
            pl.BlockSpec((QK_W, D), lambda b, i: (0, 0)),
            pl.BlockSpec((None, tm, LANES), lambda b, i: (b, i, 0)),
            pl.BlockSpec((None, tm, LANES), lambda b, i: (b, i, 0)),
            pl.BlockSpec((CONV_K, CONV_W), lambda b, i: (0, 0)),
        ],
        out_specs=[
            pl.BlockSpec((None, tm, QK_W), lambda b, i: (b, i, 0)),
            pl.BlockSpec((None, tm, QK_W), lambda b, i: (b, i, 0)),
            pl.BlockSpec((None, QK_W, tm), lambda b, i: (b, 0, i)),
            pl.BlockSpec((None, tm, CONV_W), lambda b, i: (b, i, 0)),
        ],
        out_shape=[
            jax.ShapeDtypeStruct((B, S, QK_W), BF16),
            jax.ShapeDtypeStruct((B, S, QK_W), BF16),
            jax.ShapeDtypeStruct((B, QK_W, S), BF16),
            jax.ShapeDtypeStruct((B, S, CONV_W), BF16),
        ],
        scratch_shapes=[pltpu.VMEM((tm + 2 * SUBLANES, CONV_W), F32)],
        compiler_params=pltpu.CompilerParams(
            dimension_semantics=("arbitrary", "arbitrary"),
            vmem_limit_bytes=VMEM_LIMIT),
        name="proj_rope_conv",
    )(x, w_main, wv_t, cos_t, sin_t, conv_w)


ACC_ROWS = HEAD_W + 16


def _attn_kernel(lam_ref, q_ref, k_ref, vt_ref, g_ref, o_ref, qzt_ref, *scratch, out_scale):
    nh = ATT_HEADS
    s_refs, p_refs = scratch[0:nh], scratch[nh:2 * nh]
    m_refs, a_refs, acc_refs = scratch[2 * nh:3 * nh], scratch[3 * nh:4 * nh], scratch[4 * nh:5 * nh]
    tq = q_ref.shape[0]
    tk = ATT_TK
    nblk = 2 * tq // LANES
    i = pl.program_id(1)
    lane = lax.broadcasted_iota(jnp.int32, (tq, HEAD_W), 1)
    ones_rows = jnp.ones((ACC_ROWS - HEAD_W, tk), BF16)

    for h in range(nh):
        qf = q_ref[:, h * HEAD_W:(h + 1) * HEAD_W].astype(F32)
        q1 = jnp.where(lane < HEAD_DIM, qf, 0.0)
        q2 = jnp.where(lane >= HEAD_DIM, qf, 0.0)
        qzt_ref[h, :, 0:tq] = q1.T.astype(BF16)
        qzt_ref[h, :, tq:2 * tq] = q2.T.astype(BF16)
        m_refs[h][...] = jnp.full(m_refs[h].shape, NEG_BIG, F32)
        acc_refs[h][...] = jnp.zeros(acc_refs[h].shape, F32)

    def scores(h, k0):
        kj = k_ref[pl.ds(k0, tk), h * HEAD_W:(h + 1) * HEAD_W]
        s_refs[h][:, 0:2 * tq] = _dot(kj, qzt_ref[h])

    def softmax(h, masked):
        for c in range(nblk):
            cs = slice(c * LANES, (c + 1) * LANES)
            s = s_refs[h][:, cs]
            if masked:
                key = lax.broadcasted_iota(jnp.int32, (tk, LANES), 0)
                qry = lax.broadcasted_iota(jnp.int32, (tk, LANES), 1) + (c * LANES) % tq
                s = jnp.where(key <= qry, s, NEG_BIG)
            m_old = m_refs[h][:, cs]
            m_new = jnp.maximum(m_old, jnp.max(s, axis=0, keepdims=True))
            a_refs[h][:, cs] = jnp.exp2(m_old - m_new)
            m_refs[h][:, cs] = m_new
            p_refs[h][:, cs] = jnp.exp2(s - m_new).astype(BF16)

    def accumulate(h, k0):
        vj = jnp.concatenate([vt_ref[h * HEAD_W:(h + 1) * HEAD_W, pl.ds(k0, tk)], ones_rows], axis=0)
        acc_refs[h][...] = a_refs[h][...] * acc_refs[h][...] + _dot(vj, p_refs[h][:, 0:2 * tq])

    def tile_start(j):
        return pl.multiple_of(j * tk, tk)

    for h in range(nh):
        scores(h, 0)

    @pl.when(i > 0)
    def _():
        for h in range(nh):
            softmax(h, False)
            scores(h, tile_start(1))

    def step(j):
        for h in range(nh):
            accumulate(h, tile_start(j - 1))
            softmax(h, False)
            scores(h, tile_start(j + 1))

    n_pairs = (i - 1) // 2

    def body(t, carry):
        step(2 * t + 1)
        step(2 * t + 2)
        return carry

    lax.fori_loop(0, n_pairs, body, 0)

    @pl.when((i > 0) & ((i - 1) % 2 == 1))
    def _():
        step(i - 1)

    @pl.when(i > 0)
    def _():
        for h in range(nh):
            accumulate(h, tile_start(i - 1))

    for h in range(nh):
        softmax(h, True)
        accumulate(h, tile_start(i))

    lam = lam_ref[0]
    for h in range(nh):
        acc = acc_refs[h][0:HEAD_W, :]
        l = acc_refs[h][HEAD_W:HEAD_W + 1, :]
        o = acc[:, 0:tq] / l[:, 0:tq] - lam * (acc[:, tq:2 * tq] / l[:, tq:2 * tq])
        ms = jnp.mean(o * o, axis=0, keepdims=True)
        o = o * lax.rsqrt(ms + SUBLN_EPS) * g_ref[...] * out_scale
        o_ref[:, h * HEAD_W:(h + 1) * HEAD_W] = o.T.astype(BF16)


def _attn_call(lam, q, k, vt, g_col, out_scale):
    B, S, _ = q.shape
    tq = ATT_TQ
    assert ATT_TQ == ATT_TK
    kernel = functools.partial(_attn_kernel, out_scale=out_scale)
    return pl.pallas_call(
        kernel,
        grid=(B, S // tq),
        in_specs=[
            pl.BlockSpec(memory_space=pltpu.SMEM),
            pl.BlockSpec((None, tq, QK_W), lambda b, i: (b, i, 0)),
            pl.BlockSpec((None, S, QK_W), lambda b, i: (b, 0, 0), pipeline_mode=pl.Buffered(1)),
            pl.BlockSpec((None, QK_W, S), lambda b, i: (b, 0, 0), pipeline_mode=pl.Buffered(1)),
            pl.BlockSpec((HEAD_W, 1), lambda b, i: (0, 0)),
        ],
        out_specs=pl.BlockSpec((None, tq, QK_W), lambda b, i: (b, i, 0)),
        out_shape=jax.ShapeDtypeStruct((B, S, QK_W), BF16),
        scratch_shapes=(
            [pltpu.VMEM((ATT_HEADS, HEAD_W, 2 * tq), BF16)]
            + [pltpu.VMEM((ATT_TK, 2 * tq + LANES), F32) for _ in range(ATT_HEADS)]
            + [pltpu.VMEM((ATT_TK, 2 * tq + LANES), BF16) for _ in range(ATT_HEADS)]
            + [pltpu.VMEM((1, 2 * tq), F32) for _ in range(ATT_HEADS)]
            + [pltpu.VMEM((1, 2 * tq), F32) for _ in range(ATT_HEADS)]
            + [pltpu.VMEM((ACC_ROWS, 2 * tq), F32) for _ in range(ATT_HEADS)]
        ),
        compiler_params=pltpu.CompilerParams(
            dimension_semantics=("arbitrary", "arbitrary"),
            vmem_limit_bytes=VMEM_LIMIT),
        name="diff_flash_attn",
    )(lam, q, k, vt, g_col)


def _pack_bf16_pairs(h):
    half = h.shape[1] // 2
    hb = h.astype(BF16).astype(F32)
    bits = lax.bitcast_convert_type(hb, jnp.uint32)
    return (bits[:, 0:half] >> 16) | (bits[:, half:] & jnp.uint32(0xFFFF0000))


def _unpack_bf16_pairs(w):
    lo = lax.bitcast_convert_type(w << 16, F32).astype(BF16)
    hi = lax.bitcast_convert_type(w & jnp.uint32(0xFFFF0000), F32).astype(BF16)
    return jnp.concatenate([lo, hi], axis=1)


def _first_index_of(mask, row_f, big):
    return jnp.min(jnp.where(mask, row_f, big), axis=0, keepdims=True)


def _mix_kernel(x_ref, o_ref, u_ref, wg_ref, bg_ref, woa_ref, woc_ref, wmo_ref, g1_ref, b1_ref,
                wr_ref, br_ref,
                h1_ref, h1p_ref, ri_ref, rf_ref, cnt_ref, carry_ref):
    tm = x_ref.shape[0]
    step = pl.program_id(0)

    @pl.when(step == 0)
    def _():
        carry_ref[...] = jnp.zeros(carry_ref.shape, F32)

    x = x_ref[...]
    xb = x.astype(BF16)
    gates = jax.nn.sigmoid(_dot(xb, wg_ref[...]) + bg_ref[...])
    ya = _dot(o_ref[...], woa_ref[...])
    yc = _dot(u_ref[...], woc_ref[...])
    merged = (gates[:, 0:D_MODEL] * ya + gates[:, D_MODEL:] * yc).astype(BF16)
    z = DEEPNORM_ALPHA * x + _dot(merged, wmo_ref[...])
    mu = jnp.mean(z, axis=-1, keepdims=True)
    zc = z - mu
    var = jnp.mean(zc * zc, axis=-1, keepdims=True)
    h1 = zc * lax.rsqrt(var + LN_EPS) * g1_ref[...] + b1_ref[...]
    h1_ref[...] = h1
    h1p_ref[...] = _pack_bf16_pairs(h1)

    logits = _dot(h1.astype(BF16), wr_ref[...]) + br_ref[...]
    lt = logits.T
    row8 = lax.broadcasted_iota(jnp.int32, (SUBLANES, tm), 0).astype(F32)
    gl = jnp.where(row8 < N_GROUPS, lt[0:SUBLANES, :], NEG_BIG)
    gmax = jnp.max(gl, axis=0, keepdims=True)
    gsel = _first_index_of(gl == gmax, row8, float(SUBLANES))
    gw = 1.0 / jnp.sum(jnp.exp(gl - gmax), axis=0, keepdims=True)
    el = lt[SUBLANES:2 * SUBLANES, :]
    for g in range(1, N_GROUPS):
        el = jnp.where(gsel == float(g), lt[(g + 1) * SUBLANES:(g + 2) * SUBLANES, :], el)
    v1 = jnp.max(el, axis=0, keepdims=True)
    i1 = _first_index_of(el == v1, row8, float(EPG))
    el2 = jnp.where(row8 == i1, -jnp.inf, el)
    v2 = jnp.max(el2, axis=0, keepdims=True)
    i2 = _first_index_of(el2 == v2, row8, float(EPG))
    t = jnp.exp(v2 - v1)
    p1 = gw / (1.0 + t)
    p2 = gw * t / (1.0 + t)
    e1 = gsel * float(EPG) + i1
    e2 = gsel * float(EPG) + i2

    rowe = lax.broadcasted_iota(jnp.int32, (N_EXPERTS, tm), 0).astype(F32)
    oh1 = (rowe == e1).astype(F32)
    oh2 = (rowe == e2).astype(F32)
    r_i = lax.broadcasted_iota(jnp.int32, (tm, tm), 0)
    c_i = lax.broadcasted_iota(jnp.int32, (tm, tm), 1)
    upper = jnp.where(r_i < c_i, 1.0, 0.0).astype(BF16)
    cum1 = _dot(oh1.astype(BF16), upper)
    cum2 = _dot(oh2.astype(BF16), upper)
    tot1 = jnp.sum(oh1, axis=1, keepdims=True)
    tot2 = jnp.sum(oh2, axis=1, keepdims=True)
    carry = carry_ref[:, 0:1]
    rank1 = jnp.sum(oh1 * (carry + cum1), axis=0, keepdims=True)
    rank2 = jnp.sum(oh2 * (carry + tot1 + cum2), axis=0, keepdims=True)
    new_carry = carry_ref[...] + (tot1 + tot2)
    carry_ref[...] = new_carry
    cnt_ref[...] = new_carry.astype(jnp.int32)

    ri_ref[...] = jnp.zeros(ri_ref.shape, jnp.int32)
    ri_ref[0:1, :] = e1.astype(jnp.int32)
    ri_ref[1:2, :] = e2.astype(jnp.int32)
    ri_ref[2:3, :] = rank1.astype(jnp.int32)
    ri_ref[3:4, :] = rank2.astype(jnp.int32)

    row128 = lax.broadcasted_iota(jnp.int32, (LANES, tm), 0)
    pw = jnp.where(row128 == 0, p1, jnp.where(row128 == 1, p2, 0.0))
    rf_ref[...] = pw.T


def _mix_call(x2, o_n, u, wg, bg, woa, woc, wmo, g1, b1, wr, br):
    T, D = x2.shape
    tm = MIX_TM
    const = lambda i: (0, 0)
    row = lambda i: (i, 0)
    return pl.pallas_call(
        _mix_kernel,
        grid=(T // tm,),
        in_specs=[
            pl.BlockSpec((tm, D), row),
            pl.BlockSpec((tm, QK_W), row),
            pl.BlockSpec((tm, CONV_W), row),
            pl.BlockSpec(wg.shape, const),
            pl.BlockSpec(bg.shape, const),
            pl.BlockSpec(woa.shape, const),
            pl.BlockSpec(woc.shape, const),
            pl.BlockSpec(wmo.shape, const),
            pl.BlockSpec(g1.shape, const),
            pl.BlockSpec(b1.shape, const),
            pl.BlockSpec(wr.shape, const),
            pl.BlockSpec(br.shape, const),
        ],
        out_specs=[
            pl.BlockSpec((tm, D), row),
            pl.BlockSpec((tm, D // 2), row),
            pl.BlockSpec((SUBLANES, tm), lambda i: (0, i)),
            pl.BlockSpec((tm, LANES), row),
            pl.BlockSpec((N_EXPERTS, LANES), const),
        ],
        out_shape=[
            jax.ShapeDtypeStruct((T, D), F32),
            jax.ShapeDtypeStruct((T, D // 2), jnp.uint32),
            jax.ShapeDtypeStruct((SUBLANES, T), jnp.int32),
            jax.ShapeDtypeStruct((T, LANES), F32),
            jax.ShapeDtypeStruct((N_EXPERTS, LANES), jnp.int32),
        ],
        scratch_shapes=[pltpu.VMEM((N_EXPERTS, LANES), F32)],
        compiler_params=pltpu.CompilerParams(
            dimension_semantics=("arbitrary",),
            vmem_limit_bytes=VMEM_LIMIT),
        name="mix_ln_router",
    )(x2, o_n, u, wg, bg, woa, woc, wmo, g1, b1, wr, br)


def _dispatch_kernel(pend_ref, dest_ref, h1p_ref, xs_ref, zero_ref, sem):
    tm = h1p_ref.shape[0]
    bm = zero_ref.shape[0]

    @pl.when(pl.program_id(0) == 0)
    def _():
        zero_ref[...] = jnp.zeros(zero_ref.shape, zero_ref.dtype)
        n_rows = xs_ref.shape[0]

        def zero_copy(e):
            if e < N_EXPERTS:
                prev_end = pend_ref[e - 1] if e > 0 else 0
                start = pl.multiple_of(jnp.maximum(pend_ref[e] - bm, 0), bm)
                needed = pend_ref[e] > prev_end
            else:
                start = n_rows - (e - N_EXPERTS + 1) * bm
                needed = start >= pend_ref[N_EXPERTS - 1]
            return pltpu.make_async_copy(zero_ref, xs_ref.at[pl.ds(start, bm), :], sem), needed

        for e in range(2 * N_EXPERTS):
            copy, needed = zero_copy(e)
            pl.when(needed)(copy.start)
        for e in range(2 * N_EXPERTS):
            copy, needed = zero_copy(e)
            pl.when(needed)(copy.wait)

    def row_copy(r, k):
        return pltpu.make_async_copy(
            h1p_ref.at[pl.ds(r, 1), :],
            xs_ref.at[pl.ds(dest_ref[k, r], 1), :],
            sem)

    def issue(r, carry):
        row_copy(r, 0).start()
        row_copy(r, 1).start()
        return carry

    lax.fori_loop(0, tm, issue, 0, unroll=8)
    for _ in range(2):
        pltpu.make_async_copy(h1p_ref, xs_ref.at[pl.ds(0, tm), :], sem).wait()


def _dispatch_call(pend, dest, h1p, n_rows):
    T, W = h1p.shape
    tm = DISP_TM
    return pl.pallas_call(
        _dispatch_kernel,
        grid_spec=pltpu.PrefetchScalarGridSpec(
            num_scalar_prefetch=1,
            grid=(T // tm,),
            in_specs=[
                pl.BlockSpec((2, tm), lambda i, pend: (0, i), memory_space=pltpu.SMEM),
                pl.BlockSpec((tm, W), lambda i, pend: (i, 0)),
            ],
            out_specs=pl.BlockSpec(memory_space=pl.ANY),
            scratch_shapes=[pltpu.VMEM((MOE_BM, W), h1p.dtype), pltpu.SemaphoreType.DMA],
        ),
        out_shape=jax.ShapeDtypeStruct((n_rows, W), h1p.dtype),
        compiler_params=pltpu.CompilerParams(
            dimension_semantics=("arbitrary",), has_side_effects=True),
        name="moe_dispatch",
    )(pend, dest, h1p)


def _expert_kernel(be_ref, nu_ref, xs_ref, wg_ref, wu_ref, wd_ref, ys_ref, wgb_ref, wub_ref, wdb_ref):
    i = pl.program_id(0)
    used = i < nu_ref[0]

    @pl.when(used & ((i == 0) | (be_ref[i] != be_ref[jnp.maximum(i - 1, 0)])))
    def _():
        wgb_ref[...] = wg_ref[...].astype(BF16)
        wub_ref[...] = wu_ref[...].astype(BF16)
        wdb_ref[...] = wd_ref[...].astype(BF16)

    @pl.when(used)
    def _():
        xb = _unpack_bf16_pairs(xs_ref[...])
        g = _dot(xb, wgb_ref[...])
        u = _dot(xb, wub_ref[...])
        h = (g * jax.nn.sigmoid(g) * u).astype(BF16)
        ys_ref[...] = _dot(h, wdb_ref[...])

    @pl.when(i >= nu_ref[0])
    def _():
        ys_ref[...] = jnp.zeros(ys_ref.shape, F32)


def _expert_call(block_expert, n_used, xs, wg, wu, wd):
    P = xs.shape[0]
    bm = MOE_BM
    nb = P // bm
    xs_map = lambda i, be, nu: (jnp.minimum(i, nu[0] - 1), 0)
    w_map = lambda i, be, nu: (be[i], 0, 0)
    return pl.pallas_call(
        _expert_kernel,
        grid_spec=pltpu.PrefetchScalarGridSpec(
            num_scalar_prefetch=2,
            grid=(nb,),
            in_specs=[
                pl.BlockSpec((bm, D_MODEL // 2), xs_map),
                pl.BlockSpec((None, D_MODEL, EXPERT_FF), w_map),
                pl.BlockSpec((None, D_MODEL, EXPERT_FF), w_map),
                pl.BlockSpec((None, EXPERT_FF, D_MODEL), w_map),
            ],
            out_specs=pl.BlockSpec((bm, D_MODEL), lambda i, be, nu: (i, 0)),
            scratch_shapes=[
                pltpu.VMEM((D_MODEL, EXPERT_FF), BF16),
                pltpu.VMEM((D_MODEL, EXPERT_FF), BF16),
                pltpu.VMEM((EXPERT_FF, D_MODEL), BF16),
            ],
        ),
        out_shape=jax.ShapeDtypeStruct((P, D_MODEL), F32),
        compiler_params=pltpu.CompilerParams(
            dimension_semantics=("arbitrary",),
            vmem_limit_bytes=VMEM_LIMIT),
        name="moe_experts",
    )(block_expert, n_used, xs, wg, wu, wd)


def _combine_kernel(dest_ref, dest_next_ref, ys_ref, h1_ref, rf_ref, g2_ref, b2_ref, out_ref, buf, sem):
    tm = h1_ref.shape[0]
    i = pl.program_id(0)
    n = pl.num_programs(0)
    slot = i % 2

    def issue_tile(d_ref, s):
        def issue(r, carry):
            for k in range(2):
                pltpu.make_async_copy(ys_ref.at[pl.ds(d_ref[k, r], 1), :],
                                      buf.at[s, k, pl.ds(r, 1), :], sem.at[s]).start()
            return carry
        lax.fori_loop(0, tm, issue, 0, unroll=8)

    @pl.when(i == 0)
    def _():
        issue_tile(dest_ref, 0)

    @pl.when(i + 1 < n)
    def _():
        issue_tile(dest_next_ref, 1 - slot)

    for k in range(2):
        pltpu.make_async_copy(ys_ref.at[pl.ds(0, tm), :], buf.at[slot, k], sem.at[slot]).wait()

    rf = rf_ref[...]
    ffn = rf[:, 0:1] * buf[slot, 0] + rf[:, 1:2] * buf[slot, 1]
    z = DEEPNORM_ALPHA * h1_ref[...] + ffn
    mu = jnp.mean(z, axis=-1, keepdims=True)
    zc = z - mu
    var = jnp.mean(zc * zc, axis=-1, keepdims=True)
    out_ref[...] = zc * lax.rsqrt(var + LN_EPS) * g2_ref[...] + b2_ref[...]


def _combine_call(dest, ys, h1, rf, g2, b2):
    T, D = h1.shape
    tm = COMB_TM
    const = lambda i: (0, 0)
    row = lambda i: (i, 0)
    return pl.pallas_call(
        _combine_kernel,
        grid=(T // tm,),
        in_specs=[
            pl.BlockSpec((2, tm), lambda i: (0, i), memory_space=pltpu.SMEM),
            pl.BlockSpec((2, tm), lambda i: (0, jnp.minimum(i + 1, T // tm - 1)), memory_space=pltpu.SMEM),
            pl.BlockSpec(memory_space=pl.ANY),
            pl.BlockSpec((tm, D), row),
            pl.BlockSpec((tm, LANES), row),
            pl.BlockSpec(g2.shape, const),
            pl.BlockSpec(b2.shape, const),
        ],
        out_specs=pl.BlockSpec((tm, D), row),
        out_shape=jax.ShapeDtypeStruct((T, D), F32),
        scratch_shapes=[
            pltpu.VMEM((2, 2, tm, D), F32),
            pltpu.SemaphoreType.DMA((2,)),
        ],
        compiler_params=pltpu.CompilerParams(
            dimension_semantics=("arbitrary",),
            vmem_limit_bytes=VMEM_LIMIT),
        name="moe_combine_ln",
    )(dest, dest, ys, h1, rf, g2, b2)


def kernel(x, positions, w_in, b_gate, lambda_q1, lambda_k1, lambda_q2, lambda_k2, subln_g, w_o_att, conv_w, w_o_conv, w_mix_out, ln1_g, ln1_b, w_router_group, b_router_group, w_router_expert, b_router_expert, w_exp_gate, w_exp_up, w_exp_down, ln2_g, ln2_b):
    B, S, D = x.shape
    T = B * S
    l = 0
    lambda_init = 0.8 - 0.6 * math.exp(-0.3 * l)

    inv_freq = ROPE_THETA ** (-jnp.arange(0, HEAD_DIM, 2, dtype=F32) / HEAD_DIM)
    ang = positions.astype(F32)[..., None] * inv_freq
    cos_t = jnp.tile(jnp.cos(ang), (1, 1, 4))
    sin_h = jnp.sin(ang)
    sin_t = jnp.tile(jnp.concatenate([-sin_h, sin_h], -1), (1, 1, 2))

    n_main = 2 * QK_W + QK_W + 3 * CONV_W
    w0 = w_in[l]
    w_main = w0[:, :n_main].astype(BF16)
    wv_t = w0[:, 2 * QK_W:3 * QK_W].T.astype(BF16)
    w_gates = w0[:, n_main:].astype(BF16)

    q, k, vt, u = _proj_call(x, w_main, wv_t, cos_t, sin_t, conv_w[l])

    lam = (jnp.exp(jnp.sum(lambda_q1[l].astype(F32) * lambda_k1[l].astype(F32)))
           - jnp.exp(jnp.sum(lambda_q2[l].astype(F32) * lambda_k2[l].astype(F32)))
           + lambda_init).reshape(1)
    o_n = _attn_call(lam, q, k, vt, subln_g[l].astype(F32).reshape(HEAD_W, 1), 1.0 - lambda_init)

    wr = jnp.zeros((D, LANES), F32)
    wr = wr.at[:, 0:N_GROUPS].set(w_router_group[l]).at[:, SUBLANES:SUBLANES + N_EXPERTS].set(w_router_expert[l])
    br = jnp.zeros((1, LANES), F32)
    br = br.at[0, 0:N_GROUPS].set(b_router_group[l]).at[0, SUBLANES:SUBLANES + N_EXPERTS].set(b_router_expert[l])

    h1, h1p, ri, rf, cnt = _mix_call(
        x.reshape(T, D), o_n.reshape(T, QK_W), u.reshape(T, CONV_W),
        w_gates, b_gate[l].reshape(1, -1).astype(F32),
        w_o_att[l].astype(BF16), w_o_conv[l].astype(BF16), w_mix_out[l].astype(BF16),
        ln1_g[l].reshape(1, D).astype(F32), ln1_b[l].reshape(1, D).astype(F32),
        wr.astype(BF16), br)

    bm = MOE_BM
    counts = cnt[:, 0]
    padded = (counts + bm - 1) // bm * bm
    pend = jnp.cumsum(padded)
    pstart = pend - padded
    onehot = (ri[0:2, :, None] == jnp.arange(N_EXPERTS, dtype=jnp.int32)).astype(jnp.int32)
    dest = jnp.sum(onehot * pstart, axis=-1) + ri[2:4]
    A = 2 * T
    P = A + N_EXPERTS * bm
    nb = P // bm
    blk_start = jnp.arange(nb, dtype=jnp.int32) * bm
    block_expert = jnp.minimum(
        jnp.sum((pend[None, :] <= blk_start[:, None]).astype(jnp.int32), axis=1), N_EXPERTS - 1)
    n_used = (pend[-1] // bm).astype(jnp.int32).reshape(1)

    xs = _dispatch_call(pend.astype(jnp.int32), dest, h1p, P)
    ys = _expert_call(block_expert, n_used, xs, w_exp_gate[l], w_exp_up[l], w_exp_down[l])
    out = _combine_call(dest, ys, h1, rf,
                        ln2_g[l].reshape(1, D).astype(F32), ln2_b[l].reshape(1, D).astype(F32))
    return out.reshape(B, S, D)
```

```python
import functools
import math

import jax
import jax.numpy as jnp
from jax import lax
from jax.experimental import pallas as pl
from jax.experimental.pallas import tpu as pltpu

D_MODEL = 1024
ATT_HEADS = 4
HEAD_DIM = 64
HEAD_W = 2 * HEAD_DIM
QK_W = ATT_HEADS * HEAD_W
CONV_W = D_MODEL // 2
CONV_K = 3
N_GROUPS = 4
EPG = 8
N_EXPERTS = N_GROUPS * EPG
EXPERT_FF = D_MODEL // 2
ROPE_THETA = 10000.0
SUBLN_EPS = 1e-5
LN_EPS = 1e-5
DEPTH = 1
DEEPNORM_ALPHA = (2.0 * DEPTH) ** 0.25

LANES = 128
SUBLANES = 8
VMEM_LIMIT = 48 * 1024 * 1024

PROJ_TM = 512
ATT_TQ = 256
ATT_TK = 256
ATT_UNROLL = 4
MIX_TM = 512
MOE_BM = 256
DISP_TM = 1024
COMB_TM = 256
NEG_BIG = -1e30

F32 = jnp.float32
BF16 = jnp.bfloat16


def _dot(a, b):
    return jnp.dot(a, b, preferred_element_type=F32)


def _dot_nt(a, b):
    return lax.dot_general(a, b, (((1,), (1,)), ((), ())), preferred_element_type=F32)


def _proj_kernel(x_ref, w_ref, wvt_ref, cos_ref, sin_ref, cw_ref,
                 q_ref, k_ref, vt_ref, u_ref, pbuf):
    tm = x_ref.shape[0]
    i = pl.program_id(1)
    xb = x_ref[...].astype(BF16)
    cos = cos_ref[...]
    sin = sin_ref[...]
    lane = lax.broadcasted_iota(jnp.int32, (tm, LANES), 1)
    low_half = (lane % HEAD_DIM) < (HEAD_DIM // 2)

    def rope(t):
        rot = jnp.where(low_half, pltpu.roll(t, LANES - HEAD_DIM // 2, 1),
                        pltpu.roll(t, HEAD_DIM // 2, 1))
        return t * cos + rot * sin

    q = _dot(xb, w_ref[:, 0:QK_W])
    for h in range(ATT_HEADS):
        sl = slice(h * HEAD_W, (h + 1) * HEAD_W)
        q_ref[:, sl] = (rope(q[:, sl]) * (HEAD_DIM ** -0.5 * math.log2(math.e))).astype(BF16)
    k = _dot(xb, w_ref[:, QK_W:2 * QK_W])
    for h in range(ATT_HEADS):
        sl = slice(h * HEAD_W, (h + 1) * HEAD_W)
        k_ref[:, sl] = rope(k[:, sl]).astype(BF16)
    vt_ref[...] = _dot_nt(wvt_ref[...], xb).astype(BF16)

    c0 = 3 * QK_W
    cb = _dot(xb, w_ref[:, c0:c0 + CONV_W])
    cc = _dot(xb, w_ref[:, c0 + CONV_W:c0 + 2 * CONV_W])
    cx = _dot(xb, w_ref[:, c0 + 2 * CONV_W:c0 + 3 * CONV_W])
    p = cc * cx

    @pl.when(i == 0)
    def _():
        pbuf[0:SUBLANES, :] = jnp.zeros((SUBLANES, CONV_W), F32)

    pbuf[SUBLANES:SUBLANES + tm, :] = p
    y = (cw_ref[0:1, :] * pbuf[SUBLANES - 2:SUBLANES - 2 + tm, :]
         + cw_ref[1:2, :] * pbuf[SUBLANES - 1:SUBLANES - 1 + tm, :]
         + cw_ref[2:3, :] * p)
    u_ref[...] = (cb * y).astype(BF16)
    pbuf[0:SUBLANES, :] = pbuf[tm:tm + SUBLANES, :]


def _proj_call(x, w_main, wv_t, cos_t, sin_t, conv_w):
    B, S, D = x.shape
    tm = PROJ_TM
    n_main = w_main.shape[1]
    return pl.pallas_call(
        _proj_kernel,
        grid=(B, S // tm),
        in_specs=[
            pl.BlockSpec((None, tm, D), lambda b, i: (b, i, 0)),
            pl.BlockSpec((D, n_main), lambda b, i: (0, 0)),
            pl.BlockSpec((QK_W, D), lambda b, i: (0, 0)),
            pl.BlockSpec((None, tm, LANES), lambda b, i: (b, i, 0)),
            pl.BlockSpec((None, tm, LANES), lambda b, i: (b, i, 0)),
            pl.BlockSpec((CONV_K, CONV_W), lambda b, i: (0, 0)),
        ],
        out_specs=[
            pl.BlockSpec((None, tm, QK_W), lambda b, i: (b, i, 0)),
            pl.BlockSpec((None, tm, QK_W), lambda b, i: (b, i, 0)),
            pl.BlockSpec((None, QK_W, tm), lambda b, i: (b, 0, i)),
            pl.BlockSpec((None, tm, CONV_W), lambda b, i: (b, i, 0)),
        ],
        out_shape=[
            jax.ShapeDtypeStruct((B, S, QK_W), BF16),
            jax.ShapeDtypeStruct((B, S, QK_W), BF16),
            jax.ShapeDtypeStruct((B, QK_W, S), BF16),
            jax.ShapeDtypeStruct((B, S, CONV_W), BF16),
        ],
        scratch_shapes=[pltpu.VMEM((tm + 2 * SUBLANES, CONV_W), F32)],
        compiler_params=pltpu.CompilerParams(
            dimension_semantics=("arbitrary", "arbitrary"),
            vmem_limit_bytes=VMEM_LIMIT),
        name="proj_rope_conv",
    )(x, w_main, wv_t, cos_t, sin_t, conv_w)


ACC_ROWS = HEAD_W + 16


def _attn_kernel(lam_ref, q_ref, k_ref, vt_ref, g_ref, o_ref, qzt_ref, *scratch, out_scale):
    nh = ATT_HEADS
    s_refs, p_refs = scratch[0:nh], scratch[nh:2 * nh]
    m_refs, a_refs, acc_refs = scratch[2 * nh:3 * nh], scratch[3 * nh:4 * nh], scratch[4 * nh:5 * nh]
    tq = q_ref.shape[0]
    tk = ATT_TK
    nblk = 2 * tq // LANES
    i = pl.program_id(1)
    lane = lax.broadcasted_iota(jnp.int32, (tq, HEAD_W), 1)
    ones_rows = jnp.ones((ACC_ROWS - HEAD_W, tk), BF16)

    for h in range(nh):
        qf = q_ref[:, h * HEAD_W:(h + 1) * HEAD_W].astype(F32)
        q1 = jnp.where(lane < HEAD_DIM, qf, 0.0)
        q2 = jnp.where(lane >= HEAD_DIM, qf, 0.0)
        qzt_ref[h, :, 0:tq] = q1.T.astype(BF16)
        qzt_ref[h, :, tq:2 * tq] = q2.T.astype(BF16)
        m_refs[h][...] = jnp.full(m_refs[h].shape, NEG_BIG, F32)
        acc_refs[h][...] = jnp.zeros(acc_refs[h].shape, F32)

    def scores(h, k0):
        kj = k_ref[pl.ds(k0, tk), h * HEAD_W:(h + 1) * HEAD_W]
        s_refs[h][:, 0:2 * tq] = _dot(kj, qzt_ref[h])

    def softmax(h, masked):
        for c in range(nblk):
            cs = slice(c * LANES, (c + 1) * LANES)
            s = s_refs[h][:, cs]
            if masked:
                key = lax.broadcasted_iota(jnp.int32, (tk, LANES), 0)
                qry = lax.broadcasted_iota(jnp.int32, (tk, LANES), 1) + (c * LANES) % tq
                s = jnp.where(key <= qry, s, NEG_BIG)
            m_old = m_refs[h][:, cs]
            m_new = jnp.maximum(m_old, jnp.max(s, axis=0, keepdims=True))
            a_refs[h][:, cs] = jnp.exp2(m_old - m_new)
            m_refs[h][:, cs] = m_new
            p_refs[h][:, cs] = jnp.exp2(s - m_new).astype(BF16)

    def accumulate(h, k0):
        vj = jnp.concatenate([vt_ref[h * HEAD_W:(h + 1) * HEAD_W, pl.ds(k0, tk)], ones_rows], axis=0)
        acc_refs[h][...] = a_refs[h][...] * acc_refs[h][...] + _dot(vj, p_refs[h][:, 0:2 * tq])

    def tile_start(j):
        return pl.multiple_of(j * tk, tk)

    for h in range(nh):
        scores(h, 0)

    @pl.when(i > 0)
    def _():
        for h in range(nh):
            softmax(h, False)
            scores(h, tile_start(1))

    def step(j):
        for h in range(nh):
            accumulate(h, tile_start(j - 1))
            softmax(h, False)
            scores(h, tile_start(j + 1))

    n_steps = jnp.maximum(i - 1, 0)
    n_trips = n_steps // ATT_UNROLL

    def body(t, carry):
        for u in range(ATT_UNROLL):
            step(ATT_UNROLL * t + 1 + u)
        return carry

    lax.fori_loop(0, n_trips, body, 0)

    def tail(j, carry):
        step(j)
        return carry

    lax.fori_loop(ATT_UNROLL * n_trips + 1, n_steps + 1, tail, 0)

    @pl.when(i > 0)
    def _():
        for h in range(nh):
            accumulate(h, tile_start(i - 1))

    for h in range(nh):
        softmax(h, True)
        accumulate(h, tile_start(i))

    lam = lam_ref[0]
    for h in range(nh):
        acc = acc_refs[h][0:HEAD_W, :]
        l = acc_refs[h][HEAD_W:HEAD_W + 1, :]
        o = acc[:, 0:tq] / l[:, 0:tq] - lam * (acc[:, tq:2 * tq] / l[:, tq:2 * tq])
        ms = jnp.mean(o * o, axis=0, keepdims=True)
        o = o * lax.rsqrt(ms + SUBLN_EPS) * g_ref[...] * out_scale
        o_ref[:, h * HEAD_W:(h + 1) * HEAD_W] = o.T.astype(BF16)


def _attn_call(lam, q, k, vt, g_col, out_scale):
    B, S, _ = q.shape
    tq = ATT_TQ
    assert ATT_TQ == ATT_TK
    kernel = functools.partial(_attn_kernel, out_scale=out_scale)
    return pl.pallas_call(
        kernel,
        grid=(B, S // tq),
        in_specs=[
            pl.BlockSpec(memory_space=pltpu.SMEM),
            pl.BlockSpec((None, tq, QK_W), lambda b, i: (b, i, 0)),
            pl.BlockSpec((None, S, QK_W), lambda b, i: (b, 0, 0), pipeline_mode=pl.Buffered(1)),
            pl.BlockSpec((None, QK_W, S), lambda b, i: (b, 0, 0), pipeline_mode=pl.Buffered(1)),
            pl.BlockSpec((HEAD_W, 1), lambda b, i: (0, 0)),
        ],
        out_specs=pl.BlockSpec((None, tq, QK_W), lambda b, i: (b, i, 0)),
        out_shape=jax.ShapeDtypeStruct((B, S, QK_W), BF16),
        scratch_shapes=(
            [pltpu.VMEM((ATT_HEADS, HEAD_W, 2 * tq), BF16)]
            + [pltpu.VMEM((ATT_TK, 2 * tq + LANES), F32) for _ in range(ATT_HEADS)]
            + [pltpu.VMEM((ATT_TK, 2 * tq + LANES), BF16) for _ in range(ATT_HEADS)]
            + [pltpu.VMEM((1, 2 * tq), F32) for _ in range(ATT_HEADS)]
            + [pltpu.VMEM((1, 2 * tq), F32) for _ in range(ATT_HEADS)]
            + [pltpu.VMEM((ACC_ROWS, 2 * tq), F32) for _ in range(ATT_HEADS)]
        ),
        compiler_params=pltpu.CompilerParams(
            dimension_semantics=("arbitrary", "arbitrary"),
            vmem_limit_bytes=VMEM_LIMIT),
        name="diff_flash_attn",
    )(lam, q, k, vt, g_col)


def _pack_bf16_pairs(h):
    half = h.shape[1] // 2
    hb = h.astype(BF16).astype(F32)
    bits = lax.bitcast_convert_type(hb, jnp.uint32)
    return (bits[:, 0:half] >> 16) | (bits[:, half:] & jnp.uint32(0xFFFF0000))


def _unpack_bf16_pairs(w):
    lo = lax.bitcast_convert_type(w << 16, F32).astype(BF16)
    hi = lax.bitcast_convert_type(w & jnp.uint32(0xFFFF0000), F32).astype(BF16)
    return jnp.concatenate([lo, hi], axis=1)


def _first_index_of(mask, row_f, big):
    return jnp.min(jnp.where(mask, row_f, big), axis=0, keepdims=True)


def _mix_kernel(x_ref, o_ref, u_ref, wg_ref, bg_ref, woa_ref, woc_ref, wmo_ref, g1_ref, b1_ref,
                wr_ref, br_ref,
                h1_ref, h1p_ref, ri_ref, rf_ref, cnt_ref, carry_ref):
    tm = x_ref.shape[0]
    step = pl.program_id(0)

    @pl.when(step == 0)
    def _():
        carry_ref[...] = jnp.zeros(carry_ref.shape, F32)

    x = x_ref[...]
    xb = x.astype(BF16)
    gates = jax.nn.sigmoid(_dot(xb, wg_ref[...]) + bg_ref[...])
    ya = _dot(o_ref[...], woa_ref[...])
    yc = _dot(u_ref[...], woc_ref[...])
    merged = (gates[:, 0:D_MODEL] * ya + gates[:, D_MODEL:] * yc).astype(BF16)
    z = DEEPNORM_ALPHA * x + _dot(merged, wmo_ref[...])
    mu = jnp.mean(z, axis=-1, keepdims=True)
    zc = z - mu
    var = jnp.mean(zc * zc, axis=-1, keepdims=True)
    h1 = zc * lax.rsqrt(var + LN_EPS) * g1_ref[...] + b1_ref[...]
    h1_ref[...] = h1
    h1p_ref[...] = _pack_bf16_pairs(h1)

    logits = _dot(h1.astype(BF16), wr_ref[...]) + br_ref[...]
    lt = logits.T
    row8 = lax.broadcasted_iota(jnp.int32, (SUBLANES, tm), 0).astype(F32)
    gl = jnp.where(row8 < N_GROUPS, lt[0:SUBLANES, :], NEG_BIG)
    gmax = jnp.max(gl, axis=0, keepdims=True)
    gsel = _first_index_of(gl == gmax, row8, float(SUBLANES))
    gw = 1.0 / jnp.sum(jnp.exp(gl - gmax), axis=0, keepdims=True)
    el = lt[SUBLANES:2 * SUBLANES, :]
    for g in range(1, N_GROUPS):
        el = jnp.where(gsel == float(g), lt[(g + 1) * SUBLANES:(g + 2) * SUBLANES, :], el)
    v1 = jnp.max(el, axis=0, keepdims=True)
    i1 = _first_index_of(el == v1, row8, float(EPG))
    el2 = jnp.where(row8 == i1, -jnp.inf, el)
    v2 = jnp.max(el2, axis=0, keepdims=True)
    i2 = _first_index_of(el2 == v2, row8, float(EPG))
    t = jnp.exp(v2 - v1)
    p1 = gw / (1.0 + t)
    p2 = gw * t / (1.0 + t)
    e1 = gsel * float(EPG) + i1
    e2 = gsel * float(EPG) + i2

    rowe = lax.broadcasted_iota(jnp.int32, (N_EXPERTS, tm), 0).astype(F32)
    oh1 = (rowe == e1).astype(F32)
    oh2 = (rowe == e2).astype(F32)
    r_i = lax.broadcasted_iota(jnp.int32, (tm, tm), 0)
    c_i = lax.broadcasted_iota(jnp.int32, (tm, tm), 1)
    upper = jnp.where(r_i < c_i, 1.0, 0.0).astype(BF16)
    cum1 = _dot(oh1.astype(BF16), upper)
    cum2 = _dot(oh2.astype(BF16), upper)
    tot1 = jnp.sum(oh1, axis=1, keepdims=True)
    tot2 = jnp.sum(oh2, axis=1, keepdims=True)
    carry = carry_ref[:, 0:1]
    rank1 = jnp.sum(oh1 * (carry + cum1), axis=0, keepdims=True)
    rank2 = jnp.sum(oh2 * (carry + tot1 + cum2), axis=0, keepdims=True)
    new_carry = carry_ref[...] + (tot1 + tot2)
    carry_ref[...] = new_carry
    cnt_ref[...] = new_carry.astype(jnp.int32)

    ri_ref[...] = jnp.zeros(ri_ref.shape, jnp.int32)
    ri_ref[0:1, :] = e1.astype(jnp.int32)
    ri_ref[1:2, :] = e2.astype(jnp.int32)
    ri_ref[2:3, :] = rank1.astype(jnp.int32)
    ri_ref[3:4, :] = rank2.astype(jnp.int32)

    row128 = lax.broadcasted_iota(jnp.int32, (LANES, tm), 0)
    pw = jnp.where(row128 == 0, p1, jnp.where(row128 == 1, p2, 0.0))
    rf_ref[...] = pw.T


def _mix_call(x2, o_n, u, wg, bg, woa, woc, wmo, g1, b1, wr, br):
    T, D = x2.shape
    tm = MIX_TM
    const = lambda i: (0, 0)
    row = lambda i: (i, 0)
    return pl.pallas_call(
        _mix_kernel,
        grid=(T // tm,),
        in_specs=[
            pl.BlockSpec((tm, D), row),
            pl.BlockSpec((tm, QK_W), row),
            pl.BlockSpec((tm, CONV_W), row),
            pl.BlockSpec(wg.shape, const),
            pl.BlockSpec(bg.shape, const),
            pl.BlockSpec(woa.shape, const),
            pl.BlockSpec(woc.shape, const),
            pl.BlockSpec(wmo.shape, const),
            pl.BlockSpec(g1.shape, const),
            pl.BlockSpec(b1.shape, const),
            pl.BlockSpec(wr.shape, const),
            pl.BlockSpec(br.shape, const),
        ],
        out_specs=[
            pl.BlockSpec((tm, D), row),
            pl.BlockSpec((tm, D // 2), row),
            pl.BlockSpec((SUBLANES, tm), lambda i: (0, i)),
            pl.BlockSpec((tm, LANES), row),
            pl.BlockSpec((N_EXPERTS, LANES), const),
        ],
        out_shape=[
            jax.ShapeDtypeStruct((T, D), F32),
            jax.ShapeDtypeStruct((T, D // 2), jnp.uint32),
            jax.ShapeDtypeStruct((SUBLANES, T), jnp.int32),
            jax.ShapeDtypeStruct((T, LANES), F32),
            jax.ShapeDtypeStruct((N_EXPERTS, LANES), jnp.int32),
        ],
        scratch_shapes=[pltpu.VMEM((N_EXPERTS, LANES), F32)],
        compiler_params=pltpu.CompilerParams(
            dimension_semantics=("arbitrary",),
            vmem_limit_bytes=VMEM_LIMIT),
        name="mix_ln_router",
    )(x2, o_n, u, wg, bg, woa, woc, wmo, g1, b1, wr, br)


def _dispatch_kernel(pend_ref, dest_ref, h1p_ref, xs_ref, zero_ref, sem):
    tm = h1p_ref.shape[0]
    bm = zero_ref.shape[0]

    @pl.when(pl.program_id(0) == 0)
    def _():
        zero_ref[...] = jnp.zeros(zero_ref.shape, zero_ref.dtype)
        n_rows = xs_ref.shape[0]

        def zero_copy(e):
            if e < N_EXPERTS:
                prev_end = pend_ref[e - 1] if e > 0 else 0
                start = pl.multiple_of(jnp.maximum(pend_ref[e] - bm, 0), bm)
                needed = pend_ref[e] > prev_end
            else:
                start = n_rows - (e - N_EXPERTS + 1) * bm
                needed = start >= pend_ref[N_EXPERTS - 1]
            return pltpu.make_async_copy(zero_ref, xs_ref.at[pl.ds(start, bm), :], sem), needed

        for e in range(2 * N_EXPERTS):
            copy, needed = zero_copy(e)
            pl.when(needed)(copy.start)
        for e in range(2 * N_EXPERTS):
            copy, needed = zero_copy(e)
            pl.when(needed)(copy.wait)

    def row_copy(r, k):
        return pltpu.make_async_copy(
            h1p_ref.at[pl.ds(r, 1), :],
            xs_ref.at[pl.ds(dest_ref[k, r], 1), :],
            sem)

    def issue(r, carry):
        row_copy(r, 0).start(priority=0)
        row_copy(r, 1).start(priority=1)
        return carry

    lax.fori_loop(0, tm, issue, 0, unroll=8)
    for _ in range(2):
        pltpu.make_async_copy(h1p_ref, xs_ref.at[pl.ds(0, tm), :], sem).wait()


def _dispatch_call(pend, dest, h1p, n_rows):
    T, W = h1p.shape
    tm = DISP_TM
    return pl.pallas_call(
        _dispatch_kernel,
        grid_spec=pltpu.PrefetchScalarGridSpec(
            num_scalar_prefetch=1,
            grid=(T // tm,),
            in_specs=[
                pl.BlockSpec((2, tm), lambda i, pend: (0, i), memory_space=pltpu.SMEM),
                pl.BlockSpec((tm, W), lambda i, pend: (i, 0)),
            ],
            out_specs=pl.BlockSpec(memory_space=pl.ANY),
            scratch_shapes=[pltpu.VMEM((MOE_BM, W), h1p.dtype), pltpu.SemaphoreType.DMA],
        ),
        out_shape=jax.ShapeDtypeStruct((n_rows, W), h1p.dtype),
        compiler_params=pltpu.CompilerParams(
            dimension_semantics=("arbitrary",), has_side_effects=True),
        name="moe_dispatch",
    )(pend, dest, h1p)


def _expert_kernel(be_ref, nu_ref, xs_ref, wg_ref, wu_ref, wd_ref, ys_ref, wgb_ref, wub_ref, wdb_ref):
    i = pl.program_id(0)
    used = i < nu_ref[0]

    @pl.when(used & ((i == 0) | (be_ref[i] != be_ref[jnp.maximum(i - 1, 0)])))
    def _():
        wgb_ref[...] = wg_ref[...].astype(BF16)
        wub_ref[...] = wu_ref[...].astype(BF16)
        wdb_ref[...] = wd_ref[...].astype(BF16)

    @pl.when(used)
    def _():
        xb = _unpack_bf16_pairs(xs_ref[...])
        g = _dot(xb, wgb_ref[...])
        u = _dot(xb, wub_ref[...])
        h = (g * jax.nn.sigmoid(g) * u).astype(BF16)
        ys_ref[...] = _dot(h, wdb_ref[...])

    @pl.when(i >= nu_ref[0])
    def _():
        ys_ref[...] = jnp.zeros(ys_ref.shape, F32)


def _expert_call(block_expert, n_used, xs, wg, wu, wd):
    P = xs.shape[0]
    bm = MOE_BM
    nb = P // bm
    xs_map = lambda i, be, nu: (jnp.minimum(i, nu[0] - 1), 0)
    w_map = lambda i, be, nu: (be[i], 0, 0)
    return pl.pallas_call(
        _expert_kernel,
        grid_spec=pltpu.PrefetchScalarGridSpec(
            num_scalar_prefetch=2,
            grid=(nb,),
            in_specs=[
                pl.BlockSpec((bm, D_MODEL // 2), xs_map),
                pl.BlockSpec((None, D_MODEL, EXPERT_FF), w_map),
                pl.BlockSpec((None, D_MODEL, EXPERT_FF), w_map),
                pl.BlockSpec((None, EXPERT_FF, D_MODEL), w_map),
            ],
            out_specs=pl.BlockSpec((bm, D_MODEL), lambda i, be, nu: (i, 0)),
            scratch_shapes=[
                pltpu.VMEM((D_MODEL, EXPERT_FF), BF16),
                pltpu.VMEM((D_MODEL, EXPERT_FF), BF16),
                pltpu.VMEM((EXPERT_FF, D_MODEL), BF16),
            ],
        ),
        out_shape=jax.ShapeDtypeStruct((P, D_MODEL), F32),
        compiler_params=pltpu.CompilerParams(
            dimension_semantics=("arbitrary",),
            vmem_limit_bytes=VMEM_LIMIT),
        name="moe_experts",
    )(block_expert, n_used, xs, wg, wu, wd)


def _combine_kernel(dest_ref, dest_next_ref, ys_ref, h1_ref, rf_ref, g2_ref, b2_ref, out_ref, buf, sem):
    tm = h1_ref.shape[0]
    i = pl.program_id(0)
    n = pl.num_programs(0)
    slot = i % 2

    def issue_tile(d_ref, s):
        def issue(r, carry):
            for k in range(2):
                pltpu.make_async_copy(ys_ref.at[pl.ds(d_ref[k, r], 1), :],
                                      buf.at[s, k, pl.ds(r, 1), :], sem.at[s]).start(priority=k)
            return carry
        lax.fori_loop(0, tm, issue, 0, unroll=8)

    @pl.when(i == 0)
    def _():
        issue_tile(dest_ref, 0)

    @pl.when(i + 1 < n)
    def _():
        issue_tile(dest_next_ref, 1 - slot)

    for k in range(2):
        pltpu.make_async_copy(ys_ref.at[pl.ds(0, tm), :], buf.at[slot, k], sem.at[slot]).wait()

    rf = rf_ref[...]
    ffn = rf[:, 0:1] * buf[slot, 0] + rf[:, 1:2] * buf[slot, 1]
    z = DEEPNORM_ALPHA * h1_ref[...] + ffn
    mu = jnp.mean(z, axis=-1, keepdims=True)
    zc = z - mu
    var = jnp.mean(zc * zc, axis=-1, keepdims=True)
    out_ref[...] = zc * lax.rsqrt(var + LN_EPS) * g2_ref[...] + b2_ref[...]


def _combine_call(dest, ys, h1, rf, g2, b2):
    T, D = h1.shape
    tm = COMB_TM
    const = lambda i: (0, 0)
    row = lambda i: (i, 0)
    return pl.pallas_call(
        _combine_kernel,
        grid=(T // tm,),
        in_specs=[
            pl.BlockSpec((2, tm), lambda i: (0, i), memory_space=pltpu.SMEM),
            pl.BlockSpec((2, tm), lambda i: (0, jnp.minimum(i + 1, T // tm - 1)), memory_space=pltpu.SMEM),
            pl.BlockSpec(memory_space=pl.ANY),
            pl.BlockSpec((tm, D), row),
            pl.BlockSpec((tm, LANES), row),
            pl.BlockSpec(g2.shape, const),
            pl.BlockSpec(b2.shape, const),
        ],
        out_specs=pl.BlockSpec((tm, D), row),
        out_shape=jax.ShapeDtypeStruct((T, D), F32),
        scratch_shapes=[
            pltpu.VMEM((2, 2, tm, D), F32),
            pltpu.SemaphoreType.DMA((2,)),
        ],
        compiler_params=pltpu.CompilerParams(
            dimension_semantics=("arbitrary",),
            vmem_limit_bytes=VMEM_LIMIT),
        name="moe_combine_ln",
    )(dest, dest, ys, h1, rf, g2, b2)


def kernel(x, positions, w_in, b_gate, lambda_q1, lambda_k1, lambda_q2, lambda_k2, subln_g, w_o_att, conv_w, w_o_conv, w_mix_out, ln1_g, ln1_b, w_router_group, b_router_group, w_router_expert, b_router_expert, w_exp_gate, w_exp_up, w_exp_down, ln2_g, ln2_b):
    B, S, D = x.shape
    T = B * S
    l = 0
    lambda_init = 0.8 - 0.6 * math.exp(-0.3 * l)

    inv_freq = ROPE_THETA ** (-jnp.arange(0, HEAD_DIM, 2, dtype=F32) / HEAD_DIM)
    ang = positions.astype(F32)[..., None] * inv_freq
    cos_t = jnp.tile(jnp.cos(ang), (1, 1, 4))
    sin_h = jnp.sin(ang)
    sin_t = jnp.tile(jnp.concatenate([-sin_h, sin_h], -1), (1, 1, 2))

    n_main = 2 * QK_W + QK_W + 3 * CONV_W
    w0 = w_in[l]
    w_main = w0[:, :n_main].astype(BF16)
    wv_t = w0[:, 2 * QK_W:3 * QK_W].T.astype(BF16)
    w_gates = w0[:, n_main:].astype(BF16)

    q, k, vt, u = _proj_call(x, w_main, wv_t, cos_t, sin_t, conv_w[l])

    lam = (jnp.exp(jnp.sum(lambda_q1[l].astype(F32) * lambda_k1[l].astype(F32)))
           - jnp.exp(jnp.sum(lambda_q2[l].astype(F32) * lambda_k2[l].astype(F32)))
           + lambda_init).reshape(1)
    o_n = _attn_call(lam, q, k, vt, subln_g[l].astype(F32).reshape(HEAD_W, 1), 1.0 - lambda_init)

    wr = jnp.zeros((D, LANES), F32)
    wr = wr.at[:, 0:N_GROUPS].set(w_router_group[l]).at[:, SUBLANES:SUBLANES + N_EXPERTS].set(w_router_expert[l])
    br = jnp.zeros((1, LANES), F32)
    br = br.at[0, 0:N_GROUPS].set(b_router_group[l]).at[0, SUBLANES:SUBLANES + N_EXPERTS].set(b_router_expert[l])

    h1, h1p, ri, rf, cnt = _mix_call(
        x.reshape(T, D), o_n.reshape(T, QK_W), u.reshape(T, CONV_W),
        w_gates, b_gate[l].reshape(1, -1).astype(F32),
        w_o_att[l].astype(BF16), w_o_conv[l].astype(BF16), w_mix_out[l].astype(BF16),
        ln1_g[l].reshape(1, D).astype(F32), ln1_b[l].reshape(1, D).astype(F32),
        wr.astype(BF16), br)

    bm = MOE_BM
    counts = cnt[:, 0]
    padded = (counts + bm - 1) // bm * bm
    pend = jnp.cumsum(padded)
    pstart = pend - padded
    onehot = (ri[0:2, :, None] == jnp.arange(N_EXPERTS, dtype=jnp.int32)).astype(jnp.int32)
    dest = jnp.sum(onehot * pstart, axis=-1) + ri[2:4]
    A = 2 * T
    P = A + N_EXPERTS * bm
    nb = P // bm
    blk_start = jnp.arange(nb, dtype=jnp.int32) * bm
    block_expert = jnp.minimum(
        jnp.sum((pend[None, :] <= blk_start[:, None]).astype(jnp.int32), axis=1), N_EXPERTS - 1)
    n_used = (pend[-1] // bm).astype(jnp.int32).reshape(1)

    xs = _dispatch_call(pend.astype(jnp.int32), dest, h1p, P)
    ys = _expert_call(block_expert, n_used, xs, w_exp_gate[l], w_exp_up[l], w_exp_down[l])
    out = _combine_call(dest, ys, h1, rf,
                        ln2_g[l].reshape(1, D).astype(F32), ln2_b[l].reshape(1, D).astype(F32))
    return out.reshape(B, S, D)
```

```python
import functools
import math

import jax
import jax.numpy as jnp
from jax import lax
from jax.experimental import pallas as pl
from jax.experimental.pallas import tpu as pltpu

D_MODEL = 1024
ATT_HEADS = 4
HEAD_DIM = 64
HEAD_W = 2 * HEAD_DIM
QK_W = ATT_HEADS * HEAD_W
CONV_W = D_MODEL // 2
CONV_K = 3
N_GROUPS = 4
EPG = 8
N_EXPERTS = N_GROUPS * EPG
EXPERT_FF = D_MODEL // 2
ROPE_THETA = 10000.0
SUBLN_EPS = 1e-5
LN_EPS = 1e-5
DEPTH = 1
DEEPNORM_ALPHA = (2.0 * DEPTH) ** 0.25

LANES = 128
SUBLANES = 8
VMEM_LIMIT = 48 * 1024 * 1024

PROJ_TM = 512
ATT_TQ = 256
ATT_TK = 256
ATT_UNROLL = 4
MIX_TM = 512
MOE_BM = 256
DISP_TM = 256
COMB_TM = 256
NEG_BIG = -1e30

F32 = jnp.float32
BF16 = jnp.bfloat16


def _dot(a, b):
    return jnp.dot(a, b, preferred_element_type=F32)


def _dot_nt(a, b):
    return lax.dot_general(a, b, (((1,), (1,)), ((), ())), preferred_element_type=F32)


def _proj_kernel(x_ref, w_ref, wvt_ref, cos_ref, sin_ref, cw_ref,
                 q_ref, k_ref, vt_ref, u_ref, pbuf):
    tm = x_ref.shape[0]
    i = pl.program_id(1)
    xb = x_ref[...].astype(BF16)
    cos = cos_ref[...]
    sin = sin_ref[...]
    lane = lax.broadcasted_iota(jnp.int32, (tm, LANES), 1)
    low_half = (lane % HEAD_DIM) < (HEAD_DIM // 2)

    def rope(t):
        rot = jnp.where(low_half, pltpu.roll(t, LANES - HEAD_DIM // 2, 1),
                        pltpu.roll(t, HEAD_DIM // 2, 1))
        return t * cos + rot * sin

    q = _dot(xb, w_ref[:, 0:QK_W])
    for h in range(ATT_HEADS):
        sl = slice(h * HEAD_W, (h + 1) * HEAD_W)
        q_ref[:, sl] = (rope(q[:, sl]) * (HEAD_DIM ** -0.5 * math.log2(math.e))).astype(BF16)
    k = _dot(xb, w_ref[:, QK_W:2 * QK_W])
    for h in range(ATT_HEADS):
        sl = slice(h * HEAD_W, (h + 1) * HEAD_W)
        k_ref[:, sl] = rope(k[:, sl]).astype(BF16)
    vt_ref[...] = _dot_nt(wvt_ref[...], xb).astype(BF16)

    c0 = 3 * QK_W
    cb = _dot(xb, w_ref[:, c0:c0 + CONV_W])
    cc = _dot(xb, w_ref[:, c0 + CONV_W:c0 + 2 * CONV_W])
    cx = _dot(xb, w_ref[:, c0 + 2 * CONV_W:c0 + 3 * CONV_W])
    p = cc * cx

    @pl.when(i == 0)
    def _():
        pbuf[0:SUBLANES, :] = jnp.zeros((SUBLANES, CONV_W), F32)

    pbuf[SUBLANES:SUBLANES + tm, :] = p
    y = (cw_ref[0:1, :] * pbuf[SUBLANES - 2:SUBLANES - 2 + tm, :]
         + cw_ref[1:2, :] * pbuf[SUBLANES - 1:SUBLANES - 1 + tm, :]
         + cw_ref[2:3, :] * p)
    u_ref[...] = (cb * y).astype(BF16)
    pbuf[0:SUBLANES, :] = pbuf[tm:tm + SUBLANES, :]


def _proj_call(x, w_main, wv_t, cos_t, sin_t, conv_w):
    B, S, D = x.shape
    tm = PROJ_TM
    n_main = w_main.shape[1]
    return pl.pallas_call(
        _proj_kernel,
        grid=(B, S // tm),
        in_specs=[
            pl.BlockSpec((None, tm, D), lambda b, i: (b, i, 0)),
            pl.BlockSpec((D, n_main), lambda b, i: (0, 0)),
            pl.BlockSpec((QK_W, D), lambda b, i: (0, 0)),
            pl.BlockSpec((None, tm, LANES), lambda b, i: (b, i, 0)),
            pl.BlockSpec((None, tm, LANES), lambda b, i: (b, i, 0)),
            pl.BlockSpec((CONV_K, CONV_W), lambda b, i: (0, 0)),
        ],
        out_specs=[
            pl.BlockSpec((None, tm, QK_W), lambda b, i: (b, i, 0)),
            pl.BlockSpec((None, tm, QK_W), lambda b, i: (b, i, 0)),
            pl.BlockSpec((None, QK_W, tm), lambda b, i: (b, 0, i)),
            pl.BlockSpec((None, tm, CONV_W), lambda b, i: (b, i, 0)),
        ],
        out_shape=[
            jax.ShapeDtypeStruct((B, S, QK_W), BF16),
            jax.ShapeDtypeStruct((B, S, QK_W), BF16),
            jax.ShapeDtypeStruct((B, QK_W, S), BF16),
            jax.ShapeDtypeStruct((B, S, CONV_W), BF16),
        ],
        scratch_shapes=[pltpu.VMEM((tm + 2 * SUBLANES, CONV_W), F32)],
        compiler_params=pltpu.CompilerParams(
            dimension_semantics=("arbitrary", "arbitrary"),
            vmem_limit_bytes=VMEM_LIMIT),
        name="proj_rope_conv",
    )(x, w_main, wv_t, cos_t, sin_t, conv_w)


ACC_ROWS = HEAD_W + 16


def _attn_kernel(lam_ref, q_ref, k_ref, vt_ref, g_ref, o_ref, qzt_ref, *scratch, out_scale):
    nh = ATT_HEADS
    s_refs, p_refs = scratch[0:nh], scratch[nh:2 * nh]
    m_refs, a_refs, acc_refs = scratch[2 * nh:3 * nh], scratch[3 * nh:4 * nh], scratch[4 * nh:5 * nh]
    tq = q_ref.shape[0]
    tk = ATT_TK
    nblk = 2 * tq // LANES
    i = pl.program_id(1)
    lane = lax.broadcasted_iota(jnp.int32, (tq, HEAD_W), 1)
    ones_rows = jnp.ones((ACC_ROWS - HEAD_W, tk), BF16)

    for h in range(nh):
        qf = q_ref[:, h * HEAD_W:(h + 1) * HEAD_W].astype(F32)
        q1 = jnp.where(lane < HEAD_DIM, qf, 0.0)
        q2 = jnp.where(lane >= HEAD_DIM, qf, 0.0)
        qzt_ref[h, :, 0:tq] = q1.T.astype(BF16)
        qzt_ref[h, :, tq:2 * tq] = q2.T.astype(BF16)
        m_refs[h][...] = jnp.full(m_refs[h].shape, NEG_BIG, F32)
        acc_refs[h][...] = jnp.zeros(acc_refs[h].shape, F32)

    def scores(h, k0):
        kj = k_ref[pl.ds(k0, tk), h * HEAD_W:(h + 1) * HEAD_W]
        s_refs[h][:, 0:2 * tq] = _dot(kj, qzt_ref[h])

    def softmax(h, masked):
        for c in range(nblk):
            cs = slice(c * LANES, (c + 1) * LANES)
            s = s_refs[h][:, cs]
            if masked:
                key = lax.broadcasted_iota(jnp.int32, (tk, LANES), 0)
                qry = lax.broadcasted_iota(jnp.int32, (tk, LANES), 1) + (c * LANES) % tq
                s = jnp.where(key <= qry, s, NEG_BIG)
            m_old = m_refs[h][:, cs]
            m_new = jnp.maximum(m_old, jnp.max(s, axis=0, keepdims=True))
            a_refs[h][:, cs] = jnp.exp2(m_old - m_new)
            m_refs[h][:, cs] = m_new
            p_refs[h][:, cs] = jnp.exp2(s - m_new).astype(BF16)

    def accumulate(h, k0):
        vj = jnp.concatenate([vt_ref[h * HEAD_W:(h + 1) * HEAD_W, pl.ds(k0, tk)], ones_rows], axis=0)
        acc_refs[h][...] = a_refs[h][...] * acc_refs[h][...] + _dot(vj, p_refs[h][:, 0:2 * tq])

    def tile_start(j):
        return pl.multiple_of(j * tk, tk)

    for h in range(nh):
        scores(h, 0)

    @pl.when(i > 0)
    def _():
        for h in range(nh):
            softmax(h, False)
            scores(h, tile_start(1))

    def step(j):
        for h in range(nh):
            accumulate(h, tile_start(j - 1))
            softmax(h, False)
            scores(h, tile_start(j + 1))

    n_steps = jnp.maximum(i - 1, 0)
    n_trips = n_steps // ATT_UNROLL

    def body(t, carry):
        for u in range(ATT_UNROLL):
            step(ATT_UNROLL * t + 1 + u)
        return carry

    lax.fori_loop(0, n_trips, body, 0)

    def tail(j, carry):
        step(j)
        return carry

    lax.fori_loop(ATT_UNROLL * n_trips + 1, n_steps + 1, tail, 0)

    @pl.when(i > 0)
    def _():
        for h in range(nh):
            accumulate(h, tile_start(i - 1))

    for h in range(nh):
        softmax(h, True)
        accumulate(h, tile_start(i))

    lam = lam_ref[0]
    for h in range(nh):
        acc = acc_refs[h][0:HEAD_W, :]
        l = acc_refs[h][HEAD_W:HEAD_W + 1, :]
        o = acc[:, 0:tq] / l[:, 0:tq] - lam * (acc[:, tq:2 * tq] / l[:, tq:2 * tq])
        ms = jnp.mean(o * o, axis=0, keepdims=True)
        o = o * lax.rsqrt(ms + SUBLN_EPS) * g_ref[...] * out_scale
        o_ref[:, h * HEAD_W:(h + 1) * HEAD_W] = o.T.astype(BF16)


def _attn_call(lam, q, k, vt, g_col, out_scale):
    B, S, _ = q.shape
    tq = ATT_TQ
    assert ATT_TQ == ATT_TK
    kernel = functools.partial(_attn_kernel, out_scale=out_scale)
    return pl.pallas_call(
        kernel,
        grid=(B, S // tq),
        in_specs=[
            pl.BlockSpec(memory_space=pltpu.SMEM),
            pl.BlockSpec((None, tq, QK_W), lambda b, i: (b, i, 0)),
            pl.BlockSpec((None, S, QK_W), lambda b, i: (b, 0, 0), pipeline_mode=pl.Buffered(1)),
            pl.BlockSpec((None, QK_W, S), lambda b, i: (b, 0, 0), pipeline_mode=pl.Buffered(1)),
            pl.BlockSpec((HEAD_W, 1), lambda b, i: (0, 0)),
        ],
        out_specs=pl.BlockSpec((None, tq, QK_W), lambda b, i: (b, i, 0)),
        out_shape=jax.ShapeDtypeStruct((B, S, QK_W), BF16),
        scratch_shapes=(
            [pltpu.VMEM((ATT_HEADS, HEAD_W, 2 * tq), BF16)]
            + [pltpu.VMEM((ATT_TK, 2 * tq + LANES), F32) for _ in range(ATT_HEADS)]
            + [pltpu.VMEM((ATT_TK, 2 * tq + LANES), BF16) for _ in range(ATT_HEADS)]
            + [pltpu.VMEM((1, 2 * tq), F32) for _ in range(ATT_HEADS)]
            + [pltpu.VMEM((1, 2 * tq), F32) for _ in range(ATT_HEADS)]
            + [pltpu.VMEM((ACC_ROWS, 2 * tq), F32) for _ in range(ATT_HEADS)]
        ),
        compiler_params=pltpu.CompilerParams(
            dimension_semantics=("arbitrary", "arbitrary"),
            vmem_limit_bytes=VMEM_LIMIT),
        name="diff_flash_attn",
    )(lam, q, k, vt, g_col)


def _pack_bf16_pairs(h):
    half = h.shape[1] // 2
    hb = h.astype(BF16).astype(F32)
    bits = lax.bitcast_convert_type(hb, jnp.uint32)
    return (bits[:, 0:half] >> 16) | (bits[:, half:] & jnp.uint32(0xFFFF0000))


def _unpack_bf16_pairs(w):
    lo = lax.bitcast_convert_type(w << 16, F32).astype(BF16)
    hi = lax.bitcast_convert_type(w & jnp.uint32(0xFFFF0000), F32).astype(BF16)
    return jnp.concatenate([lo, hi], axis=1)


def _first_index_of(mask, row_f, big):
    return jnp.min(jnp.where(mask, row_f, big), axis=0, keepdims=True)


def _mix_kernel(x_ref, o_ref, u_ref, wg_ref, bg_ref, woa_ref, woc_ref, wmo_ref, g1_ref, b1_ref,
                wr_ref, br_ref,
                h1_ref, h1p_ref, ri_ref, rf_ref, cnt_ref, carry_ref):
    tm = x_ref.shape[0]
    step = pl.program_id(0)

    @pl.when(step == 0)
    def _():
        carry_ref[...] = jnp.zeros(carry_ref.shape, F32)

    x = x_ref[...]
    xb = x.astype(BF16)
    gates = jax.nn.sigmoid(_dot(xb, wg_ref[...]) + bg_ref[...])
    ya = _dot(o_ref[...], woa_ref[...])
    yc = _dot(u_ref[...], woc_ref[...])
    merged = (gates[:, 0:D_MODEL] * ya + gates[:, D_MODEL:] * yc).astype(BF16)
    z = DEEPNORM_ALPHA * x + _dot(merged, wmo_ref[...])
    mu = jnp.mean(z, axis=-1, keepdims=True)
    zc = z - mu
    var = jnp.mean(zc * zc, axis=-1, keepdims=True)
    h1 = zc * lax.rsqrt(var + LN_EPS) * g1_ref[...] + b1_ref[...]
    h1_ref[...] = h1
    h1p_ref[...] = _pack_bf16_pairs(h1)

    logits = _dot(h1.astype(BF16), wr_ref[...]) + br_ref[...]
    lt = logits.T
    row8 = lax.broadcasted_iota(jnp.int32, (SUBLANES, tm), 0).astype(F32)
    gl = jnp.where(row8 < N_GROUPS, lt[0:SUBLANES, :], NEG_BIG)
    gmax = jnp.max(gl, axis=0, keepdims=True)
    gsel = _first_index_of(gl == gmax, row8, float(SUBLANES))
    gw = 1.0 / jnp.sum(jnp.exp(gl - gmax), axis=0, keepdims=True)
    el = lt[SUBLANES:2 * SUBLANES, :]
    for g in range(1, N_GROUPS):
        el = jnp.where(gsel == float(g), lt[(g + 1) * SUBLANES:(g + 2) * SUBLANES, :], el)
    v1 = jnp.max(el, axis=0, keepdims=True)
    i1 = _first_index_of(el == v1, row8, float(EPG))
    el2 = jnp.where(row8 == i1, -jnp.inf, el)
    v2 = jnp.max(el2, axis=0, keepdims=True)
    i2 = _first_index_of(el2 == v2, row8, float(EPG))
    t = jnp.exp(v2 - v1)
    p1 = gw / (1.0 + t)
    p2 = gw * t / (1.0 + t)
    e1 = gsel * float(EPG) + i1
    e2 = gsel * float(EPG) + i2

    rowe = lax.broadcasted_iota(jnp.int32, (N_EXPERTS, tm), 0).astype(F32)
    oh1 = (rowe == e1).astype(F32)
    oh2 = (rowe == e2).astype(F32)
    r_i = lax.broadcasted_iota(jnp.int32, (tm, tm), 0)
    c_i = lax.broadcasted_iota(jnp.int32, (tm, tm), 1)
    upper = jnp.where(r_i < c_i, 1.0, 0.0).astype(BF16)
    cum1 = _dot(oh1.astype(BF16), upper)
    cum2 = _dot(oh2.astype(BF16), upper)
    tot1 = jnp.sum(oh1, axis=1, keepdims=True)
    tot2 = jnp.sum(oh2, axis=1, keepdims=True)
    carry = carry_ref[:, 0:1]
    rank1 = jnp.sum(oh1 * (carry + cum1), axis=0, keepdims=True)
    rank2 = jnp.sum(oh2 * (carry + tot1 + cum2), axis=0, keepdims=True)
    new_carry = carry_ref[...] + (tot1 + tot2)
    carry_ref[...] = new_carry
    cnt_ref[...] = new_carry.astype(jnp.int32)

    ri_ref[...] = jnp.zeros(ri_ref.shape, jnp.int32)
    ri_ref[0:1, :] = e1.astype(jnp.int32)
    ri_ref[1:2, :] = e2.astype(jnp.int32)
    ri_ref[2:3, :] = rank1.astype(jnp.int32)
    ri_ref[3:4, :] = rank2.astype(jnp.int32)

    row128 = lax.broadcasted_iota(jnp.int32, (LANES, tm), 0)
    pw = jnp.where(row128 == 0, p1, jnp.where(row128 == 1, p2, 0.0))
    rf_ref[...] = pw.T


def _mix_call(x2, o_n, u, wg, bg, woa, woc, wmo, g1, b1, wr, br):
    T, D = x2.shape
    tm = MIX_TM
    const = lambda i: (0, 0)
    row = lambda i: (i, 0)
    return pl.pallas_call(
        _mix_kernel,
        grid=(T // tm,),
        in_specs=[
            pl.BlockSpec((tm, D), row),
            pl.BlockSpec((tm, QK_W), row),
            pl.BlockSpec((tm, CONV_W), row),
            pl.BlockSpec(wg.shape, const),
            pl.BlockSpec(bg.shape, const),
            pl.BlockSpec(woa.shape, const),
            pl.BlockSpec(woc.shape, const),
            pl.BlockSpec(wmo.shape, const),
            pl.BlockSpec(g1.shape, const),
            pl.BlockSpec(b1.shape, const),
            pl.BlockSpec(wr.shape, const),
            pl.BlockSpec(br.shape, const),
        ],
        out_specs=[
            pl.BlockSpec((tm, D), row),
            pl.BlockSpec((tm, D // 2), row),
            pl.BlockSpec((SUBLANES, tm), lambda i: (0, i)),
            pl.BlockSpec((tm, LANES), row),
            pl.BlockSpec((N_EXPERTS, LANES), const),
        ],
        out_shape=[
            jax.ShapeDtypeStruct((T, D), F32),
            jax.ShapeDtypeStruct((T, D // 2), jnp.uint32),
            jax.ShapeDtypeStruct((SUBLANES, T), jnp.int32),
            jax.ShapeDtypeStruct((T, LANES), F32),
            jax.ShapeDtypeStruct((N_EXPERTS, LANES), jnp.int32),
        ],
        scratch_shapes=[pltpu.VMEM((N_EXPERTS, LANES), F32)],
        compiler_params=pltpu.CompilerParams(
            dimension_semantics=("arbitrary",),
            vmem_limit_bytes=VMEM_LIMIT),
        name="mix_ln_router",
    )(x2, o_n, u, wg, bg, woa, woc, wmo, g1, b1, wr, br)


def _dispatch_kernel(pend_ref, dest_ref, h1p_ref, xs_ref, zero_ref, sem):
    tm = h1p_ref.shape[0]
    bm = zero_ref.shape[0]

    @pl.when(pl.program_id(0) == 0)
    def _():
        zero_ref[...] = jnp.zeros(zero_ref.shape, zero_ref.dtype)
        n_rows = xs_ref.shape[0]

        def zero_copy(e):
            if e < N_EXPERTS:
                prev_end = pend_ref[e - 1] if e > 0 else 0
                start = pl.multiple_of(jnp.maximum(pend_ref[e] - bm, 0), bm)
                needed = pend_ref[e] > prev_end
            else:
                start = n_rows - (e - N_EXPERTS + 1) * bm
                needed = start >= pend_ref[N_EXPERTS - 1]
            return pltpu.make_async_copy(zero_ref, xs_ref.at[pl.ds(start, bm), :], sem), needed

        for e in range(2 * N_EXPERTS):
            copy, needed = zero_copy(e)
            pl.when(needed)(copy.start)
        for e in range(2 * N_EXPERTS):
            copy, needed = zero_copy(e)
            pl.when(needed)(copy.wait)

    def row_copy(r, k):
        return pltpu.make_async_copy(
            h1p_ref.at[pl.ds(r, 1), :],
            xs_ref.at[pl.ds(dest_ref[k, r], 1), :],
            sem)

    for r in range(tm):
        row_copy(r, 0).start(priority=0)
        row_copy(r, 1).start(priority=1)
    for _ in range(2):
        pltpu.make_async_copy(h1p_ref, xs_ref.at[pl.ds(0, tm), :], sem).wait()


def _dispatch_call(pend, dest, h1p, n_rows):
    T, W = h1p.shape
    tm = DISP_TM
    return pl.pallas_call(
        _dispatch_kernel,
        grid_spec=pltpu.PrefetchScalarGridSpec(
            num_scalar_prefetch=1,
            grid=(T // tm,),
            in_specs=[
                pl.BlockSpec((2, tm), lambda i, pend: (0, i), memory_space=pltpu.SMEM),
                pl.BlockSpec((tm, W), lambda i, pend: (i, 0)),
            ],
            out_specs=pl.BlockSpec(memory_space=pl.ANY),
            scratch_shapes=[pltpu.VMEM((MOE_BM, W), h1p.dtype), pltpu.SemaphoreType.DMA],
        ),
        out_shape=jax.ShapeDtypeStruct((n_rows, W), h1p.dtype),
        compiler_params=pltpu.CompilerParams(
            dimension_semantics=("arbitrary",), has_side_effects=True),
        name="moe_dispatch",
    )(pend, dest, h1p)


def _expert_kernel(be_ref, nu_ref, xs_ref, wg_ref, wu_ref, wd_ref, ys_ref, wgb_ref, wub_ref, wdb_ref):
    i = pl.program_id(0)
    used = i < nu_ref[0]

    @pl.when(used & ((i == 0) | (be_ref[i] != be_ref[jnp.maximum(i - 1, 0)])))
    def _():
        wgb_ref[...] = wg_ref[...].astype(BF16)
        wub_ref[...] = wu_ref[...].astype(BF16)
        wdb_ref[...] = wd_ref[...].astype(BF16)

    @pl.when(used)
    def _():
        xb = _unpack_bf16_pairs(xs_ref[...])
        g = _dot(xb, wgb_ref[...])
        u = _dot(xb, wub_ref[...])
        h = (g * jax.nn.sigmoid(g) * u).astype(BF16)
        ys_ref[...] = _dot(h, wdb_ref[...])

    @pl.when(i >= nu_ref[0])
    def _():
        ys_ref[...] = jnp.zeros(ys_ref.shape, F32)


def _expert_call(block_expert, n_used, xs, wg, wu, wd):
    P = xs.shape[0]
    bm = MOE_BM
    nb = P // bm
    xs_map = lambda i, be, nu: (jnp.minimum(i, nu[0] - 1), 0)
    w_map = lambda i, be, nu: (be[i], 0, 0)
    return pl.pallas_call(
        _expert_kernel,
        grid_spec=pltpu.PrefetchScalarGridSpec(
            num_scalar_prefetch=2,
            grid=(nb,),
            in_specs=[
                pl.BlockSpec((bm, D_MODEL // 2), xs_map),
                pl.BlockSpec((None, D_MODEL, EXPERT_FF), w_map),
                pl.BlockSpec((None, D_MODEL, EXPERT_FF), w_map),
                pl.BlockSpec((None, EXPERT_FF, D_MODEL), w_map),
            ],
            out_specs=pl.BlockSpec((bm, D_MODEL), lambda i, be, nu: (i, 0)),
            scratch_shapes=[
                pltpu.VMEM((D_MODEL, EXPERT_FF), BF16),
                pltpu.VMEM((D_MODEL, EXPERT_FF), BF16),
                pltpu.VMEM((EXPERT_FF, D_MODEL), BF16),
            ],
        ),
        out_shape=jax.ShapeDtypeStruct((P, D_MODEL), F32),
        compiler_params=pltpu.CompilerParams(
            dimension_semantics=("arbitrary",),
            vmem_limit_bytes=VMEM_LIMIT),
        name="moe_experts",
    )(block_expert, n_used, xs, wg, wu, wd)


def _combine_kernel(dest_ref, dest_next_ref, ys_ref, h1_ref, rf_ref, g2_ref, b2_ref, out_ref, buf, sem):
    tm = h1_ref.shape[0]
    i = pl.program_id(0)
    n = pl.num_programs(0)
    slot = i % 2

    def issue_tile(d_ref, s):
        for r in range(tm):
            for k in range(2):
                pltpu.make_async_copy(ys_ref.at[pl.ds(d_ref[k, r], 1), :],
                                      buf.at[s, k, pl.ds(r, 1), :], sem.at[s]).start(priority=k)

    @pl.when(i == 0)
    def _():
        issue_tile(dest_ref, 0)

    @pl.when(i + 1 < n)
    def _():
        issue_tile(dest_next_ref, 1 - slot)

    for k in range(2):
        pltpu.make_async_copy(ys_ref.at[pl.ds(0, tm), :], buf.at[slot, k], sem.at[slot]).wait()

    rf = rf_ref[...]
    ffn = rf[:, 0:1] * buf[slot, 0] + rf[:, 1:2] * buf[slot, 1]
    z = DEEPNORM_ALPHA * h1_ref[...] + ffn
    mu = jnp.mean(z, axis=-1, keepdims=True)
    zc = z - mu
    var = jnp.mean(zc * zc, axis=-1, keepdims=True)
    out_ref[...] = zc * lax.rsqrt(var + LN_EPS) * g2_ref[...] + b2_ref[...]


def _combine_call(dest, ys, h1, rf, g2, b2):
    T, D = h1.shape
    tm = COMB_TM
    const = lambda i: (0, 0)
    row = lambda i: (i, 0)
    return pl.pallas_call(
        _combine_kernel,
        grid=(T // tm,),
        in_specs=[
            pl.BlockSpec((2, tm), lambda i: (0, i), memory_space=pltpu.SMEM),
            pl.BlockSpec((2, tm), lambda i: (0, jnp.minimum(i + 1, T // tm - 1)), memory_space=pltpu.SMEM),
            pl.BlockSpec(memory_space=pl.ANY),
            pl.BlockSpec((tm, D), row),
            pl.BlockSpec((tm, LANES), row),
            pl.BlockSpec(g2.shape, const),
            pl.BlockSpec(b2.shape, const),
        ],
        out_specs=pl.BlockSpec((tm, D), row),
        out_shape=jax.ShapeDtypeStruct((T, D), F32),
        scratch_shapes=[
            pltpu.VMEM((2, 2, tm, D), F32),
            pltpu.SemaphoreType.DMA((2,)),
        ],
        compiler_params=pltpu.CompilerParams(
            dimension_semantics=("arbitrary",),
            vmem_limit_bytes=VMEM_LIMIT),
        name="moe_combine_ln",
    )(dest, dest, ys, h1, rf, g2, b2)


def kernel(x, positions, w_in, b_gate, lambda_q1, lambda_k1, lambda_q2, lambda_k2, subln_g, w_o_att, conv_w, w_o_conv, w_mix_out, ln1_g, ln1_b, w_router_group, b_router_group, w_router_expert, b_router_expert, w_exp_gate, w_exp_up, w_exp_down, ln2_g, ln2_b):
    B, S, D = x.shape
    T = B * S
    l = 0
    lambda_init = 0.8 - 0.6 * math.exp(-0.3 * l)

    inv_freq = ROPE_THETA ** (-jnp.arange(0, HEAD_DIM, 2, dtype=F32) / HEAD_DIM)
    ang = positions.astype(F32)[..., None] * inv_freq
    cos_t = jnp.tile(jnp.cos(ang), (1, 1, 4))
    sin_h = jnp.sin(ang)
    sin_t = jnp.tile(jnp.concatenate([-sin_h, sin_h], -1), (1, 1, 2))

    n_main = 2 * QK_W + QK_W + 3 * CONV_W
    w0 = w_in[l]
    w_main = w0[:, :n_main].astype(BF16)
    wv_t = w0[:, 2 * QK_W:3 * QK_W].T.astype(BF16)
    w_gates = w0[:, n_main:].astype(BF16)

    q, k, vt, u = _proj_call(x, w_main, wv_t, cos_t, sin_t, conv_w[l])

    lam = (jnp.exp(jnp.sum(lambda_q1[l].astype(F32) * lambda_k1[l].astype(F32)))
           - jnp.exp(jnp.sum(lambda_q2[l].astype(F32) * lambda_k2[l].astype(F32)))
           + lambda_init).reshape(1)
    o_n = _attn_call(lam, q, k, vt, subln_g[l].astype(F32).reshape(HEAD_W, 1), 1.0 - lambda_init)

    wr = jnp.zeros((D, LANES), F32)
    wr = wr.at[:, 0:N_GROUPS].set(w_router_group[l]).at[:, SUBLANES:SUBLANES + N_EXPERTS].set(w_router_expert[l])
    br = jnp.zeros((1, LANES), F32)
    br = br.at[0, 0:N_GROUPS].set(b_router_group[l]).at[0, SUBLANES:SUBLANES + N_EXPERTS].set(b_router_expert[l])

    h1, h1p, ri, rf, cnt = _mix_call(
        x.reshape(T, D), o_n.reshape(T, QK_W), u.reshape(T, CONV_W),
        w_gates, b_gate[l].reshape(1, -1).astype(F32),
        w_o_att[l].astype(BF16), w_o_conv[l].astype(BF16), w_mix_out[l].astype(BF16),
        ln1_g[l].reshape(1, D).astype(F32), ln1_b[l].reshape(1, D).astype(F32),
        wr.astype(BF16), br)

    bm = MOE_BM
    counts = cnt[:, 0]
    padded = (counts + bm - 1) // bm * bm
    pend = jnp.cumsum(padded)
    pstart = pend - padded
    onehot = (ri[0:2, :, None] == jnp.arange(N_EXPERTS, dtype=jnp.int32)).astype(jnp.int32)
    dest = jnp.sum(onehot * pstart, axis=-1) + ri[2:4]
    A = 2 * T
    P = A + N_EXPERTS * bm
    nb = P // bm
    blk_start = jnp.arange(nb, dtype=jnp.int32) * bm
    block_expert = jnp.minimum(
        jnp.sum((pend[None, :] <= blk_start[:, None]).astype(jnp.int32), axis=1), N_EXPERTS - 1)
    n_used = (pend[-1] // bm).astype(jnp.int32).reshape(1)

    xs = _dispatch_call(pend.astype(jnp.int32), dest, h1p, P)
    ys = _expert_call(block_expert, n_used, xs, w_exp_gate[l], w_exp_up[l], w_exp_down[l])
    out = _combine_call(dest, ys, h1, rf,
                        ln2_g[l].reshape(1, D).astype(F32), ln2_b[l].reshape(1, D).astype(F32))
    return out.reshape(B, S, D)
```

```python
import functools
import math

import jax
import jax.numpy as jnp
from jax import lax
from jax.experimental import pallas as pl
from jax.experimental.pallas import tpu as pltpu

D_MODEL = 1024
ATT_HEADS = 4
HEAD_DIM = 64
HEAD_W = 2 * HEAD_DIM
QK_W = ATT_HEADS * HEAD_W
CONV_W = D_MODEL // 2
CONV_K = 3
N_GROUPS = 4
EPG = 8
N_EXPERTS = N_GROUPS * EPG
EXPERT_FF = D_MODEL // 2
ROPE_THETA = 10000.0
SUBLN_EPS = 1e-5
LN_EPS = 1e-5
DEPTH = 1
DEEPNORM_ALPHA = (2.0 * DEPTH) ** 0.25

LANES = 128
SUBLANES = 8
VMEM_LIMIT = 48 * 1024 * 1024

PROJ_TM = 512
ATT_TQ = 256
ATT_TK = 256
ATT_UNROLL = 4
MIX_TM = 512
MOE_BM = 512
DISP_TM = 256
COMB_TM = 256
NEG_BIG = -1e30

F32 = jnp.float32
BF16 = jnp.bfloat16


def _dot(a, b):
    return jnp.dot(a, b, preferred_element_type=F32)


def _dot_nt(a, b):
    return lax.dot_general(a, b, (((1,), (1,)), ((), ())), preferred_element_type=F32)


def _proj_kernel(x_ref, w_ref, wvt_ref, cos_ref, sin_ref, cw_ref,
                 q_ref, k_ref, vt_ref, u_ref, pbuf):
    tm = x_ref.shape[0]
    i = pl.program_id(1)
    xb = x_ref[...].astype(BF16)
    cos = cos_ref[...]
    sin = sin_ref[...]
    lane = lax.broadcasted_iota(jnp.int32, (tm, LANES), 1)
    low_half = (lane % HEAD_DIM) < (HEAD_DIM // 2)

    def rope(t):
        rot = jnp.where(low_half, pltpu.roll(t, LANES - HEAD_DIM // 2, 1),
                        pltpu.roll(t, HEAD_DIM // 2, 1))
        return t * cos + rot * sin

    q = _dot(xb, w_ref[:, 0:QK_W])
    for h in range(ATT_HEADS):
        sl = slice(h * HEAD_W, (h + 1) * HEAD_W)
        q_ref[:, sl] = (rope(q[:, sl]) * (HEAD_DIM ** -0.5 * math.log2(math.e))).astype(BF16)
    k = _dot(xb, w_ref[:, QK_W:2 * QK_W])
    for h in range(ATT_HEADS):
        sl = slice(h * HEAD_W, (h + 1) * HEAD_W)
        k_ref[:, sl] = rope(k[:, sl]).astype(BF16)
    vt_ref[...] = _dot_nt(wvt_ref[...], xb).astype(BF16)

    c0 = 3 * QK_W
    cb = _dot(xb, w_ref[:, c0:c0 + CONV_W])
    cc = _dot(xb, w_ref[:, c0 + CONV_W:c0 + 2 * CONV_W])
    cx = _dot(xb, w_ref[:, c0 + 2 * CONV_W:c0 + 3 * CONV_W])
    p = cc * cx

    @pl.when(i == 0)
    def _():
        pbuf[0:SUBLANES, :] = jnp.zeros((SUBLANES, CONV_W), F32)

    pbuf[SUBLANES:SUBLANES + tm, :] = p
    y = (cw_ref[0:1, :] * pbuf[SUBLANES - 2:SUBLANES - 2 + tm, :]
         + cw_ref[1:2, :] * pbuf[SUBLANES - 1:SUBLANES - 1 + tm, :]
         + cw_ref[2:3, :] * p)
    u_ref[...] = (cb * y).astype(BF16)
    pbuf[0:SUBLANES, :] = pbuf[tm:tm + SUBLANES, :]


def _proj_call(x, w_main, wv_t, cos_t, sin_t, conv_w):
    B, S, D = x.shape
    tm = PROJ_TM
    n_main = w_main.shape[1]
    return pl.pallas_call(
        _proj_kernel,
        grid=(B, S // tm),
        in_specs=[
            pl.BlockSpec((None, tm, D), lambda b, i: (b, i, 0)),
            pl.BlockSpec((D, n_main), lambda b, i: (0, 0)),
            pl.BlockSpec((QK_W, D), lambda b, i: (0, 0)),
            pl.BlockSpec((None, tm, LANES), lambda b, i: (b, i, 0)),
            pl.BlockSpec((None, tm, LANES), lambda b, i: (b, i, 0)),
            pl.BlockSpec((CONV_K, CONV_W), lambda b, i: (0, 0)),
        ],
        out_specs=[
            pl.BlockSpec((None, tm, QK_W), lambda b, i: (b, i, 0)),
            pl.BlockSpec((None, tm, QK_W), lambda b, i: (b, i, 0)),
            pl.BlockSpec((None, QK_W, tm), lambda b, i: (b, 0, i)),
            pl.BlockSpec((None, tm, CONV_W), lambda b, i: (b, i, 0)),
        ],
        out_shape=[
            jax.ShapeDtypeStruct((B, S, QK_W), BF16),
            jax.ShapeDtypeStruct((B, S, QK_W), BF16),
            jax.ShapeDtypeStruct((B, QK_W, S), BF16),
            jax.ShapeDtypeStruct((B, S, CONV_W), BF16),
        ],
        scratch_shapes=[pltpu.VMEM((tm + 2 * SUBLANES, CONV_W), F32)],
        compiler_params=pltpu.CompilerParams(
            dimension_semantics=("arbitrary", "arbitrary"),
            vmem_limit_bytes=VMEM_LIMIT),
        name="proj_rope_conv",
    )(x, w_main, wv_t, cos_t, sin_t, conv_w)


ACC_ROWS = HEAD_W + 16


def _attn_kernel(lam_ref, q_ref, k_ref, vt_ref, g_ref, o_ref, qzt_ref, *scratch, out_scale):
    nh = ATT_HEADS
    s_refs, p_refs = scratch[0:nh], scratch[nh:2 * nh]
    m_refs, a_refs, acc_refs = scratch[2 * nh:3 * nh], scratch[3 * nh:4 * nh], scratch[4 * nh:5 * nh]
    tq = q_ref.shape[0]
    tk = ATT_TK
    nblk = 2 * tq // LANES
    i = pl.program_id(1)
    lane = lax.broadcasted_iota(jnp.int32, (tq, HEAD_W), 1)
    ones_rows = jnp.ones((ACC_ROWS - HEAD_W, tk), BF16)

    for h in range(nh):
        qf = q_ref[:, h * HEAD_W:(h + 1) * HEAD_W].astype(F32)
        q1 = jnp.where(lane < HEAD_DIM, qf, 0.0)
        q2 = jnp.where(lane >= HEAD_DIM, qf, 0.0)
        qzt_ref[h, :, 0:tq] = q1.T.astype(BF16)
        qzt_ref[h, :, tq:2 * tq] = q2.T.astype(BF16)
        m_refs[h][...] = jnp.full(m_refs[h].shape, NEG_BIG, F32)
        acc_refs[h][...] = jnp.zeros(acc_refs[h].shape, F32)

    def scores(h, k0):
        kj = k_ref[pl.ds(k0, tk), h * HEAD_W:(h + 1) * HEAD_W]
        s_refs[h][:, 0:2 * tq] = _dot(kj, qzt_ref[h])

    def softmax(h, diag=None):
        for c in range(nblk):
            cs = slice(c * LANES, (c + 1) * LANES)
            s = s_refs[h][:, cs]
            if diag is not None:
                key = lax.broadcasted_iota(jnp.int32, (tk, LANES), 0) + diag * tk
                qry = lax.broadcasted_iota(jnp.int32, (tk, LANES), 1) + (c * LANES) % tq
                s = jnp.where(key <= qry, s, NEG_BIG)
            m_old = m_refs[h][:, cs]
            m_new = jnp.maximum(m_old, jnp.max(s, axis=0, keepdims=True))
            a_refs[h][:, cs] = jnp.exp2(m_old - m_new)
            m_refs[h][:, cs] = m_new
            p_refs[h][:, cs] = jnp.exp2(s - m_new).astype(BF16)

    def accumulate(h, k0):
        vj = jnp.concatenate([vt_ref[h * HEAD_W:(h + 1) * HEAD_W, pl.ds(k0, tk)], ones_rows], axis=0)
        acc_refs[h][...] = a_refs[h][...] * acc_refs[h][...] + _dot(vj, p_refs[h][:, 0:2 * tq])

    n_diag = tq // tk
    n_full = n_diag * i

    def tile_start(j):
        return pl.multiple_of(j * tk, tk)

    for h in range(nh):
        scores(h, 0)

    @pl.when(i > 0)
    def _():
        for h in range(nh):
            softmax(h)
            scores(h, tile_start(1))

    def step(j):
        for h in range(nh):
            accumulate(h, tile_start(j - 1))
            softmax(h)
            scores(h, tile_start(j + 1))

    n_steps = jnp.maximum(n_full - 1, 0)
    n_trips = n_steps // ATT_UNROLL

    def body(t, carry):
        for u in range(ATT_UNROLL):
            step(ATT_UNROLL * t + 1 + u)
        return carry

    lax.fori_loop(0, n_trips, body, 0)

    def tail(j, carry):
        step(j)
        return carry

    lax.fori_loop(ATT_UNROLL * n_trips + 1, n_steps + 1, tail, 0)

    @pl.when(i > 0)
    def _():
        for h in range(nh):
            accumulate(h, tile_start(n_full - 1))

    for d in range(n_diag):
        for h in range(nh):
            softmax(h, diag=d)
            if d + 1 < n_diag:
                scores(h, tile_start(n_full + d + 1))
            accumulate(h, tile_start(n_full + d))

    lam = lam_ref[0]
    for h in range(nh):
        acc = acc_refs[h][0:HEAD_W, :]
        l = acc_refs[h][HEAD_W:HEAD_W + 1, :]
        o = acc[:, 0:tq] / l[:, 0:tq] - lam * (acc[:, tq:2 * tq] / l[:, tq:2 * tq])
        ms = jnp.mean(o * o, axis=0, keepdims=True)
        o = o * lax.rsqrt(ms + SUBLN_EPS) * g_ref[...] * out_scale
        o_ref[:, h * HEAD_W:(h + 1) * HEAD_W] = o.T.astype(BF16)


def _attn_call(lam, q, k, vt, g_col, out_scale):
    B, S, _ = q.shape
    tq = ATT_TQ
    assert ATT_TQ % ATT_TK == 0
    kernel = functools.partial(_attn_kernel, out_scale=out_scale)
    return pl.pallas_call(
        kernel,
        grid=(B, S // tq),
        in_specs=[
            pl.BlockSpec(memory_space=pltpu.SMEM),
            pl.BlockSpec((None, tq, QK_W), lambda b, i: (b, i, 0)),
            pl.BlockSpec((None, S, QK_W), lambda b, i: (b, 0, 0), pipeline_mode=pl.Buffered(1)),
            pl.BlockSpec((None, QK_W, S), lambda b, i: (b, 0, 0), pipeline_mode=pl.Buffered(1)),
            pl.BlockSpec((HEAD_W, 1), lambda b, i: (0, 0)),
        ],
        out_specs=pl.BlockSpec((None, tq, QK_W), lambda b, i: (b, i, 0)),
        out_shape=jax.ShapeDtypeStruct((B, S, QK_W), BF16),
        scratch_shapes=(
            [pltpu.VMEM((ATT_HEADS, HEAD_W, 2 * tq), BF16)]
            + [pltpu.VMEM((ATT_TK, 2 * tq + LANES), F32) for _ in range(ATT_HEADS)]
            + [pltpu.VMEM((ATT_TK, 2 * tq + LANES), BF16) for _ in range(ATT_HEADS)]
            + [pltpu.VMEM((1, 2 * tq), F32) for _ in range(ATT_HEADS)]
            + [pltpu.VMEM((1, 2 * tq), F32) for _ in range(ATT_HEADS)]
            + [pltpu.VMEM((ACC_ROWS, 2 * tq), F32) for _ in range(ATT_HEADS)]
        ),
        compiler_params=pltpu.CompilerParams(
            dimension_semantics=("arbitrary", "arbitrary"),
            vmem_limit_bytes=VMEM_LIMIT),
        name="diff_flash_attn",
    )(lam, q, k, vt, g_col)


def _pack_bf16_pairs(h):
    half = h.shape[1] // 2
    hb = h.astype(BF16).astype(F32)
    bits = lax.bitcast_convert_type(hb, jnp.uint32)
    return (bits[:, 0:half] >> 16) | (bits[:, half:] & jnp.uint32(0xFFFF0000))


def _unpack_bf16_pairs(w):
    lo = lax.bitcast_convert_type(w << 16, F32).astype(BF16)
    hi = lax.bitcast_convert_type(w & jnp.uint32(0xFFFF0000), F32).astype(BF16)
    return jnp.concatenate([lo, hi], axis=1)


def _first_index_of(mask, row_f, big):
    return jnp.min(jnp.where(mask, row_f, big), axis=0, keepdims=True)


def _mix_kernel(x_ref, o_ref, u_ref, wg_ref, bg_ref, woa_ref, woc_ref, wmo_ref, g1_ref, b1_ref,
                wr_ref, br_ref,
                h1_ref, h1p_ref, ri_ref, rf_ref, cnt_ref, carry_ref):
    tm = x_ref.shape[0]
    step = pl.program_id(0)

    @pl.when(step == 0)
    def _():
        carry_ref[...] = jnp.zeros(carry_ref.shape, F32)

    x = x_ref[...]
    xb = x.astype(BF16)
    gates = jax.nn.sigmoid(_dot(xb, wg_ref[...]) + bg_ref[...])
    ya = _dot(o_ref[...], woa_ref[...])
    yc = _dot(u_ref[...], woc_ref[...])
    merged = (gates[:, 0:D_MODEL] * ya + gates[:, D_MODEL:] * yc).astype(BF16)
    z = DEEPNORM_ALPHA * x + _dot(merged, wmo_ref[...])
    mu = jnp.mean(z, axis=-1, keepdims=True)
    zc = z - mu
    var = jnp.mean(zc * zc, axis=-1, keepdims=True)
    h1 = zc * lax.rsqrt(var + LN_EPS) * g1_ref[...] + b1_ref[...]
    h1_ref[...] = h1
    h1p_ref[...] = _pack_bf16_pairs(h1)

    logits = _dot(h1.astype(BF16), wr_ref[...]) + br_ref[...]
    lt = logits.T
    row8 = lax.broadcasted_iota(jnp.int32, (SUBLANES, tm), 0).astype(F32)
    gl = jnp.where(row8 < N_GROUPS, lt[0:SUBLANES, :], NEG_BIG)
    gmax = jnp.max(gl, axis=0, keepdims=True)
    gsel = _first_index_of(gl == gmax, row8, float(SUBLANES))
    gw = 1.0 / jnp.sum(jnp.exp(gl - gmax), axis=0, keepdims=True)
    el = lt[SUBLANES:2 * SUBLANES, :]
    for g in range(1, N_GROUPS):
        el = jnp.where(gsel == float(g), lt[(g + 1) * SUBLANES:(g + 2) * SUBLANES, :], el)
    v1 = jnp.max(el, axis=0, keepdims=True)
    i1 = _first_index_of(el == v1, row8, float(EPG))
    el2 = jnp.where(row8 == i1, -jnp.inf, el)
    v2 = jnp.max(el2, axis=0, keepdims=True)
    i2 = _first_index_of(el2 == v2, row8, float(EPG))
    t = jnp.exp(v2 - v1)
    p1 = gw / (1.0 + t)
    p2 = gw * t / (1.0 + t)
    e1 = gsel * float(EPG) + i1
    e2 = gsel * float(EPG) + i2

    rowe = lax.broadcasted_iota(jnp.int32, (N_EXPERTS, tm), 0).astype(F32)
    oh1 = (rowe == e1).astype(F32)
    oh2 = (rowe == e2).astype(F32)
    r_i = lax.broadcasted_iota(jnp.int32, (tm, tm), 0)
    c_i = lax.broadcasted_iota(jnp.int32, (tm, tm), 1)
    upper = jnp.where(r_i < c_i, 1.0, 0.0).astype(BF16)
    cum1 = _dot(oh1.astype(BF16), upper)
    cum2 = _dot(oh2.astype(BF16), upper)
    tot1 = jnp.sum(oh1, axis=1, keepdims=True)
    tot2 = jnp.sum(oh2, axis=1, keepdims=True)
    carry = carry_ref[:, 0:1]
    rank1 = jnp.sum(oh1 * (carry + cum1), axis=0, keepdims=True)
    rank2 = jnp.sum(oh2 * (carry + tot1 + cum2), axis=0, keepdims=True)
    new_carry = carry_ref[...] + (tot1 + tot2)
    carry_ref[...] = new_carry
    cnt_ref[...] = new_carry.astype(jnp.int32)

    ri_ref[...] = jnp.zeros(ri_ref.shape, jnp.int32)
    ri_ref[0:1, :] = e1.astype(jnp.int32)
    ri_ref[1:2, :] = e2.astype(jnp.int32)
    ri_ref[2:3, :] = rank1.astype(jnp.int32)
    ri_ref[3:4, :] = rank2.astype(jnp.int32)

    row128 = lax.broadcasted_iota(jnp.int32, (LANES, tm), 0)
    pw = jnp.where(row128 == 0, p1, jnp.where(row128 == 1, p2, 0.0))
    rf_ref[...] = pw.T


def _mix_call(x2, o_n, u, wg, bg, woa, woc, wmo, g1, b1, wr, br):
    T, D = x2.shape
    tm = MIX_TM
    const = lambda i: (0, 0)
    row = lambda i: (i, 0)
    return pl.pallas_call(
        _mix_kernel,
        grid=(T // tm,),
        in_specs=[
            pl.BlockSpec((tm, D), row),
            pl.BlockSpec((tm, QK_W), row),
            pl.BlockSpec((tm, CONV_W), row),
            pl.BlockSpec(wg.shape, const),
            pl.BlockSpec(bg.shape, const),
            pl.BlockSpec(woa.shape, const),
            pl.BlockSpec(woc.shape, const),
            pl.BlockSpec(wmo.shape, const),
            pl.BlockSpec(g1.shape, const),
            pl.BlockSpec(b1.shape, const),
            pl.BlockSpec(wr.shape, const),
            pl.BlockSpec(br.shape, const),
        ],
        out_specs=[
            pl.BlockSpec((tm, D), row),
            pl.BlockSpec((tm, D // 2), row),
            pl.BlockSpec((SUBLANES, tm), lambda i: (0, i)),
            pl.BlockSpec((tm, LANES), row),
            pl.BlockSpec((N_EXPERTS, LANES), const),
        ],
        out_shape=[
            jax.ShapeDtypeStruct((T, D), F32),
            jax.ShapeDtypeStruct((T, D // 2), jnp.uint32),
            jax.ShapeDtypeStruct((SUBLANES, T), jnp.int32),
            jax.ShapeDtypeStruct((T, LANES), F32),
            jax.ShapeDtypeStruct((N_EXPERTS, LANES), jnp.int32),
        ],
        scratch_shapes=[pltpu.VMEM((N_EXPERTS, LANES), F32)],
        compiler_params=pltpu.CompilerParams(
            dimension_semantics=("arbitrary",),
            vmem_limit_bytes=VMEM_LIMIT),
        name="mix_ln_router",
    )(x2, o_n, u, wg, bg, woa, woc, wmo, g1, b1, wr, br)


def _dispatch_kernel(pend_ref, dest_ref, h1p_ref, xs_ref, zero_ref, sem):
    tm = h1p_ref.shape[0]
    bm = zero_ref.shape[0]

    @pl.when(pl.program_id(0) == 0)
    def _():
        zero_ref[...] = jnp.zeros(zero_ref.shape, zero_ref.dtype)
        n_rows = xs_ref.shape[0]

        def zero_copy(e):
            if e < N_EXPERTS:
                prev_end = pend_ref[e - 1] if e > 0 else 0
                start = pl.multiple_of(jnp.maximum(pend_ref[e] - bm, 0), bm)
                needed = pend_ref[e] > prev_end
            else:
                start = n_rows - (e - N_EXPERTS + 1) * bm
                needed = start >= pend_ref[N_EXPERTS - 1]
            return pltpu.make_async_copy(zero_ref, xs_ref.at[pl.ds(start, bm), :], sem), needed

        for e in range(2 * N_EXPERTS):
            copy, needed = zero_copy(e)
            pl.when(needed)(copy.start)
        for e in range(2 * N_EXPERTS):
            copy, needed = zero_copy(e)
            pl.when(needed)(copy.wait)

    def row_copy(r, k):
        return pltpu.make_async_copy(
            h1p_ref.at[pl.ds(r, 1), :],
            xs_ref.at[pl.ds(dest_ref[k, r], 1), :],
            sem)

    for r in range(tm):
        row_copy(r, 0).start(priority=0)
        row_copy(r, 1).start(priority=1)
    for _ in range(2):
        pltpu.make_async_copy(h1p_ref, xs_ref.at[pl.ds(0, tm), :], sem).wait()


def _dispatch_call(pend, dest, h1p, n_rows):
    T, W = h1p.shape
    tm = DISP_TM
    return pl.pallas_call(
        _dispatch_kernel,
        grid_spec=pltpu.PrefetchScalarGridSpec(
            num_scalar_prefetch=1,
            grid=(T // tm,),
            in_specs=[
                pl.BlockSpec((2, tm), lambda i, pend: (0, i), memory_space=pltpu.SMEM),
                pl.BlockSpec((tm, W), lambda i, pend: (i, 0)),
            ],
            out_specs=pl.BlockSpec(memory_space=pl.ANY),
            scratch_shapes=[pltpu.VMEM((MOE_BM, W), h1p.dtype), pltpu.SemaphoreType.DMA],
        ),
        out_shape=jax.ShapeDtypeStruct((n_rows, W), h1p.dtype),
        compiler_params=pltpu.CompilerParams(
            dimension_semantics=("arbitrary",), has_side_effects=True),
        name="moe_dispatch",
    )(pend, dest, h1p)


def _expert_kernel(be_ref, nu_ref, xs_ref, wg_ref, wu_ref, wd_ref, ys_ref, wgb_ref, wub_ref, wdb_ref):
    i = pl.program_id(0)
    used = i < nu_ref[0]

    @pl.when(used & ((i == 0) | (be_ref[i] != be_ref[jnp.maximum(i - 1, 0)])))
    def _():
        wgb_ref[...] = wg_ref[...].astype(BF16)
        wub_ref[...] = wu_ref[...].astype(BF16)
        wdb_ref[...] = wd_ref[...].astype(BF16)

    @pl.when(used)
    def _():
        xb = _unpack_bf16_pairs(xs_ref[...])
        g = _dot(xb, wgb_ref[...])
        u = _dot(xb, wub_ref[...])
        h = (g * jax.nn.sigmoid(g) * u).astype(BF16)
        ys_ref[...] = _dot(h, wdb_ref[...])

    @pl.when(i >= nu_ref[0])
    def _():
        ys_ref[...] = jnp.zeros(ys_ref.shape, F32)


def _expert_call(block_expert, n_used, xs, wg, wu, wd):
    P = xs.shape[0]
    bm = MOE_BM
    nb = P // bm
    xs_map = lambda i, be, nu: (jnp.minimum(i, nu[0] - 1), 0)
    w_map = lambda i, be, nu: (be[i], 0, 0)
    return pl.pallas_call(
        _expert_kernel,
        grid_spec=pltpu.PrefetchScalarGridSpec(
            num_scalar_prefetch=2,
            grid=(nb,),
            in_specs=[
                pl.BlockSpec((bm, D_MODEL // 2), xs_map),
                pl.BlockSpec((None, D_MODEL, EXPERT_FF), w_map),
                pl.BlockSpec((None, D_MODEL, EXPERT_FF), w_map),
                pl.BlockSpec((None, EXPERT_FF, D_MODEL), w_map),
            ],
            out_specs=pl.BlockSpec((bm, D_MODEL), lambda i, be, nu: (i, 0)),
            scratch_shapes=[
                pltpu.VMEM((D_MODEL, EXPERT_FF), BF16),
                pltpu.VMEM((D_MODEL, EXPERT_FF), BF16),
                pltpu.VMEM((EXPERT_FF, D_MODEL), BF16),
            ],
        ),
        out_shape=jax.ShapeDtypeStruct((P, D_MODEL), F32),
        compiler_params=pltpu.CompilerParams(
            dimension_semantics=("arbitrary",),
            vmem_limit_bytes=VMEM_LIMIT),
        name="moe_experts",
    )(block_expert, n_used, xs, wg, wu, wd)


def _combine_kernel(dest_ref, dest_next_ref, ys_ref, h1_ref, rf_ref, g2_ref, b2_ref, out_ref, buf, sem):
    tm = h1_ref.shape[0]
    i = pl.program_id(0)
    n = pl.num_programs(0)
    slot = i % 2

    def issue_tile(d_ref, s):
        for r in range(tm):
            for k in range(2):
                pltpu.make_async_copy(ys_ref.at[pl.ds(d_ref[k, r], 1), :],
                                      buf.at[s, k, pl.ds(r, 1), :], sem.at[s]).start(priority=k)

    @pl.when(i == 0)
    def _():
        issue_tile(dest_ref, 0)

    @pl.when(i + 1 < n)
    def _():
        issue_tile(dest_next_ref, 1 - slot)

    for k in range(2):
        pltpu.make_async_copy(ys_ref.at[pl.ds(0, tm), :], buf.at[slot, k], sem.at[slot]).wait()

    rf = rf_ref[...]
    ffn = rf[:, 0:1] * buf[slot, 0] + rf[:, 1:2] * buf[slot, 1]
    z = DEEPNORM_ALPHA * h1_ref[...] + ffn
    mu = jnp.mean(z, axis=-1, keepdims=True)
    zc = z - mu
    var = jnp.mean(zc * zc, axis=-1, keepdims=True)
    out_ref[...] = zc * lax.rsqrt(var + LN_EPS) * g2_ref[...] + b2_ref[...]


def _combine_call(dest, ys, h1, rf, g2, b2):
    T, D = h1.shape
    tm = COMB_TM
    const = lambda i: (0, 0)
    row = lambda i: (i, 0)
    return pl.pallas_call(
        _combine_kernel,
        grid=(T // tm,),
        in_specs=[
            pl.BlockSpec((2, tm), lambda i: (0, i), memory_space=pltpu.SMEM),
            pl.BlockSpec((2, tm), lambda i: (0, jnp.minimum(i + 1, T // tm - 1)), memory_space=pltpu.SMEM),
            pl.BlockSpec(memory_space=pl.ANY),
            pl.BlockSpec((tm, D), row),
            pl.BlockSpec((tm, LANES), row),
            pl.BlockSpec(g2.shape, const),
            pl.BlockSpec(b2.shape, const),
        ],
        out_specs=pl.BlockSpec((tm, D), row),
        out_shape=jax.ShapeDtypeStruct((T, D), F32),
        scratch_shapes=[
            pltpu.VMEM((2, 2, tm, D), F32),
            pltpu.SemaphoreType.DMA((2,)),
        ],
        compiler_params=pltpu.CompilerParams(
            dimension_semantics=("arbitrary",),
            vmem_limit_bytes=VMEM_LIMIT),
        name="moe_combine_ln",
    )(dest, dest, ys, h1, rf, g2, b2)


def kernel(x, positions, w_in, b_gate, lambda_q1, lambda_k1, lambda_q2, lambda_k2, subln_g, w_o_att, conv_w, w_o_conv, w_mix_out, ln1_g, ln1_b, w_router_group, b_router_group, w_router_expert, b_router_expert, w_exp_gate, w_exp_up, w_exp_down, ln2_g, ln2_b):
    B, S, D = x.shape
    T = B * S
    l = 0
    lambda_init = 0.8 - 0.6 * math.exp(-0.3 * l)

    inv_freq = ROPE_THETA ** (-jnp.arange(0, HEAD_DIM, 2, dtype=F32) / HEAD_DIM)
    ang = positions.astype(F32)[..., None] * inv_freq
    cos_t = jnp.tile(jnp.cos(ang), (1, 1, 4))
    sin_h = jnp.sin(ang)
    sin_t = jnp.tile(jnp.concatenate([-sin_h, sin_h], -1), (1, 1, 2))

    n_main = 2 * QK_W + QK_W + 3 * CONV_W
    w0 = w_in[l]
    w_main = w0[:, :n_main].astype(BF16)
    wv_t = w0[:, 2 * QK_W:3 * QK_W].T.astype(BF16)
    w_gates = w0[:, n_main:].astype(BF16)

    q, k, vt, u = _proj_call(x, w_main, wv_t, cos_t, sin_t, conv_w[l])

    lam = (jnp.exp(jnp.sum(lambda_q1[l].astype(F32) * lambda_k1[l].astype(F32)))
           - jnp.exp(jnp.sum(lambda_q2[l].astype(F32) * lambda_k2[l].astype(F32)))
           + lambda_init).reshape(1)
    o_n = _attn_call(lam, q, k, vt, subln_g[l].astype(F32).reshape(HEAD_W, 1), 1.0 - lambda_init)

    wr = jnp.zeros((D, LANES), F32)
    wr = wr.at[:, 0:N_GROUPS].set(w_router_group[l]).at[:, SUBLANES:SUBLANES + N_EXPERTS].set(w_router_expert[l])
    br = jnp.zeros((1, LANES), F32)
    br = br.at[0, 0:N_GROUPS].set(b_router_group[l]).at[0, SUBLANES:SUBLANES + N_EXPERTS].set(b_router_expert[l])

    h1, h1p, ri, rf, cnt = _mix_call(
        x.reshape(T, D), o_n.reshape(T, QK_W), u.reshape(T, CONV_W),
        w_gates, b_gate[l].reshape(1, -1).astype(F32),
        w_o_att[l].astype(BF16), w_o_conv[l].astype(BF16), w_mix_out[l].astype(BF16),
        ln1_g[l].reshape(1, D).astype(F32), ln1_b[l].reshape(1, D).astype(F32),
        wr.astype(BF16), br)

    bm = MOE_BM
    counts = cnt[:, 0]
    padded = (counts + bm - 1) // bm * bm
    pend = jnp.cumsum(padded)
    pstart = pend - padded
    onehot = (ri[0:2, :, None] == jnp.arange(N_EXPERTS, dtype=jnp.int32)).astype(jnp.int32)
    dest = jnp.sum(onehot * pstart, axis=-1) + ri[2:4]
    A = 2 * T
    P = A + N_EXPERTS * bm
    nb = P // bm
    blk_start = jnp.arange(nb, dtype=jnp.int32) * bm
    block_expert = jnp.minimum(
        jnp.sum((pend[None, :] <= blk_start[:, None]).astype(jnp.int32), axis=1), N_EXPERTS - 1)
    n_used = (pend[-1] // bm).astype(jnp.int32).reshape(1)

    xs = _dispatch_call(pend.astype(jnp.int32), dest, h1p, P)
    ys = _expert_call(block_expert, n_used, xs, w_exp_gate[l], w_exp_up[l], w_exp_down[l])
    out = _combine_call(dest, ys, h1, rf,
                        ln2_g[l].reshape(1, D).astype(F32), ln2_b[l].reshape(1, D).astype(F32))
    return out.reshape(B, S, D)
```

```python
import functools
import math

import jax
import jax.numpy as jnp
from jax import lax
from jax.experimental import pallas as pl
from jax.experimental.pallas import tpu as pltpu
from jax.experimental.pallas import tpu_sc as plsc

D_MODEL = 1024
ATT_HEADS = 4
HEAD_DIM = 64
HEAD_W = 2 * HEAD_DIM
QK_W = ATT_HEADS * HEAD_W
CONV_W = D_MODEL // 2
CONV_K = 3
N_GROUPS = 4
EPG = 8
N_EXPERTS = N_GROUPS * EPG
EXPERT_FF = D_MODEL // 2
ROPE_THETA = 10000.0
SUBLN_EPS = 1e-5
LN_EPS = 1e-5
DEPTH = 1
DEEPNORM_ALPHA = (2.0 * DEPTH) ** 0.25

LANES = 128
SUBLANES = 8
VMEM_LIMIT = 48 * 1024 * 1024

PROJ_TM = 512
ATT_TQ = 256
ATT_TK = 256
ATT_UNROLL = 4
MIX_TM = 512
MOE_BM = 512
SC_CORES = 2
SC_SUBCORES = 16
SC_WIN = 64
COMB_TM = 256
NEG_BIG = -1e30

F32 = jnp.float32
BF16 = jnp.bfloat16


def _dot(a, b):
    return jnp.dot(a, b, preferred_element_type=F32)


def _dot_nt(a, b):
    return lax.dot_general(a, b, (((1,), (1,)), ((), ())), preferred_element_type=F32)


def _proj_kernel(x_ref, w_ref, wvt_ref, cos_ref, sin_ref, cw_ref,
                 q_ref, k_ref, vt_ref, u_ref, pbuf):
    tm = x_ref.shape[0]
    i = pl.program_id(1)
    xb = x_ref[...].astype(BF16)
    c32 = cos_ref[...]
    s32 = sin_ref[...]
    cos = jnp.concatenate([c32, c32, c32, c32], axis=1)
    sin = jnp.concatenate([-s32, s32, -s32, s32], axis=1)
    lane = lax.broadcasted_iota(jnp.int32, (tm, LANES), 1)
    low_half = (lane % HEAD_DIM) < (HEAD_DIM // 2)

    def rope(t):
        rot = jnp.where(low_half, pltpu.roll(t, LANES - HEAD_DIM // 2, 1),
                        pltpu.roll(t, HEAD_DIM // 2, 1))
        return t * cos + rot * sin

    q = _dot(xb, w_ref[:, 0:QK_W])
    for h in range(ATT_HEADS):
        sl = slice(h * HEAD_W, (h + 1) * HEAD_W)
        q_ref[:, sl] = (rope(q[:, sl]) * (HEAD_DIM ** -0.5 * math.log2(math.e))).astype(BF16)
    k = _dot(xb, w_ref[:, QK_W:2 * QK_W])
    for h in range(ATT_HEADS):
        sl = slice(h * HEAD_W, (h + 1) * HEAD_W)
        k_ref[:, sl] = rope(k[:, sl]).astype(BF16)
    vt_ref[...] = _dot_nt(wvt_ref[...], xb).astype(BF16)

    c0 = 3 * QK_W
    cb = _dot(xb, w_ref[:, c0:c0 + CONV_W])
    cc = _dot(xb, w_ref[:, c0 + CONV_W:c0 + 2 * CONV_W])
    cx = _dot(xb, w_ref[:, c0 + 2 * CONV_W:c0 + 3 * CONV_W])
    p = cc * cx

    @pl.when(i == 0)
    def _():
        pbuf[0:SUBLANES, :] = jnp.zeros((SUBLANES, CONV_W), F32)

    pbuf[SUBLANES:SUBLANES + tm, :] = p
    y = (cw_ref[0:1, :] * pbuf[SUBLANES - 2:SUBLANES - 2 + tm, :]
         + cw_ref[1:2, :] * pbuf[SUBLANES - 1:SUBLANES - 1 + tm, :]
         + cw_ref[2:3, :] * p)
    u_ref[...] = (cb * y).astype(BF16)
    pbuf[0:SUBLANES, :] = pbuf[tm:tm + SUBLANES, :]


def _proj_call(x, w_main, wv_t, cos_t, sin_t, conv_w):
    B, S, D = x.shape
    tm = PROJ_TM
    n_main = w_main.shape[1]
    return pl.pallas_call(
        _proj_kernel,
        grid=(B, S // tm),
        in_specs=[
            pl.BlockSpec((None, tm, D), lambda b, i: (b, i, 0)),
            pl.BlockSpec((D, n_main), lambda b, i: (0, 0)),
            pl.BlockSpec((QK_W, D), lambda b, i: (0, 0)),
            pl.BlockSpec((None, tm, HEAD_DIM // 2), lambda b, i: (b, i, 0)),
            pl.BlockSpec((None, tm, HEAD_DIM // 2), lambda b, i: (b, i, 0)),
            pl.BlockSpec((CONV_K, CONV_W), lambda b, i: (0, 0)),
        ],
        out_specs=[
            pl.BlockSpec((None, tm, QK_W), lambda b, i: (b, i, 0)),
            pl.BlockSpec((None, tm, QK_W), lambda b, i: (b, i, 0)),
            pl.BlockSpec((None, QK_W, tm), lambda b, i: (b, 0, i)),
            pl.BlockSpec((None, tm, CONV_W), lambda b, i: (b, i, 0)),
        ],
        out_shape=[
            jax.ShapeDtypeStruct((B, S, QK_W), BF16),
            jax.ShapeDtypeStruct((B, S, QK_W), BF16),
            jax.ShapeDtypeStruct((B, QK_W, S), BF16),
            jax.ShapeDtypeStruct((B, S, CONV_W), BF16),
        ],
        scratch_shapes=[pltpu.VMEM((tm + 2 * SUBLANES, CONV_W), F32)],
        compiler_params=pltpu.CompilerParams(
            dimension_semantics=("arbitrary", "arbitrary"),
            vmem_limit_bytes=VMEM_LIMIT),
        name="proj_rope_conv",
    )(x, w_main, wv_t, cos_t, sin_t, conv_w)


ACC_ROWS = HEAD_W + 16


def _attn_kernel(lam_ref, q_ref, k_ref, vt_ref, g_ref, o_ref, qzt_ref, *scratch, out_scale):
    nh = ATT_HEADS
    s_refs, p_refs = scratch[0:nh], scratch[nh:2 * nh]
    m_refs, a_refs, acc_refs = scratch[2 * nh:3 * nh], scratch[3 * nh:4 * nh], scratch[4 * nh:5 * nh]
    tq = q_ref.shape[0]
    tk = ATT_TK
    nblk = 2 * tq // LANES
    i = pl.program_id(1)
    lane = lax.broadcasted_iota(jnp.int32, (tq, HEAD_W), 1)
    ones_rows = jnp.ones((ACC_ROWS - HEAD_W, tk), BF16)

    for h in range(nh):
        qf = q_ref[:, h * HEAD_W:(h + 1) * HEAD_W].astype(F32)
        q1 = jnp.where(lane < HEAD_DIM, qf, 0.0)
        q2 = jnp.where(lane >= HEAD_DIM, qf, 0.0)
        qzt_ref[h, :, 0:tq] = q1.T.astype(BF16)
        qzt_ref[h, :, tq:2 * tq] = q2.T.astype(BF16)
        m_refs[h][...] = jnp.full(m_refs[h].shape, NEG_BIG, F32)
        acc_refs[h][...] = jnp.zeros(acc_refs[h].shape, F32)

    def scores(h, k0):
        kj = k_ref[pl.ds(k0, tk), h * HEAD_W:(h + 1) * HEAD_W]
        s_refs[h][:, 0:2 * tq] = _dot(kj, qzt_ref[h])

    def softmax(h, diag=None):
        for c in range(nblk):
            cs = slice(c * LANES, (c + 1) * LANES)
            s = s_refs[h][:, cs]
            if diag is not None:
                key = lax.broadcasted_iota(jnp.int32, (tk, LANES), 0) + diag * tk
                qry = lax.broadcasted_iota(jnp.int32, (tk, LANES), 1) + (c * LANES) % tq
                s = jnp.where(key <= qry, s, NEG_BIG)
            m_old = m_refs[h][:, cs]
            m_new = jnp.maximum(m_old, jnp.max(s, axis=0, keepdims=True))
            a_refs[h][:, cs] = jnp.exp2(m_old - m_new)
            m_refs[h][:, cs] = m_new
            p_refs[h][:, cs] = jnp.exp2(s - m_new).astype(BF16)

    def accumulate(h, k0):
        vj = jnp.concatenate([vt_ref[h * HEAD_W:(h + 1) * HEAD_W, pl.ds(k0, tk)], ones_rows], axis=0)
        acc_refs[h][...] = a_refs[h][...] * acc_refs[h][...] + _dot(vj, p_refs[h][:, 0:2 * tq])

    n_diag = tq // tk
    n_full = n_diag * i

    def tile_start(j):
        return pl.multiple_of(j * tk, tk)

    for h in range(nh):
        scores(h, 0)

    @pl.when(i > 0)
    def _():
        for h in range(nh):
            softmax(h)
            scores(h, tile_start(1))

    def step(j):
        for h in range(nh):
            accumulate(h, tile_start(j - 1))
            softmax(h)
            scores(h, tile_start(j + 1))

    n_steps = jnp.maximum(n_full - 1, 0)
    n_trips = n_steps // ATT_UNROLL

    def body(t, carry):
        for u in range(ATT_UNROLL):
            step(ATT_UNROLL * t + 1 + u)
        return carry

    lax.fori_loop(0, n_trips, body, 0)

    def tail(j, carry):
        step(j)
        return carry

    lax.fori_loop(ATT_UNROLL * n_trips + 1, n_steps + 1, tail, 0)

    @pl.when(i > 0)
    def _():
        for h in range(nh):
            accumulate(h, tile_start(n_full - 1))

    for d in range(n_diag):
        for h in range(nh):
            softmax(h, diag=d)
            if d + 1 < n_diag:
                scores(h, tile_start(n_full + d + 1))
            accumulate(h, tile_start(n_full + d))

    lam = lam_ref[0]
    for h in range(nh):
        acc = acc_refs[h][0:HEAD_W, :]
        l = acc_refs[h][HEAD_W:HEAD_W + 1, :]
        o = acc[:, 0:tq] / l[:, 0:tq] - lam * (acc[:, tq:2 * tq] / l[:, tq:2 * tq])
        ms = jnp.mean(o * o, axis=0, keepdims=True)
        o = o * lax.rsqrt(ms + SUBLN_EPS) * g_ref[...] * out_scale
        o_ref[:, h * HEAD_W:(h + 1) * HEAD_W] = o.T.astype(BF16)


def _attn_call(lam, q, k, vt, g_col, out_scale):
    B, S, _ = q.shape
    tq = ATT_TQ
    assert ATT_TQ % ATT_TK == 0
    kernel = functools.partial(_attn_kernel, out_scale=out_scale)
    return pl.pallas_call(
        kernel,
        grid=(B, S // tq),
        in_specs=[
            pl.BlockSpec(memory_space=pltpu.SMEM),
            pl.BlockSpec((None, tq, QK_W), lambda b, i: (b, i, 0)),
            pl.BlockSpec((None, S, QK_W), lambda b, i: (b, 0, 0), pipeline_mode=pl.Buffered(1)),
            pl.BlockSpec((None, QK_W, S), lambda b, i: (b, 0, 0), pipeline_mode=pl.Buffered(1)),
            pl.BlockSpec((HEAD_W, 1), lambda b, i: (0, 0)),
        ],
        out_specs=pl.BlockSpec((None, tq, QK_W), lambda b, i: (b, i, 0)),
        out_shape=jax.ShapeDtypeStruct((B, S, QK_W), BF16),
        scratch_shapes=(
            [pltpu.VMEM((ATT_HEADS, HEAD_W, 2 * tq), BF16)]
            + [pltpu.VMEM((ATT_TK, 2 * tq + LANES), F32) for _ in range(ATT_HEADS)]
            + [pltpu.VMEM((ATT_TK, 2 * tq + LANES), BF16) for _ in range(ATT_HEADS)]
            + [pltpu.VMEM((1, 2 * tq), F32) for _ in range(ATT_HEADS)]
            + [pltpu.VMEM((1, 2 * tq), F32) for _ in range(ATT_HEADS)]
            + [pltpu.VMEM((ACC_ROWS, 2 * tq), F32) for _ in range(ATT_HEADS)]
        ),
        compiler_params=pltpu.CompilerParams(
            dimension_semantics=("arbitrary", "arbitrary"),
            vmem_limit_bytes=VMEM_LIMIT),
        name="diff_flash_attn",
    )(lam, q, k, vt, g_col)


def _pack_bf16_pairs(h):
    half = h.shape[1] // 2
    hb = h.astype(BF16).astype(F32)
    bits = lax.bitcast_convert_type(hb, jnp.uint32)
    return (bits[:, 0:half] >> 16) | (bits[:, half:] & jnp.uint32(0xFFFF0000))


def _unpack_bf16_pairs(w):
    lo = lax.bitcast_convert_type(w << 16, F32).astype(BF16)
    hi = lax.bitcast_convert_type(w & jnp.uint32(0xFFFF0000), F32).astype(BF16)
    return jnp.concatenate([lo, hi], axis=1)


def _first_index_of(mask, row_f, big):
    return jnp.min(jnp.where(mask, row_f, big), axis=0, keepdims=True)


def _mix_kernel(x_ref, o_ref, u_ref, wg_ref, bg_ref, woa_ref, woc_ref, wmo_ref, g1_ref, b1_ref,
                wr_ref, br_ref,
                h1_ref, h1p_ref, ri_ref, rf_ref, cnt_ref, xs0_ref, carry_ref):
    tm = x_ref.shape[0]
    step = pl.program_id(0)
    xs0_ref[...] = jnp.zeros(xs0_ref.shape, xs0_ref.dtype)

    @pl.when(step == 0)
    def _():
        carry_ref[...] = jnp.zeros(carry_ref.shape, F32)

    x = x_ref[...]
    xb = x.astype(BF16)
    gates = jax.nn.sigmoid(_dot(xb, wg_ref[...]) + bg_ref[...])
    ya = _dot(o_ref[...], woa_ref[...])
    yc = _dot(u_ref[...], woc_ref[...])
    merged = (gates[:, 0:D_MODEL] * ya + gates[:, D_MODEL:] * yc).astype(BF16)
    z = DEEPNORM_ALPHA * x + _dot(merged, wmo_ref[...])
    mu = jnp.mean(z, axis=-1, keepdims=True)
    zc = z - mu
    var = jnp.mean(zc * zc, axis=-1, keepdims=True)
    h1 = zc * lax.rsqrt(var + LN_EPS) * g1_ref[...] + b1_ref[...]
    h1_ref[...] = h1
    h1p_ref[...] = _pack_bf16_pairs(h1)

    logits = _dot(h1.astype(BF16), wr_ref[...]) + br_ref[...]
    lt = logits.T
    row8 = lax.broadcasted_iota(jnp.int32, (SUBLANES, tm), 0).astype(F32)
    gl = jnp.where(row8 < N_GROUPS, lt[0:SUBLANES, :], NEG_BIG)
    gmax = jnp.max(gl, axis=0, keepdims=True)
    gsel = _first_index_of(gl == gmax, row8, float(SUBLANES))
    gw = 1.0 / jnp.sum(jnp.exp(gl - gmax), axis=0, keepdims=True)
    el = lt[SUBLANES:2 * SUBLANES, :]
    for g in range(1, N_GROUPS):
        el = jnp.where(gsel == float(g), lt[(g + 1) * SUBLANES:(g + 2) * SUBLANES, :], el)
    v1 = jnp.max(el, axis=0, keepdims=True)
    i1 = _first_index_of(el == v1, row8, float(EPG))
    el2 = jnp.where(row8 == i1, -jnp.inf, el)
    v2 = jnp.max(el2, axis=0, keepdims=True)
    i2 = _first_index_of(el2 == v2, row8, float(EPG))
    t = jnp.exp(v2 - v1)
    p1 = gw / (1.0 + t)
    p2 = gw * t / (1.0 + t)
    e1 = gsel * float(EPG) + i1
    e2 = gsel * float(EPG) + i2

    rowe = lax.broadcasted_iota(jnp.int32, (N_EXPERTS, tm), 0).astype(F32)
    oh1 = (rowe == e1).astype(F32)
    oh2 = (rowe == e2).astype(F32)
    r_i = lax.broadcasted_iota(jnp.int32, (tm, tm), 0)
    c_i = lax.broadcasted_iota(jnp.int32, (tm, tm), 1)
    upper = jnp.where(r_i < c_i, 1.0, 0.0).astype(BF16)
    cum1 = _dot(oh1.astype(BF16), upper)
    cum2 = _dot(oh2.astype(BF16), upper)
    tot1 = jnp.sum(oh1, axis=1, keepdims=True)
    tot2 = jnp.sum(oh2, axis=1, keepdims=True)
    carry = carry_ref[:, 0:1]
    rank1 = jnp.sum(oh1 * (carry + cum1), axis=0, keepdims=True)
    rank2 = jnp.sum(oh2 * (carry + tot1 + cum2), axis=0, keepdims=True)
    new_carry = carry_ref[...] + (tot1 + tot2)
    carry_ref[...] = new_carry
    cnt_ref[...] = new_carry.astype(jnp.int32)

    ri_ref[...] = jnp.zeros(ri_ref.shape, jnp.int32)
    ri_ref[0:1, :] = e1.astype(jnp.int32)
    ri_ref[1:2, :] = e2.astype(jnp.int32)
    ri_ref[2:3, :] = rank1.astype(jnp.int32)
    ri_ref[3:4, :] = rank2.astype(jnp.int32)

    row128 = lax.broadcasted_iota(jnp.int32, (LANES, tm), 0)
    pw = jnp.where(row128 == 0, p1, jnp.where(row128 == 1, p2, 0.0))
    rf_ref[...] = pw.T


def _mix_call(x2, o_n, u, wg, bg, woa, woc, wmo, g1, b1, wr, br, n_slots):
    T, D = x2.shape
    tm = MIX_TM
    slots_per_step = n_slots // (T // tm)
    assert slots_per_step * (T // tm) == n_slots and slots_per_step % SUBLANES == 0
    const = lambda i: (0, 0)
    row = lambda i: (i, 0)
    return pl.pallas_call(
        _mix_kernel,
        grid=(T // tm,),
        in_specs=[
            pl.BlockSpec((tm, D), row),
            pl.BlockSpec((tm, QK_W), row),
            pl.BlockSpec((tm, CONV_W), row),
            pl.BlockSpec(wg.shape, const),
            pl.BlockSpec(bg.shape, const),
            pl.BlockSpec(woa.shape, const),
            pl.BlockSpec(woc.shape, const),
            pl.BlockSpec(wmo.shape, const),
            pl.BlockSpec(g1.shape, const),
            pl.BlockSpec(b1.shape, const),
            pl.BlockSpec(wr.shape, const),
            pl.BlockSpec(br.shape, const),
        ],
        out_specs=[
            pl.BlockSpec((tm, D), row),
            pl.BlockSpec((tm, D // 2), row),
            pl.BlockSpec((SUBLANES, tm), lambda i: (0, i)),
            pl.BlockSpec((tm, LANES), row),
            pl.BlockSpec((N_EXPERTS, LANES), const),
            pl.BlockSpec((slots_per_step, D // 2), row),
        ],
        out_shape=[
            jax.ShapeDtypeStruct((T, D), F32),
            jax.ShapeDtypeStruct((T, D // 2), jnp.uint32),
            jax.ShapeDtypeStruct((SUBLANES, T), jnp.int32),
            jax.ShapeDtypeStruct((T, LANES), F32),
            jax.ShapeDtypeStruct((N_EXPERTS, LANES), jnp.int32),
            jax.ShapeDtypeStruct((n_slots, D // 2), jnp.uint32),
        ],
        scratch_shapes=[pltpu.VMEM((N_EXPERTS, LANES), F32)],
        compiler_params=pltpu.CompilerParams(
            dimension_semantics=("arbitrary",),
            vmem_limit_bytes=VMEM_LIMIT),
        name="mix_ln_router",
    )(x2, o_n, u, wg, bg, woa, woc, wmo, g1, b1, wr, br)


def _sc_scatter_rows(h1p, dest, xs_ref):
    T, W = h1p.shape
    n_workers = SC_CORES * SC_SUBCORES
    per = T // n_workers
    n_win = per // SC_WIN
    assert n_win * SC_WIN * n_workers == T
    mesh = plsc.VectorSubcoreMesh(core_axis_name="core", subcore_axis_name="subcore")

    @pl.kernel(out_type=(), mesh=mesh,
               scratch_types=[pltpu.VMEM((2, SC_WIN, W), h1p.dtype),
                              pltpu.VMEM((2, SC_WIN), jnp.int32), pltpu.VMEM((2, SC_WIN), jnp.int32),
                              pltpu.SemaphoreType.DMA((2,)), pltpu.SemaphoreType.DMA((2,))],
               name="moe_dispatch_sc")
    def scatter(x_hbm, d0_hbm, d1_hbm, o_hbm, xbuf, i0, i1, load_sem, scat_sem):
        wid = lax.axis_index("core") * SC_SUBCORES + lax.axis_index("subcore")

        def loads(j, s):
            base = wid * per + j * SC_WIN
            return [pltpu.make_async_copy(x_hbm.at[pl.ds(base, SC_WIN)], xbuf.at[s], load_sem.at[s]),
                    pltpu.make_async_copy(d0_hbm.at[pl.ds(base, SC_WIN)], i0.at[s], load_sem.at[s]),
                    pltpu.make_async_copy(d1_hbm.at[pl.ds(base, SC_WIN)], i1.at[s], load_sem.at[s])]

        def scatters(s):
            return [pltpu.make_async_copy(xbuf.at[s], o_hbm.at[i0.at[s]], scat_sem.at[s]),
                    pltpu.make_async_copy(xbuf.at[s], o_hbm.at[i1.at[s]], scat_sem.at[s])]

        for c in loads(0, 0):
            c.start()
        for j in range(n_win):
            s = j % 2
            for c in loads(j, s):
                c.wait()
            for c in scatters(s):
                c.start()
            if j >= 1:
                for c in scatters(1 - s):
                    c.wait()
            if j + 1 < n_win:
                for c in loads(j + 1, 1 - s):
                    c.start()
        for c in scatters((n_win - 1) % 2):
            c.wait()

    scatter(h1p, dest[0], dest[1], xs_ref)


def _expert_kernel(be_ref, nu_ref, xs_ref, wg_ref, wu_ref, wd_ref, ys_ref, wgb_ref, wub_ref, wdb_ref):
    i = pl.program_id(0)
    used = i < nu_ref[0]

    @pl.when(used & ((i == 0) | (be_ref[i] != be_ref[jnp.maximum(i - 1, 0)])))
    def _():
        wgb_ref[...] = wg_ref[...].astype(BF16)
        wub_ref[...] = wu_ref[...].astype(BF16)
        wdb_ref[...] = wd_ref[...].astype(BF16)

    @pl.when(used)
    def _():
        xb = _unpack_bf16_pairs(xs_ref[...])
        g = _dot(xb, wgb_ref[...])
        u = _dot(xb, wub_ref[...])
        h = (g * jax.nn.sigmoid(g) * u).astype(BF16)
        ys_ref[...] = _dot(h, wdb_ref[...])

    @pl.when(i >= nu_ref[0])
    def _():
        ys_ref[...] = jnp.zeros(ys_ref.shape, F32)


def _expert_call(block_expert, n_used, xs, wg, wu, wd):
    P = xs.shape[0]
    bm = MOE_BM
    nb = P // bm
    xs_map = lambda i, be, nu: (jnp.minimum(i, nu[0] - 1), 0)
    w_map = lambda i, be, nu: (be[i], 0, 0)
    return pl.pallas_call(
        _expert_kernel,
        grid_spec=pltpu.PrefetchScalarGridSpec(
            num_scalar_prefetch=2,
            grid=(nb,),
            in_specs=[
                pl.BlockSpec((bm, D_MODEL // 2), xs_map),
                pl.BlockSpec((None, D_MODEL, EXPERT_FF), w_map),
                pl.BlockSpec((None, D_MODEL, EXPERT_FF), w_map),
                pl.BlockSpec((None, EXPERT_FF, D_MODEL), w_map),
            ],
            out_specs=pl.BlockSpec((bm, D_MODEL), lambda i, be, nu: (i, 0)),
            scratch_shapes=[
                pltpu.VMEM((D_MODEL, EXPERT_FF), BF16),
                pltpu.VMEM((D_MODEL, EXPERT_FF), BF16),
                pltpu.VMEM((EXPERT_FF, D_MODEL), BF16),
            ],
        ),
        out_shape=jax.ShapeDtypeStruct((P, D_MODEL), F32),
        compiler_params=pltpu.CompilerParams(
            dimension_semantics=("arbitrary",),
            vmem_limit_bytes=VMEM_LIMIT),
        name="moe_experts",
    )(block_expert, n_used, xs, wg, wu, wd)


def _combine_kernel(dest_ref, dest_next_ref, ys_ref, h1_ref, rf_ref, g2_ref, b2_ref, out_ref, buf, sem):
    tm = h1_ref.shape[0]
    i = pl.program_id(0)
    n = pl.num_programs(0)
    slot = i % 2

    def issue_tile(d_ref, s):
        for r in range(tm):
            for k in range(2):
                pltpu.make_async_copy(ys_ref.at[pl.ds(d_ref[k, r], 1), :],
                                      buf.at[s, k, pl.ds(r, 1), :], sem.at[s]).start(priority=k)

    @pl.when(i == 0)
    def _():
        issue_tile(dest_ref, 0)

    @pl.when(i + 1 < n)
    def _():
        issue_tile(dest_next_ref, 1 - slot)

    for k in range(2):
        pltpu.make_async_copy(ys_ref.at[pl.ds(0, tm), :], buf.at[slot, k], sem.at[slot]).wait()

    rf = rf_ref[...]
    ffn = rf[:, 0:1] * buf[slot, 0] + rf[:, 1:2] * buf[slot, 1]
    z = DEEPNORM_ALPHA * h1_ref[...] + ffn
    mu = jnp.mean(z, axis=-1, keepdims=True)
    zc = z - mu
    var = jnp.mean(zc * zc, axis=-1, keepdims=True)
    out_ref[...] = zc * lax.rsqrt(var + LN_EPS) * g2_ref[...] + b2_ref[...]


def _combine_call(dest, ys, h1, rf, g2, b2):
    T, D = h1.shape
    tm = COMB_TM
    const = lambda i: (0, 0)
    row = lambda i: (i, 0)
    return pl.pallas_call(
        _combine_kernel,
        grid=(T // tm,),
        in_specs=[
            pl.BlockSpec((2, tm), lambda i: (0, i), memory_space=pltpu.SMEM),
            pl.BlockSpec((2, tm), lambda i: (0, jnp.minimum(i + 1, T // tm - 1)), memory_space=pltpu.SMEM),
            pl.BlockSpec(memory_space=pl.ANY),
            pl.BlockSpec((tm, D), row),
            pl.BlockSpec((tm, LANES), row),
            pl.BlockSpec(g2.shape, const),
            pl.BlockSpec(b2.shape, const),
        ],
        out_specs=pl.BlockSpec((tm, D), row),
        out_shape=jax.ShapeDtypeStruct((T, D), F32),
        scratch_shapes=[
            pltpu.VMEM((2, 2, tm, D), F32),
            pltpu.SemaphoreType.DMA((2,)),
        ],
        compiler_params=pltpu.CompilerParams(
            dimension_semantics=("arbitrary",),
            vmem_limit_bytes=VMEM_LIMIT),
        name="moe_combine_ln",
    )(dest, dest, ys, h1, rf, g2, b2)


def kernel(x, positions, w_in, b_gate, lambda_q1, lambda_k1, lambda_q2, lambda_k2, subln_g, w_o_att, conv_w, w_o_conv, w_mix_out, ln1_g, ln1_b, w_router_group, b_router_group, w_router_expert, b_router_expert, w_exp_gate, w_exp_up, w_exp_down, ln2_g, ln2_b):
    B, S, D = x.shape
    T = B * S
    l = 0
    lambda_init = 0.8 - 0.6 * math.exp(-0.3 * l)

    inv_freq = ROPE_THETA ** (-jnp.arange(0, HEAD_DIM, 2, dtype=F32) / HEAD_DIM)
    ang = positions.astype(F32)[..., None] * inv_freq
    cos_t, sin_t = jnp.cos(ang), jnp.sin(ang)

    n_main = 2 * QK_W + QK_W + 3 * CONV_W
    w0 = w_in[l]
    w_main = w0[:, :n_main].astype(BF16)
    wv_t = w0[:, 2 * QK_W:3 * QK_W].T.astype(BF16)
    w_gates = w0[:, n_main:].astype(BF16)

    q, k, vt, u = _proj_call(x, w_main, wv_t, cos_t, sin_t, conv_w[l])

    lam = (jnp.exp(jnp.sum(lambda_q1[l].astype(F32) * lambda_k1[l].astype(F32)))
           - jnp.exp(jnp.sum(lambda_q2[l].astype(F32) * lambda_k2[l].astype(F32)))
           + lambda_init).reshape(1)
    o_n = _attn_call(lam, q, k, vt, subln_g[l].astype(F32).reshape(HEAD_W, 1), 1.0 - lambda_init)

    wr = jnp.zeros((D, LANES), F32)
    wr = wr.at[:, 0:N_GROUPS].set(w_router_group[l]).at[:, SUBLANES:SUBLANES + N_EXPERTS].set(w_router_expert[l])
    br = jnp.zeros((1, LANES), F32)
    br = br.at[0, 0:N_GROUPS].set(b_router_group[l]).at[0, SUBLANES:SUBLANES + N_EXPERTS].set(b_router_expert[l])

    bm = MOE_BM
    A = 2 * T
    P = A + N_EXPERTS * bm
    nb = P // bm
    h1, h1p, ri, rf, cnt, xs0 = _mix_call(
        x.reshape(T, D), o_n.reshape(T, QK_W), u.reshape(T, CONV_W),
        w_gates, b_gate[l].reshape(1, -1).astype(F32),
        w_o_att[l].astype(BF16), w_o_conv[l].astype(BF16), w_mix_out[l].astype(BF16),
        ln1_g[l].reshape(1, D).astype(F32), ln1_b[l].reshape(1, D).astype(F32),
        wr.astype(BF16), br, P)

    counts = cnt[:, 0]
    padded = (counts + bm - 1) // bm * bm
    pend = jnp.cumsum(padded)
    pstart = pend - padded
    onehot = (ri[0:2, :, None] == jnp.arange(N_EXPERTS, dtype=jnp.int32)).astype(jnp.int32)
    dest = jnp.sum(onehot * pstart, axis=-1) + ri[2:4]
    blk_start = jnp.arange(nb, dtype=jnp.int32) * bm
    block_expert = jnp.minimum(
        jnp.sum((pend[None, :] <= blk_start[:, None]).astype(jnp.int32), axis=1), N_EXPERTS - 1)
    n_used = (pend[-1] // bm).astype(jnp.int32).reshape(1)

    xs_ref = jax.new_ref(xs0)
    _sc_scatter_rows(h1p, dest, xs_ref)
    xs = xs_ref[...]
    ys = _expert_call(block_expert, n_used, xs, w_exp_gate[l], w_exp_up[l], w_exp_down[l])
    out = _combine_call(dest, ys, h1, rf,
                        ln2_g[l].reshape(1, D).astype(F32), ln2_b[l].reshape(1, D).astype(F32))
    return out.reshape(B, S, D)
```

```python
import functools
import math

import jax
import jax.numpy as jnp
from jax import lax
from jax.experimental import pallas as pl
from jax.experimental.pallas import tpu as pltpu
from jax.experimental.pallas import tpu_sc as plsc

D_MODEL = 1024
ATT_HEADS = 4
HEAD_DIM = 64
HEAD_W = 2 * HEAD_DIM
QK_W = ATT_HEADS * HEAD_W
CONV_W = D_MODEL // 2
CONV_K = 3
N_GROUPS = 4
EPG = 8
N_EXPERTS = N_GROUPS * EPG
EXPERT_FF = D_MODEL // 2
ROPE_THETA = 10000.0
SUBLN_EPS = 1e-5
LN_EPS = 1e-5
DEPTH = 1
DEEPNORM_ALPHA = (2.0 * DEPTH) ** 0.25

LANES = 128
SUBLANES = 8
VMEM_LIMIT = 48 * 1024 * 1024

PROJ_TM = 512
ATT_TQ = 256
ATT_TK = 256
ATT_UNROLL = 4
MIX_TM = 512
MOE_BM = 512
SC_CORES = 2
SC_SUBCORES = 16
SC_WIN = 64
COMB_TM = 512
NEG_BIG = -1e30

F32 = jnp.float32
BF16 = jnp.bfloat16


def _dot(a, b):
    return jnp.dot(a, b, preferred_element_type=F32)


def _dot_nt(a, b):
    return lax.dot_general(a, b, (((1,), (1,)), ((), ())), preferred_element_type=F32)


def _proj_kernel(x_ref, w_ref, wvt_ref, cos_ref, sin_ref, cw_ref,
                 q_ref, k_ref, vt_ref, u_ref, pbuf):
    tm = x_ref.shape[0]
    i = pl.program_id(1)
    xb = x_ref[...].astype(BF16)
    c32 = cos_ref[...]
    s32 = sin_ref[...]
    cos = jnp.concatenate([c32, c32, c32, c32], axis=1)
    sin = jnp.concatenate([-s32, s32, -s32, s32], axis=1)
    lane = lax.broadcasted_iota(jnp.int32, (tm, LANES), 1)
    low_half = (lane % HEAD_DIM) < (HEAD_DIM // 2)

    def rope(t):
        rot = jnp.where(low_half, pltpu.roll(t, LANES - HEAD_DIM // 2, 1),
                        pltpu.roll(t, HEAD_DIM // 2, 1))
        return t * cos + rot * sin

    q = _dot(xb, w_ref[:, 0:QK_W])
    for h in range(ATT_HEADS):
        sl = slice(h * HEAD_W, (h + 1) * HEAD_W)
        q_ref[:, sl] = (rope(q[:, sl]) * (HEAD_DIM ** -0.5 * math.log2(math.e))).astype(BF16)
    k = _dot(xb, w_ref[:, QK_W:2 * QK_W])
    for h in range(ATT_HEADS):
        sl = slice(h * HEAD_W, (h + 1) * HEAD_W)
        k_ref[:, sl] = rope(k[:, sl]).astype(BF16)
    vt_ref[...] = _dot_nt(wvt_ref[...], xb).astype(BF16)

    c0 = 3 * QK_W
    cb = _dot(xb, w_ref[:, c0:c0 + CONV_W])
    cc = _dot(xb, w_ref[:, c0 + CONV_W:c0 + 2 * CONV_W])
    cx = _dot(xb, w_ref[:, c0 + 2 * CONV_W:c0 + 3 * CONV_W])
    p = cc * cx

    @pl.when(i == 0)
    def _():
        pbuf[0:SUBLANES, :] = jnp.zeros((SUBLANES, CONV_W), F32)

    pbuf[SUBLANES:SUBLANES + tm, :] = p
    y = (cw_ref[0:1, :] * pbuf[SUBLANES - 2:SUBLANES - 2 + tm, :]
         + cw_ref[1:2, :] * pbuf[SUBLANES - 1:SUBLANES - 1 + tm, :]
         + cw_ref[2:3, :] * p)
    u_ref[...] = (cb * y).astype(BF16)
    pbuf[0:SUBLANES, :] = pbuf[tm:tm + SUBLANES, :]


def _proj_call(x, w_main, wv_t, cos_t, sin_t, conv_w):
    B, S, D = x.shape
    tm = PROJ_TM
    n_main = w_main.shape[1]
    return pl.pallas_call(
        _proj_kernel,
        grid=(B, S // tm),
        in_specs=[
            pl.BlockSpec((None, tm, D), lambda b, i: (b, i, 0)),
            pl.BlockSpec((D, n_main), lambda b, i: (0, 0)),
            pl.BlockSpec((QK_W, D), lambda b, i: (0, 0)),
            pl.BlockSpec((None, tm, HEAD_DIM // 2), lambda b, i: (b, i, 0)),
            pl.BlockSpec((None, tm, HEAD_DIM // 2), lambda b, i: (b, i, 0)),
            pl.BlockSpec((CONV_K, CONV_W), lambda b, i: (0, 0)),
        ],
        out_specs=[
            pl.BlockSpec((None, tm, QK_W), lambda b, i: (b, i, 0)),
            pl.BlockSpec((None, tm, QK_W), lambda b, i: (b, i, 0)),
            pl.BlockSpec((None, QK_W, tm), lambda b, i: (b, 0, i)),
            pl.BlockSpec((None, tm, CONV_W), lambda b, i: (b, i, 0)),
        ],
        out_shape=[
            jax.ShapeDtypeStruct((B, S, QK_W), BF16),
            jax.ShapeDtypeStruct((B, S, QK_W), BF16),
            jax.ShapeDtypeStruct((B, QK_W, S), BF16),
            jax.ShapeDtypeStruct((B, S, CONV_W), BF16),
        ],
        scratch_shapes=[pltpu.VMEM((tm + 2 * SUBLANES, CONV_W), F32)],
        compiler_params=pltpu.CompilerParams(
            dimension_semantics=("arbitrary", "arbitrary"),
            vmem_limit_bytes=VMEM_LIMIT),
        name="proj_rope_conv",
    )(x, w_main, wv_t, cos_t, sin_t, conv_w)


ACC_ROWS = HEAD_W + 16


def _attn_kernel(lam_ref, q_ref, k_ref, vt_ref, g_ref, o_ref, qzt_ref, *scratch, out_scale):
    nh = ATT_HEADS
    s_refs, p_refs = scratch[0:nh], scratch[nh:2 * nh]
    m_refs, a_refs, acc_refs = scratch[2 * nh:3 * nh], scratch[3 * nh:4 * nh], scratch[4 * nh:5 * nh]
    tq = q_ref.shape[0]
    tk = ATT_TK
    nblk = 2 * tq // LANES
    i = pl.program_id(1)
    lane = lax.broadcasted_iota(jnp.int32, (tq, HEAD_W), 1)
    ones_rows = jnp.ones((ACC_ROWS - HEAD_W, tk), BF16)

    for h in range(nh):
        qf = q_ref[:, h * HEAD_W:(h + 1) * HEAD_W].astype(F32)
        q1 = jnp.where(lane < HEAD_DIM, qf, 0.0)
        q2 = jnp.where(lane >= HEAD_DIM, qf, 0.0)
        qzt_ref[h, :, 0:tq] = q1.T.astype(BF16)
        qzt_ref[h, :, tq:2 * tq] = q2.T.astype(BF16)
        m_refs[h][...] = jnp.full(m_refs[h].shape, NEG_BIG, F32)
        acc_refs[h][...] = jnp.zeros(acc_refs[h].shape, F32)

    def scores(h, k0):
        kj = k_ref[pl.ds(k0, tk), h * HEAD_W:(h + 1) * HEAD_W]
        s_refs[h][:, 0:2 * tq] = _dot(kj, qzt_ref[h])

    def softmax(h, diag=None):
        for c in range(nblk):
            cs = slice(c * LANES, (c + 1) * LANES)
            s = s_refs[h][:, cs]
            if diag is not None:
                key = lax.broadcasted_iota(jnp.int32, (tk, LANES), 0) + diag * tk
                qry = lax.broadcasted_iota(jnp.int32, (tk, LANES), 1) + (c * LANES) % tq
                s = jnp.where(key <= qry, s, NEG_BIG)
            m_old = m_refs[h][:, cs]
            m_new = jnp.maximum(m_old, jnp.max(s, axis=0, keepdims=True))
            a_refs[h][:, cs] = jnp.exp2(m_old - m_new)
            m_refs[h][:, cs] = m_new
            p_refs[h][:, cs] = jnp.exp2(s - m_new).astype(BF16)

    def accumulate(h, k0):
        vj = jnp.concatenate([vt_ref[h * HEAD_W:(h + 1) * HEAD_W, pl.ds(k0, tk)], ones_rows], axis=0)
        acc_refs[h][...] = a_refs[h][...] * acc_refs[h][...] + _dot(vj, p_refs[h][:, 0:2 * tq])

    n_diag = tq // tk
    n_full = n_diag * i

    def tile_start(j):
        return pl.multiple_of(j * tk, tk)

    for h in range(nh):
        scores(h, 0)

    @pl.when(i > 0)
    def _():
        for h in range(nh):
            softmax(h)
            scores(h, tile_start(1))

    def step(j):
        for h in range(nh):
            accumulate(h, tile_start(j - 1))
            softmax(h)
            scores(h, tile_start(j + 1))

    n_steps = jnp.maximum(n_full - 1, 0)
    n_trips = n_steps // ATT_UNROLL

    def body(t, carry):
        for u in range(ATT_UNROLL):
            step(ATT_UNROLL * t + 1 + u)
        return carry

    lax.fori_loop(0, n_trips, body, 0)

    def tail(j, carry):
        step(j)
        return carry

    lax.fori_loop(ATT_UNROLL * n_trips + 1, n_steps + 1, tail, 0)

    @pl.when(i > 0)
    def _():
        for h in range(nh):
            accumulate(h, tile_start(n_full - 1))

    for d in range(n_diag):
        for h in range(nh):
            softmax(h, diag=d)
            if d + 1 < n_diag:
                scores(h, tile_start(n_full + d + 1))
            accumulate(h, tile_start(n_full + d))

    lam = lam_ref[0]
    for h in range(nh):
        acc = acc_refs[h][0:HEAD_W, :]
        l = acc_refs[h][HEAD_W:HEAD_W + 1, :]
        o = acc[:, 0:tq] / l[:, 0:tq] - lam * (acc[:, tq:2 * tq] / l[:, tq:2 * tq])
        ms = jnp.mean(o * o, axis=0, keepdims=True)
        o = o * lax.rsqrt(ms + SUBLN_EPS) * g_ref[...] * out_scale
        o_ref[:, h * HEAD_W:(h + 1) * HEAD_W] = o.T.astype(BF16)


def _attn_call(lam, q, k, vt, g_col, out_scale):
    B, S, _ = q.shape
    tq = ATT_TQ
    assert ATT_TQ % ATT_TK == 0
    kernel = functools.partial(_attn_kernel, out_scale=out_scale)
    return pl.pallas_call(
        kernel,
        grid=(B, S // tq),
        in_specs=[
            pl.BlockSpec(memory_space=pltpu.SMEM),
            pl.BlockSpec((None, tq, QK_W), lambda b, i: (b, i, 0)),
            pl.BlockSpec((None, S, QK_W), lambda b, i: (b, 0, 0), pipeline_mode=pl.Buffered(1)),
            pl.BlockSpec((None, QK_W, S), lambda b, i: (b, 0, 0), pipeline_mode=pl.Buffered(1)),
            pl.BlockSpec((HEAD_W, 1), lambda b, i: (0, 0)),
        ],
        out_specs=pl.BlockSpec((None, tq, QK_W), lambda b, i: (b, i, 0)),
        out_shape=jax.ShapeDtypeStruct((B, S, QK_W), BF16),
        scratch_shapes=(
            [pltpu.VMEM((ATT_HEADS, HEAD_W, 2 * tq), BF16)]
            + [pltpu.VMEM((ATT_TK, 2 * tq + LANES), F32) for _ in range(ATT_HEADS)]
            + [pltpu.VMEM((ATT_TK, 2 * tq + LANES), BF16) for _ in range(ATT_HEADS)]
            + [pltpu.VMEM((1, 2 * tq), F32) for _ in range(ATT_HEADS)]
            + [pltpu.VMEM((1, 2 * tq), F32) for _ in range(ATT_HEADS)]
            + [pltpu.VMEM((ACC_ROWS, 2 * tq), F32) for _ in range(ATT_HEADS)]
        ),
        compiler_params=pltpu.CompilerParams(
            dimension_semantics=("arbitrary", "arbitrary"),
            vmem_limit_bytes=VMEM_LIMIT),
        name="diff_flash_attn",
    )(lam, q, k, vt, g_col)


def _pack_bf16_pairs(h):
    half = h.shape[1] // 2
    hb = h.astype(BF16).astype(F32)
    bits = lax.bitcast_convert_type(hb, jnp.uint32)
    return (bits[:, 0:half] >> 16) | (bits[:, half:] & jnp.uint32(0xFFFF0000))


def _unpack_bf16_pairs(w):
    lo = lax.bitcast_convert_type(w << 16, F32).astype(BF16)
    hi = lax.bitcast_convert_type(w & jnp.uint32(0xFFFF0000), F32).astype(BF16)
    return jnp.concatenate([lo, hi], axis=1)


def _first_index_of(mask, row_f, big):
    return jnp.min(jnp.where(mask, row_f, big), axis=0, keepdims=True)


def _mix_kernel(x_ref, o_ref, u_ref, wg_ref, bg_ref, woa_ref, woc_ref, wmo_ref, g1_ref, b1_ref,
                wr_ref, br_ref,
                h1_ref, h1p_ref, ri_ref, rf_ref, cnt_ref, xs0_ref, carry_ref):
    tm = x_ref.shape[0]
    step = pl.program_id(0)
    xs0_ref[...] = jnp.zeros(xs0_ref.shape, xs0_ref.dtype)

    @pl.when(step == 0)
    def _():
        carry_ref[...] = jnp.zeros(carry_ref.shape, F32)

    x = x_ref[...]
    xb = x.astype(BF16)
    gates = jax.nn.sigmoid(_dot(xb, wg_ref[...]) + bg_ref[...])
    ya = _dot(o_ref[...], woa_ref[...])
    yc = _dot(u_ref[...], woc_ref[...])
    merged = (gates[:, 0:D_MODEL] * ya + gates[:, D_MODEL:] * yc).astype(BF16)
    z = DEEPNORM_ALPHA * x + _dot(merged, wmo_ref[...])
    mu = jnp.mean(z, axis=-1, keepdims=True)
    zc = z - mu
    var = jnp.mean(zc * zc, axis=-1, keepdims=True)
    h1 = zc * lax.rsqrt(var + LN_EPS) * g1_ref[...] + b1_ref[...]
    h1_ref[...] = h1
    h1p_ref[...] = _pack_bf16_pairs(h1)

    logits = _dot(h1.astype(BF16), wr_ref[...]) + br_ref[...]
    lt = logits.T
    row8 = lax.broadcasted_iota(jnp.int32, (SUBLANES, tm), 0).astype(F32)
    gl = jnp.where(row8 < N_GROUPS, lt[0:SUBLANES, :], NEG_BIG)
    gmax = jnp.max(gl, axis=0, keepdims=True)
    gsel = _first_index_of(gl == gmax, row8, float(SUBLANES))
    gw = 1.0 / jnp.sum(jnp.exp(gl - gmax), axis=0, keepdims=True)
    el = lt[SUBLANES:2 * SUBLANES, :]
    for g in range(1, N_GROUPS):
        el = jnp.where(gsel == float(g), lt[(g + 1) * SUBLANES:(g + 2) * SUBLANES, :], el)
    v1 = jnp.max(el, axis=0, keepdims=True)
    i1 = _first_index_of(el == v1, row8, float(EPG))
    el2 = jnp.where(row8 == i1, -jnp.inf, el)
    v2 = jnp.max(el2, axis=0, keepdims=True)
    i2 = _first_index_of(el2 == v2, row8, float(EPG))
    t = jnp.exp(v2 - v1)
    p1 = gw / (1.0 + t)
    p2 = gw * t / (1.0 + t)
    e1 = gsel * float(EPG) + i1
    e2 = gsel * float(EPG) + i2

    rowe = lax.broadcasted_iota(jnp.int32, (N_EXPERTS, tm), 0).astype(F32)
    oh1 = (rowe == e1).astype(F32)
    oh2 = (rowe == e2).astype(F32)
    r_i = lax.broadcasted_iota(jnp.int32, (tm, tm), 0)
    c_i = lax.broadcasted_iota(jnp.int32, (tm, tm), 1)
    upper = jnp.where(r_i < c_i, 1.0, 0.0).astype(BF16)
    cum1 = _dot(oh1.astype(BF16), upper)
    cum2 = _dot(oh2.astype(BF16), upper)
    tot1 = jnp.sum(oh1, axis=1, keepdims=True)
    tot2 = jnp.sum(oh2, axis=1, keepdims=True)
    carry = carry_ref[:, 0:1]
    rank1 = jnp.sum(oh1 * (carry + cum1), axis=0, keepdims=True)
    rank2 = jnp.sum(oh2 * (carry + tot1 + cum2), axis=0, keepdims=True)
    new_carry = carry_ref[...] + (tot1 + tot2)
    carry_ref[...] = new_carry
    cnt_ref[...] = new_carry.astype(jnp.int32)

    ri_ref[...] = jnp.zeros(ri_ref.shape, jnp.int32)
    ri_ref[0:1, :] = e1.astype(jnp.int32)
    ri_ref[1:2, :] = e2.astype(jnp.int32)
    ri_ref[2:3, :] = rank1.astype(jnp.int32)
    ri_ref[3:4, :] = rank2.astype(jnp.int32)

    row128 = lax.broadcasted_iota(jnp.int32, (LANES, tm), 0)
    pw = jnp.where(row128 == 0, p1, jnp.where(row128 == 1, p2, 0.0))
    rf_ref[...] = pw.T


def _mix_call(x2, o_n, u, wg, bg, woa, woc, wmo, g1, b1, wr, br, n_slots):
    T, D = x2.shape
    tm = MIX_TM
    slots_per_step = n_slots // (T // tm)
    assert slots_per_step * (T // tm) == n_slots and slots_per_step % SUBLANES == 0
    const = lambda i: (0, 0)
    row = lambda i: (i, 0)
    return pl.pallas_call(
        _mix_kernel,
        grid=(T // tm,),
        in_specs=[
            pl.BlockSpec((tm, D), row),
            pl.BlockSpec((tm, QK_W), row),
            pl.BlockSpec((tm, CONV_W), row),
            pl.BlockSpec(wg.shape, const),
            pl.BlockSpec(bg.shape, const),
            pl.BlockSpec(woa.shape, const),
            pl.BlockSpec(woc.shape, const),
            pl.BlockSpec(wmo.shape, const),
            pl.BlockSpec(g1.shape, const),
            pl.BlockSpec(b1.shape, const),
            pl.BlockSpec(wr.shape, const),
            pl.BlockSpec(br.shape, const),
        ],
        out_specs=[
            pl.BlockSpec((tm, D), row),
            pl.BlockSpec((tm, D // 2), row),
            pl.BlockSpec((SUBLANES, tm), lambda i: (0, i)),
            pl.BlockSpec((tm, LANES), row),
            pl.BlockSpec((N_EXPERTS, LANES), const),
            pl.BlockSpec((slots_per_step, D // 2), row),
        ],
        out_shape=[
            jax.ShapeDtypeStruct((T, D), F32),
            jax.ShapeDtypeStruct((T, D // 2), jnp.uint32),
            jax.ShapeDtypeStruct((SUBLANES, T), jnp.int32),
            jax.ShapeDtypeStruct((T, LANES), F32),
            jax.ShapeDtypeStruct((N_EXPERTS, LANES), jnp.int32),
            jax.ShapeDtypeStruct((n_slots, D // 2), jnp.uint32),
        ],
        scratch_shapes=[pltpu.VMEM((N_EXPERTS, LANES), F32)],
        compiler_params=pltpu.CompilerParams(
            dimension_semantics=("arbitrary",),
            vmem_limit_bytes=VMEM_LIMIT),
        name="mix_ln_router",
    )(x2, o_n, u, wg, bg, woa, woc, wmo, g1, b1, wr, br)


def _sc_scatter_rows(h1p, dest, xs_ref):
    T, W = h1p.shape
    n_workers = SC_CORES * SC_SUBCORES
    per = T // n_workers
    n_win = per // SC_WIN
    assert n_win * SC_WIN * n_workers == T
    mesh = plsc.VectorSubcoreMesh(core_axis_name="core", subcore_axis_name="subcore")

    @pl.kernel(out_type=(), mesh=mesh,
               scratch_types=[pltpu.VMEM((2, SC_WIN, W), h1p.dtype),
                              pltpu.VMEM((2, SC_WIN), jnp.int32), pltpu.VMEM((2, SC_WIN), jnp.int32),
                              pltpu.SemaphoreType.DMA((2,)), pltpu.SemaphoreType.DMA((2,))],
               name="moe_dispatch_sc")
    def scatter(x_hbm, d0_hbm, d1_hbm, o_hbm, xbuf, i0, i1, load_sem, scat_sem):
        wid = lax.axis_index("core") * SC_SUBCORES + lax.axis_index("subcore")

        def loads(j, s):
            base = wid * per + j * SC_WIN
            return [pltpu.make_async_copy(x_hbm.at[pl.ds(base, SC_WIN)], xbuf.at[s], load_sem.at[s]),
                    pltpu.make_async_copy(d0_hbm.at[pl.ds(base, SC_WIN)], i0.at[s], load_sem.at[s]),
                    pltpu.make_async_copy(d1_hbm.at[pl.ds(base, SC_WIN)], i1.at[s], load_sem.at[s])]

        def scatters(s):
            return [pltpu.make_async_copy(xbuf.at[s], o_hbm.at[i0.at[s]], scat_sem.at[s]),
                    pltpu.make_async_copy(xbuf.at[s], o_hbm.at[i1.at[s]], scat_sem.at[s])]

        for c in loads(0, 0):
            c.start()
        for j in range(n_win):
            s = j % 2
            for c in loads(j, s):
                c.wait()
            for c in scatters(s):
                c.start()
            if j >= 1:
                for c in scatters(1 - s):
                    c.wait()
            if j + 1 < n_win:
                for c in loads(j + 1, 1 - s):
                    c.start()
        for c in scatters((n_win - 1) % 2):
            c.wait()

    scatter(h1p, dest[0], dest[1], xs_ref)


def _expert_kernel(be_ref, nu_ref, xs_ref, wg_ref, wu_ref, wd_ref, ys_ref, wgb_ref, wub_ref, wdb_ref):
    i = pl.program_id(0)
    used = i < nu_ref[0]

    @pl.when(used & ((i == 0) | (be_ref[i] != be_ref[jnp.maximum(i - 1, 0)])))
    def _():
        wgb_ref[...] = wg_ref[...].astype(BF16)
        wub_ref[...] = wu_ref[...].astype(BF16)
        wdb_ref[...] = wd_ref[...].astype(BF16)

    @pl.when(used)
    def _():
        xb = _unpack_bf16_pairs(xs_ref[...])
        g = _dot(xb, wgb_ref[...])
        u = _dot(xb, wub_ref[...])
        h = (g * jax.nn.sigmoid(g) * u).astype(BF16)
        ys_ref[...] = _pack_bf16_pairs(_dot(h, wdb_ref[...]))

    @pl.when(i >= nu_ref[0])
    def _():
        ys_ref[...] = jnp.zeros(ys_ref.shape, ys_ref.dtype)


def _expert_call(block_expert, n_used, xs, wg, wu, wd):
    P = xs.shape[0]
    bm = MOE_BM
    nb = P // bm
    xs_map = lambda i, be, nu: (jnp.minimum(i, nu[0] - 1), 0)
    w_map = lambda i, be, nu: (be[i], 0, 0)
    return pl.pallas_call(
        _expert_kernel,
        grid_spec=pltpu.PrefetchScalarGridSpec(
            num_scalar_prefetch=2,
            grid=(nb,),
            in_specs=[
                pl.BlockSpec((bm, D_MODEL // 2), xs_map),
                pl.BlockSpec((None, D_MODEL, EXPERT_FF), w_map),
                pl.BlockSpec((None, D_MODEL, EXPERT_FF), w_map),
                pl.BlockSpec((None, EXPERT_FF, D_MODEL), w_map),
            ],
            out_specs=pl.BlockSpec((bm, D_MODEL // 2), lambda i, be, nu: (i, 0)),
            scratch_shapes=[
                pltpu.VMEM((D_MODEL, EXPERT_FF), BF16),
                pltpu.VMEM((D_MODEL, EXPERT_FF), BF16),
                pltpu.VMEM((EXPERT_FF, D_MODEL), BF16),
            ],
        ),
        out_shape=jax.ShapeDtypeStruct((P, D_MODEL // 2), jnp.uint32),
        compiler_params=pltpu.CompilerParams(
            dimension_semantics=("arbitrary",),
            vmem_limit_bytes=VMEM_LIMIT),
        name="moe_experts",
    )(block_expert, n_used, xs, wg, wu, wd)


def _sc_gather_rows(ys, dest):
    _, T = dest.shape
    W = ys.shape[1]
    n_workers = SC_CORES * SC_SUBCORES
    per = T // n_workers
    win = SC_WIN // 2
    n_win = per // win
    assert n_win * win * n_workers == T
    mesh = plsc.VectorSubcoreMesh(core_axis_name="core", subcore_axis_name="subcore")

    @pl.kernel(out_type=jax.ShapeDtypeStruct((2, T, W), ys.dtype), mesh=mesh,
               scratch_types=[pltpu.VMEM((2, 2, win, W), ys.dtype), pltpu.VMEM((2, 2, win), jnp.int32),
                              pltpu.SemaphoreType.DMA((2,)), pltpu.SemaphoreType.DMA((2,)),
                              pltpu.SemaphoreType.DMA((2,))],
               name="moe_combine_sc")
    def gather(y_hbm, d0_hbm, d1_hbm, o_hbm, buf, idx, idx_sem, gat_sem, out_sem):
        wid = lax.axis_index("core") * SC_SUBCORES + lax.axis_index("subcore")
        d_hbm = (d0_hbm, d1_hbm)

        def base(j):
            return wid * per + j * win

        def idx_loads(j, s):
            return [pltpu.make_async_copy(d_hbm[k].at[pl.ds(base(j), win)], idx.at[s, k], idx_sem.at[s])
                    for k in range(2)]

        def gathers(s):
            return [pltpu.make_async_copy(y_hbm.at[idx.at[s, k]], buf.at[s, k], gat_sem.at[s]) for k in range(2)]

        def stores(j, s):
            return [pltpu.make_async_copy(buf.at[s, k], o_hbm.at[k, pl.ds(base(j), win)], out_sem.at[s])
                    for k in range(2)]

        for c in idx_loads(0, 0):
            c.start()
        for j in range(n_win):
            s = j % 2
            for c in idx_loads(j, s):
                c.wait()
            if j >= 2:
                for c in stores(j - 2, s):
                    c.wait()
            for c in gathers(s):
                c.start()
            if j >= 1:
                for c in gathers(1 - s):
                    c.wait()
                for c in stores(j - 1, 1 - s):
                    c.start()
            if j + 1 < n_win:
                for c in idx_loads(j + 1, 1 - s):
                    c.start()
        s_last = (n_win - 1) % 2
        for c in gathers(s_last):
            c.wait()
        for c in stores(n_win - 1, s_last):
            c.start()
        if n_win >= 2:
            for c in stores(n_win - 2, 1 - s_last):
                c.wait()
        for c in stores(n_win - 1, s_last):
            c.wait()

    return gather(ys, dest[0], dest[1])


def _combine_kernel(y_ref, h1_ref, rf_ref, g2_ref, b2_ref, out_ref):
    rf = rf_ref[...]
    def unpack_f32(w):
        return jnp.concatenate([lax.bitcast_convert_type(w << 16, F32),
                                lax.bitcast_convert_type(w & jnp.uint32(0xFFFF0000), F32)], axis=1)

    ffn = rf[:, 0:1] * unpack_f32(y_ref[0]) + rf[:, 1:2] * unpack_f32(y_ref[1])
    z = DEEPNORM_ALPHA * h1_ref[...] + ffn
    mu = jnp.mean(z, axis=-1, keepdims=True)
    zc = z - mu
    var = jnp.mean(zc * zc, axis=-1, keepdims=True)
    out_ref[...] = zc * lax.rsqrt(var + LN_EPS) * g2_ref[...] + b2_ref[...]


def _combine_call(y12, h1, rf, g2, b2):
    T, D = h1.shape
    tm = COMB_TM
    const = lambda i: (0, 0)
    row = lambda i: (i, 0)
    return pl.pallas_call(
        _combine_kernel,
        grid=(T // tm,),
        in_specs=[
            pl.BlockSpec((2, tm, D // 2), lambda i: (0, i, 0)),
            pl.BlockSpec((tm, D), row),
            pl.BlockSpec((tm, LANES), row),
            pl.BlockSpec(g2.shape, const),
            pl.BlockSpec(b2.shape, const),
        ],
        out_specs=pl.BlockSpec((tm, D), row),
        out_shape=jax.ShapeDtypeStruct((T, D), F32),
        compiler_params=pltpu.CompilerParams(
            dimension_semantics=("arbitrary",),
            vmem_limit_bytes=VMEM_LIMIT),
        name="moe_combine_ln",
    )(y12, h1, rf, g2, b2)


def kernel(x, positions, w_in, b_gate, lambda_q1, lambda_k1, lambda_q2, lambda_k2, subln_g, w_o_att, conv_w, w_o_conv, w_mix_out, ln1_g, ln1_b, w_router_group, b_router_group, w_router_expert, b_router_expert, w_exp_gate, w_exp_up, w_exp_down, ln2_g, ln2_b):
    B, S, D = x.shape
    T = B * S
    l = 0
    lambda_init = 0.8 - 0.6 * math.exp(-0.3 * l)

    inv_freq = ROPE_THETA ** (-jnp.arange(0, HEAD_DIM, 2, dtype=F32) / HEAD_DIM)
    ang = positions.astype(F32)[..., None] * inv_freq
    cos_t, sin_t = jnp.cos(ang), jnp.sin(ang)

    n_main = 2 * QK_W + QK_W + 3 * CONV_W
    w0 = w_in[l]
    w_main = w0[:, :n_main].astype(BF16)
    wv_t = w0[:, 2 * QK_W:3 * QK_W].T.astype(BF16)
    w_gates = w0[:, n_main:].astype(BF16)

    q, k, vt, u = _proj_call(x, w_main, wv_t, cos_t, sin_t, conv_w[l])

    lam = (jnp.exp(jnp.sum(lambda_q1[l].astype(F32) * lambda_k1[l].astype(F32)))
           - jnp.exp(jnp.sum(lambda_q2[l].astype(F32) * lambda_k2[l].astype(F32)))
           + lambda_init).reshape(1)
    o_n = _attn_call(lam, q, k, vt, subln_g[l].astype(F32).reshape(HEAD_W, 1), 1.0 - lambda_init)

    wr = jnp.zeros((D, LANES), F32)
    wr = wr.at[:, 0:N_GROUPS].set(w_router_group[l]).at[:, SUBLANES:SUBLANES + N_EXPERTS].set(w_router_expert[l])
    br = jnp.zeros((1, LANES), F32)
    br = br.at[0, 0:N_GROUPS].set(b_router_group[l]).at[0, SUBLANES:SUBLANES + N_EXPERTS].set(b_router_expert[l])

    bm = MOE_BM
    A = 2 * T
    P = A + N_EXPERTS * bm
    nb = P // bm
    h1, h1p, ri, rf, cnt, xs0 = _mix_call(
        x.reshape(T, D), o_n.reshape(T, QK_W), u.reshape(T, CONV_W),
        w_gates, b_gate[l].reshape(1, -1).astype(F32),
        w_o_att[l].astype(BF16), w_o_conv[l].astype(BF16), w_mix_out[l].astype(BF16),
        ln1_g[l].reshape(1, D).astype(F32), ln1_b[l].reshape(1, D).astype(F32),
        wr.astype(BF16), br, P)

    counts = cnt[:, 0]
    padded = (counts + bm - 1) // bm * bm
    pend = jnp.cumsum(padded)
    pstart = pend - padded
    onehot = (ri[0:2, :, None] == jnp.arange(N_EXPERTS, dtype=jnp.int32)).astype(jnp.int32)
    dest = jnp.sum(onehot * pstart, axis=-1) + ri[2:4]
    blk_start = jnp.arange(nb, dtype=jnp.int32) * bm
    block_expert = jnp.minimum(
        jnp.sum((pend[None, :] <= blk_start[:, None]).astype(jnp.int32), axis=1), N_EXPERTS - 1)
    n_used = (pend[-1] // bm).astype(jnp.int32).reshape(1)

    xs_ref = jax.new_ref(xs0)
    _sc_scatter_rows(h1p, dest, xs_ref)
    xs = xs_ref[...]
    ys = _expert_call(block_expert, n_used, xs, w_exp_gate[l], w_exp_up[l], w_exp_down[l])
    out = _combine_call(_sc_gather_rows(ys, dest), h1, rf,
                        ln2_g[l].reshape(1, D).astype(F32), ln2_b[l].reshape(1, D).astype(F32))
    return out.reshape(B, S, D)
```

```python
import functools
import math

import jax
import jax.numpy as jnp
from jax import lax
from jax.experimental import pallas as pl
from jax.experimental.pallas import tpu as pltpu
from jax.experimental.pallas import tpu_sc as plsc

D_MODEL = 1024
ATT_HEADS = 4
HEAD_DIM = 64
HEAD_W = 2 * HEAD_DIM
QK_W = ATT_HEADS * HEAD_W
CONV_W = D_MODEL // 2
CONV_K = 3
N_GROUPS = 4
EPG = 8
N_EXPERTS = N_GROUPS * EPG
EXPERT_FF = D_MODEL // 2
ROPE_THETA = 10000.0
SUBLN_EPS = 1e-5
LN_EPS = 1e-5
DEPTH = 1
DEEPNORM_ALPHA = (2.0 * DEPTH) ** 0.25

LANES = 128
SUBLANES = 8
VMEM_LIMIT = 48 * 1024 * 1024

PROJ_TM = 512
ATT_TQ = 256
ATT_TK = 256
ATT_UNROLL = 4
MIX_TM = 512
MIX_SUB = 512
MOE_BM = 512
SC_CORES = 2
SC_SUBCORES = 16
SC_WIN = 64
COMB_TM = 512
NEG_BIG = -1e30

F32 = jnp.float32
BF16 = jnp.bfloat16


def _dot(a, b):
    return jnp.dot(a, b, preferred_element_type=F32)


def _dot_nt(a, b):
    return lax.dot_general(a, b, (((1,), (1,)), ((), ())), preferred_element_type=F32)


def _proj_kernel(x_ref, w_ref, wvt_ref, pos_ref, invf_ref, cw_ref,
                 q_ref, k_ref, vt_ref, u_ref, pbuf):
    tm = x_ref.shape[0]
    i = pl.program_id(1)
    xb = x_ref[...].astype(BF16)
    lane = lax.broadcasted_iota(jnp.int32, (tm, LANES), 1)
    low_half = (lane % HEAD_DIM) < (HEAD_DIM // 2)
    ang = pos_ref[...] * invf_ref[...]
    cos = jnp.cos(ang)
    sin = jnp.where(low_half, -jnp.sin(ang), jnp.sin(ang))

    def rope(t):
        rot = jnp.where(low_half, pltpu.roll(t, LANES - HEAD_DIM // 2, 1),
                        pltpu.roll(t, HEAD_DIM // 2, 1))
        return t * cos + rot * sin

    q = _dot(xb, w_ref[:, 0:QK_W])
    for h in range(ATT_HEADS):
        sl = slice(h * HEAD_W, (h + 1) * HEAD_W)
        q_ref[:, sl] = (rope(q[:, sl]) * (HEAD_DIM ** -0.5 * math.log2(math.e))).astype(BF16)
    k = _dot(xb, w_ref[:, QK_W:2 * QK_W])
    for h in range(ATT_HEADS):
        sl = slice(h * HEAD_W, (h + 1) * HEAD_W)
        k_ref[:, sl] = rope(k[:, sl]).astype(BF16)
    vt_ref[...] = _dot_nt(wvt_ref[...], xb).astype(BF16)

    c0 = 3 * QK_W
    cb = _dot(xb, w_ref[:, c0:c0 + CONV_W])
    cc = _dot(xb, w_ref[:, c0 + CONV_W:c0 + 2 * CONV_W])
    cx = _dot(xb, w_ref[:, c0 + 2 * CONV_W:c0 + 3 * CONV_W])
    p = cc * cx

    @pl.when(i == 0)
    def _():
        pbuf[0:SUBLANES, :] = jnp.zeros((SUBLANES, CONV_W), F32)

    pbuf[SUBLANES:SUBLANES + tm, :] = p
    y = (cw_ref[0:1, :] * pbuf[SUBLANES - 2:SUBLANES - 2 + tm, :]
         + cw_ref[1:2, :] * pbuf[SUBLANES - 1:SUBLANES - 1 + tm, :]
         + cw_ref[2:3, :] * p)
    u_ref[...] = (cb * y).astype(BF16)
    pbuf[0:SUBLANES, :] = pbuf[tm:tm + SUBLANES, :]


def _proj_call(x, w_main, wv_t, pos_f, inv_f, conv_w):
    B, S, D = x.shape
    tm = PROJ_TM
    n_main = w_main.shape[1]
    return pl.pallas_call(
        _proj_kernel,
        grid=(B, S // tm),
        in_specs=[
            pl.BlockSpec((None, tm, D), lambda b, i: (b, i, 0)),
            pl.BlockSpec((D, n_main), lambda b, i: (0, 0)),
            pl.BlockSpec((QK_W, D), lambda b, i: (0, 0)),
            pl.BlockSpec((None, tm, 1), lambda b, i: (b, i, 0)),
            pl.BlockSpec((1, LANES), lambda b, i: (0, 0)),
            pl.BlockSpec((CONV_K, CONV_W), lambda b, i: (0, 0)),
        ],
        out_specs=[
            pl.BlockSpec((None, tm, QK_W), lambda b, i: (b, i, 0)),
            pl.BlockSpec((None, tm, QK_W), lambda b, i: (b, i, 0)),
            pl.BlockSpec((None, QK_W, tm), lambda b, i: (b, 0, i)),
            pl.BlockSpec((None, tm, CONV_W), lambda b, i: (b, i, 0)),
        ],
        out_shape=[
            jax.ShapeDtypeStruct((B, S, QK_W), BF16),
            jax.ShapeDtypeStruct((B, S, QK_W), BF16),
            jax.ShapeDtypeStruct((B, QK_W, S), BF16),
            jax.ShapeDtypeStruct((B, S, CONV_W), BF16),
        ],
        scratch_shapes=[pltpu.VMEM((tm + 2 * SUBLANES, CONV_W), F32)],
        compiler_params=pltpu.CompilerParams(
            dimension_semantics=("arbitrary", "arbitrary"),
            vmem_limit_bytes=VMEM_LIMIT),
        name="proj_rope_conv",
    )(x, w_main, wv_t, pos_f, inv_f, conv_w)


ACC_ROWS = HEAD_W + 16


def _attn_kernel(lam_ref, q_ref, k_ref, vt_ref, g_ref, o_ref, qzt_ref, *scratch, out_scale):
    nh = ATT_HEADS
    s_refs, p_refs = scratch[0:nh], scratch[nh:2 * nh]
    m_refs, a_refs, acc_refs = scratch[2 * nh:3 * nh], scratch[3 * nh:4 * nh], scratch[4 * nh:5 * nh]
    tq = q_ref.shape[0]
    tk = ATT_TK
    nblk = 2 * tq // LANES
    i = pl.program_id(1)
    lane = lax.broadcasted_iota(jnp.int32, (tq, HEAD_W), 1)
    ones_rows = jnp.ones((ACC_ROWS - HEAD_W, tk), BF16)

    for h in range(nh):
        qf = q_ref[:, h * HEAD_W:(h + 1) * HEAD_W].astype(F32)
        q1 = jnp.where(lane < HEAD_DIM, qf, 0.0)
        q2 = jnp.where(lane >= HEAD_DIM, qf, 0.0)
        qzt_ref[h, :, 0:tq] = q1.T.astype(BF16)
        qzt_ref[h, :, tq:2 * tq] = q2.T.astype(BF16)
        m_refs[h][...] = jnp.full(m_refs[h].shape, NEG_BIG, F32)
        acc_refs[h][...] = jnp.zeros(acc_refs[h].shape, F32)

    def scores(h, k0):
        kj = k_ref[pl.ds(k0, tk), h * HEAD_W:(h + 1) * HEAD_W]
        s_refs[h][:, 0:2 * tq] = _dot(kj, qzt_ref[h])

    def softmax(h, diag=None):
        for c in range(nblk):
            cs = slice(c * LANES, (c + 1) * LANES)
            s = s_refs[h][:, cs]
            if diag is not None:
                key = lax.broadcasted_iota(jnp.int32, (tk, LANES), 0) + diag * tk
                qry = lax.broadcasted_iota(jnp.int32, (tk, LANES), 1) + (c * LANES) % tq
                s = jnp.where(key <= qry, s, NEG_BIG)
            m_old = m_refs[h][:, cs]
            m_new = jnp.maximum(m_old, jnp.max(s, axis=0, keepdims=True))
            a_refs[h][:, cs] = jnp.exp2(m_old - m_new)
            m_refs[h][:, cs] = m_new
            p_refs[h][:, cs] = jnp.exp2(s - m_new).astype(BF16)

    def accumulate(h, k0):
        vj = jnp.concatenate([vt_ref[h * HEAD_W:(h + 1) * HEAD_W, pl.ds(k0, tk)], ones_rows], axis=0)
        acc_refs[h][...] = a_refs[h][...] * acc_refs[h][...] + _dot(vj, p_refs[h][:, 0:2 * tq])

    n_diag = tq // tk
    n_full = n_diag * i

    def tile_start(j):
        return pl.multiple_of(j * tk, tk)

    for h in range(nh):
        scores(h, 0)

    @pl.when(i > 0)
    def _():
        for h in range(nh):
            softmax(h)
            scores(h, tile_start(1))

    def step(j):
        for h in range(nh):
            accumulate(h, tile_start(j - 1))
            softmax(h)
            scores(h, tile_start(j + 1))

    n_steps = jnp.maximum(n_full - 1, 0)
    n_trips = n_steps // ATT_UNROLL

    def body(t, carry):
        for u in range(ATT_UNROLL):
            step(ATT_UNROLL * t + 1 + u)
        return carry

    lax.fori_loop(0, n_trips, body, 0)

    def tail(j, carry):
        step(j)
        return carry

    lax.fori_loop(ATT_UNROLL * n_trips + 1, n_steps + 1, tail, 0)

    @pl.when(i > 0)
    def _():
        for h in range(nh):
            accumulate(h, tile_start(n_full - 1))

    for d in range(n_diag):
        for h in range(nh):
            softmax(h, diag=d)
            if d + 1 < n_diag:
                scores(h, tile_start(n_full + d + 1))
            accumulate(h, tile_start(n_full + d))

    lam = lam_ref[0]
    for h in range(nh):
        acc = acc_refs[h][0:HEAD_W, :]
        l = acc_refs[h][HEAD_W:HEAD_W + 1, :]
        o = acc[:, 0:tq] / l[:, 0:tq] - lam * (acc[:, tq:2 * tq] / l[:, tq:2 * tq])
        ms = jnp.mean(o * o, axis=0, keepdims=True)
        o = o * lax.rsqrt(ms + SUBLN_EPS) * g_ref[...] * out_scale
        o_ref[:, h * HEAD_W:(h + 1) * HEAD_W] = o.T.astype(BF16)


def _attn_call(lam, q, k, vt, g_col, out_scale):
    B, S, _ = q.shape
    tq = ATT_TQ
    assert ATT_TQ % ATT_TK == 0
    kernel = functools.partial(_attn_kernel, out_scale=out_scale)
    return pl.pallas_call(
        kernel,
        grid=(B, S // tq),
        in_specs=[
            pl.BlockSpec(memory_space=pltpu.SMEM),
            pl.BlockSpec((None, tq, QK_W), lambda b, i: (b, i, 0)),
            pl.BlockSpec((None, S, QK_W), lambda b, i: (b, 0, 0), pipeline_mode=pl.Buffered(1)),
            pl.BlockSpec((None, QK_W, S), lambda b, i: (b, 0, 0), pipeline_mode=pl.Buffered(1)),
            pl.BlockSpec((HEAD_W, 1), lambda b, i: (0, 0)),
        ],
        out_specs=pl.BlockSpec((None, tq, QK_W), lambda b, i: (b, i, 0)),
        out_shape=jax.ShapeDtypeStruct((B, S, QK_W), BF16),
        scratch_shapes=(
            [pltpu.VMEM((ATT_HEADS, HEAD_W, 2 * tq), BF16)]
            + [pltpu.VMEM((ATT_TK, 2 * tq + LANES), F32) for _ in range(ATT_HEADS)]
            + [pltpu.VMEM((ATT_TK, 2 * tq + LANES), BF16) for _ in range(ATT_HEADS)]
            + [pltpu.VMEM((1, 2 * tq), F32) for _ in range(ATT_HEADS)]
            + [pltpu.VMEM((1, 2 * tq), F32) for _ in range(ATT_HEADS)]
            + [pltpu.VMEM((ACC_ROWS, 2 * tq), F32) for _ in range(ATT_HEADS)]
        ),
        compiler_params=pltpu.CompilerParams(
            dimension_semantics=("arbitrary", "arbitrary"),
            vmem_limit_bytes=VMEM_LIMIT),
        name="diff_flash_attn",
    )(lam, q, k, vt, g_col)


def _pack_bf16_pairs(h):
    half = h.shape[1] // 2
    hb = h.astype(BF16).astype(F32)
    bits = lax.bitcast_convert_type(hb, jnp.uint32)
    return (bits[:, 0:half] >> 16) | (bits[:, half:] & jnp.uint32(0xFFFF0000))


def _unpack_bf16_pairs(w):
    lo = lax.bitcast_convert_type(w << 16, F32).astype(BF16)
    hi = lax.bitcast_convert_type(w & jnp.uint32(0xFFFF0000), F32).astype(BF16)
    return jnp.concatenate([lo, hi], axis=1)


def _first_index_of(mask, row_f, big):
    return jnp.min(jnp.where(mask, row_f, big), axis=0, keepdims=True)


def _mix_kernel(x_ref, o_ref, u_ref, wg_ref, bg_ref, woa_ref, woc_ref, wmo_ref, g1_ref, b1_ref,
                wr_ref, br_ref,
                h1_ref, h1p_ref, ri_ref, rf_ref, cnt_ref, xs0_ref, carry_ref):
    tm = x_ref.shape[0]
    step = pl.program_id(0)
    xs0_ref[...] = jnp.zeros(xs0_ref.shape, xs0_ref.dtype)

    @pl.when(step == 0)
    def _():
        carry_ref[...] = jnp.zeros(carry_ref.shape, F32)

    ri_ref[4:8, :] = jnp.zeros((4, tm), jnp.int32)
    carry = carry_ref[:, 0:1]
    for sb in range(tm // MIX_SUB):
        carry = _mix_subtile(slice(sb * MIX_SUB, (sb + 1) * MIX_SUB), carry,
                             x_ref, o_ref, u_ref, wg_ref, bg_ref, woa_ref, woc_ref, wmo_ref, g1_ref, b1_ref,
                             wr_ref, br_ref, h1_ref, h1p_ref, ri_ref, rf_ref)
    new_carry = jnp.broadcast_to(carry, carry_ref.shape)
    carry_ref[...] = new_carry
    cnt_ref[...] = new_carry.astype(jnp.int32)


def _mix_subtile(rs, carry, x_ref, o_ref, u_ref, wg_ref, bg_ref, woa_ref, woc_ref, wmo_ref, g1_ref, b1_ref,
                 wr_ref, br_ref, h1_ref, h1p_ref, ri_ref, rf_ref):
    tm = rs.stop - rs.start
    x = x_ref[rs, :]
    xb = x.astype(BF16)
    gates = jax.nn.sigmoid(_dot(xb, wg_ref[...]) + bg_ref[...])
    ya = _dot(o_ref[rs, :], woa_ref[...])
    yc = _dot(u_ref[rs, :], woc_ref[...])
    merged = (gates[:, 0:D_MODEL] * ya + gates[:, D_MODEL:] * yc).astype(BF16)
    z = DEEPNORM_ALPHA * x + _dot(merged, wmo_ref[...])
    mu = jnp.mean(z, axis=-1, keepdims=True)
    zc = z - mu
    var = jnp.mean(zc * zc, axis=-1, keepdims=True)
    h1 = zc * lax.rsqrt(var + LN_EPS) * g1_ref[...] + b1_ref[...]
    h1_ref[rs, :] = h1
    h1p_ref[rs, :] = _pack_bf16_pairs(h1)

    logits = _dot(h1.astype(BF16), wr_ref[...]) + br_ref[...]
    lt = logits.T
    row8 = lax.broadcasted_iota(jnp.int32, (SUBLANES, tm), 0).astype(F32)
    gl = jnp.where(row8 < N_GROUPS, lt[0:SUBLANES, :], NEG_BIG)
    gmax = jnp.max(gl, axis=0, keepdims=True)
    gsel = _first_index_of(gl == gmax, row8, float(SUBLANES))
    gw = 1.0 / jnp.sum(jnp.exp(gl - gmax), axis=0, keepdims=True)
    el = lt[SUBLANES:2 * SUBLANES, :]
    for g in range(1, N_GROUPS):
        el = jnp.where(gsel == float(g), lt[(g + 1) * SUBLANES:(g + 2) * SUBLANES, :], el)
    v1 = jnp.max(el, axis=0, keepdims=True)
    i1 = _first_index_of(el == v1, row8, float(EPG))
    el2 = jnp.where(row8 == i1, -jnp.inf, el)
    v2 = jnp.max(el2, axis=0, keepdims=True)
    i2 = _first_index_of(el2 == v2, row8, float(EPG))
    t = jnp.exp(v2 - v1)
    p1 = gw / (1.0 + t)
    p2 = gw * t / (1.0 + t)
    e1 = gsel * float(EPG) + i1
    e2 = gsel * float(EPG) + i2

    rowe = lax.broadcasted_iota(jnp.int32, (N_EXPERTS, tm), 0).astype(F32)
    oh1 = (rowe == e1).astype(F32)
    oh2 = (rowe == e2).astype(F32)
    r_i = lax.broadcasted_iota(jnp.int32, (tm, tm), 0)
    c_i = lax.broadcasted_iota(jnp.int32, (tm, tm), 1)
    upper = jnp.where(r_i < c_i, 1.0, 0.0).astype(BF16)
    cum1 = _dot(oh1.astype(BF16), upper)
    cum2 = _dot(oh2.astype(BF16), upper)
    tot1 = jnp.sum(oh1, axis=1, keepdims=True)
    tot2 = jnp.sum(oh2, axis=1, keepdims=True)
    rank1 = jnp.sum(oh1 * (carry + cum1), axis=0, keepdims=True)
    rank2 = jnp.sum(oh2 * (carry + tot1 + cum2), axis=0, keepdims=True)

    ri_ref[0:1, rs] = e1.astype(jnp.int32)
    ri_ref[1:2, rs] = e2.astype(jnp.int32)
    ri_ref[2:3, rs] = rank1.astype(jnp.int32)
    ri_ref[3:4, rs] = rank2.astype(jnp.int32)

    row128 = lax.broadcasted_iota(jnp.int32, (LANES, tm), 0)
    pw = jnp.where(row128 == 0, p1, jnp.where(row128 == 1, p2, 0.0))
    rf_ref[rs, :] = pw.T
    return carry + tot1 + tot2


def _mix_call(x2, o_n, u, wg, bg, woa, woc, wmo, g1, b1, wr, br, n_slots):
    T, D = x2.shape
    tm = MIX_TM
    slots_per_step = n_slots // (T // tm)
    assert slots_per_step * (T // tm) == n_slots and slots_per_step % SUBLANES == 0
    const = lambda i: (0, 0)
    row = lambda i: (i, 0)
    return pl.pallas_call(
        _mix_kernel,
        grid=(T // tm,),
        in_specs=[
            pl.BlockSpec((tm, D), row),
            pl.BlockSpec((tm, QK_W), row),
            pl.BlockSpec((tm, CONV_W), row),
            pl.BlockSpec(wg.shape, const),
            pl.BlockSpec(bg.shape, const),
            pl.BlockSpec(woa.shape, const),
            pl.BlockSpec(woc.shape, const),
            pl.BlockSpec(wmo.shape, const),
            pl.BlockSpec(g1.shape, const),
            pl.BlockSpec(b1.shape, const),
            pl.BlockSpec(wr.shape, const),
            pl.BlockSpec(br.shape, const),
        ],
        out_specs=[
            pl.BlockSpec((tm, D), row),
            pl.BlockSpec((tm, D // 2), row),
            pl.BlockSpec((SUBLANES, tm), lambda i: (0, i)),
            pl.BlockSpec((tm, LANES), row),
            pl.BlockSpec((N_EXPERTS, LANES), const),
            pl.BlockSpec((slots_per_step, D // 2), row),
        ],
        out_shape=[
            jax.ShapeDtypeStruct((T, D), F32),
            jax.ShapeDtypeStruct((T, D // 2), jnp.uint32),
            jax.ShapeDtypeStruct((SUBLANES, T), jnp.int32),
            jax.ShapeDtypeStruct((T, LANES), F32),
            jax.ShapeDtypeStruct((N_EXPERTS, LANES), jnp.int32),
            jax.ShapeDtypeStruct((n_slots, D // 2), jnp.uint32),
        ],
        scratch_shapes=[pltpu.VMEM((N_EXPERTS, LANES), F32)],
        compiler_params=pltpu.CompilerParams(
            dimension_semantics=("arbitrary",),
            vmem_limit_bytes=VMEM_LIMIT),
        name="mix_ln_router",
    )(x2, o_n, u, wg, bg, woa, woc, wmo, g1, b1, wr, br)


def _sc_scatter_rows(h1p, dest, xs_ref):
    T, W = h1p.shape
    n_workers = SC_CORES * SC_SUBCORES
    per = T // n_workers
    n_win = per // SC_WIN
    assert n_win * SC_WIN * n_workers == T
    mesh = plsc.VectorSubcoreMesh(core_axis_name="core", subcore_axis_name="subcore")

    @pl.kernel(out_type=(), mesh=mesh,
               scratch_types=[pltpu.VMEM((2, SC_WIN, W), h1p.dtype),
                              pltpu.VMEM((2, SC_WIN), jnp.int32), pltpu.VMEM((2, SC_WIN), jnp.int32),
                              pltpu.SemaphoreType.DMA((2,)), pltpu.SemaphoreType.DMA((2,))],
               name="moe_dispatch_sc")
    def scatter(x_hbm, d0_hbm, d1_hbm, o_hbm, xbuf, i0, i1, load_sem, scat_sem):
        wid = lax.axis_index("core") * SC_SUBCORES + lax.axis_index("subcore")

        def loads(j, s):
            base = wid * per + j * SC_WIN
            return [pltpu.make_async_copy(x_hbm.at[pl.ds(base, SC_WIN)], xbuf.at[s], load_sem.at[s]),
                    pltpu.make_async_copy(d0_hbm.at[pl.ds(base, SC_WIN)], i0.at[s], load_sem.at[s]),
                    pltpu.make_async_copy(d1_hbm.at[pl.ds(base, SC_WIN)], i1.at[s], load_sem.at[s])]

        def scatters(s):
            return [pltpu.make_async_copy(xbuf.at[s], o_hbm.at[i0.at[s]], scat_sem.at[s]),
                    pltpu.make_async_copy(xbuf.at[s], o_hbm.at[i1.at[s]], scat_sem.at[s])]

        for c in loads(0, 0):
            c.start()
        for j in range(n_win):
            s = j % 2
            for c in loads(j, s):
                c.wait()
            for c in scatters(s):
                c.start()
            if j >= 1:
                for c in scatters(1 - s):
                    c.wait()
            if j + 1 < n_win:
                for c in loads(j + 1, 1 - s):
                    c.start()
        for c in scatters((n_win - 1) % 2):
            c.wait()

    scatter(h1p, dest[0], dest[1], xs_ref)


def _expert_kernel(be_ref, nu_ref, xs_ref, wg_ref, wu_ref, wd_ref, ys_ref, wgb_ref, wub_ref, wdb_ref):
    i = pl.program_id(0)
    used = i < nu_ref[0]

    @pl.when(used & ((i == 0) | (be_ref[i] != be_ref[jnp.maximum(i - 1, 0)])))
    def _():
        wgb_ref[...] = wg_ref[...].astype(BF16)
        wub_ref[...] = wu_ref[...].astype(BF16)
        wdb_ref[...] = wd_ref[...].astype(BF16)

    @pl.when(used)
    def _():
        xb = _unpack_bf16_pairs(xs_ref[...])
        g = _dot(xb, wgb_ref[...])
        u = _dot(xb, wub_ref[...])
        h = (g * jax.nn.sigmoid(g) * u).astype(BF16)
        ys_ref[...] = _pack_bf16_pairs(_dot(h, wdb_ref[...]))

    @pl.when(i >= nu_ref[0])
    def _():
        ys_ref[...] = jnp.zeros(ys_ref.shape, ys_ref.dtype)


def _expert_call(block_expert, n_used, xs, wg, wu, wd):
    P = xs.shape[0]
    bm = MOE_BM
    nb = P // bm
    xs_map = lambda i, be, nu: (jnp.minimum(i, nu[0] - 1), 0)
    w_map = lambda i, be, nu: (be[i], 0, 0)
    return pl.pallas_call(
        _expert_kernel,
        grid_spec=pltpu.PrefetchScalarGridSpec(
            num_scalar_prefetch=2,
            grid=(nb,),
            in_specs=[
                pl.BlockSpec((bm, D_MODEL // 2), xs_map),
                pl.BlockSpec((None, D_MODEL, EXPERT_FF), w_map),
                pl.BlockSpec((None, D_MODEL, EXPERT_FF), w_map),
                pl.BlockSpec((None, EXPERT_FF, D_MODEL), w_map),
            ],
            out_specs=pl.BlockSpec((bm, D_MODEL // 2), lambda i, be, nu: (i, 0)),
            scratch_shapes=[
                pltpu.VMEM((D_MODEL, EXPERT_FF), BF16),
                pltpu.VMEM((D_MODEL, EXPERT_FF), BF16),
                pltpu.VMEM((EXPERT_FF, D_MODEL), BF16),
            ],
        ),
        out_shape=jax.ShapeDtypeStruct((P, D_MODEL // 2), jnp.uint32),
        compiler_params=pltpu.CompilerParams(
            dimension_semantics=("arbitrary",),
            vmem_limit_bytes=VMEM_LIMIT),
        name="moe_experts",
    )(block_expert, n_used, xs, wg, wu, wd)


def _sc_gather_rows(ys, dest):
    _, T = dest.shape
    W = ys.shape[1]
    n_workers = SC_CORES * SC_SUBCORES
    per = T // n_workers
    win = SC_WIN // 2
    n_win = per // win
    assert n_win * win * n_workers == T
    mesh = plsc.VectorSubcoreMesh(core_axis_name="core", subcore_axis_name="subcore")

    @pl.kernel(out_type=jax.ShapeDtypeStruct((2, T, W), ys.dtype), mesh=mesh,
               scratch_types=[pltpu.VMEM((2, 2, win, W), ys.dtype), pltpu.VMEM((2, 2, win), jnp.int32),
                              pltpu.SemaphoreType.DMA((2,)), pltpu.SemaphoreType.DMA((2,)),
                              pltpu.SemaphoreType.DMA((2,))],
               name="moe_combine_sc")
    def gather(y_hbm, d0_hbm, d1_hbm, o_hbm, buf, idx, idx_sem, gat_sem, out_sem):
        wid = lax.axis_index("core") * SC_SUBCORES + lax.axis_index("subcore")
        d_hbm = (d0_hbm, d1_hbm)

        def base(j):
            return wid * per + j * win

        def idx_loads(j, s):
            return [pltpu.make_async_copy(d_hbm[k].at[pl.ds(base(j), win)], idx.at[s, k], idx_sem.at[s])
                    for k in range(2)]

        def gathers(s):
            return [pltpu.make_async_copy(y_hbm.at[idx.at[s, k]], buf.at[s, k], gat_sem.at[s]) for k in range(2)]

        def stores(j, s):
            return [pltpu.make_async_copy(buf.at[s, k], o_hbm.at[k, pl.ds(base(j), win)], out_sem.at[s])
                    for k in range(2)]

        for c in idx_loads(0, 0):
            c.start()
        for j in range(n_win):
            s = j % 2
            for c in idx_loads(j, s):
                c.wait()
            if j >= 2:
                for c in stores(j - 2, s):
                    c.wait()
            for c in gathers(s):
                c.start()
            if j >= 1:
                for c in gathers(1 - s):
                    c.wait()
                for c in stores(j - 1, 1 - s):
                    c.start()
            if j + 1 < n_win:
                for c in idx_loads(j + 1, 1 - s):
                    c.start()
        s_last = (n_win - 1) % 2
        for c in gathers(s_last):
            c.wait()
        for c in stores(n_win - 1, s_last):
            c.start()
        if n_win >= 2:
            for c in stores(n_win - 2, 1 - s_last):
                c.wait()
        for c in stores(n_win - 1, s_last):
            c.wait()

    return gather(ys, dest[0], dest[1])


def _combine_kernel(y_ref, h1_ref, rf_ref, g2_ref, b2_ref, out_ref):
    rf = rf_ref[...]
    def unpack_f32(w):
        return jnp.concatenate([lax.bitcast_convert_type(w << 16, F32),
                                lax.bitcast_convert_type(w & jnp.uint32(0xFFFF0000), F32)], axis=1)

    ffn = rf[:, 0:1] * unpack_f32(y_ref[0]) + rf[:, 1:2] * unpack_f32(y_ref[1])
    z = DEEPNORM_ALPHA * h1_ref[...] + ffn
    mu = jnp.mean(z, axis=-1, keepdims=True)
    zc = z - mu
    var = jnp.mean(zc * zc, axis=-1, keepdims=True)
    out_ref[...] = zc * lax.rsqrt(var + LN_EPS) * g2_ref[...] + b2_ref[...]


def _combine_call(y12, h1, rf, g2, b2):
    T, D = h1.shape
    tm = COMB_TM
    const = lambda i: (0, 0)
    row = lambda i: (i, 0)
    return pl.pallas_call(
        _combine_kernel,
        grid=(T // tm,),
        in_specs=[
            pl.BlockSpec((2, tm, D // 2), lambda i: (0, i, 0)),
            pl.BlockSpec((tm, D), row),
            pl.BlockSpec((tm, LANES), row),
            pl.BlockSpec(g2.shape, const),
            pl.BlockSpec(b2.shape, const),
        ],
        out_specs=pl.BlockSpec((tm, D), row),
        out_shape=jax.ShapeDtypeStruct((T, D), F32),
        compiler_params=pltpu.CompilerParams(
            dimension_semantics=("arbitrary",),
            vmem_limit_bytes=VMEM_LIMIT),
        name="moe_combine_ln",
    )(y12, h1, rf, g2, b2)


def kernel(x, positions, w_in, b_gate, lambda_q1, lambda_k1, lambda_q2, lambda_k2, subln_g, w_o_att, conv_w, w_o_conv, w_mix_out, ln1_g, ln1_b, w_router_group, b_router_group, w_router_expert, b_router_expert, w_exp_gate, w_exp_up, w_exp_down, ln2_g, ln2_b):
    B, S, D = x.shape
    T = B * S
    l = 0
    lambda_init = 0.8 - 0.6 * math.exp(-0.3 * l)

    inv_freq = ROPE_THETA ** (-jnp.arange(0, HEAD_DIM, 2, dtype=F32) / HEAD_DIM)
    pos_f = positions.astype(F32)[..., None]
    inv_f = jnp.tile(inv_freq, 4).reshape(1, LANES)

    n_main = 2 * QK_W + QK_W + 3 * CONV_W
    w0 = w_in[l]
    w_main = w0[:, :n_main].astype(BF16)
    wv_t = w0[:, 2 * QK_W:3 * QK_W].T.astype(BF16)
    w_gates = w0[:, n_main:].astype(BF16)

    q, k, vt, u = _proj_call(x, w_main, wv_t, pos_f, inv_f, conv_w[l])

    lam = (jnp.exp(jnp.sum(lambda_q1[l].astype(F32) * lambda_k1[l].astype(F32)))
           - jnp.exp(jnp.sum(lambda_q2[l].astype(F32) * lambda_k2[l].astype(F32)))
           + lambda_init).reshape(1)
    o_n = _attn_call(lam, q, k, vt, subln_g[l].astype(F32).reshape(HEAD_W, 1), 1.0 - lambda_init)

    wr = jnp.zeros((D, LANES), F32)
    wr = wr.at[:, 0:N_GROUPS].set(w_router_group[l]).at[:, SUBLANES:SUBLANES + N_EXPERTS].set(w_router_expert[l])
    br = jnp.zeros((1, LANES), F32)
    br = br.at[0, 0:N_GROUPS].set(b_router_group[l]).at[0, SUBLANES:SUBLANES + N_EXPERTS].set(b_router_expert[l])

    bm = MOE_BM
    A = 2 * T
    P = A + N_EXPERTS * bm
    nb = P // bm
    h1, h1p, ri, rf, cnt, xs0 = _mix_call(
        x.reshape(T, D), o_n.reshape(T, QK_W), u.reshape(T, CONV_W),
        w_gates, b_gate[l].reshape(1, -1).astype(F32),
        w_o_att[l].astype(BF16), w_o_conv[l].astype(BF16), w_mix_out[l].astype(BF16),
        ln1_g[l].reshape(1, D).astype(F32), ln1_b[l].reshape(1, D).astype(F32),
        wr.astype(BF16), br, P)

    counts = cnt[:, 0]
    padded = (counts + bm - 1) // bm * bm
    pend = jnp.cumsum(padded)
    pstart = pend - padded
    onehot = (ri[0:2, :, None] == jnp.arange(N_EXPERTS, dtype=jnp.int32)).astype(jnp.int32)
    dest = jnp.sum(onehot * pstart, axis=-1) + ri[2:4]
    blk_start = jnp.arange(nb, dtype=jnp.int32) * bm
    block_expert = jnp.minimum(
        jnp.sum((pend[None, :] <= blk_start[:, None]).astype(jnp.int32), axis=1), N_EXPERTS - 1)
    n_used = (pend[-1] // bm).astype(jnp.int32).reshape(1)

    xs_ref = jax.new_ref(xs0)
    _sc_scatter_rows(h1p, dest, xs_ref)
    xs = xs_ref[...]
    ys = _expert_call(block_expert, n_used, xs, w_exp_gate[l], w_exp_up[l], w_exp_down[l])
    out = _combine_call(_sc_gather_rows(ys, dest), h1, rf,
                        ln2_g[l].reshape(1, D).astype(F32), ln2_b[l].reshape(1, D).astype(F32))
    return out.reshape(B, S, D)
```

```python
import functools
import math

import jax
import jax.numpy as jnp
from jax import lax
from jax.experimental import pallas as pl
from jax.experimental.pallas import tpu as pltpu
from jax.experimental.pallas import tpu_sc as plsc

D_MODEL = 1024
ATT_HEADS = 4
HEAD_DIM = 64
HEAD_W = 2 * HEAD_DIM
QK_W = ATT_HEADS * HEAD_W
CONV_W = D_MODEL // 2
CONV_K = 3
N_GROUPS = 4
EPG = 8
N_EXPERTS = N_GROUPS * EPG
EXPERT_FF = D_MODEL // 2
ROPE_THETA = 10000.0
SUBLN_EPS = 1e-5
LN_EPS = 1e-5
DEPTH = 1
DEEPNORM_ALPHA = (2.0 * DEPTH) ** 0.25

LANES = 128
SUBLANES = 8
VMEM_LIMIT = 48 * 1024 * 1024

PROJ_TM = 512
ATT_TQ = 256
ATT_TK = 256
ATT_UNROLL = 4
MIX_TM = 512
MOE_BM = 512
SC_CORES = 2
SC_SUBCORES = 16
SC_WIN = 64
COMB_TM = 512
NEG_BIG = -1e30

F32 = jnp.float32
BF16 = jnp.bfloat16


def _dot(a, b):
    return jnp.dot(a, b, preferred_element_type=F32)


def _dot_nt(a, b):
    return lax.dot_general(a, b, (((1,), (1,)), ((), ())), preferred_element_type=F32)


def _proj_kernel(x_ref, w_ref, wvt_ref, pos_ref, invf_ref, cw_ref,
                 q_ref, k_ref, vt_ref, u_ref, pbuf):
    tm = x_ref.shape[0]
    i = pl.program_id(1)
    xb = x_ref[...].astype(BF16)
    lane = lax.broadcasted_iota(jnp.int32, (tm, LANES), 1)
    low_half = (lane % HEAD_DIM) < (HEAD_DIM // 2)
    ang = pos_ref[...] * invf_ref[...]
    cos = jnp.cos(ang)
    sin = jnp.where(low_half, -jnp.sin(ang), jnp.sin(ang))

    def rope(t):
        rot = jnp.where(low_half, pltpu.roll(t, LANES - HEAD_DIM // 2, 1),
                        pltpu.roll(t, HEAD_DIM // 2, 1))
        return t * cos + rot * sin

    q = _dot(xb, w_ref[:, 0:QK_W])
    for h in range(ATT_HEADS):
        sl = slice(h * HEAD_W, (h + 1) * HEAD_W)
        q_ref[:, sl] = (rope(q[:, sl]) * (HEAD_DIM ** -0.5 * math.log2(math.e))).astype(BF16)
    k = _dot(xb, w_ref[:, QK_W:2 * QK_W])
    for h in range(ATT_HEADS):
        sl = slice(h * HEAD_W, (h + 1) * HEAD_W)
        k_ref[:, sl] = rope(k[:, sl]).astype(BF16)
    vt_ref[...] = _dot_nt(wvt_ref[...], xb).astype(BF16)

    c0 = 3 * QK_W
    cb = _dot(xb, w_ref[:, c0:c0 + CONV_W])
    cc = _dot(xb, w_ref[:, c0 + CONV_W:c0 + 2 * CONV_W])
    cx = _dot(xb, w_ref[:, c0 + 2 * CONV_W:c0 + 3 * CONV_W])
    p = cc * cx

    @pl.when(i == 0)
    def _():
        pbuf[0:SUBLANES, :] = jnp.zeros((SUBLANES, CONV_W), F32)

    pbuf[SUBLANES:SUBLANES + tm, :] = p
    y = (cw_ref[0:1, :] * pbuf[SUBLANES - 2:SUBLANES - 2 + tm, :]
         + cw_ref[1:2, :] * pbuf[SUBLANES - 1:SUBLANES - 1 + tm, :]
         + cw_ref[2:3, :] * p)
    u_ref[...] = (cb * y).astype(BF16)
    pbuf[0:SUBLANES, :] = pbuf[tm:tm + SUBLANES, :]


def _proj_call(x, w_main, wv_t, pos_f, inv_f, conv_w):
    B, S, D = x.shape
    tm = PROJ_TM
    n_main = w_main.shape[1]
    return pl.pallas_call(
        _proj_kernel,
        grid=(B, S // tm),
        in_specs=[
            pl.BlockSpec((None, tm, D), lambda b, i: (b, i, 0)),
            pl.BlockSpec((D, n_main), lambda b, i: (0, 0)),
            pl.BlockSpec((QK_W, D), lambda b, i: (0, 0)),
            pl.BlockSpec((None, tm, 1), lambda b, i: (b, i, 0)),
            pl.BlockSpec((1, LANES), lambda b, i: (0, 0)),
            pl.BlockSpec((CONV_K, CONV_W), lambda b, i: (0, 0)),
        ],
        out_specs=[
            pl.BlockSpec((None, tm, QK_W), lambda b, i: (b, i, 0)),
            pl.BlockSpec((None, tm, QK_W), lambda b, i: (b, i, 0)),
            pl.BlockSpec((None, QK_W, tm), lambda b, i: (b, 0, i)),
            pl.BlockSpec((None, tm, CONV_W), lambda b, i: (b, i, 0)),
        ],
        out_shape=[
            jax.ShapeDtypeStruct((B, S, QK_W), BF16),
            jax.ShapeDtypeStruct((B, S, QK_W), BF16),
            jax.ShapeDtypeStruct((B, QK_W, S), BF16),
            jax.ShapeDtypeStruct((B, S, CONV_W), BF16),
        ],
        scratch_shapes=[pltpu.VMEM((tm + 2 * SUBLANES, CONV_W), F32)],
        compiler_params=pltpu.CompilerParams(
            dimension_semantics=("arbitrary", "arbitrary"),
            vmem_limit_bytes=VMEM_LIMIT),
        name="proj_rope_conv",
    )(x, w_main, wv_t, pos_f, inv_f, conv_w)


ACC_ROWS = HEAD_W + 16


def _attn_kernel(lam_ref, q_ref, k_ref, vt_ref, g_ref, o_ref, qzt_ref, *scratch, out_scale):
    nh = ATT_HEADS
    s_refs, p_refs = scratch[0:nh], scratch[nh:2 * nh]
    m_refs, a_refs, acc_refs = scratch[2 * nh:3 * nh], scratch[3 * nh:4 * nh], scratch[4 * nh:5 * nh]
    tq = q_ref.shape[0]
    tk = ATT_TK
    nblk = 2 * tq // LANES
    i = pl.program_id(1)
    lane = lax.broadcasted_iota(jnp.int32, (tq, HEAD_W), 1)
    ones_rows = jnp.ones((ACC_ROWS - HEAD_W, tk), BF16)

    for h in range(nh):
        qf = q_ref[:, h * HEAD_W:(h + 1) * HEAD_W].astype(F32)
        q1 = jnp.where(lane < HEAD_DIM, qf, 0.0)
        q2 = jnp.where(lane >= HEAD_DIM, qf, 0.0)
        qzt_ref[h, :, 0:tq] = q1.T.astype(BF16)
        qzt_ref[h, :, tq:2 * tq] = q2.T.astype(BF16)
        m_refs[h][...] = jnp.full(m_refs[h].shape, NEG_BIG, F32)
        acc_refs[h][...] = jnp.zeros(acc_refs[h].shape, F32)

    def scores(h, k0):
        kj = k_ref[pl.ds(k0, tk), h * HEAD_W:(h + 1) * HEAD_W]
        s_refs[h][:, 0:2 * tq] = _dot(kj, qzt_ref[h])

    def softmax(h, diag=None):
        for c in range(nblk):
            cs = slice(c * LANES, (c + 1) * LANES)
            s = s_refs[h][:, cs]
            if diag is not None:
                key = lax.broadcasted_iota(jnp.int32, (tk, LANES), 0) + diag * tk
                qry = lax.broadcasted_iota(jnp.int32, (tk, LANES), 1) + (c * LANES) % tq
                s = jnp.where(key <= qry, s, NEG_BIG)
            m_old = m_refs[h][:, cs]
            m_new = jnp.maximum(m_old, jnp.max(s, axis=0, keepdims=True))
            a_refs[h][:, cs] = jnp.exp2(m_old - m_new)
            m_refs[h][:, cs] = m_new
            p_refs[h][:, cs] = jnp.exp2(s - m_new).astype(BF16)

    def accumulate(h, k0):
        vj = jnp.concatenate([vt_ref[h * HEAD_W:(h + 1) * HEAD_W, pl.ds(k0, tk)], ones_rows], axis=0)
        acc_refs[h][...] = a_refs[h][...] * acc_refs[h][...] + _dot(vj, p_refs[h][:, 0:2 * tq])

    n_diag = tq // tk
    n_full = n_diag * i

    def tile_start(j):
        return pl.multiple_of(j * tk, tk)

    for h in range(nh):
        scores(h, 0)

    @pl.when(i > 0)
    def _():
        for h in range(nh):
            softmax(h)
            scores(h, tile_start(1))

    def step(j):
        for h in range(nh):
            accumulate(h, tile_start(j - 1))
            softmax(h)
            scores(h, tile_start(j + 1))

    n_steps = jnp.maximum(n_full - 1, 0)
    n_trips = n_steps // ATT_UNROLL

    def body(t, carry):
        for u in range(ATT_UNROLL):
            step(ATT_UNROLL * t + 1 + u)
        return carry

    lax.fori_loop(0, n_trips, body, 0)

    def tail(j, carry):
        step(j)
        return carry

    lax.fori_loop(ATT_UNROLL * n_trips + 1, n_steps + 1, tail, 0)

    @pl.when(i > 0)
    def _():
        for h in range(nh):
            accumulate(h, tile_start(n_full - 1))

    for d in range(n_diag):
        for h in range(nh):
            softmax(h, diag=d)
            if d + 1 < n_diag:
                scores(h, tile_start(n_full + d + 1))
            accumulate(h, tile_start(n_full + d))

    lam = lam_ref[0]
    for h in range(nh):
        acc = acc_refs[h][0:HEAD_W, :]
        l = acc_refs[h][HEAD_W:HEAD_W + 1, :]
        o = acc[:, 0:tq] / l[:, 0:tq] - lam * (acc[:, tq:2 * tq] / l[:, tq:2 * tq])
        ms = jnp.mean(o * o, axis=0, keepdims=True)
        o = o * lax.rsqrt(ms + SUBLN_EPS) * g_ref[...] * out_scale
        o_ref[:, h * HEAD_W:(h + 1) * HEAD_W] = o.T.astype(BF16)


def _attn_call(lam, q, k, vt, g_col, out_scale):
    B, S, _ = q.shape
    tq = ATT_TQ
    assert ATT_TQ % ATT_TK == 0
    kernel = functools.partial(_attn_kernel, out_scale=out_scale)
    return pl.pallas_call(
        kernel,
        grid=(B, S // tq),
        in_specs=[
            pl.BlockSpec(memory_space=pltpu.SMEM),
            pl.BlockSpec((None, tq, QK_W), lambda b, i: (b, i, 0)),
            pl.BlockSpec((None, S, QK_W), lambda b, i: (b, 0, 0), pipeline_mode=pl.Buffered(1)),
            pl.BlockSpec((None, QK_W, S), lambda b, i: (b, 0, 0), pipeline_mode=pl.Buffered(1)),
            pl.BlockSpec((HEAD_W, 1), lambda b, i: (0, 0)),
        ],
        out_specs=pl.BlockSpec((None, tq, QK_W), lambda b, i: (b, i, 0)),
        out_shape=jax.ShapeDtypeStruct((B, S, QK_W), BF16),
        scratch_shapes=(
            [pltpu.VMEM((ATT_HEADS, HEAD_W, 2 * tq), BF16)]
            + [pltpu.VMEM((ATT_TK, 2 * tq + LANES), F32) for _ in range(ATT_HEADS)]
            + [pltpu.VMEM((ATT_TK, 2 * tq + LANES), BF16) for _ in range(ATT_HEADS)]
            + [pltpu.VMEM((1, 2 * tq), F32) for _ in range(ATT_HEADS)]
            + [pltpu.VMEM((1, 2 * tq), F32) for _ in range(ATT_HEADS)]
            + [pltpu.VMEM((ACC_ROWS, 2 * tq), F32) for _ in range(ATT_HEADS)]
        ),
        compiler_params=pltpu.CompilerParams(
            dimension_semantics=("arbitrary", "arbitrary"),
            vmem_limit_bytes=VMEM_LIMIT),
        name="diff_flash_attn",
    )(lam, q, k, vt, g_col)


def _pack_bf16_pairs(h):
    half = h.shape[1] // 2
    hb = h.astype(BF16).astype(F32)
    bits = lax.bitcast_convert_type(hb, jnp.uint32)
    return (bits[:, 0:half] >> 16) | (bits[:, half:] & jnp.uint32(0xFFFF0000))


def _unpack_bf16_pairs(w):
    lo = lax.bitcast_convert_type(w << 16, F32).astype(BF16)
    hi = lax.bitcast_convert_type(w & jnp.uint32(0xFFFF0000), F32).astype(BF16)
    return jnp.concatenate([lo, hi], axis=1)


def _first_index_of(mask, row_f, big):
    return jnp.min(jnp.where(mask, row_f, big), axis=0, keepdims=True)


def _mix_kernel(x_ref, o_ref, u_ref, wg_ref, bg_ref, woa_ref, woc_ref, wmo_ref, g1_ref, b1_ref,
                wr_ref, br_ref,
                h1_ref, h1p_ref, ri_ref, rf_ref, cnt_ref, xs0_ref, carry_ref):
    step = pl.program_id(0)
    xs0_ref[...] = jnp.zeros(xs0_ref.shape, xs0_ref.dtype)

    @pl.when(step == 0)
    def _():
        carry_ref[...] = jnp.zeros(carry_ref.shape, F32)

    x = x_ref[...]
    xb = x.astype(BF16)
    gates = jax.nn.sigmoid(_dot(xb, wg_ref[...]) + bg_ref[...])
    ya = _dot(o_ref[...], woa_ref[...])
    yc = _dot(u_ref[...], woc_ref[...])
    merged = (gates[:, 0:D_MODEL] * ya + gates[:, D_MODEL:] * yc).astype(BF16)
    z = DEEPNORM_ALPHA * x + _dot(merged, wmo_ref[...])

    carry = carry_ref[:, 0:1]
    carry = _mix_tail(z, carry, g1_ref, b1_ref, wr_ref, br_ref, h1_ref, h1p_ref, ri_ref, rf_ref)
    new_carry = jnp.broadcast_to(carry, carry_ref.shape)
    carry_ref[...] = new_carry
    cnt_ref[...] = new_carry.astype(jnp.int32)


def _mix_tail(z, carry, g1_ref, b1_ref, wr_ref, br_ref, h1_ref, h1p_ref, ri_ref, rf_ref):
    tm = z.shape[0]
    mu = jnp.mean(z, axis=-1, keepdims=True)
    zc = z - mu
    var = jnp.mean(zc * zc, axis=-1, keepdims=True)
    h1 = zc * lax.rsqrt(var + LN_EPS) * g1_ref[...] + b1_ref[...]
    h1_ref[...] = h1
    h1p_ref[...] = _pack_bf16_pairs(h1)

    logits = _dot(h1.astype(BF16), wr_ref[...]) + br_ref[...]
    lt = logits.T
    row8 = lax.broadcasted_iota(jnp.int32, (SUBLANES, tm), 0).astype(F32)
    gl = jnp.where(row8 < N_GROUPS, lt[0:SUBLANES, :], NEG_BIG)
    gmax = jnp.max(gl, axis=0, keepdims=True)
    gsel = _first_index_of(gl == gmax, row8, float(SUBLANES))
    gw = 1.0 / jnp.sum(jnp.exp(gl - gmax), axis=0, keepdims=True)
    el = lt[SUBLANES:2 * SUBLANES, :]
    for g in range(1, N_GROUPS):
        el = jnp.where(gsel == float(g), lt[(g + 1) * SUBLANES:(g + 2) * SUBLANES, :], el)
    v1 = jnp.max(el, axis=0, keepdims=True)
    i1 = _first_index_of(el == v1, row8, float(EPG))
    el2 = jnp.where(row8 == i1, -jnp.inf, el)
    v2 = jnp.max(el2, axis=0, keepdims=True)
    i2 = _first_index_of(el2 == v2, row8, float(EPG))
    t = jnp.exp(v2 - v1)
    p1 = gw / (1.0 + t)
    p2 = gw * t / (1.0 + t)
    e1 = gsel * float(EPG) + i1
    e2 = gsel * float(EPG) + i2

    rowe = lax.broadcasted_iota(jnp.int32, (N_EXPERTS, tm), 0).astype(F32)
    oh1 = (rowe == e1).astype(F32)
    oh2 = (rowe == e2).astype(F32)
    r_i = lax.broadcasted_iota(jnp.int32, (tm, tm), 0)
    c_i = lax.broadcasted_iota(jnp.int32, (tm, tm), 1)
    upper = jnp.where(r_i < c_i, 1.0, 0.0).astype(BF16)
    cum1 = _dot(oh1.astype(BF16), upper)
    cum2 = _dot(oh2.astype(BF16), upper)
    tot1 = jnp.sum(oh1, axis=1, keepdims=True)
    tot2 = jnp.sum(oh2, axis=1, keepdims=True)
    rank1 = jnp.sum(oh1 * (carry + cum1), axis=0, keepdims=True)
    rank2 = jnp.sum(oh2 * (carry + tot1 + cum2), axis=0, keepdims=True)

    ri_ref[4:8, :] = jnp.zeros((4, tm), jnp.int32)
    ri_ref[0:1, :] = e1.astype(jnp.int32)
    ri_ref[1:2, :] = e2.astype(jnp.int32)
    ri_ref[2:3, :] = rank1.astype(jnp.int32)
    ri_ref[3:4, :] = rank2.astype(jnp.int32)

    row128 = lax.broadcasted_iota(jnp.int32, (LANES, tm), 0)
    pw = jnp.where(row128 == 0, p1, jnp.where(row128 == 1, p2, 0.0))
    rf_ref[...] = pw.T
    return carry + tot1 + tot2


def _mix_call(x2, o_n, u, wg, bg, woa, woc, wmo, g1, b1, wr, br, n_slots):
    T, D = x2.shape
    tm = MIX_TM
    n = T // tm
    slots_per_step = n_slots // n
    assert slots_per_step * n == n_slots and slots_per_step % SUBLANES == 0
    const = lambda i: (0, 0)
    row = lambda i: (i, 0)
    return pl.pallas_call(
        _mix_kernel,
        grid=(n,),
        in_specs=[
            pl.BlockSpec((tm, D), row),
            pl.BlockSpec((tm, QK_W), row),
            pl.BlockSpec((tm, CONV_W), row),
            pl.BlockSpec(wg.shape, const),
            pl.BlockSpec(bg.shape, const),
            pl.BlockSpec(woa.shape, const),
            pl.BlockSpec(woc.shape, const),
            pl.BlockSpec(wmo.shape, const),
            pl.BlockSpec(g1.shape, const),
            pl.BlockSpec(b1.shape, const),
            pl.BlockSpec(wr.shape, const),
            pl.BlockSpec(br.shape, const),
        ],
        out_specs=[
            pl.BlockSpec((tm, D), row),
            pl.BlockSpec((tm, D // 2), row),
            pl.BlockSpec((SUBLANES, tm), lambda i: (0, i)),
            pl.BlockSpec((tm, LANES), row),
            pl.BlockSpec((N_EXPERTS, LANES), const),
            pl.BlockSpec((slots_per_step, D // 2), row),
        ],
        out_shape=[
            jax.ShapeDtypeStruct((T, D), F32),
            jax.ShapeDtypeStruct((T, D // 2), jnp.uint32),
            jax.ShapeDtypeStruct((SUBLANES, T), jnp.int32),
            jax.ShapeDtypeStruct((T, LANES), F32),
            jax.ShapeDtypeStruct((N_EXPERTS, LANES), jnp.int32),
            jax.ShapeDtypeStruct((n_slots, D // 2), jnp.uint32),
        ],
        scratch_shapes=[pltpu.VMEM((N_EXPERTS, LANES), F32)],
        compiler_params=pltpu.CompilerParams(
            dimension_semantics=("arbitrary",),
            vmem_limit_bytes=VMEM_LIMIT),
        name="mix_ln_router",
    )(x2, o_n, u, wg, bg, woa, woc, wmo, g1, b1, wr, br)


def _sc_scatter_rows(h1p, dest, xs_ref):
    T, W = h1p.shape
    n_workers = SC_CORES * SC_SUBCORES
    per = T // n_workers
    n_win = per // SC_WIN
    assert n_win * SC_WIN * n_workers == T
    mesh = plsc.VectorSubcoreMesh(core_axis_name="core", subcore_axis_name="subcore")

    @pl.kernel(out_type=(), mesh=mesh,
               scratch_types=[pltpu.VMEM((2, SC_WIN, W), h1p.dtype),
                              pltpu.VMEM((2, SC_WIN), jnp.int32), pltpu.VMEM((2, SC_WIN), jnp.int32),
                              pltpu.SemaphoreType.DMA((2,)), pltpu.SemaphoreType.DMA((2,))],
               name="moe_dispatch_sc")
    def scatter(x_hbm, d0_hbm, d1_hbm, o_hbm, xbuf, i0, i1, load_sem, scat_sem):
        wid = lax.axis_index("core") * SC_SUBCORES + lax.axis_index("subcore")

        def loads(j, s):
            base = wid * per + j * SC_WIN
            return [pltpu.make_async_copy(x_hbm.at[pl.ds(base, SC_WIN)], xbuf.at[s], load_sem.at[s]),
                    pltpu.make_async_copy(d0_hbm.at[pl.ds(base, SC_WIN)], i0.at[s], load_sem.at[s]),
                    pltpu.make_async_copy(d1_hbm.at[pl.ds(base, SC_WIN)], i1.at[s], load_sem.at[s])]

        def scatters(s):
            return [pltpu.make_async_copy(xbuf.at[s], o_hbm.at[i0.at[s]], scat_sem.at[s]),
                    pltpu.make_async_copy(xbuf.at[s], o_hbm.at[i1.at[s]], scat_sem.at[s])]

        for c in loads(0, 0):
            c.start()
        for j in range(n_win):
            s = j % 2
            for c in loads(j, s):
                c.wait()
            for c in scatters(s):
                c.start()
            if j >= 1:
                for c in scatters(1 - s):
                    c.wait()
            if j + 1 < n_win:
                for c in loads(j + 1, 1 - s):
                    c.start()
        for c in scatters((n_win - 1) % 2):
            c.wait()

    scatter(h1p, dest[0], dest[1], xs_ref)


def _expert_kernel(be_ref, nu_ref, xs_ref, wg_ref, wu_ref, wd_ref, ys_ref, wgb_ref, wub_ref, wdb_ref):
    i = pl.program_id(0)
    used = i < nu_ref[0]

    @pl.when(used & ((i == 0) | (be_ref[i] != be_ref[jnp.maximum(i - 1, 0)])))
    def _():
        wgb_ref[...] = wg_ref[...].astype(BF16)
        wub_ref[...] = wu_ref[...].astype(BF16)
        wdb_ref[...] = wd_ref[...].astype(BF16)

    @pl.when(used)
    def _():
        xb = _unpack_bf16_pairs(xs_ref[...])
        g = _dot(xb, wgb_ref[...])
        u = _dot(xb, wub_ref[...])
        h = (g * jax.nn.sigmoid(g) * u).astype(BF16)
        ys_ref[...] = _pack_bf16_pairs(_dot(h, wdb_ref[...]))

    @pl.when(i >= nu_ref[0])
    def _():
        ys_ref[...] = jnp.zeros(ys_ref.shape, ys_ref.dtype)


def _expert_call(block_expert, n_used, xs, wg, wu, wd):
    P = xs.shape[0]
    bm = MOE_BM
    nb = P // bm
    xs_map = lambda i, be, nu: (jnp.minimum(i, nu[0] - 1), 0)
    w_map = lambda i, be, nu: (be[i], 0, 0)
    return pl.pallas_call(
        _expert_kernel,
        grid_spec=pltpu.PrefetchScalarGridSpec(
            num_scalar_prefetch=2,
            grid=(nb,),
            in_specs=[
                pl.BlockSpec((bm, D_MODEL // 2), xs_map),
                pl.BlockSpec((None, D_MODEL, EXPERT_FF), w_map),
                pl.BlockSpec((None, D_MODEL, EXPERT_FF), w_map),
                pl.BlockSpec((None, EXPERT_FF, D_MODEL), w_map),
            ],
            out_specs=pl.BlockSpec((bm, D_MODEL // 2), lambda i, be, nu: (i, 0)),
            scratch_shapes=[
                pltpu.VMEM((D_MODEL, EXPERT_FF), BF16),
                pltpu.VMEM((D_MODEL, EXPERT_FF), BF16),
                pltpu.VMEM((EXPERT_FF, D_MODEL), BF16),
            ],
        ),
        out_shape=jax.ShapeDtypeStruct((P, D_MODEL // 2), jnp.uint32),
        compiler_params=pltpu.CompilerParams(
            dimension_semantics=("arbitrary",),
            vmem_limit_bytes=VMEM_LIMIT),
        name="moe_experts",
    )(block_expert, n_used, xs, wg, wu, wd)


def _sc_gather_rows(ys, dest):
    _, T = dest.shape
    W = ys.shape[1]
    n_workers = SC_CORES * SC_SUBCORES
    per = T // n_workers
    win = SC_WIN // 2
    n_win = per // win
    assert n_win * win * n_workers == T
    mesh = plsc.VectorSubcoreMesh(core_axis_name="core", subcore_axis_name="subcore")

    @pl.kernel(out_type=jax.ShapeDtypeStruct((2, T, W), ys.dtype), mesh=mesh,
               scratch_types=[pltpu.VMEM((2, 2, win, W), ys.dtype), pltpu.VMEM((2, 2, win), jnp.int32),
                              pltpu.SemaphoreType.DMA((2,)), pltpu.SemaphoreType.DMA((2,)),
                              pltpu.SemaphoreType.DMA((2,))],
               name="moe_combine_sc")
    def gather(y_hbm, d0_hbm, d1_hbm, o_hbm, buf, idx, idx_sem, gat_sem, out_sem):
        wid = lax.axis_index("core") * SC_SUBCORES + lax.axis_index("subcore")
        d_hbm = (d0_hbm, d1_hbm)

        def base(j):
            return wid * per + j * win

        def idx_loads(j, s):
            return [pltpu.make_async_copy(d_hbm[k].at[pl.ds(base(j), win)], idx.at[s, k], idx_sem.at[s])
                    for k in range(2)]

        def gathers(s):
            return [pltpu.make_async_copy(y_hbm.at[idx.at[s, k]], buf.at[s, k], gat_sem.at[s]) for k in range(2)]

        def stores(j, s):
            return [pltpu.make_async_copy(buf.at[s, k], o_hbm.at[k, pl.ds(base(j), win)], out_sem.at[s])
                    for k in range(2)]

        for c in idx_loads(0, 0):
            c.start()
        for j in range(n_win):
            s = j % 2
            for c in idx_loads(j, s):
                c.wait()
            if j >= 2:
                for c in stores(j - 2, s):
                    c.wait()
            for c in gathers(s):
                c.start()
            if j >= 1:
                for c in gathers(1 - s):
                    c.wait()
                for c in stores(j - 1, 1 - s):
                    c.start()
            if j + 1 < n_win:
                for c in idx_loads(j + 1, 1 - s):
                    c.start()
        s_last = (n_win - 1) % 2
        for c in gathers(s_last):
            c.wait()
        for c in stores(n_win - 1, s_last):
            c.start()
        if n_win >= 2:
            for c in stores(n_win - 2, 1 - s_last):
                c.wait()
        for c in stores(n_win - 1, s_last):
            c.wait()

    return gather(ys, dest[0], dest[1])


def _combine_kernel(y_ref, h1_ref, rf_ref, g2_ref, b2_ref, out_ref):
    rf = rf_ref[...]
    def unpack_f32(w):
        return jnp.concatenate([lax.bitcast_convert_type(w << 16, F32),
                                lax.bitcast_convert_type(w & jnp.uint32(0xFFFF0000), F32)], axis=1)

    ffn = rf[:, 0:1] * unpack_f32(y_ref[0]) + rf[:, 1:2] * unpack_f32(y_ref[1])
    z = DEEPNORM_ALPHA * h1_ref[...] + ffn
    mu = jnp.mean(z, axis=-1, keepdims=True)
    zc = z - mu
    var = jnp.mean(zc * zc, axis=-1, keepdims=True)
    out_ref[...] = zc * lax.rsqrt(var + LN_EPS) * g2_ref[...] + b2_ref[...]


def _combine_call(y12, h1, rf, g2, b2):
    T, D = h1.shape
    tm = COMB_TM
    const = lambda i: (0, 0)
    row = lambda i: (i, 0)
    return pl.pallas_call(
        _combine_kernel,
        grid=(T // tm,),
        in_specs=[
            pl.BlockSpec((2, tm, D // 2), lambda i: (0, i, 0)),
            pl.BlockSpec((tm, D), row),
            pl.BlockSpec((tm, LANES), row),
            pl.BlockSpec(g2.shape, const),
            pl.BlockSpec(b2.shape, const),
        ],
        out_specs=pl.BlockSpec((tm, D), row),
        out_shape=jax.ShapeDtypeStruct((T, D), F32),
        compiler_params=pltpu.CompilerParams(
            dimension_semantics=("arbitrary",),
            vmem_limit_bytes=VMEM_LIMIT),
        name="moe_combine_ln",
    )(y12, h1, rf, g2, b2)


def kernel(x, positions, w_in, b_gate, lambda_q1, lambda_k1, lambda_q2, lambda_k2, subln_g, w_o_att, conv_w, w_o_conv, w_mix_out, ln1_g, ln1_b, w_router_group, b_router_group, w_router_expert, b_router_expert, w_exp_gate, w_exp_up, w_exp_down, ln2_g, ln2_b):
    B, S, D = x.shape
    T = B * S
    l = 0
    lambda_init = 0.8 - 0.6 * math.exp(-0.3 * l)

    inv_freq = ROPE_THETA ** (-jnp.arange(0, HEAD_DIM, 2, dtype=F32) / HEAD_DIM)
    pos_f = positions.astype(F32)[..., None]
    inv_f = jnp.tile(inv_freq, 4).reshape(1, LANES)

    n_main = 2 * QK_W + QK_W + 3 * CONV_W
    w0 = w_in[l]
    w_main = w0[:, :n_main].astype(BF16)
    wv_t = w0[:, 2 * QK_W:3 * QK_W].T.astype(BF16)
    w_gates = w0[:, n_main:].astype(BF16)

    q, k, vt, u = _proj_call(x, w_main, wv_t, pos_f, inv_f, conv_w[l])

    lam = (jnp.exp(jnp.sum(lambda_q1[l].astype(F32) * lambda_k1[l].astype(F32)))
           - jnp.exp(jnp.sum(lambda_q2[l].astype(F32) * lambda_k2[l].astype(F32)))
           + lambda_init).reshape(1)
    o_n = _attn_call(lam, q, k, vt, subln_g[l].astype(F32).reshape(HEAD_W, 1), 1.0 - lambda_init)

    wr = jnp.zeros((D, LANES), F32)
    wr = wr.at[:, 0:N_GROUPS].set(w_router_group[l]).at[:, SUBLANES:SUBLANES + N_EXPERTS].set(w_router_expert[l])
    br = jnp.zeros((1, LANES), F32)
    br = br.at[0, 0:N_GROUPS].set(b_router_group[l]).at[0, SUBLANES:SUBLANES + N_EXPERTS].set(b_router_expert[l])

    bm = MOE_BM
    A = 2 * T
    P = A + N_EXPERTS * bm
    nb = P // bm
    h1, h1p, ri, rf, cnt, xs0 = _mix_call(
        x.reshape(T, D), o_n.reshape(T, QK_W), u.reshape(T, CONV_W),
        w_gates, b_gate[l].reshape(1, -1).astype(F32),
        w_o_att[l].astype(BF16), w_o_conv[l].astype(BF16), w_mix_out[l].astype(BF16),
        ln1_g[l].reshape(1, D).astype(F32), ln1_b[l].reshape(1, D).astype(F32),
        wr.astype(BF16), br, P)

    counts = cnt[:, 0]
    padded = (counts + bm - 1) // bm * bm
    pend = jnp.cumsum(padded)
    pstart = pend - padded
    onehot = (ri[0:2, :, None] == jnp.arange(N_EXPERTS, dtype=jnp.int32)).astype(jnp.int32)
    dest = jnp.sum(onehot * pstart, axis=-1) + ri[2:4]
    blk_start = jnp.arange(nb, dtype=jnp.int32) * bm
    block_expert = jnp.minimum(
        jnp.sum((pend[None, :] <= blk_start[:, None]).astype(jnp.int32), axis=1), N_EXPERTS - 1)
    n_used = (pend[-1] // bm).astype(jnp.int32).reshape(1)

    xs_ref = jax.new_ref(xs0)
    _sc_scatter_rows(h1p, dest, xs_ref)
    xs = xs_ref[...]
    ys = _expert_call(block_expert, n_used, xs, w_exp_gate[l], w_exp_up[l], w_exp_down[l])
    out = _combine_call(_sc_gather_rows(ys, dest), h1, rf,
                        ln2_g[l].reshape(1, D).astype(F32), ln2_b[l].reshape(1, D).astype(F32))
    return out.reshape(B, S, D)
```

```python
import functools
import math

import jax
import jax.numpy as jnp
from jax import lax
from jax.experimental import pallas as pl
from jax.experimental.pallas import tpu as pltpu
from jax.experimental.pallas import tpu_sc as plsc

D_MODEL = 1024
ATT_HEADS = 4
HEAD_DIM = 64
HEAD_W = 2 * HEAD_DIM
QK_W = ATT_HEADS * HEAD_W
CONV_W = D_MODEL // 2
CONV_K = 3
N_GROUPS = 4
EPG = 8
N_EXPERTS = N_GROUPS * EPG
EXPERT_FF = D_MODEL // 2
ROPE_THETA = 10000.0
SUBLN_EPS = 1e-5
LN_EPS = 1e-5
DEPTH = 1
DEEPNORM_ALPHA = (2.0 * DEPTH) ** 0.25

LANES = 128
SUBLANES = 8
VMEM_LIMIT = 48 * 1024 * 1024

PROJ_TM = 512
ATT_TQ = 256
ATT_TK = 256
ATT_UNROLL = 4
MIX_TM = 512
MOE_BM = 512
SC_CORES = 2
SC_SUBCORES = 16
SC_WIN = 64
COMB_TM = 512
NEG_BIG = -1e30

F32 = jnp.float32
BF16 = jnp.bfloat16


def _dot(a, b):
    return jnp.dot(a, b, preferred_element_type=F32)


def _dot_nt(a, b):
    return lax.dot_general(a, b, (((1,), (1,)), ((), ())), preferred_element_type=F32)


def _proj_kernel(x_ref, w_ref, wqt_ref, wvt_ref, pos_ref, invf_ref, cw_ref,
                 qt_ref, k_ref, vt_ref, u_ref, pbuf):
    tm = x_ref.shape[0]
    i = pl.program_id(1)

    @pl.when(i == 0)
    def _():
        pbuf[0:SUBLANES, :] = jnp.zeros((SUBLANES, CONV_W), F32)

    xb = x_ref[...].astype(BF16)
    half = HEAD_DIM // 2
    ang_t = invf_ref[...] * pos_ref[...]
    cos_t = jnp.cos(ang_t)
    sin_t = jnp.sin(ang_t)

    q_scale = HEAD_DIM ** -0.5 * math.log2(math.e)
    qt = _dot_nt(wqt_ref[...], xb)
    for m in range(QK_W // HEAD_DIM):
        lo = qt[m * HEAD_DIM:m * HEAD_DIM + half, :]
        hi = qt[m * HEAD_DIM + half:(m + 1) * HEAD_DIM, :]
        qt_ref[m * HEAD_DIM:m * HEAD_DIM + half, :] = ((lo * cos_t - hi * sin_t) * q_scale).astype(BF16)
        qt_ref[m * HEAD_DIM + half:(m + 1) * HEAD_DIM, :] = ((hi * cos_t + lo * sin_t) * q_scale).astype(BF16)

    lane = lax.broadcasted_iota(jnp.int32, (tm, LANES), 1)
    low_half = (lane % HEAD_DIM) < half
    cos = jnp.concatenate([cos_t, cos_t, cos_t, cos_t], axis=0).T
    sin = jnp.concatenate([-sin_t, sin_t, -sin_t, sin_t], axis=0).T

    def rope(t):
        rot = jnp.where(low_half, pltpu.roll(t, LANES - half, 1), pltpu.roll(t, half, 1))
        return t * cos + rot * sin

    k = _dot(xb, w_ref[:, 0:QK_W])
    for h in range(ATT_HEADS):
        sl = slice(h * HEAD_W, (h + 1) * HEAD_W)
        k_ref[:, sl] = rope(k[:, sl]).astype(BF16)
    vt_ref[...] = _dot_nt(wvt_ref[...], xb).astype(BF16)

    c0 = QK_W
    cb = _dot(xb, w_ref[:, c0:c0 + CONV_W])
    cc = _dot(xb, w_ref[:, c0 + CONV_W:c0 + 2 * CONV_W])
    cx = _dot(xb, w_ref[:, c0 + 2 * CONV_W:c0 + 3 * CONV_W])
    p = cc * cx
    pbuf[SUBLANES:SUBLANES + tm, :] = p
    y = (cw_ref[0:1, :] * pbuf[SUBLANES - 2:SUBLANES - 2 + tm, :]
         + cw_ref[1:2, :] * pbuf[SUBLANES - 1:SUBLANES - 1 + tm, :]
         + cw_ref[2:3, :] * p)
    u_ref[...] = (cb * y).astype(BF16)
    pbuf[0:SUBLANES, :] = pbuf[tm:tm + SUBLANES, :]


def _proj_call(x, w_main, wq_t, wv_t, pos_f, inv_f, conv_w):
    B, S, D = x.shape
    tm = PROJ_TM
    n_main = w_main.shape[1]
    return pl.pallas_call(
        _proj_kernel,
        grid=(B, S // tm),
        in_specs=[
            pl.BlockSpec((None, tm, D), lambda b, i: (b, i, 0)),
            pl.BlockSpec((D, n_main), lambda b, i: (0, 0)),
            pl.BlockSpec((QK_W, D), lambda b, i: (0, 0)),
            pl.BlockSpec((QK_W, D), lambda b, i: (0, 0)),
            pl.BlockSpec((None, 1, tm), lambda b, i: (b, 0, i)),
            pl.BlockSpec((HEAD_DIM // 2, 1), lambda b, i: (0, 0)),
            pl.BlockSpec((CONV_K, CONV_W), lambda b, i: (0, 0)),
        ],
        out_specs=[
            pl.BlockSpec((None, QK_W, tm), lambda b, i: (b, 0, i)),
            pl.BlockSpec((None, tm, QK_W), lambda b, i: (b, i, 0)),
            pl.BlockSpec((None, QK_W, tm), lambda b, i: (b, 0, i)),
            pl.BlockSpec((None, tm, CONV_W), lambda b, i: (b, i, 0)),
        ],
        out_shape=[
            jax.ShapeDtypeStruct((B, QK_W, S), BF16),
            jax.ShapeDtypeStruct((B, S, QK_W), BF16),
            jax.ShapeDtypeStruct((B, QK_W, S), BF16),
            jax.ShapeDtypeStruct((B, S, CONV_W), BF16),
        ],
        scratch_shapes=[pltpu.VMEM((tm + 2 * SUBLANES, CONV_W), F32)],
        compiler_params=pltpu.CompilerParams(
            dimension_semantics=("arbitrary", "arbitrary"),
            vmem_limit_bytes=VMEM_LIMIT),
        name="proj_rope_conv",
    )(x, w_main, wq_t, wv_t, pos_f, inv_f, conv_w)


ACC_ROWS = HEAD_W + 16


def _attn_kernel(lam_ref, q_ref, k_ref, vt_ref, g_ref, o_ref, qzt_ref, *scratch, out_scale):
    nh = ATT_HEADS
    s_refs, p_refs = scratch[0:nh], scratch[nh:2 * nh]
    m_refs, a_refs, acc_refs = scratch[2 * nh:3 * nh], scratch[3 * nh:4 * nh], scratch[4 * nh:5 * nh]
    tq = q_ref.shape[1]
    tk = ATT_TK
    nblk = 2 * tq // LANES
    i = pl.program_id(1)
    feat = lax.broadcasted_iota(jnp.int32, (HEAD_W, tq), 0)
    ones_rows = jnp.ones((ACC_ROWS - HEAD_W, tk), BF16)

    for h in range(nh):
        qf = q_ref[h * HEAD_W:(h + 1) * HEAD_W, :].astype(F32)
        qzt_ref[h, :, 0:tq] = jnp.where(feat < HEAD_DIM, qf, 0.0).astype(BF16)
        qzt_ref[h, :, tq:2 * tq] = jnp.where(feat >= HEAD_DIM, qf, 0.0).astype(BF16)
        m_refs[h][...] = jnp.full(m_refs[h].shape, NEG_BIG, F32)
        acc_refs[h][...] = jnp.zeros(acc_refs[h].shape, F32)

    def scores(h, k0):
        kj = k_ref[pl.ds(k0, tk), h * HEAD_W:(h + 1) * HEAD_W]
        s_refs[h][:, 0:2 * tq] = _dot(kj, qzt_ref[h])

    def softmax(h, diag=None):
        for c in range(nblk):
            cs = slice(c * LANES, (c + 1) * LANES)
            s = s_refs[h][:, cs]
            if diag is not None:
                key = lax.broadcasted_iota(jnp.int32, (tk, LANES), 0) + diag * tk
                qry = lax.broadcasted_iota(jnp.int32, (tk, LANES), 1) + (c * LANES) % tq
                s = jnp.where(key <= qry, s, NEG_BIG)
            m_old = m_refs[h][:, cs]
            m_new = jnp.maximum(m_old, jnp.max(s, axis=0, keepdims=True))
            a_refs[h][:, cs] = jnp.exp2(m_old - m_new)
            m_refs[h][:, cs] = m_new
            p_refs[h][:, cs] = jnp.exp2(s - m_new).astype(BF16)

    def accumulate(h, k0):
        vj = jnp.concatenate([vt_ref[h * HEAD_W:(h + 1) * HEAD_W, pl.ds(k0, tk)], ones_rows], axis=0)
        acc_refs[h][...] = a_refs[h][...] * acc_refs[h][...] + _dot(vj, p_refs[h][:, 0:2 * tq])

    n_diag = tq // tk
    n_full = n_diag * i

    def tile_start(j):
        return pl.multiple_of(j * tk, tk)

    for h in range(nh):
        scores(h, 0)

    @pl.when(i > 0)
    def _():
        for h in range(nh):
            softmax(h)
            scores(h, tile_start(1))

    def step(j):
        for h in range(nh):
            accumulate(h, tile_start(j - 1))
            softmax(h)
            scores(h, tile_start(j + 1))

    n_steps = jnp.maximum(n_full - 1, 0)
    n_trips = n_steps // ATT_UNROLL

    def body(t, carry):
        for u in range(ATT_UNROLL):
            step(ATT_UNROLL * t + 1 + u)
        return carry

    lax.fori_loop(0, n_trips, body, 0)

    def tail(j, carry):
        step(j)
        return carry

    lax.fori_loop(ATT_UNROLL * n_trips + 1, n_steps + 1, tail, 0)

    @pl.when(i > 0)
    def _():
        for h in range(nh):
            accumulate(h, tile_start(n_full - 1))

    for d in range(n_diag):
        for h in range(nh):
            softmax(h, diag=d)
            if d + 1 < n_diag:
                scores(h, tile_start(n_full + d + 1))
            accumulate(h, tile_start(n_full + d))

    lam = lam_ref[0]
    for h in range(nh):
        acc = acc_refs[h][0:HEAD_W, :]
        l = acc_refs[h][HEAD_W:HEAD_W + 1, :]
        o = acc[:, 0:tq] / l[:, 0:tq] - lam * (acc[:, tq:2 * tq] / l[:, tq:2 * tq])
        ms = jnp.mean(o * o, axis=0, keepdims=True)
        o = o * lax.rsqrt(ms + SUBLN_EPS) * g_ref[...] * out_scale
        o_ref[:, h * HEAD_W:(h + 1) * HEAD_W] = o.T.astype(BF16)


def _attn_call(lam, q, k, vt, g_col, out_scale):
    B, S, _ = k.shape
    tq = ATT_TQ
    assert ATT_TQ % ATT_TK == 0
    kernel = functools.partial(_attn_kernel, out_scale=out_scale)
    return pl.pallas_call(
        kernel,
        grid=(B, S // tq),
        in_specs=[
            pl.BlockSpec(memory_space=pltpu.SMEM),
            pl.BlockSpec((None, QK_W, tq), lambda b, i: (b, 0, i)),
            pl.BlockSpec((None, S, QK_W), lambda b, i: (b, 0, 0), pipeline_mode=pl.Buffered(1)),
            pl.BlockSpec((None, QK_W, S), lambda b, i: (b, 0, 0), pipeline_mode=pl.Buffered(1)),
            pl.BlockSpec((HEAD_W, 1), lambda b, i: (0, 0)),
        ],
        out_specs=pl.BlockSpec((None, tq, QK_W), lambda b, i: (b, i, 0)),
        out_shape=jax.ShapeDtypeStruct((B, S, QK_W), BF16),
        scratch_shapes=(
            [pltpu.VMEM((ATT_HEADS, HEAD_W, 2 * tq), BF16)]
            + [pltpu.VMEM((ATT_TK, 2 * tq + LANES), F32) for _ in range(ATT_HEADS)]
            + [pltpu.VMEM((ATT_TK, 2 * tq + LANES), BF16) for _ in range(ATT_HEADS)]
            + [pltpu.VMEM((1, 2 * tq), F32) for _ in range(ATT_HEADS)]
            + [pltpu.VMEM((1, 2 * tq), F32) for _ in range(ATT_HEADS)]
            + [pltpu.VMEM((ACC_ROWS, 2 * tq), F32) for _ in range(ATT_HEADS)]
        ),
        compiler_params=pltpu.CompilerParams(
            dimension_semantics=("arbitrary", "arbitrary"),
            vmem_limit_bytes=VMEM_LIMIT),
        name="diff_flash_attn",
    )(lam, q, k, vt, g_col)


def _pack_bf16_pairs(h):
    half = h.shape[1] // 2
    hb = h.astype(BF16).astype(F32)
    bits = lax.bitcast_convert_type(hb, jnp.uint32)
    return (bits[:, 0:half] >> 16) | (bits[:, half:] & jnp.uint32(0xFFFF0000))


def _unpack_bf16_pairs(w):
    lo = lax.bitcast_convert_type(w << 16, F32).astype(BF16)
    hi = lax.bitcast_convert_type(w & jnp.uint32(0xFFFF0000), F32).astype(BF16)
    return jnp.concatenate([lo, hi], axis=1)


def _first_index_of(mask, row_f, big):
    return jnp.min(jnp.where(mask, row_f, big), axis=0, keepdims=True)


def _mix_kernel(x_ref, o_ref, u_ref, wg_ref, bg_ref, woa_ref, woc_ref, wmo_ref, g1_ref, b1_ref,
                wr_ref, br_ref,
                h1_ref, h1p_ref, ri_ref, rf_ref, cnt_ref, xs0_ref, carry_ref):
    step = pl.program_id(0)
    xs0_ref[...] = jnp.zeros(xs0_ref.shape, xs0_ref.dtype)

    @pl.when(step == 0)
    def _():
        carry_ref[...] = jnp.zeros(carry_ref.shape, F32)

    x = x_ref[...]
    xb = x.astype(BF16)
    gates = jax.nn.sigmoid(_dot(xb, wg_ref[...]) + bg_ref[...])
    ya = _dot(o_ref[...], woa_ref[...])
    yc = _dot(u_ref[...], woc_ref[...])
    merged = (gates[:, 0:D_MODEL] * ya + gates[:, D_MODEL:] * yc).astype(BF16)
    z = DEEPNORM_ALPHA * x + _dot(merged, wmo_ref[...])

    carry = carry_ref[:, 0:1]
    carry = _mix_tail(z, carry, g1_ref, b1_ref, wr_ref, br_ref, h1_ref, h1p_ref, ri_ref, rf_ref)
    new_carry = jnp.broadcast_to(carry, carry_ref.shape)
    carry_ref[...] = new_carry
    cnt_ref[...] = new_carry.astype(jnp.int32)


def _mix_tail(z, carry, g1_ref, b1_ref, wr_ref, br_ref, h1_ref, h1p_ref, ri_ref, rf_ref):
    tm = z.shape[0]
    mu = jnp.mean(z, axis=-1, keepdims=True)
    zc = z - mu
    var = jnp.mean(zc * zc, axis=-1, keepdims=True)
    h1 = zc * lax.rsqrt(var + LN_EPS) * g1_ref[...] + b1_ref[...]
    h1_ref[...] = h1
    h1p_ref[...] = _pack_bf16_pairs(h1)

    logits = _dot(h1.astype(BF16), wr_ref[...]) + br_ref[...]
    lt = logits.T
    row8 = lax.broadcasted_iota(jnp.int32, (SUBLANES, tm), 0).astype(F32)
    gl = jnp.where(row8 < N_GROUPS, lt[0:SUBLANES, :], NEG_BIG)
    gmax = jnp.max(gl, axis=0, keepdims=True)
    gsel = _first_index_of(gl == gmax, row8, float(SUBLANES))
    gw = 1.0 / jnp.sum(jnp.exp(gl - gmax), axis=0, keepdims=True)
    el = lt[SUBLANES:2 * SUBLANES, :]
    for g in range(1, N_GROUPS):
        el = jnp.where(gsel == float(g), lt[(g + 1) * SUBLANES:(g + 2) * SUBLANES, :], el)
    v1 = jnp.max(el, axis=0, keepdims=True)
    i1 = _first_index_of(el == v1, row8, float(EPG))
    el2 = jnp.where(row8 == i1, -jnp.inf, el)
    v2 = jnp.max(el2, axis=0, keepdims=True)
    i2 = _first_index_of(el2 == v2, row8, float(EPG))
    t = jnp.exp(v2 - v1)
    p1 = gw / (1.0 + t)
    p2 = gw * t / (1.0 + t)
    e1 = gsel * float(EPG) + i1
    e2 = gsel * float(EPG) + i2

    rowe = lax.broadcasted_iota(jnp.int32, (N_EXPERTS, tm), 0).astype(F32)
    oh1 = (rowe == e1).astype(F32)
    oh2 = (rowe == e2).astype(F32)
    r_i = lax.broadcasted_iota(jnp.int32, (tm, tm), 0)
    c_i = lax.broadcasted_iota(jnp.int32, (tm, tm), 1)
    upper = jnp.where(r_i < c_i, 1.0, 0.0).astype(BF16)
    cum1 = _dot(oh1.astype(BF16), upper)
    cum2 = _dot(oh2.astype(BF16), upper)
    tot1 = jnp.sum(oh1, axis=1, keepdims=True)
    tot2 = jnp.sum(oh2, axis=1, keepdims=True)
    rank1 = jnp.sum(oh1 * (carry + cum1), axis=0, keepdims=True)
    rank2 = jnp.sum(oh2 * (carry + tot1 + cum2), axis=0, keepdims=True)

    ri_ref[4:8, :] = jnp.zeros((4, tm), jnp.int32)
    ri_ref[0:1, :] = e1.astype(jnp.int32)
    ri_ref[1:2, :] = e2.astype(jnp.int32)
    ri_ref[2:3, :] = rank1.astype(jnp.int32)
    ri_ref[3:4, :] = rank2.astype(jnp.int32)

    row128 = lax.broadcasted_iota(jnp.int32, (LANES, tm), 0)
    pw = jnp.where(row128 == 0, p1, jnp.where(row128 == 1, p2, 0.0))
    rf_ref[...] = pw.T
    return carry + tot1 + tot2


def _mix_call(x2, o_n, u, wg, bg, woa, woc, wmo, g1, b1, wr, br, n_slots):
    T, D = x2.shape
    tm = MIX_TM
    n = T // tm
    slots_per_step = n_slots // n
    assert slots_per_step * n == n_slots and slots_per_step % SUBLANES == 0
    const = lambda i: (0, 0)
    row = lambda i: (i, 0)
    return pl.pallas_call(
        _mix_kernel,
        grid=(n,),
        in_specs=[
            pl.BlockSpec((tm, D), row),
            pl.BlockSpec((tm, QK_W), row),
            pl.BlockSpec((tm, CONV_W), row),
            pl.BlockSpec(wg.shape, const),
            pl.BlockSpec(bg.shape, const),
            pl.BlockSpec(woa.shape, const),
            pl.BlockSpec(woc.shape, const),
            pl.BlockSpec(wmo.shape, const),
            pl.BlockSpec(g1.shape, const),
            pl.BlockSpec(b1.shape, const),
            pl.BlockSpec(wr.shape, const),
            pl.BlockSpec(br.shape, const),
        ],
        out_specs=[
            pl.BlockSpec((tm, D), row),
            pl.BlockSpec((tm, D // 2), row),
            pl.BlockSpec((SUBLANES, tm), lambda i: (0, i)),
            pl.BlockSpec((tm, LANES), row),
            pl.BlockSpec((N_EXPERTS, LANES), const),
            pl.BlockSpec((slots_per_step, D // 2), row),
        ],
        out_shape=[
            jax.ShapeDtypeStruct((T, D), F32),
            jax.ShapeDtypeStruct((T, D // 2), jnp.uint32),
            jax.ShapeDtypeStruct((SUBLANES, T), jnp.int32),
            jax.ShapeDtypeStruct((T, LANES), F32),
            jax.ShapeDtypeStruct((N_EXPERTS, LANES), jnp.int32),
            jax.ShapeDtypeStruct((n_slots, D // 2), jnp.uint32),
        ],
        scratch_shapes=[pltpu.VMEM((N_EXPERTS, LANES), F32)],
        compiler_params=pltpu.CompilerParams(
            dimension_semantics=("arbitrary",),
            vmem_limit_bytes=VMEM_LIMIT),
        name="mix_ln_router",
    )(x2, o_n, u, wg, bg, woa, woc, wmo, g1, b1, wr, br)


def _sc_scatter_rows(h1p, dest, xs_ref):
    T, W = h1p.shape
    n_workers = SC_CORES * SC_SUBCORES
    per = T // n_workers
    n_win = per // SC_WIN
    assert n_win * SC_WIN * n_workers == T
    mesh = plsc.VectorSubcoreMesh(core_axis_name="core", subcore_axis_name="subcore")

    @pl.kernel(out_type=(), mesh=mesh,
               scratch_types=[pltpu.VMEM((2, SC_WIN, W), h1p.dtype),
                              pltpu.VMEM((2, SC_WIN), jnp.int32), pltpu.VMEM((2, SC_WIN), jnp.int32),
                              pltpu.SemaphoreType.DMA((2,)), pltpu.SemaphoreType.DMA((2,))],
               name="moe_dispatch_sc")
    def scatter(x_hbm, d0_hbm, d1_hbm, o_hbm, xbuf, i0, i1, load_sem, scat_sem):
        wid = lax.axis_index("core") * SC_SUBCORES + lax.axis_index("subcore")

        def loads(j, s):
            base = wid * per + j * SC_WIN
            return [pltpu.make_async_copy(x_hbm.at[pl.ds(base, SC_WIN)], xbuf.at[s], load_sem.at[s]),
                    pltpu.make_async_copy(d0_hbm.at[pl.ds(base, SC_WIN)], i0.at[s], load_sem.at[s]),
                    pltpu.make_async_copy(d1_hbm.at[pl.ds(base, SC_WIN)], i1.at[s], load_sem.at[s])]

        def scatters(s):
            return [pltpu.make_async_copy(xbuf.at[s], o_hbm.at[i0.at[s]], scat_sem.at[s]),
                    pltpu.make_async_copy(xbuf.at[s], o_hbm.at[i1.at[s]], scat_sem.at[s])]

        for c in loads(0, 0):
            c.start()
        for j in range(n_win):
            s = j % 2
            for c in loads(j, s):
                c.wait()
            for c in scatters(s):
                c.start()
            if j >= 1:
                for c in scatters(1 - s):
                    c.wait()
            if j + 1 < n_win:
                for c in loads(j + 1, 1 - s):
                    c.start()
        for c in scatters((n_win - 1) % 2):
            c.wait()

    scatter(h1p, dest[0], dest[1], xs_ref)


def _expert_kernel(be_ref, nu_ref, xs_ref, wg_ref, wu_ref, wd_ref, ys_ref, wgb_ref, wub_ref, wdb_ref):
    i = pl.program_id(0)
    used = i < nu_ref[0]

    @pl.when(used & ((i == 0) | (be_ref[i] != be_ref[jnp.maximum(i - 1, 0)])))
    def _():
        wgb_ref[...] = wg_ref[...].astype(BF16)
        wub_ref[...] = wu_ref[...].astype(BF16)
        wdb_ref[...] = wd_ref[...].astype(BF16)

    @pl.when(used)
    def _():
        xb = _unpack_bf16_pairs(xs_ref[...])
        g = _dot(xb, wgb_ref[...])
        u = _dot(xb, wub_ref[...])
        h = (g * jax.nn.sigmoid(g) * u).astype(BF16)
        ys_ref[...] = _pack_bf16_pairs(_dot(h, wdb_ref[...]))

    @pl.when(i >= nu_ref[0])
    def _():
        ys_ref[...] = jnp.zeros(ys_ref.shape, ys_ref.dtype)


def _expert_call(block_expert, n_used, xs, wg, wu, wd):
    P = xs.shape[0]
    bm = MOE_BM
    nb = P // bm
    xs_map = lambda i, be, nu: (jnp.minimum(i, nu[0] - 1), 0)
    w_map = lambda i, be, nu: (be[i], 0, 0)
    return pl.pallas_call(
        _expert_kernel,
        grid_spec=pltpu.PrefetchScalarGridSpec(
            num_scalar_prefetch=2,
            grid=(nb,),
            in_specs=[
                pl.BlockSpec((bm, D_MODEL // 2), xs_map),
                pl.BlockSpec((None, D_MODEL, EXPERT_FF), w_map),
                pl.BlockSpec((None, D_MODEL, EXPERT_FF), w_map),
                pl.BlockSpec((None, EXPERT_FF, D_MODEL), w_map),
            ],
            out_specs=pl.BlockSpec((bm, D_MODEL // 2), lambda i, be, nu: (i, 0)),
            scratch_shapes=[
                pltpu.VMEM((D_MODEL, EXPERT_FF), BF16),
                pltpu.VMEM((D_MODEL, EXPERT_FF), BF16),
                pltpu.VMEM((EXPERT_FF, D_MODEL), BF16),
            ],
        ),
        out_shape=jax.ShapeDtypeStruct((P, D_MODEL // 2), jnp.uint32),
        compiler_params=pltpu.CompilerParams(
            dimension_semantics=("arbitrary",),
            vmem_limit_bytes=VMEM_LIMIT),
        name="moe_experts",
    )(block_expert, n_used, xs, wg, wu, wd)


def _sc_gather_rows(ys, dest):
    _, T = dest.shape
    W = ys.shape[1]
    n_workers = SC_CORES * SC_SUBCORES
    per = T // n_workers
    win = SC_WIN // 2
    n_win = per // win
    assert n_win * win * n_workers == T
    mesh = plsc.VectorSubcoreMesh(core_axis_name="core", subcore_axis_name="subcore")

    @pl.kernel(out_type=jax.ShapeDtypeStruct((2, T, W), ys.dtype), mesh=mesh,
               scratch_types=[pltpu.VMEM((2, 2, win, W), ys.dtype), pltpu.VMEM((2, 2, win), jnp.int32),
                              pltpu.SemaphoreType.DMA((2,)), pltpu.SemaphoreType.DMA((2,)),
                              pltpu.SemaphoreType.DMA((2,))],
               name="moe_combine_sc")
    def gather(y_hbm, d0_hbm, d1_hbm, o_hbm, buf, idx, idx_sem, gat_sem, out_sem):
        wid = lax.axis_index("core") * SC_SUBCORES + lax.axis_index("subcore")
        d_hbm = (d0_hbm, d1_hbm)

        def base(j):
            return wid * per + j * win

        def idx_loads(j, s):
            return [pltpu.make_async_copy(d_hbm[k].at[pl.ds(base(j), win)], idx.at[s, k], idx_sem.at[s])
                    for k in range(2)]

        def gathers(s):
            return [pltpu.make_async_copy(y_hbm.at[idx.at[s, k]], buf.at[s, k], gat_sem.at[s]) for k in range(2)]

        def stores(j, s):
            return [pltpu.make_async_copy(buf.at[s, k], o_hbm.at[k, pl.ds(base(j), win)], out_sem.at[s])
                    for k in range(2)]

        for c in idx_loads(0, 0):
            c.start()
        for j in range(n_win):
            s = j % 2
            for c in idx_loads(j, s):
                c.wait()
            if j >= 2:
                for c in stores(j - 2, s):
                    c.wait()
            for c in gathers(s):
                c.start()
            if j >= 1:
                for c in gathers(1 - s):
                    c.wait()
                for c in stores(j - 1, 1 - s):
                    c.start()
            if j + 1 < n_win:
                for c in idx_loads(j + 1, 1 - s):
                    c.start()
        s_last = (n_win - 1) % 2
        for c in gathers(s_last):
            c.wait()
        for c in stores(n_win - 1, s_last):
            c.start()
        if n_win >= 2:
            for c in stores(n_win - 2, 1 - s_last):
                c.wait()
        for c in stores(n_win - 1, s_last):
            c.wait()

    return gather(ys, dest[0], dest[1])


def _combine_kernel(y_ref, h1_ref, rf_ref, g2_ref, b2_ref, out_ref):
    rf = rf_ref[...]
    def unpack_f32(w):
        return jnp.concatenate([lax.bitcast_convert_type(w << 16, F32),
                                lax.bitcast_convert_type(w & jnp.uint32(0xFFFF0000), F32)], axis=1)

    ffn = rf[:, 0:1] * unpack_f32(y_ref[0]) + rf[:, 1:2] * unpack_f32(y_ref[1])
    z = DEEPNORM_ALPHA * h1_ref[...] + ffn
    mu = jnp.mean(z, axis=-1, keepdims=True)
    zc = z - mu
    var = jnp.mean(zc * zc, axis=-1, keepdims=True)
    out_ref[...] = zc * lax.rsqrt(var + LN_EPS) * g2_ref[...] + b2_ref[...]


def _combine_call(y12, h1, rf, g2, b2):
    T, D = h1.shape
    tm = COMB_TM
    const = lambda i: (0, 0)
    row = lambda i: (i, 0)
    return pl.pallas_call(
        _combine_kernel,
        grid=(T // tm,),
        in_specs=[
            pl.BlockSpec((2, tm, D // 2), lambda i: (0, i, 0)),
            pl.BlockSpec((tm, D), row),
            pl.BlockSpec((tm, LANES), row),
            pl.BlockSpec(g2.shape, const),
            pl.BlockSpec(b2.shape, const),
        ],
        out_specs=pl.BlockSpec((tm, D), row),
        out_shape=jax.ShapeDtypeStruct((T, D), F32),
        compiler_params=pltpu.CompilerParams(
            dimension_semantics=("arbitrary",),
            vmem_limit_bytes=VMEM_LIMIT),
        name="moe_combine_ln",
    )(y12, h1, rf, g2, b2)


def kernel(x, positions, w_in, b_gate, lambda_q1, lambda_k1, lambda_q2, lambda_k2, subln_g, w_o_att, conv_w, w_o_conv, w_mix_out, ln1_g, ln1_b, w_router_group, b_router_group, w_router_expert, b_router_expert, w_exp_gate, w_exp_up, w_exp_down, ln2_g, ln2_b):
    B, S, D = x.shape
    T = B * S
    l = 0
    lambda_init = 0.8 - 0.6 * math.exp(-0.3 * l)

    inv_freq = ROPE_THETA ** (-jnp.arange(0, HEAD_DIM, 2, dtype=F32) / HEAD_DIM)
    pos_f = positions.astype(F32)[:, None, :]
    inv_f = inv_freq.reshape(HEAD_DIM // 2, 1)

    n_main = 2 * QK_W + QK_W + 3 * CONV_W
    w0 = w_in[l]
    w_main = jnp.concatenate([w0[:, QK_W:2 * QK_W], w0[:, 3 * QK_W:n_main]], axis=1).astype(BF16)
    wq_t = w0[:, 0:QK_W].T.astype(BF16)
    wv_t = w0[:, 2 * QK_W:3 * QK_W].T.astype(BF16)
    w_gates = w0[:, n_main:].astype(BF16)

    q, k, vt, u = _proj_call(x, w_main, wq_t, wv_t, pos_f, inv_f, conv_w[l])

    lam = (jnp.exp(jnp.sum(lambda_q1[l].astype(F32) * lambda_k1[l].astype(F32)))
           - jnp.exp(jnp.sum(lambda_q2[l].astype(F32) * lambda_k2[l].astype(F32)))
           + lambda_init).reshape(1)
    o_n = _attn_call(lam, q, k, vt, subln_g[l].astype(F32).reshape(HEAD_W, 1), 1.0 - lambda_init)

    wr = jnp.zeros((D, LANES), F32)
    wr = wr.at[:, 0:N_GROUPS].set(w_router_group[l]).at[:, SUBLANES:SUBLANES + N_EXPERTS].set(w_router_expert[l])
    br = jnp.zeros((1, LANES), F32)
    br = br.at[0, 0:N_GROUPS].set(b_router_group[l]).at[0, SUBLANES:SUBLANES + N_EXPERTS].set(b_router_expert[l])

    bm = MOE_BM
    A = 2 * T
    P = A + N_EXPERTS * bm
    nb = P // bm
    h1, h1p, ri, rf, cnt, xs0 = _mix_call(
        x.reshape(T, D), o_n.reshape(T, QK_W), u.reshape(T, CONV_W),
        w_gates, b_gate[l].reshape(1, -1).astype(F32),
        w_o_att[l].astype(BF16), w_o_conv[l].astype(BF16), w_mix_out[l].astype(BF16),
        ln1_g[l].reshape(1, D).astype(F32), ln1_b[l].reshape(1, D).astype(F32),
        wr.astype(BF16), br, P)

    counts = cnt[:, 0]
    padded = (counts + bm - 1) // bm * bm
    pend = jnp.cumsum(padded)
    pstart = pend - padded
    onehot = (ri[0:2, :, None] == jnp.arange(N_EXPERTS, dtype=jnp.int32)).astype(jnp.int32)
    dest = jnp.sum(onehot * pstart, axis=-1) + ri[2:4]
    blk_start = jnp.arange(nb, dtype=jnp.int32) * bm
    block_expert = jnp.minimum(
        jnp.sum((pend[None, :] <= blk_start[:, None]).astype(jnp.int32), axis=1), N_EXPERTS - 1)
    n_used = (pend[-1] // bm).astype(jnp.int32).reshape(1)

    xs_ref = jax.new_ref(xs0)
    _sc_scatter_rows(h1p, dest, xs_ref)
    xs = xs_ref[...]
    ys = _expert_call(block_expert, n_used, xs, w_exp_gate[l], w_exp_up[l], w_exp_down[l])
    out = _combine_call(_sc_gather_rows(ys, dest), h1, rf,
                        ln2_g[l].reshape(1, D).astype(F32), ln2_b[l].reshape(1, D).astype(F32))
    return out.reshape(B, S, D)
```

```python
import functools
import math

import jax
import jax.numpy as jnp
from jax import lax
from jax.experimental import pallas as pl
from jax.experimental.pallas import tpu as pltpu
from jax.experimental.pallas import tpu_sc as plsc

D_MODEL = 1024
ATT_HEADS = 4
HEAD_DIM = 64
HEAD_W = 2 * HEAD_DIM
QK_W = ATT_HEADS * HEAD_W
CONV_W = D_MODEL // 2
CONV_K = 3
N_GROUPS = 4
EPG = 8
N_EXPERTS = N_GROUPS * EPG
EXPERT_FF = D_MODEL // 2
ROPE_THETA = 10000.0
SUBLN_EPS = 1e-5
LN_EPS = 1e-5
DEPTH = 1
DEEPNORM_ALPHA = (2.0 * DEPTH) ** 0.25

LANES = 128
SUBLANES = 8
VMEM_LIMIT = 48 * 1024 * 1024

PROJ_TM = 1024
ATT_TQ = 256
ATT_TK = 256
ATT_UNROLL = 4
MIX_TM = 1024
MOE_BM = 512
SC_CORES = 2
SC_SUBCORES = 16
SC_WIN = 64
COMB_TM = 512
NEG_BIG = -1e30

F32 = jnp.float32
BF16 = jnp.bfloat16


def _dot(a, b):
    return jnp.dot(a, b, preferred_element_type=F32)


def _dot_nt(a, b):
    return lax.dot_general(a, b, (((1,), (1,)), ((), ())), preferred_element_type=F32)


def _proj_kernel(x_ref, w_ref, wqt_ref, wvt_ref, pos_ref, invf_ref, cw_ref,
                 qt_ref, k_ref, vt_ref, u_ref, pbuf):
    tm = x_ref.shape[0]
    i = pl.program_id(1)

    @pl.when(i == 0)
    def _():
        pbuf[0:SUBLANES, :] = jnp.zeros((SUBLANES, CONV_W), F32)

    xb = x_ref[...].astype(BF16)
    half = HEAD_DIM // 2
    ang_t = invf_ref[...] * pos_ref[...]
    cos_t = jnp.cos(ang_t)
    sin_t = jnp.sin(ang_t)

    q_scale = HEAD_DIM ** -0.5 * math.log2(math.e)
    qt = _dot_nt(wqt_ref[...], xb)
    for m in range(QK_W // HEAD_DIM):
        lo = qt[m * HEAD_DIM:m * HEAD_DIM + half, :]
        hi = qt[m * HEAD_DIM + half:(m + 1) * HEAD_DIM, :]
        qt_ref[m * HEAD_DIM:m * HEAD_DIM + half, :] = ((lo * cos_t - hi * sin_t) * q_scale).astype(BF16)
        qt_ref[m * HEAD_DIM + half:(m + 1) * HEAD_DIM, :] = ((hi * cos_t + lo * sin_t) * q_scale).astype(BF16)

    lane = lax.broadcasted_iota(jnp.int32, (tm, LANES), 1)
    low_half = (lane % HEAD_DIM) < half
    cos = jnp.concatenate([cos_t, cos_t, cos_t, cos_t], axis=0).T
    sin = jnp.concatenate([-sin_t, sin_t, -sin_t, sin_t], axis=0).T

    def rope(t):
        rot = jnp.where(low_half, pltpu.roll(t, LANES - half, 1), pltpu.roll(t, half, 1))
        return t * cos + rot * sin

    k = _dot(xb, w_ref[:, 0:QK_W])
    for h in range(ATT_HEADS):
        sl = slice(h * HEAD_W, (h + 1) * HEAD_W)
        k_ref[:, sl] = rope(k[:, sl]).astype(BF16)
    vt_ref[...] = _dot_nt(wvt_ref[...], xb).astype(BF16)

    c0 = QK_W
    cb = _dot(xb, w_ref[:, c0:c0 + CONV_W])
    cc = _dot(xb, w_ref[:, c0 + CONV_W:c0 + 2 * CONV_W])
    cx = _dot(xb, w_ref[:, c0 + 2 * CONV_W:c0 + 3 * CONV_W])
    p = cc * cx
    pbuf[SUBLANES:SUBLANES + tm, :] = p
    y = (cw_ref[0:1, :] * pbuf[SUBLANES - 2:SUBLANES - 2 + tm, :]
         + cw_ref[1:2, :] * pbuf[SUBLANES - 1:SUBLANES - 1 + tm, :]
         + cw_ref[2:3, :] * p)
    u_ref[...] = (cb * y).astype(BF16)
    pbuf[0:SUBLANES, :] = pbuf[tm:tm + SUBLANES, :]


def _proj_call(x, w_main, wq_t, wv_t, pos_f, inv_f, conv_w):
    B, S, D = x.shape
    tm = PROJ_TM
    n_main = w_main.shape[1]
    return pl.pallas_call(
        _proj_kernel,
        grid=(B, S // tm),
        in_specs=[
            pl.BlockSpec((None, tm, D), lambda b, i: (b, i, 0)),
            pl.BlockSpec((D, n_main), lambda b, i: (0, 0)),
            pl.BlockSpec((QK_W, D), lambda b, i: (0, 0)),
            pl.BlockSpec((QK_W, D), lambda b, i: (0, 0)),
            pl.BlockSpec((None, 1, tm), lambda b, i: (b, 0, i)),
            pl.BlockSpec((HEAD_DIM // 2, 1), lambda b, i: (0, 0)),
            pl.BlockSpec((CONV_K, CONV_W), lambda b, i: (0, 0)),
        ],
        out_specs=[
            pl.BlockSpec((None, QK_W, tm), lambda b, i: (b, 0, i)),
            pl.BlockSpec((None, tm, QK_W), lambda b, i: (b, i, 0)),
            pl.BlockSpec((None, QK_W, tm), lambda b, i: (b, 0, i)),
            pl.BlockSpec((None, tm, CONV_W), lambda b, i: (b, i, 0)),
        ],
        out_shape=[
            jax.ShapeDtypeStruct((B, QK_W, S), BF16),
            jax.ShapeDtypeStruct((B, S, QK_W), BF16),
            jax.ShapeDtypeStruct((B, QK_W, S), BF16),
            jax.ShapeDtypeStruct((B, S, CONV_W), BF16),
        ],
        scratch_shapes=[pltpu.VMEM((tm + 2 * SUBLANES, CONV_W), F32)],
        compiler_params=pltpu.CompilerParams(
            dimension_semantics=("arbitrary", "arbitrary"),
            vmem_limit_bytes=VMEM_LIMIT),
        name="proj_rope_conv",
    )(x, w_main, wq_t, wv_t, pos_f, inv_f, conv_w)


ACC_ROWS = HEAD_W + 16


def _attn_kernel(lam_ref, q_ref, k_ref, vt_ref, g_ref, o_ref, qzt_ref, *scratch, out_scale):
    nh = ATT_HEADS
    s_refs, p_refs = scratch[0:nh], scratch[nh:2 * nh]
    m_refs, a_refs, acc_refs = scratch[2 * nh:3 * nh], scratch[3 * nh:4 * nh], scratch[4 * nh:5 * nh]
    tq = q_ref.shape[1]
    tk = ATT_TK
    nblk = 2 * tq // LANES
    i = pl.program_id(1)
    feat = lax.broadcasted_iota(jnp.int32, (HEAD_W, tq), 0)
    ones_rows = jnp.ones((ACC_ROWS - HEAD_W, tk), BF16)

    for h in range(nh):
        qf = q_ref[h * HEAD_W:(h + 1) * HEAD_W, :].astype(F32)
        qzt_ref[h, :, 0:tq] = jnp.where(feat < HEAD_DIM, qf, 0.0).astype(BF16)
        qzt_ref[h, :, tq:2 * tq] = jnp.where(feat >= HEAD_DIM, qf, 0.0).astype(BF16)
        m_refs[h][...] = jnp.full(m_refs[h].shape, NEG_BIG, F32)
        acc_refs[h][...] = jnp.zeros(acc_refs[h].shape, F32)

    def scores(h, k0):
        kj = k_ref[pl.ds(k0, tk), h * HEAD_W:(h + 1) * HEAD_W]
        s_refs[h][:, 0:2 * tq] = _dot(kj, qzt_ref[h])

    def softmax(h, diag=None):
        for c in range(nblk):
            cs = slice(c * LANES, (c + 1) * LANES)
            s = s_refs[h][:, cs]
            if diag is not None:
                key = lax.broadcasted_iota(jnp.int32, (tk, LANES), 0) + diag * tk
                qry = lax.broadcasted_iota(jnp.int32, (tk, LANES), 1) + (c * LANES) % tq
                s = jnp.where(key <= qry, s, NEG_BIG)
            m_old = m_refs[h][:, cs]
            m_new = jnp.maximum(m_old, jnp.max(s, axis=0, keepdims=True))
            a_refs[h][:, cs] = jnp.exp2(m_old - m_new)
            m_refs[h][:, cs] = m_new
            p_refs[h][:, cs] = jnp.exp2(s - m_new).astype(BF16)

    def accumulate(h, k0):
        vj = jnp.concatenate([vt_ref[h * HEAD_W:(h + 1) * HEAD_W, pl.ds(k0, tk)], ones_rows], axis=0)
        acc_refs[h][...] = a_refs[h][...] * acc_refs[h][...] + _dot(vj, p_refs[h][:, 0:2 * tq])

    n_diag = tq // tk
    n_full = n_diag * i

    def tile_start(j):
        return pl.multiple_of(j * tk, tk)

    for h in range(nh):
        scores(h, 0)

    @pl.when(i > 0)
    def _():
        for h in range(nh):
            softmax(h)
            scores(h, tile_start(1))

    def step(j):
        for h in range(nh):
            accumulate(h, tile_start(j - 1))
            softmax(h)
            scores(h, tile_start(j + 1))

    n_steps = jnp.maximum(n_full - 1, 0)
    n_trips = n_steps // ATT_UNROLL

    def body(t, carry):
        for u in range(ATT_UNROLL):
            step(ATT_UNROLL * t + 1 + u)
        return carry

    lax.fori_loop(0, n_trips, body, 0)

    def tail(j, carry):
        step(j)
        return carry

    lax.fori_loop(ATT_UNROLL * n_trips + 1, n_steps + 1, tail, 0)

    @pl.when(i > 0)
    def _():
        for h in range(nh):
            accumulate(h, tile_start(n_full - 1))

    for d in range(n_diag):
        for h in range(nh):
            softmax(h, diag=d)
            if d + 1 < n_diag:
                scores(h, tile_start(n_full + d + 1))
            accumulate(h, tile_start(n_full + d))

    lam = lam_ref[0]
    for h in range(nh):
        acc = acc_refs[h][0:HEAD_W, :]
        l = acc_refs[h][HEAD_W:HEAD_W + 1, :]
        o = acc[:, 0:tq] / l[:, 0:tq] - lam * (acc[:, tq:2 * tq] / l[:, tq:2 * tq])
        ms = jnp.mean(o * o, axis=0, keepdims=True)
        o = o * lax.rsqrt(ms + SUBLN_EPS) * g_ref[...] * out_scale
        o_ref[:, h * HEAD_W:(h + 1) * HEAD_W] = o.T.astype(BF16)


def _attn_call(lam, q, k, vt, g_col, out_scale):
    B, S, _ = k.shape
    tq = ATT_TQ
    assert ATT_TQ % ATT_TK == 0
    kernel = functools.partial(_attn_kernel, out_scale=out_scale)
    return pl.pallas_call(
        kernel,
        grid=(B, S // tq),
        in_specs=[
            pl.BlockSpec(memory_space=pltpu.SMEM),
            pl.BlockSpec((None, QK_W, tq), lambda b, i: (b, 0, i)),
            pl.BlockSpec((None, S, QK_W), lambda b, i: (b, 0, 0), pipeline_mode=pl.Buffered(1)),
            pl.BlockSpec((None, QK_W, S), lambda b, i: (b, 0, 0), pipeline_mode=pl.Buffered(1)),
            pl.BlockSpec((HEAD_W, 1), lambda b, i: (0, 0)),
        ],
        out_specs=pl.BlockSpec((None, tq, QK_W), lambda b, i: (b, i, 0)),
        out_shape=jax.ShapeDtypeStruct((B, S, QK_W), BF16),
        scratch_shapes=(
            [pltpu.VMEM((ATT_HEADS, HEAD_W, 2 * tq), BF16)]
            + [pltpu.VMEM((ATT_TK, 2 * tq + LANES), F32) for _ in range(ATT_HEADS)]
            + [pltpu.VMEM((ATT_TK, 2 * tq + LANES), BF16) for _ in range(ATT_HEADS)]
            + [pltpu.VMEM((1, 2 * tq), F32) for _ in range(ATT_HEADS)]
            + [pltpu.VMEM((1, 2 * tq), F32) for _ in range(ATT_HEADS)]
            + [pltpu.VMEM((ACC_ROWS, 2 * tq), F32) for _ in range(ATT_HEADS)]
        ),
        compiler_params=pltpu.CompilerParams(
            dimension_semantics=("arbitrary", "arbitrary"),
            vmem_limit_bytes=VMEM_LIMIT),
        name="diff_flash_attn",
    )(lam, q, k, vt, g_col)


def _pack_bf16_pairs(h):
    half = h.shape[1] // 2
    hb = h.astype(BF16).astype(F32)
    bits = lax.bitcast_convert_type(hb, jnp.uint32)
    return (bits[:, 0:half] >> 16) | (bits[:, half:] & jnp.uint32(0xFFFF0000))


def _unpack_bf16_pairs(w):
    lo = lax.bitcast_convert_type(w << 16, F32).astype(BF16)
    hi = lax.bitcast_convert_type(w & jnp.uint32(0xFFFF0000), F32).astype(BF16)
    return jnp.concatenate([lo, hi], axis=1)


def _first_index_of(mask, row_f, big):
    return jnp.min(jnp.where(mask, row_f, big), axis=0, keepdims=True)


def _mix_kernel(x_ref, o_ref, u_ref, wg_ref, bg_ref, woa_ref, woc_ref, wmo_ref, g1_ref, b1_ref,
                wr_ref, br_ref,
                h1_ref, h1p_ref, ri_ref, rf_ref, cnt_ref, xs0_ref, carry_ref):
    step = pl.program_id(0)
    xs0_ref[...] = jnp.zeros(xs0_ref.shape, xs0_ref.dtype)

    @pl.when(step == 0)
    def _():
        carry_ref[...] = jnp.zeros(carry_ref.shape, F32)

    x = x_ref[...]
    xb = x.astype(BF16)
    gates = jax.nn.sigmoid(_dot(xb, wg_ref[...]) + bg_ref[...])
    ya = _dot(o_ref[...], woa_ref[...])
    yc = _dot(u_ref[...], woc_ref[...])
    merged = (gates[:, 0:D_MODEL] * ya + gates[:, D_MODEL:] * yc).astype(BF16)
    z = DEEPNORM_ALPHA * x + _dot(merged, wmo_ref[...])

    carry = carry_ref[:, 0:1]
    carry = _mix_tail(z, carry, g1_ref, b1_ref, wr_ref, br_ref, h1_ref, h1p_ref, ri_ref, rf_ref)
    new_carry = jnp.broadcast_to(carry, carry_ref.shape)
    carry_ref[...] = new_carry
    cnt_ref[...] = new_carry.astype(jnp.int32)


def _mix_tail(z, carry, g1_ref, b1_ref, wr_ref, br_ref, h1_ref, h1p_ref, ri_ref, rf_ref):
    tm = z.shape[0]
    mu = jnp.mean(z, axis=-1, keepdims=True)
    zc = z - mu
    var = jnp.mean(zc * zc, axis=-1, keepdims=True)
    h1 = zc * lax.rsqrt(var + LN_EPS) * g1_ref[...] + b1_ref[...]
    h1_ref[...] = h1
    h1p_ref[...] = _pack_bf16_pairs(h1)

    logits = _dot(h1.astype(BF16), wr_ref[...]) + br_ref[...]
    lt = logits.T
    row8 = lax.broadcasted_iota(jnp.int32, (SUBLANES, tm), 0).astype(F32)
    gl = jnp.where(row8 < N_GROUPS, lt[0:SUBLANES, :], NEG_BIG)
    gmax = jnp.max(gl, axis=0, keepdims=True)
    gsel = _first_index_of(gl == gmax, row8, float(SUBLANES))
    gw = 1.0 / jnp.sum(jnp.exp(gl - gmax), axis=0, keepdims=True)
    el = lt[SUBLANES:2 * SUBLANES, :]
    for g in range(1, N_GROUPS):
        el = jnp.where(gsel == float(g), lt[(g + 1) * SUBLANES:(g + 2) * SUBLANES, :], el)
    v1 = jnp.max(el, axis=0, keepdims=True)
    i1 = _first_index_of(el == v1, row8, float(EPG))
    el2 = jnp.where(row8 == i1, -jnp.inf, el)
    v2 = jnp.max(el2, axis=0, keepdims=True)
    i2 = _first_index_of(el2 == v2, row8, float(EPG))
    t = jnp.exp(v2 - v1)
    p1 = gw / (1.0 + t)
    p2 = gw * t / (1.0 + t)
    e1 = gsel * float(EPG) + i1
    e2 = gsel * float(EPG) + i2

    rowe = lax.broadcasted_iota(jnp.int32, (N_EXPERTS, tm), 0).astype(F32)
    oh1 = (rowe == e1).astype(F32)
    oh2 = (rowe == e2).astype(F32)
    r_i = lax.broadcasted_iota(jnp.int32, (tm, tm), 0)
    c_i = lax.broadcasted_iota(jnp.int32, (tm, tm), 1)
    upper = jnp.where(r_i < c_i, 1.0, 0.0).astype(BF16)
    cum1 = _dot(oh1.astype(BF16), upper)
    cum2 = _dot(oh2.astype(BF16), upper)
    tot1 = jnp.sum(oh1, axis=1, keepdims=True)
    tot2 = jnp.sum(oh2, axis=1, keepdims=True)
    rank1 = jnp.sum(oh1 * (carry + cum1), axis=0, keepdims=True)
    rank2 = jnp.sum(oh2 * (carry + tot1 + cum2), axis=0, keepdims=True)

    ri_ref[4:8, :] = jnp.zeros((4, tm), jnp.int32)
    ri_ref[0:1, :] = e1.astype(jnp.int32)
    ri_ref[1:2, :] = e2.astype(jnp.int32)
    ri_ref[2:3, :] = rank1.astype(jnp.int32)
    ri_ref[3:4, :] = rank2.astype(jnp.int32)

    row128 = lax.broadcasted_iota(jnp.int32, (LANES, tm), 0)
    pw = jnp.where(row128 == 0, p1, jnp.where(row128 == 1, p2, 0.0))
    rf_ref[...] = pw.T
    return carry + tot1 + tot2


def _mix_call(x2, o_n, u, wg, bg, woa, woc, wmo, g1, b1, wr, br, n_slots):
    T, D = x2.shape
    tm = MIX_TM
    n = T // tm
    slots_per_step = n_slots // n
    assert slots_per_step * n == n_slots and slots_per_step % SUBLANES == 0
    const = lambda i: (0, 0)
    row = lambda i: (i, 0)
    return pl.pallas_call(
        _mix_kernel,
        grid=(n,),
        in_specs=[
            pl.BlockSpec((tm, D), row),
            pl.BlockSpec((tm, QK_W), row),
            pl.BlockSpec((tm, CONV_W), row),
            pl.BlockSpec(wg.shape, const),
            pl.BlockSpec(bg.shape, const),
            pl.BlockSpec(woa.shape, const),
            pl.BlockSpec(woc.shape, const),
            pl.BlockSpec(wmo.shape, const),
            pl.BlockSpec(g1.shape, const),
            pl.BlockSpec(b1.shape, const),
            pl.BlockSpec(wr.shape, const),
            pl.BlockSpec(br.shape, const),
        ],
        out_specs=[
            pl.BlockSpec((tm, D), row),
            pl.BlockSpec((tm, D // 2), row),
            pl.BlockSpec((SUBLANES, tm), lambda i: (0, i)),
            pl.BlockSpec((tm, LANES), row),
            pl.BlockSpec((N_EXPERTS, LANES), const),
            pl.BlockSpec((slots_per_step, D // 2), row),
        ],
        out_shape=[
            jax.ShapeDtypeStruct((T, D), F32),
            jax.ShapeDtypeStruct((T, D // 2), jnp.uint32),
            jax.ShapeDtypeStruct((SUBLANES, T), jnp.int32),
            jax.ShapeDtypeStruct((T, LANES), F32),
            jax.ShapeDtypeStruct((N_EXPERTS, LANES), jnp.int32),
            jax.ShapeDtypeStruct((n_slots, D // 2), jnp.uint32),
        ],
        scratch_shapes=[pltpu.VMEM((N_EXPERTS, LANES), F32)],
        compiler_params=pltpu.CompilerParams(
            dimension_semantics=("arbitrary",),
            vmem_limit_bytes=VMEM_LIMIT),
        name="mix_ln_router",
    )(x2, o_n, u, wg, bg, woa, woc, wmo, g1, b1, wr, br)


def _sc_scatter_rows(h1p, dest, xs_ref):
    T, W = h1p.shape
    n_workers = SC_CORES * SC_SUBCORES
    per = T // n_workers
    n_win = per // SC_WIN
    assert n_win * SC_WIN * n_workers == T
    mesh = plsc.VectorSubcoreMesh(core_axis_name="core", subcore_axis_name="subcore")

    @pl.kernel(out_type=(), mesh=mesh,
               scratch_types=[pltpu.VMEM((2, SC_WIN, W), h1p.dtype),
                              pltpu.VMEM((2, SC_WIN), jnp.int32), pltpu.VMEM((2, SC_WIN), jnp.int32),
                              pltpu.SemaphoreType.DMA((2,)), pltpu.SemaphoreType.DMA((2,))],
               name="moe_dispatch_sc")
    def scatter(x_hbm, d0_hbm, d1_hbm, o_hbm, xbuf, i0, i1, load_sem, scat_sem):
        wid = lax.axis_index("core") * SC_SUBCORES + lax.axis_index("subcore")

        def loads(j, s):
            base = wid * per + j * SC_WIN
            return [pltpu.make_async_copy(x_hbm.at[pl.ds(base, SC_WIN)], xbuf.at[s], load_sem.at[s]),
                    pltpu.make_async_copy(d0_hbm.at[pl.ds(base, SC_WIN)], i0.at[s], load_sem.at[s]),
                    pltpu.make_async_copy(d1_hbm.at[pl.ds(base, SC_WIN)], i1.at[s], load_sem.at[s])]

        def scatters(s):
            return [pltpu.make_async_copy(xbuf.at[s], o_hbm.at[i0.at[s]], scat_sem.at[s]),
                    pltpu.make_async_copy(xbuf.at[s], o_hbm.at[i1.at[s]], scat_sem.at[s])]

        for c in loads(0, 0):
            c.start()
        for j in range(n_win):
            s = j % 2
            for c in loads(j, s):
                c.wait()
            for c in scatters(s):
                c.start()
            if j >= 1:
                for c in scatters(1 - s):
                    c.wait()
            if j + 1 < n_win:
                for c in loads(j + 1, 1 - s):
                    c.start()
        for c in scatters((n_win - 1) % 2):
            c.wait()

    scatter(h1p, dest[0], dest[1], xs_ref)


def _expert_kernel(be_ref, nu_ref, xs_ref, wg_ref, wu_ref, wd_ref, ys_ref, wgb_ref, wub_ref, wdb_ref):
    i = pl.program_id(0)
    used = i < nu_ref[0]

    @pl.when(used & ((i == 0) | (be_ref[i] != be_ref[jnp.maximum(i - 1, 0)])))
    def _():
        wgb_ref[...] = wg_ref[...].astype(BF16)
        wub_ref[...] = wu_ref[...].astype(BF16)
        wdb_ref[...] = wd_ref[...].astype(BF16)

    @pl.when(used)
    def _():
        xb = _unpack_bf16_pairs(xs_ref[...])
        g = _dot(xb, wgb_ref[...])
        u = _dot(xb, wub_ref[...])
        h = (g * jax.nn.sigmoid(g) * u).astype(BF16)
        ys_ref[...] = _pack_bf16_pairs(_dot(h, wdb_ref[...]))

    @pl.when(i >= nu_ref[0])
    def _():
        ys_ref[...] = jnp.zeros(ys_ref.shape, ys_ref.dtype)


def _expert_call(block_expert, n_used, xs, wg, wu, wd):
    P = xs.shape[0]
    bm = MOE_BM
    nb = P // bm
    xs_map = lambda i, be, nu: (jnp.minimum(i, nu[0] - 1), 0)
    w_map = lambda i, be, nu: (be[i], 0, 0)
    return pl.pallas_call(
        _expert_kernel,
        grid_spec=pltpu.PrefetchScalarGridSpec(
            num_scalar_prefetch=2,
            grid=(nb,),
            in_specs=[
                pl.BlockSpec((bm, D_MODEL // 2), xs_map),
                pl.BlockSpec((None, D_MODEL, EXPERT_FF), w_map),
                pl.BlockSpec((None, D_MODEL, EXPERT_FF), w_map),
                pl.BlockSpec((None, EXPERT_FF, D_MODEL), w_map),
            ],
            out_specs=pl.BlockSpec((bm, D_MODEL // 2), lambda i, be, nu: (i, 0)),
            scratch_shapes=[
                pltpu.VMEM((D_MODEL, EXPERT_FF), BF16),
                pltpu.VMEM((D_MODEL, EXPERT_FF), BF16),
                pltpu.VMEM((EXPERT_FF, D_MODEL), BF16),
            ],
        ),
        out_shape=jax.ShapeDtypeStruct((P, D_MODEL // 2), jnp.uint32),
        compiler_params=pltpu.CompilerParams(
            dimension_semantics=("arbitrary",),
            vmem_limit_bytes=VMEM_LIMIT),
        name="moe_experts",
    )(block_expert, n_used, xs, wg, wu, wd)


def _sc_gather_rows(ys, dest):
    _, T = dest.shape
    W = ys.shape[1]
    n_workers = SC_CORES * SC_SUBCORES
    per = T // n_workers
    win = SC_WIN // 2
    n_win = per // win
    assert n_win * win * n_workers == T
    mesh = plsc.VectorSubcoreMesh(core_axis_name="core", subcore_axis_name="subcore")

    @pl.kernel(out_type=jax.ShapeDtypeStruct((2, T, W), ys.dtype), mesh=mesh,
               scratch_types=[pltpu.VMEM((2, 2, win, W), ys.dtype), pltpu.VMEM((2, 2, win), jnp.int32),
                              pltpu.SemaphoreType.DMA((2,)), pltpu.SemaphoreType.DMA((2,)),
                              pltpu.SemaphoreType.DMA((2,))],
               name="moe_combine_sc")
    def gather(y_hbm, d0_hbm, d1_hbm, o_hbm, buf, idx, idx_sem, gat_sem, out_sem):
        wid = lax.axis_index("core") * SC_SUBCORES + lax.axis_index("subcore")
        d_hbm = (d0_hbm, d1_hbm)

        def base(j):
            return wid * per + j * win

        def idx_loads(j, s):
            return [pltpu.make_async_copy(d_hbm[k].at[pl.ds(base(j), win)], idx.at[s, k], idx_sem.at[s])
                    for k in range(2)]

        def gathers(s):
            return [pltpu.make_async_copy(y_hbm.at[idx.at[s, k]], buf.at[s, k], gat_sem.at[s]) for k in range(2)]

        def stores(j, s):
            return [pltpu.make_async_copy(buf.at[s, k], o_hbm.at[k, pl.ds(base(j), win)], out_sem.at[s])
                    for k in range(2)]

        for c in idx_loads(0, 0):
            c.start()
        for j in range(n_win):
            s = j % 2
            for c in idx_loads(j, s):
                c.wait()
            if j >= 2:
                for c in stores(j - 2, s):
                    c.wait()
            for c in gathers(s):
                c.start()
            if j >= 1:
                for c in gathers(1 - s):
                    c.wait()
                for c in stores(j - 1, 1 - s):
                    c.start()
            if j + 1 < n_win:
                for c in idx_loads(j + 1, 1 - s):
                    c.start()
        s_last = (n_win - 1) % 2
        for c in gathers(s_last):
            c.wait()
        for c in stores(n_win - 1, s_last):
            c.start()
        if n_win >= 2:
            for c in stores(n_win - 2, 1 - s_last):
                c.wait()
        for c in stores(n_win - 1, s_last):
            c.wait()

    return gather(ys, dest[0], dest[1])


def _combine_kernel(y_ref, h1_ref, rf_ref, g2_ref, b2_ref, out_ref):
    rf = rf_ref[...]
    def unpack_f32(w):
        return jnp.concatenate([lax.bitcast_convert_type(w << 16, F32),
                                lax.bitcast_convert_type(w & jnp.uint32(0xFFFF0000), F32)], axis=1)

    ffn = rf[:, 0:1] * unpack_f32(y_ref[0]) + rf[:, 1:2] * unpack_f32(y_ref[1])
    z = DEEPNORM_ALPHA * h1_ref[...] + ffn
    mu = jnp.mean(z, axis=-1, keepdims=True)
    zc = z - mu
    var = jnp.mean(zc * zc, axis=-1, keepdims=True)
    out_ref[...] = zc * lax.rsqrt(var + LN_EPS) * g2_ref[...] + b2_ref[...]


def _combine_call(y12, h1, rf, g2, b2):
    T, D = h1.shape
    tm = COMB_TM
    const = lambda i: (0, 0)
    row = lambda i: (i, 0)
    return pl.pallas_call(
        _combine_kernel,
        grid=(T // tm,),
        in_specs=[
            pl.BlockSpec((2, tm, D // 2), lambda i: (0, i, 0)),
            pl.BlockSpec((tm, D), row),
            pl.BlockSpec((tm, LANES), row),
            pl.BlockSpec(g2.shape, const),
            pl.BlockSpec(b2.shape, const),
        ],
        out_specs=pl.BlockSpec((tm, D), row),
        out_shape=jax.ShapeDtypeStruct((T, D), F32),
        compiler_params=pltpu.CompilerParams(
            dimension_semantics=("arbitrary",),
            vmem_limit_bytes=VMEM_LIMIT),
        name="moe_combine_ln",
    )(y12, h1, rf, g2, b2)


def kernel(x, positions, w_in, b_gate, lambda_q1, lambda_k1, lambda_q2, lambda_k2, subln_g, w_o_att, conv_w, w_o_conv, w_mix_out, ln1_g, ln1_b, w_router_group, b_router_group, w_router_expert, b_router_expert, w_exp_gate, w_exp_up, w_exp_down, ln2_g, ln2_b):
    B, S, D = x.shape
    T = B * S
    l = 0
    lambda_init = 0.8 - 0.6 * math.exp(-0.3 * l)

    inv_freq = ROPE_THETA ** (-jnp.arange(0, HEAD_DIM, 2, dtype=F32) / HEAD_DIM)
    pos_f = positions.astype(F32)[:, None, :]
    inv_f = inv_freq.reshape(HEAD_DIM // 2, 1)

    n_main = 2 * QK_W + QK_W + 3 * CONV_W
    w0 = w_in[l]
    w_main = jnp.concatenate([w0[:, QK_W:2 * QK_W], w0[:, 3 * QK_W:n_main]], axis=1).astype(BF16)
    wq_t = w0[:, 0:QK_W].T.astype(BF16)
    wv_t = w0[:, 2 * QK_W:3 * QK_W].T.astype(BF16)
    w_gates = w0[:, n_main:].astype(BF16)

    q, k, vt, u = _proj_call(x, w_main, wq_t, wv_t, pos_f, inv_f, conv_w[l])

    lam = (jnp.exp(jnp.sum(lambda_q1[l].astype(F32) * lambda_k1[l].astype(F32)))
           - jnp.exp(jnp.sum(lambda_q2[l].astype(F32) * lambda_k2[l].astype(F32)))
           + lambda_init).reshape(1)
    o_n = _attn_call(lam, q, k, vt, subln_g[l].astype(F32).reshape(HEAD_W, 1), 1.0 - lambda_init)

    wr = jnp.zeros((D, LANES), F32)
    wr = wr.at[:, 0:N_GROUPS].set(w_router_group[l]).at[:, SUBLANES:SUBLANES + N_EXPERTS].set(w_router_expert[l])
    br = jnp.zeros((1, LANES), F32)
    br = br.at[0, 0:N_GROUPS].set(b_router_group[l]).at[0, SUBLANES:SUBLANES + N_EXPERTS].set(b_router_expert[l])

    bm = MOE_BM
    A = 2 * T
    P = A + N_EXPERTS * bm
    nb = P // bm
    h1, h1p, ri, rf, cnt, xs0 = _mix_call(
        x.reshape(T, D), o_n.reshape(T, QK_W), u.reshape(T, CONV_W),
        w_gates, b_gate[l].reshape(1, -1).astype(F32),
        w_o_att[l].astype(BF16), w_o_conv[l].astype(BF16), w_mix_out[l].astype(BF16),
        ln1_g[l].reshape(1, D).astype(F32), ln1_b[l].reshape(1, D).astype(F32),
        wr.astype(BF16), br, P)

    counts = cnt[:, 0]
    padded = (counts + bm - 1) // bm * bm
    pend = jnp.cumsum(padded)
    pstart = pend - padded
    onehot = (ri[0:2, :, None] == jnp.arange(N_EXPERTS, dtype=jnp.int32)).astype(jnp.int32)
    dest = jnp.sum(onehot * pstart, axis=-1) + ri[2:4]
    blk_start = jnp.arange(nb, dtype=jnp.int32) * bm
    block_expert = jnp.minimum(
        jnp.sum((pend[None, :] <= blk_start[:, None]).astype(jnp.int32), axis=1), N_EXPERTS - 1)
    n_used = (pend[-1] // bm).astype(jnp.int32).reshape(1)

    xs_ref = jax.new_ref(xs0)
    _sc_scatter_rows(h1p, dest, xs_ref)
    xs = xs_ref[...]
    ys = _expert_call(block_expert, n_used, xs, w_exp_gate[l], w_exp_up[l], w_exp_down[l])
    out = _combine_call(_sc_gather_rows(ys, dest), h1, rf,
                        ln2_g[l].reshape(1, D).astype(F32), ln2_b[l].reshape(1, D).astype(F32))
    return out.reshape(B, S, D)
```

```python
import functools
import math

import jax
import jax.numpy as jnp
from jax import lax
from jax.experimental import pallas as pl
from jax.experimental.pallas import tpu as pltpu
from jax.experimental.pallas import tpu_sc as plsc

D_MODEL = 1024
ATT_HEADS = 4
HEAD_DIM = 64
HEAD_W = 2 * HEAD_DIM
QK_W = ATT_HEADS * HEAD_W
CONV_W = D_MODEL // 2
CONV_K = 3
N_GROUPS = 4
EPG = 8
N_EXPERTS = N_GROUPS * EPG
EXPERT_FF = D_MODEL // 2
ROPE_THETA = 10000.0
SUBLN_EPS = 1e-5
LN_EPS = 1e-5
DEPTH = 1
DEEPNORM_ALPHA = (2.0 * DEPTH) ** 0.25

LANES = 128
SUBLANES = 8
VMEM_LIMIT = 48 * 1024 * 1024

PROJ_TM = 1024
ATT_TQ = 256
ATT_TK = 256
ATT_UNROLL = 4
MIX_TM = 1024
MOE_BM = 512
SC_CORES = 2
SC_SUBCORES = 16
SC_WIN = 64
COMB_TM = 512
NEG_BIG = -1e30

F32 = jnp.float32
BF16 = jnp.bfloat16


def _dot(a, b):
    return jnp.dot(a, b, preferred_element_type=F32)


def _dot_nt(a, b):
    return lax.dot_general(a, b, (((1,), (1,)), ((), ())), preferred_element_type=F32)


def _proj_kernel(x_ref, w_ref, wqt_ref, wvt_ref, pos_ref, invf_ref, cw_ref,
                 qt_ref, k_ref, vt_ref, u_ref, pbuf):
    tm = x_ref.shape[0]
    i = pl.program_id(1)

    @pl.when(i == 0)
    def _():
        pbuf[0:SUBLANES, :] = jnp.zeros((SUBLANES, CONV_W), F32)

    xb = x_ref[...].astype(BF16)
    half = HEAD_DIM // 2
    ang_t = invf_ref[...] * pos_ref[...]
    cos_t = jnp.cos(ang_t)
    sin_t = jnp.sin(ang_t)

    q_scale = HEAD_DIM ** -0.5 * math.log2(math.e)
    qt = _dot_nt(wqt_ref[...], xb)
    for m in range(QK_W // HEAD_DIM):
        lo = qt[m * HEAD_DIM:m * HEAD_DIM + half, :]
        hi = qt[m * HEAD_DIM + half:(m + 1) * HEAD_DIM, :]
        qt_ref[m * HEAD_DIM:m * HEAD_DIM + half, :] = ((lo * cos_t - hi * sin_t) * q_scale).astype(BF16)
        qt_ref[m * HEAD_DIM + half:(m + 1) * HEAD_DIM, :] = ((hi * cos_t + lo * sin_t) * q_scale).astype(BF16)

    lane = lax.broadcasted_iota(jnp.int32, (tm, LANES), 1)
    low_half = (lane % HEAD_DIM) < half
    cos = jnp.concatenate([cos_t, cos_t, cos_t, cos_t], axis=0).T
    sin = jnp.concatenate([-sin_t, sin_t, -sin_t, sin_t], axis=0).T

    def rope(t):
        rot = jnp.where(low_half, pltpu.roll(t, LANES - half, 1), pltpu.roll(t, half, 1))
        return t * cos + rot * sin

    k = _dot(xb, w_ref[:, 0:QK_W])
    for h in range(ATT_HEADS):
        sl = slice(h * HEAD_W, (h + 1) * HEAD_W)
        k_ref[:, sl] = rope(k[:, sl]).astype(BF16)
    vt_ref[...] = _dot_nt(wvt_ref[...], xb).astype(BF16)

    c0 = QK_W
    cb = _dot(xb, w_ref[:, c0:c0 + CONV_W])
    cc = _dot(xb, w_ref[:, c0 + CONV_W:c0 + 2 * CONV_W])
    cx = _dot(xb, w_ref[:, c0 + 2 * CONV_W:c0 + 3 * CONV_W])
    p = cc * cx
    pbuf[SUBLANES:SUBLANES + tm, :] = p
    y = (cw_ref[0:1, :] * pbuf[SUBLANES - 2:SUBLANES - 2 + tm, :]
         + cw_ref[1:2, :] * pbuf[SUBLANES - 1:SUBLANES - 1 + tm, :]
         + cw_ref[2:3, :] * p)
    u_ref[...] = (cb * y).astype(BF16)
    pbuf[0:SUBLANES, :] = pbuf[tm:tm + SUBLANES, :]


def _proj_call(x, w_main, wq_t, wv_t, pos_f, inv_f, conv_w):
    B, S, D = x.shape
    tm = PROJ_TM
    n_main = w_main.shape[1]
    return pl.pallas_call(
        _proj_kernel,
        grid=(B, S // tm),
        in_specs=[
            pl.BlockSpec((None, tm, D), lambda b, i: (b, i, 0)),
            pl.BlockSpec((D, n_main), lambda b, i: (0, 0)),
            pl.BlockSpec((QK_W, D), lambda b, i: (0, 0)),
            pl.BlockSpec((QK_W, D), lambda b, i: (0, 0)),
            pl.BlockSpec((None, 1, tm), lambda b, i: (b, 0, i)),
            pl.BlockSpec((HEAD_DIM // 2, 1), lambda b, i: (0, 0)),
            pl.BlockSpec((CONV_K, CONV_W), lambda b, i: (0, 0)),
        ],
        out_specs=[
            pl.BlockSpec((None, QK_W, tm), lambda b, i: (b, 0, i)),
            pl.BlockSpec((None, tm, QK_W), lambda b, i: (b, i, 0)),
            pl.BlockSpec((None, QK_W, tm), lambda b, i: (b, 0, i)),
            pl.BlockSpec((None, tm, CONV_W), lambda b, i: (b, i, 0)),
        ],
        out_shape=[
            jax.ShapeDtypeStruct((B, QK_W, S), BF16),
            jax.ShapeDtypeStruct((B, S, QK_W), BF16),
            jax.ShapeDtypeStruct((B, QK_W, S), BF16),
            jax.ShapeDtypeStruct((B, S, CONV_W), BF16),
        ],
        scratch_shapes=[pltpu.VMEM((tm + 2 * SUBLANES, CONV_W), F32)],
        compiler_params=pltpu.CompilerParams(
            dimension_semantics=("arbitrary", "arbitrary"),
            vmem_limit_bytes=VMEM_LIMIT),
        name="proj_rope_conv",
    )(x, w_main, wq_t, wv_t, pos_f, inv_f, conv_w)


ACC_ROWS = HEAD_W + 16


def _attn_kernel(lam_ref, q_ref, qn_ref, k_ref, vt_ref, g_ref, o_ref, qzt_ref, *scratch, out_scale):
    nh = ATT_HEADS
    s_refs, p_refs = scratch[0:nh], scratch[nh:2 * nh]
    m_refs, a_refs, acc_refs = scratch[2 * nh:3 * nh], scratch[3 * nh:4 * nh], scratch[4 * nh:5 * nh]
    tq = q_ref.shape[1]
    tk = ATT_TK
    nblk = 2 * tq // LANES
    i = pl.program_id(1)
    feat = lax.broadcasted_iota(jnp.int32, (HEAD_W, tq), 0)
    ones_rows = jnp.ones((ACC_ROWS - HEAD_W, tk), BF16)

    def load_q(src_ref):
        for h in range(nh):
            qf = src_ref[h * HEAD_W:(h + 1) * HEAD_W, :].astype(F32)
            qzt_ref[h, :, 0:tq] = jnp.where(feat < HEAD_DIM, qf, 0.0).astype(BF16)
            qzt_ref[h, :, tq:2 * tq] = jnp.where(feat >= HEAD_DIM, qf, 0.0).astype(BF16)

    for h in range(nh):
        m_refs[h][...] = jnp.full(m_refs[h].shape, NEG_BIG, F32)
        acc_refs[h][...] = jnp.zeros(acc_refs[h].shape, F32)

    def scores(h, k0):
        kj = k_ref[pl.ds(k0, tk), h * HEAD_W:(h + 1) * HEAD_W]
        s_refs[h][:, 0:2 * tq] = _dot(kj, qzt_ref[h])

    def softmax(h, diag=None):
        for c in range(nblk):
            cs = slice(c * LANES, (c + 1) * LANES)
            s = s_refs[h][:, cs]
            if diag is not None:
                key = lax.broadcasted_iota(jnp.int32, (tk, LANES), 0) + diag * tk
                qry = lax.broadcasted_iota(jnp.int32, (tk, LANES), 1) + (c * LANES) % tq
                s = jnp.where(key <= qry, s, NEG_BIG)
            m_old = m_refs[h][:, cs]
            m_new = jnp.maximum(m_old, jnp.max(s, axis=0, keepdims=True))
            a_refs[h][:, cs] = jnp.exp2(m_old - m_new)
            m_refs[h][:, cs] = m_new
            p_refs[h][:, cs] = jnp.exp2(s - m_new).astype(BF16)

    def accumulate(h, k0):
        vj = jnp.concatenate([vt_ref[h * HEAD_W:(h + 1) * HEAD_W, pl.ds(k0, tk)], ones_rows], axis=0)
        acc_refs[h][...] = a_refs[h][...] * acc_refs[h][...] + _dot(vj, p_refs[h][:, 0:2 * tq])

    n_diag = tq // tk
    n_full = n_diag * i

    def tile_start(j):
        return pl.multiple_of(j * tk, tk)

    @pl.when(i == 0)
    def _():
        load_q(q_ref)
        for h in range(nh):
            scores(h, 0)

    @pl.when(i > 0)
    def _():
        for h in range(nh):
            softmax(h)
            scores(h, tile_start(1))

    def step(j):
        for h in range(nh):
            accumulate(h, tile_start(j - 1))
            softmax(h)
            scores(h, tile_start(j + 1))

    n_steps = jnp.maximum(n_full - 1, 0)
    n_trips = n_steps // ATT_UNROLL

    def body(t, carry):
        for u in range(ATT_UNROLL):
            step(ATT_UNROLL * t + 1 + u)
        return carry

    lax.fori_loop(0, n_trips, body, 0)

    def tail(j, carry):
        step(j)
        return carry

    lax.fori_loop(ATT_UNROLL * n_trips + 1, n_steps + 1, tail, 0)

    @pl.when(i > 0)
    def _():
        for h in range(nh):
            accumulate(h, tile_start(n_full - 1))

    for d in range(n_diag):
        for h in range(nh):
            softmax(h, diag=d)
            if d + 1 < n_diag:
                scores(h, tile_start(n_full + d + 1))
            accumulate(h, tile_start(n_full + d))

    load_q(qn_ref)
    for h in range(nh):
        scores(h, 0)

    lam = lam_ref[0]
    for h in range(nh):
        acc = acc_refs[h][0:HEAD_W, :]
        l = acc_refs[h][HEAD_W:HEAD_W + 1, :]
        o = acc[:, 0:tq] / l[:, 0:tq] - lam * (acc[:, tq:2 * tq] / l[:, tq:2 * tq])
        ms = jnp.mean(o * o, axis=0, keepdims=True)
        o = o * lax.rsqrt(ms + SUBLN_EPS) * g_ref[...] * out_scale
        o_ref[:, h * HEAD_W:(h + 1) * HEAD_W] = o.T.astype(BF16)


def _attn_call(lam, q, k, vt, g_col, out_scale):
    B, S, _ = k.shape
    tq = ATT_TQ
    assert ATT_TQ % ATT_TK == 0
    kernel = functools.partial(_attn_kernel, out_scale=out_scale)
    return pl.pallas_call(
        kernel,
        grid=(B, S // tq),
        in_specs=[
            pl.BlockSpec(memory_space=pltpu.SMEM),
            pl.BlockSpec((None, QK_W, tq), lambda b, i: (b, 0, i)),
            pl.BlockSpec((None, QK_W, tq), lambda b, i: (b, 0, jnp.minimum(i + 1, S // tq - 1))),
            pl.BlockSpec((None, S, QK_W), lambda b, i: (b, 0, 0), pipeline_mode=pl.Buffered(1)),
            pl.BlockSpec((None, QK_W, S), lambda b, i: (b, 0, 0), pipeline_mode=pl.Buffered(1)),
            pl.BlockSpec((HEAD_W, 1), lambda b, i: (0, 0)),
        ],
        out_specs=pl.BlockSpec((None, tq, QK_W), lambda b, i: (b, i, 0)),
        out_shape=jax.ShapeDtypeStruct((B, S, QK_W), BF16),
        scratch_shapes=(
            [pltpu.VMEM((ATT_HEADS, HEAD_W, 2 * tq), BF16)]
            + [pltpu.VMEM((ATT_TK, 2 * tq + LANES), F32) for _ in range(ATT_HEADS)]
            + [pltpu.VMEM((ATT_TK, 2 * tq + LANES), BF16) for _ in range(ATT_HEADS)]
            + [pltpu.VMEM((1, 2 * tq), F32) for _ in range(ATT_HEADS)]
            + [pltpu.VMEM((1, 2 * tq), F32) for _ in range(ATT_HEADS)]
            + [pltpu.VMEM((ACC_ROWS, 2 * tq), F32) for _ in range(ATT_HEADS)]
        ),
        compiler_params=pltpu.CompilerParams(
            dimension_semantics=("arbitrary", "arbitrary"),
            vmem_limit_bytes=VMEM_LIMIT),
        name="diff_flash_attn",
    )(lam, q, q, k, vt, g_col)


def _pack_bf16_pairs(h):
    half = h.shape[1] // 2
    hb = h.astype(BF16).astype(F32)
    bits = lax.bitcast_convert_type(hb, jnp.uint32)
    return (bits[:, 0:half] >> 16) | (bits[:, half:] & jnp.uint32(0xFFFF0000))


def _unpack_bf16_pairs(w):
    lo = lax.bitcast_convert_type(w << 16, F32).astype(BF16)
    hi = lax.bitcast_convert_type(w & jnp.uint32(0xFFFF0000), F32).astype(BF16)
    return jnp.concatenate([lo, hi], axis=1)


def _first_index_of(mask, row_f, big):
    return jnp.min(jnp.where(mask, row_f, big), axis=0, keepdims=True)


def _mix_kernel(x_ref, o_ref, u_ref, wg_ref, bg_ref, woa_ref, woc_ref, wmo_ref, g1_ref, b1_ref,
                wr_ref, br_ref,
                h1_ref, h1p_ref, ri_ref, rf_ref, cnt_ref, xs0_ref, carry_ref):
    step = pl.program_id(0)
    xs0_ref[...] = jnp.zeros(xs0_ref.shape, xs0_ref.dtype)

    @pl.when(step == 0)
    def _():
        carry_ref[...] = jnp.zeros(carry_ref.shape, F32)

    x = x_ref[...]
    xb = x.astype(BF16)
    gates = jax.nn.sigmoid(_dot(xb, wg_ref[...]) + bg_ref[...])
    ya = _dot(o_ref[...], woa_ref[...])
    yc = _dot(u_ref[...], woc_ref[...])
    merged = (gates[:, 0:D_MODEL] * ya + gates[:, D_MODEL:] * yc).astype(BF16)
    z = DEEPNORM_ALPHA * x + _dot(merged, wmo_ref[...])

    carry = carry_ref[:, 0:1]
    carry = _mix_tail(z, carry, g1_ref, b1_ref, wr_ref, br_ref, h1_ref, h1p_ref, ri_ref, rf_ref)
    new_carry = jnp.broadcast_to(carry, carry_ref.shape)
    carry_ref[...] = new_carry
    cnt_ref[...] = new_carry.astype(jnp.int32)


def _mix_tail(z, carry, g1_ref, b1_ref, wr_ref, br_ref, h1_ref, h1p_ref, ri_ref, rf_ref):
    tm = z.shape[0]
    mu = jnp.mean(z, axis=-1, keepdims=True)
    zc = z - mu
    var = jnp.mean(zc * zc, axis=-1, keepdims=True)
    h1 = zc * lax.rsqrt(var + LN_EPS) * g1_ref[...] + b1_ref[...]
    h1_ref[...] = h1
    h1p_ref[...] = _pack_bf16_pairs(h1)

    logits = _dot(h1.astype(BF16), wr_ref[...]) + br_ref[...]
    lt = logits.T
    row8 = lax.broadcasted_iota(jnp.int32, (SUBLANES, tm), 0).astype(F32)
    gl = jnp.where(row8 < N_GROUPS, lt[0:SUBLANES, :], NEG_BIG)
    gmax = jnp.max(gl, axis=0, keepdims=True)
    gsel = _first_index_of(gl == gmax, row8, float(SUBLANES))
    gw = 1.0 / jnp.sum(jnp.exp(gl - gmax), axis=0, keepdims=True)
    el = lt[SUBLANES:2 * SUBLANES, :]
    for g in range(1, N_GROUPS):
        el = jnp.where(gsel == float(g), lt[(g + 1) * SUBLANES:(g + 2) * SUBLANES, :], el)
    v1 = jnp.max(el, axis=0, keepdims=True)
    i1 = _first_index_of(el == v1, row8, float(EPG))
    el2 = jnp.where(row8 == i1, -jnp.inf, el)
    v2 = jnp.max(el2, axis=0, keepdims=True)
    i2 = _first_index_of(el2 == v2, row8, float(EPG))
    t = jnp.exp(v2 - v1)
    p1 = gw / (1.0 + t)
    p2 = gw * t / (1.0 + t)
    e1 = gsel * float(EPG) + i1
    e2 = gsel * float(EPG) + i2

    rowe = lax.broadcasted_iota(jnp.int32, (N_EXPERTS, tm), 0).astype(F32)
    oh1 = (rowe == e1).astype(F32)
    oh2 = (rowe == e2).astype(F32)
    r_i = lax.broadcasted_iota(jnp.int32, (tm, tm), 0)
    c_i = lax.broadcasted_iota(jnp.int32, (tm, tm), 1)
    upper = jnp.where(r_i < c_i, 1.0, 0.0).astype(BF16)
    cum1 = _dot(oh1.astype(BF16), upper)
    cum2 = _dot(oh2.astype(BF16), upper)
    tot1 = jnp.sum(oh1, axis=1, keepdims=True)
    tot2 = jnp.sum(oh2, axis=1, keepdims=True)
    rank1 = jnp.sum(oh1 * (carry + cum1), axis=0, keepdims=True)
    rank2 = jnp.sum(oh2 * (carry + tot1 + cum2), axis=0, keepdims=True)

    ri_ref[4:8, :] = jnp.zeros((4, tm), jnp.int32)
    ri_ref[0:1, :] = e1.astype(jnp.int32)
    ri_ref[1:2, :] = e2.astype(jnp.int32)
    ri_ref[2:3, :] = rank1.astype(jnp.int32)
    ri_ref[3:4, :] = rank2.astype(jnp.int32)

    row128 = lax.broadcasted_iota(jnp.int32, (LANES, tm), 0)
    pw = jnp.where(row128 == 0, p1, jnp.where(row128 == 1, p2, 0.0))
    rf_ref[...] = pw.T
    return carry + tot1 + tot2


def _mix_call(x2, o_n, u, wg, bg, woa, woc, wmo, g1, b1, wr, br, n_slots):
    T, D = x2.shape
    tm = MIX_TM
    n = T // tm
    slots_per_step = n_slots // n
    assert slots_per_step * n == n_slots and slots_per_step % SUBLANES == 0
    const = lambda i: (0, 0)
    row = lambda i: (i, 0)
    return pl.pallas_call(
        _mix_kernel,
        grid=(n,),
        in_specs=[
            pl.BlockSpec((tm, D), row),
            pl.BlockSpec((tm, QK_W), row),
            pl.BlockSpec((tm, CONV_W), row),
            pl.BlockSpec(wg.shape, const),
            pl.BlockSpec(bg.shape, const),
            pl.BlockSpec(woa.shape, const),
            pl.BlockSpec(woc.shape, const),
            pl.BlockSpec(wmo.shape, const),
            pl.BlockSpec(g1.shape, const),
            pl.BlockSpec(b1.shape, const),
            pl.BlockSpec(wr.shape, const),
            pl.BlockSpec(br.shape, const),
        ],
        out_specs=[
            pl.BlockSpec((tm, D), row),
            pl.BlockSpec((tm, D // 2), row),
            pl.BlockSpec((SUBLANES, tm), lambda i: (0, i)),
            pl.BlockSpec((tm, LANES), row),
            pl.BlockSpec((N_EXPERTS, LANES), const),
            pl.BlockSpec((slots_per_step, D // 2), row),
        ],
        out_shape=[
            jax.ShapeDtypeStruct((T, D), F32),
            jax.ShapeDtypeStruct((T, D // 2), jnp.uint32),
            jax.ShapeDtypeStruct((SUBLANES, T), jnp.int32),
            jax.ShapeDtypeStruct((T, LANES), F32),
            jax.ShapeDtypeStruct((N_EXPERTS, LANES), jnp.int32),
            jax.ShapeDtypeStruct((n_slots, D // 2), jnp.uint32),
        ],
        scratch_shapes=[pltpu.VMEM((N_EXPERTS, LANES), F32)],
        compiler_params=pltpu.CompilerParams(
            dimension_semantics=("arbitrary",),
            vmem_limit_bytes=VMEM_LIMIT),
        name="mix_ln_router",
    )(x2, o_n, u, wg, bg, woa, woc, wmo, g1, b1, wr, br)


def _sc_scatter_rows(h1p, dest, xs_ref):
    T, W = h1p.shape
    n_workers = SC_CORES * SC_SUBCORES
    per = T // n_workers
    n_win = per // SC_WIN
    assert n_win * SC_WIN * n_workers == T
    mesh = plsc.VectorSubcoreMesh(core_axis_name="core", subcore_axis_name="subcore")

    @pl.kernel(out_type=(), mesh=mesh,
               scratch_types=[pltpu.VMEM((2, SC_WIN, W), h1p.dtype),
                              pltpu.VMEM((2, SC_WIN), jnp.int32), pltpu.VMEM((2, SC_WIN), jnp.int32),
                              pltpu.SemaphoreType.DMA((2,)), pltpu.SemaphoreType.DMA((2,))],
               name="moe_dispatch_sc")
    def scatter(x_hbm, d0_hbm, d1_hbm, o_hbm, xbuf, i0, i1, load_sem, scat_sem):
        wid = lax.axis_index("core") * SC_SUBCORES + lax.axis_index("subcore")

        def loads(j, s):
            base = wid * per + j * SC_WIN
            return [pltpu.make_async_copy(x_hbm.at[pl.ds(base, SC_WIN)], xbuf.at[s], load_sem.at[s]),
                    pltpu.make_async_copy(d0_hbm.at[pl.ds(base, SC_WIN)], i0.at[s], load_sem.at[s]),
                    pltpu.make_async_copy(d1_hbm.at[pl.ds(base, SC_WIN)], i1.at[s], load_sem.at[s])]

        def scatters(s):
            return [pltpu.make_async_copy(xbuf.at[s], o_hbm.at[i0.at[s]], scat_sem.at[s]),
                    pltpu.make_async_copy(xbuf.at[s], o_hbm.at[i1.at[s]], scat_sem.at[s])]

        for c in loads(0, 0):
            c.start()
        for j in range(n_win):
            s = j % 2
            for c in loads(j, s):
                c.wait()
            for c in scatters(s):
                c.start()
            if j >= 1:
                for c in scatters(1 - s):
                    c.wait()
            if j + 1 < n_win:
                for c in loads(j + 1, 1 - s):
                    c.start()
        for c in scatters((n_win - 1) % 2):
            c.wait()

    scatter(h1p, dest[0], dest[1], xs_ref)


def _expert_kernel(be_ref, nu_ref, xs_ref, wg_ref, wu_ref, wd_ref, ys_ref, wgb_ref, wub_ref, wdb_ref):
    i = pl.program_id(0)
    used = i < nu_ref[0]

    @pl.when(used & ((i == 0) | (be_ref[i] != be_ref[jnp.maximum(i - 1, 0)])))
    def _():
        wgb_ref[...] = wg_ref[...].astype(BF16)
        wub_ref[...] = wu_ref[...].astype(BF16)
        wdb_ref[...] = wd_ref[...].astype(BF16)

    @pl.when(used)
    def _():
        xb = _unpack_bf16_pairs(xs_ref[...])
        g = _dot(xb, wgb_ref[...])
        u = _dot(xb, wub_ref[...])
        h = (g * jax.nn.sigmoid(g) * u).astype(BF16)
        ys_ref[...] = _pack_bf16_pairs(_dot(h, wdb_ref[...]))

    @pl.when(i >= nu_ref[0])
    def _():
        ys_ref[...] = jnp.zeros(ys_ref.shape, ys_ref.dtype)


def _expert_call(block_expert, n_used, xs, wg, wu, wd):
    P = xs.shape[0]
    bm = MOE_BM
    nb = P // bm
    xs_map = lambda i, be, nu: (jnp.minimum(i, nu[0] - 1), 0)
    w_map = lambda i, be, nu: (be[i], 0, 0)
    return pl.pallas_call(
        _expert_kernel,
        grid_spec=pltpu.PrefetchScalarGridSpec(
            num_scalar_prefetch=2,
            grid=(nb,),
            in_specs=[
                pl.BlockSpec((bm, D_MODEL // 2), xs_map),
                pl.BlockSpec((None, D_MODEL, EXPERT_FF), w_map),
                pl.BlockSpec((None, D_MODEL, EXPERT_FF), w_map),
                pl.BlockSpec((None, EXPERT_FF, D_MODEL), w_map),
            ],
            out_specs=pl.BlockSpec((bm, D_MODEL // 2), lambda i, be, nu: (i, 0)),
            scratch_shapes=[
                pltpu.VMEM((D_MODEL, EXPERT_FF), BF16),
                pltpu.VMEM((D_MODEL, EXPERT_FF), BF16),
                pltpu.VMEM((EXPERT_FF, D_MODEL), BF16),
            ],
        ),
        out_shape=jax.ShapeDtypeStruct((P, D_MODEL // 2), jnp.uint32),
        compiler_params=pltpu.CompilerParams(
            dimension_semantics=("arbitrary",),
            vmem_limit_bytes=VMEM_LIMIT),
        name="moe_experts",
    )(block_expert, n_used, xs, wg, wu, wd)


def _sc_gather_rows(ys, dest):
    _, T = dest.shape
    W = ys.shape[1]
    n_workers = SC_CORES * SC_SUBCORES
    per = T // n_workers
    win = SC_WIN // 2
    n_win = per // win
    assert n_win * win * n_workers == T
    mesh = plsc.VectorSubcoreMesh(core_axis_name="core", subcore_axis_name="subcore")

    @pl.kernel(out_type=jax.ShapeDtypeStruct((2, T, W), ys.dtype), mesh=mesh,
               scratch_types=[pltpu.VMEM((2, 2, win, W), ys.dtype), pltpu.VMEM((2, 2, win), jnp.int32),
                              pltpu.SemaphoreType.DMA((2,)), pltpu.SemaphoreType.DMA((2,)),
                              pltpu.SemaphoreType.DMA((2,))],
               name="moe_combine_sc")
    def gather(y_hbm, d0_hbm, d1_hbm, o_hbm, buf, idx, idx_sem, gat_sem, out_sem):
        wid = lax.axis_index("core") * SC_SUBCORES + lax.axis_index("subcore")
        d_hbm = (d0_hbm, d1_hbm)

        def base(j):
            return wid * per + j * win

        def idx_loads(j, s):
            return [pltpu.make_async_copy(d_hbm[k].at[pl.ds(base(j), win)], idx.at[s, k], idx_sem.at[s])
                    for k in range(2)]

        def gathers(s):
            return [pltpu.make_async_copy(y_hbm.at[idx.at[s, k]], buf.at[s, k], gat_sem.at[s]) for k in range(2)]

        def stores(j, s):
            return [pltpu.make_async_copy(buf.at[s, k], o_hbm.at[k, pl.ds(base(j), win)], out_sem.at[s])
                    for k in range(2)]

        for c in idx_loads(0, 0):
            c.start()
        for j in range(n_win):
            s = j % 2
            for c in idx_loads(j, s):
                c.wait()
            if j >= 2:
                for c in stores(j - 2, s):
                    c.wait()
            for c in gathers(s):
                c.start()
            if j >= 1:
                for c in gathers(1 - s):
                    c.wait()
                for c in stores(j - 1, 1 - s):
                    c.start()
            if j + 1 < n_win:
                for c in idx_loads(j + 1, 1 - s):
                    c.start()
        s_last = (n_win - 1) % 2
        for c in gathers(s_last):
            c.wait()
        for c in stores(n_win - 1, s_last):
            c.start()
        if n_win >= 2:
            for c in stores(n_win - 2, 1 - s_last):
                c.wait()
        for c in stores(n_win - 1, s_last):
            c.wait()

    return gather(ys, dest[0], dest[1])


def _combine_kernel(y_ref, h1_ref, rf_ref, g2_ref, b2_ref, out_ref):
    rf = rf_ref[...]
    def unpack_f32(w):
        return jnp.concatenate([lax.bitcast_convert_type(w << 16, F32),
                                lax.bitcast_convert_type(w & jnp.uint32(0xFFFF0000), F32)], axis=1)

    ffn = rf[:, 0:1] * unpack_f32(y_ref[0]) + rf[:, 1:2] * unpack_f32(y_ref[1])
    z = DEEPNORM_ALPHA * h1_ref[...] + ffn
    mu = jnp.mean(z, axis=-1, keepdims=True)
    zc = z - mu
    var = jnp.mean(zc * zc, axis=-1, keepdims=True)
    out_ref[...] = zc * lax.rsqrt(var + LN_EPS) * g2_ref[...] + b2_ref[...]


def _combine_call(y12, h1, rf, g2, b2):
    T, D = h1.shape
    tm = COMB_TM
    const = lambda i: (0, 0)
    row = lambda i: (i, 0)
    return pl.pallas_call(
        _combine_kernel,
        grid=(T // tm,),
        in_specs=[
            pl.BlockSpec((2, tm, D // 2), lambda i: (0, i, 0)),
            pl.BlockSpec((tm, D), row),
            pl.BlockSpec((tm, LANES), row),
            pl.BlockSpec(g2.shape, const),
            pl.BlockSpec(b2.shape, const),
        ],
        out_specs=pl.BlockSpec((tm, D), row),
        out_shape=jax.ShapeDtypeStruct((T, D), F32),
        compiler_params=pltpu.CompilerParams(
            dimension_semantics=("arbitrary",),
            vmem_limit_bytes=VMEM_LIMIT),
        name="moe_combine_ln",
    )(y12, h1, rf, g2, b2)


def kernel(x, positions, w_in, b_gate, lambda_q1, lambda_k1, lambda_q2, lambda_k2, subln_g, w_o_att, conv_w, w_o_conv, w_mix_out, ln1_g, ln1_b, w_router_group, b_router_group, w_router_expert, b_router_expert, w_exp_gate, w_exp_up, w_exp_down, ln2_g, ln2_b):
    B, S, D = x.shape
    T = B * S
    l = 0
    lambda_init = 0.8 - 0.6 * math.exp(-0.3 * l)

    inv_freq = ROPE_THETA ** (-jnp.arange(0, HEAD_DIM, 2, dtype=F32) / HEAD_DIM)
    pos_f = positions.astype(F32)[:, None, :]
    inv_f = inv_freq.reshape(HEAD_DIM // 2, 1)

    n_main = 2 * QK_W + QK_W + 3 * CONV_W
    w0 = w_in[l]
    w_main = jnp.concatenate([w0[:, QK_W:2 * QK_W], w0[:, 3 * QK_W:n_main]], axis=1).astype(BF16)
    wq_t = w0[:, 0:QK_W].T.astype(BF16)
    wv_t = w0[:, 2 * QK_W:3 * QK_W].T.astype(BF16)
    w_gates = w0[:, n_main:].astype(BF16)

    q, k, vt, u = _proj_call(x, w_main, wq_t, wv_t, pos_f, inv_f, conv_w[l])

    lam = (jnp.exp(jnp.sum(lambda_q1[l].astype(F32) * lambda_k1[l].astype(F32)))
           - jnp.exp(jnp.sum(lambda_q2[l].astype(F32) * lambda_k2[l].astype(F32)))
           + lambda_init).reshape(1)
    o_n = _attn_call(lam, q, k, vt, subln_g[l].astype(F32).reshape(HEAD_W, 1), 1.0 - lambda_init)

    wr = jnp.zeros((D, LANES), F32)
    wr = wr.at[:, 0:N_GROUPS].set(w_router_group[l]).at[:, SUBLANES:SUBLANES + N_EXPERTS].set(w_router_expert[l])
    br = jnp.zeros((1, LANES), F32)
    br = br.at[0, 0:N_GROUPS].set(b_router_group[l]).at[0, SUBLANES:SUBLANES + N_EXPERTS].set(b_router_expert[l])

    bm = MOE_BM
    A = 2 * T
    P = A + N_EXPERTS * bm
    nb = P // bm
    h1, h1p, ri, rf, cnt, xs0 = _mix_call(
        x.reshape(T, D), o_n.reshape(T, QK_W), u.reshape(T, CONV_W),
        w_gates, b_gate[l].reshape(1, -1).astype(F32),
        w_o_att[l].astype(BF16), w_o_conv[l].astype(BF16), w_mix_out[l].astype(BF16),
        ln1_g[l].reshape(1, D).astype(F32), ln1_b[l].reshape(1, D).astype(F32),
        wr.astype(BF16), br, P)

    counts = cnt[:, 0]
    padded = (counts + bm - 1) // bm * bm
    pend = jnp.cumsum(padded)
    pstart = pend - padded
    onehot = (ri[0:2, :, None] == jnp.arange(N_EXPERTS, dtype=jnp.int32)).astype(jnp.int32)
    dest = jnp.sum(onehot * pstart, axis=-1) + ri[2:4]
    blk_start = jnp.arange(nb, dtype=jnp.int32) * bm
    block_expert = jnp.minimum(
        jnp.sum((pend[None, :] <= blk_start[:, None]).astype(jnp.int32), axis=1), N_EXPERTS - 1)
    n_used = (pend[-1] // bm).astype(jnp.int32).reshape(1)

    xs_ref = jax.new_ref(xs0)
    _sc_scatter_rows(h1p, dest, xs_ref)
    xs = xs_ref[...]
    ys = _expert_call(block_expert, n_used, xs, w_exp_gate[l], w_exp_up[l], w_exp_down[l])
    out = _combine_call(_sc_gather_rows(ys, dest), h1, rf,
                        ln2_g[l].reshape(1, D).astype(F32), ln2_b[l].reshape(1, D).astype(F32))
    return out.reshape(B, S, D)
```

```python
import functools
import math

import jax
import jax.numpy as jnp
from jax import lax
from jax.experimental import pallas as pl
from jax.experimental.pallas import tpu as pltpu
from jax.experimental.pallas import tpu_sc as plsc

D_MODEL = 1024
ATT_HEADS = 4
HEAD_DIM = 64
HEAD_W = 2 * HEAD_DIM
QK_W = ATT_HEADS * HEAD_W
CONV_W = D_MODEL // 2
CONV_K = 3
N_GROUPS = 4
EPG = 8
N_EXPERTS = N_GROUPS * EPG
EXPERT_FF = D_MODEL // 2
ROPE_THETA = 10000.0
SUBLN_EPS = 1e-5
LN_EPS = 1e-5
DEPTH = 1
DEEPNORM_ALPHA = (2.0 * DEPTH) ** 0.25

LANES = 128
SUBLANES = 8
VMEM_LIMIT = 48 * 1024 * 1024

PROJ_TM = 1024
ATT_TQ = 256
ATT_TK = 256
ATT_UNROLL = 4
MIX_TM = 1024
MOE_BM = 512
SC_CORES = 2
SC_SUBCORES = 16
SC_WIN = 64
COMB_TM = 512
NEG_BIG = -1e30

F32 = jnp.float32
BF16 = jnp.bfloat16


def _dot(a, b):
    return jnp.dot(a, b, preferred_element_type=F32)


def _dot_nt(a, b):
    return lax.dot_general(a, b, (((1,), (1,)), ((), ())), preferred_element_type=F32)


def _proj_kernel(x_ref, w_ref, wqt_ref, wvt_ref, pos_ref, invf_ref, cw_ref,
                 qt_ref, k_ref, vt_ref, u_ref, pbuf):
    tm = x_ref.shape[0]
    i = pl.program_id(1)

    @pl.when(i == 0)
    def _():
        pbuf[0:SUBLANES, :] = jnp.zeros((SUBLANES, CONV_W), F32)

    xb = x_ref[...].astype(BF16)
    half = HEAD_DIM // 2
    ang_t = invf_ref[...] * pos_ref[...]
    cos_t = jnp.cos(ang_t)
    sin_t = jnp.sin(ang_t)

    q_scale = HEAD_DIM ** -0.5 * math.log2(math.e)
    qt = _dot_nt(wqt_ref[...], xb)
    for m in range(QK_W // HEAD_DIM):
        lo = qt[m * HEAD_DIM:m * HEAD_DIM + half, :]
        hi = qt[m * HEAD_DIM + half:(m + 1) * HEAD_DIM, :]
        qt_ref[m * HEAD_DIM:m * HEAD_DIM + half, :] = ((lo * cos_t - hi * sin_t) * q_scale).astype(BF16)
        qt_ref[m * HEAD_DIM + half:(m + 1) * HEAD_DIM, :] = ((hi * cos_t + lo * sin_t) * q_scale).astype(BF16)

    lane = lax.broadcasted_iota(jnp.int32, (tm, LANES), 1)
    low_half = (lane % HEAD_DIM) < half
    cos = jnp.concatenate([cos_t, cos_t, cos_t, cos_t], axis=0).T
    sin = jnp.concatenate([-sin_t, sin_t, -sin_t, sin_t], axis=0).T

    def rope(t):
        rot = jnp.where(low_half, pltpu.roll(t, LANES - half, 1), pltpu.roll(t, half, 1))
        return t * cos + rot * sin

    k = _dot(xb, w_ref[:, 0:QK_W])
    for h in range(ATT_HEADS):
        sl = slice(h * HEAD_W, (h + 1) * HEAD_W)
        k_ref[:, sl] = rope(k[:, sl]).astype(BF16)
    vt_ref[...] = _dot_nt(wvt_ref[...], xb).astype(BF16)

    c0 = QK_W
    cb = _dot(xb, w_ref[:, c0:c0 + CONV_W])
    cc = _dot(xb, w_ref[:, c0 + CONV_W:c0 + 2 * CONV_W])
    cx = _dot(xb, w_ref[:, c0 + 2 * CONV_W:c0 + 3 * CONV_W])
    p = cc * cx
    pbuf[SUBLANES:SUBLANES + tm, :] = p
    y = (cw_ref[0:1, :] * pbuf[SUBLANES - 2:SUBLANES - 2 + tm, :]
         + cw_ref[1:2, :] * pbuf[SUBLANES - 1:SUBLANES - 1 + tm, :]
         + cw_ref[2:3, :] * p)
    u_ref[...] = (cb * y).astype(BF16)
    pbuf[0:SUBLANES, :] = pbuf[tm:tm + SUBLANES, :]


def _proj_call(x, w_main, wq_t, wv_t, pos_f, inv_f, conv_w):
    B, S, D = x.shape
    tm = PROJ_TM
    n_main = w_main.shape[1]
    return pl.pallas_call(
        _proj_kernel,
        grid=(B, S // tm),
        in_specs=[
            pl.BlockSpec((None, tm, D), lambda b, i: (b, i, 0)),
            pl.BlockSpec((D, n_main), lambda b, i: (0, 0)),
            pl.BlockSpec((QK_W, D), lambda b, i: (0, 0)),
            pl.BlockSpec((QK_W, D), lambda b, i: (0, 0)),
            pl.BlockSpec((None, 1, tm), lambda b, i: (b, 0, i)),
            pl.BlockSpec((HEAD_DIM // 2, 1), lambda b, i: (0, 0)),
            pl.BlockSpec((CONV_K, CONV_W), lambda b, i: (0, 0)),
        ],
        out_specs=[
            pl.BlockSpec((None, QK_W, tm), lambda b, i: (b, 0, i)),
            pl.BlockSpec((None, tm, QK_W), lambda b, i: (b, i, 0)),
            pl.BlockSpec((None, QK_W, tm), lambda b, i: (b, 0, i)),
            pl.BlockSpec((None, tm, CONV_W), lambda b, i: (b, i, 0)),
        ],
        out_shape=[
            jax.ShapeDtypeStruct((B, QK_W, S), BF16),
            jax.ShapeDtypeStruct((B, S, QK_W), BF16),
            jax.ShapeDtypeStruct((B, QK_W, S), BF16),
            jax.ShapeDtypeStruct((B, S, CONV_W), BF16),
        ],
        scratch_shapes=[pltpu.VMEM((tm + 2 * SUBLANES, CONV_W), F32)],
        compiler_params=pltpu.CompilerParams(
            dimension_semantics=("arbitrary", "arbitrary"),
            vmem_limit_bytes=VMEM_LIMIT),
        name="proj_rope_conv",
    )(x, w_main, wq_t, wv_t, pos_f, inv_f, conv_w)


ACC_ROWS = HEAD_W + 16


def _attn_kernel(lam_ref, q_ref, qn_ref, k_ref, vt_ref, g_ref, o_ref, qzt_ref, *scratch, out_scale):
    nh = ATT_HEADS
    s_refs, p_refs = scratch[0:nh], scratch[nh:2 * nh]
    m_refs, a_refs, acc_refs = scratch[2 * nh:3 * nh], scratch[3 * nh:4 * nh], scratch[4 * nh:5 * nh]
    tq = q_ref.shape[1]
    tk = ATT_TK
    nblk = 2 * tq // LANES
    i = pl.program_id(1)
    feat = lax.broadcasted_iota(jnp.int32, (HEAD_W, tq), 0)
    ones_rows = jnp.ones((ACC_ROWS - HEAD_W, tk), BF16)

    def load_q(src_ref):
        for h in range(nh):
            qf = src_ref[h * HEAD_W:(h + 1) * HEAD_W, :].astype(F32)
            qzt_ref[h, :, 0:tq] = jnp.where(feat < HEAD_DIM, qf, 0.0).astype(BF16)
            qzt_ref[h, :, tq:2 * tq] = jnp.where(feat >= HEAD_DIM, qf, 0.0).astype(BF16)

    def reset_max():
        for h in range(nh):
            m_refs[h][...] = jnp.full(m_refs[h].shape, NEG_BIG, F32)

    for h in range(nh):
        acc_refs[h][...] = jnp.zeros(acc_refs[h].shape, F32)

    def scores(h, k0):
        kj = k_ref[pl.ds(k0, tk), h * HEAD_W:(h + 1) * HEAD_W]
        s_refs[h][:, 0:2 * tq] = _dot(kj, qzt_ref[h])

    def softmax(h, diag=None):
        for c in range(nblk):
            cs = slice(c * LANES, (c + 1) * LANES)
            s = s_refs[h][:, cs]
            if diag is not None:
                key = lax.broadcasted_iota(jnp.int32, (tk, LANES), 0) + diag * tk
                qry = lax.broadcasted_iota(jnp.int32, (tk, LANES), 1) + (c * LANES) % tq
                s = jnp.where(key <= qry, s, NEG_BIG)
            m_old = m_refs[h][:, cs]
            m_new = jnp.maximum(m_old, jnp.max(s, axis=0, keepdims=True))
            a_refs[h][:, cs] = jnp.exp2(m_old - m_new)
            m_refs[h][:, cs] = m_new
            p_refs[h][:, cs] = jnp.exp2(s - m_new).astype(BF16)

    def accumulate(h, k0):
        vj = jnp.concatenate([vt_ref[h * HEAD_W:(h + 1) * HEAD_W, pl.ds(k0, tk)], ones_rows], axis=0)
        acc_refs[h][...] = a_refs[h][...] * acc_refs[h][...] + _dot(vj, p_refs[h][:, 0:2 * tq])

    n_diag = tq // tk
    n_full = n_diag * i

    def tile_start(j):
        return pl.multiple_of(j * tk, tk)

    @pl.when(i == 0)
    def _():
        load_q(q_ref)
        reset_max()
        for h in range(nh):
            scores(h, 0)

    def step(j):
        for h in range(nh):
            accumulate(h, tile_start(j - 1))
            softmax(h)
            scores(h, tile_start(j + 1))

    n_steps = jnp.maximum(n_full - 1, 0)
    n_trips = n_steps // ATT_UNROLL

    def body(t, carry):
        for u in range(ATT_UNROLL):
            step(ATT_UNROLL * t + 1 + u)
        return carry

    lax.fori_loop(0, n_trips, body, 0)

    def tail(j, carry):
        step(j)
        return carry

    lax.fori_loop(ATT_UNROLL * n_trips + 1, n_steps + 1, tail, 0)

    @pl.when(i > 0)
    def _():
        for h in range(nh):
            accumulate(h, tile_start(n_full - 1))

    for d in range(n_diag):
        for h in range(nh):
            softmax(h, diag=d)
            if d + 1 < n_diag:
                scores(h, tile_start(n_full + d + 1))
            accumulate(h, tile_start(n_full + d))

    load_q(qn_ref)
    reset_max()
    for h in range(nh):
        scores(h, 0)
    for h in range(nh):
        softmax(h)
        scores(h, tile_start(1))

    lam = lam_ref[0]
    for h in range(nh):
        acc = acc_refs[h][0:HEAD_W, :]
        l = acc_refs[h][HEAD_W:HEAD_W + 1, :]
        o = acc[:, 0:tq] / l[:, 0:tq] - lam * (acc[:, tq:2 * tq] / l[:, tq:2 * tq])
        ms = jnp.mean(o * o, axis=0, keepdims=True)
        o = o * lax.rsqrt(ms + SUBLN_EPS) * g_ref[...] * out_scale
        o_ref[:, h * HEAD_W:(h + 1) * HEAD_W] = o.T.astype(BF16)


def _attn_call(lam, q, k, vt, g_col, out_scale):
    B, S, _ = k.shape
    tq = ATT_TQ
    assert ATT_TQ % ATT_TK == 0
    kernel = functools.partial(_attn_kernel, out_scale=out_scale)
    return pl.pallas_call(
        kernel,
        grid=(B, S // tq),
        in_specs=[
            pl.BlockSpec(memory_space=pltpu.SMEM),
            pl.BlockSpec((None, QK_W, tq), lambda b, i: (b, 0, i)),
            pl.BlockSpec((None, QK_W, tq), lambda b, i: (b, 0, jnp.minimum(i + 1, S // tq - 1))),
            pl.BlockSpec((None, S, QK_W), lambda b, i: (b, 0, 0), pipeline_mode=pl.Buffered(1)),
            pl.BlockSpec((None, QK_W, S), lambda b, i: (b, 0, 0), pipeline_mode=pl.Buffered(1)),
            pl.BlockSpec((HEAD_W, 1), lambda b, i: (0, 0)),
        ],
        out_specs=pl.BlockSpec((None, tq, QK_W), lambda b, i: (b, i, 0)),
        out_shape=jax.ShapeDtypeStruct((B, S, QK_W), BF16),
        scratch_shapes=(
            [pltpu.VMEM((ATT_HEADS, HEAD_W, 2 * tq), BF16)]
            + [pltpu.VMEM((ATT_TK, 2 * tq + LANES), F32) for _ in range(ATT_HEADS)]
            + [pltpu.VMEM((ATT_TK, 2 * tq + LANES), BF16) for _ in range(ATT_HEADS)]
            + [pltpu.VMEM((1, 2 * tq), F32) for _ in range(ATT_HEADS)]
            + [pltpu.VMEM((1, 2 * tq), F32) for _ in range(ATT_HEADS)]
            + [pltpu.VMEM((ACC_ROWS, 2 * tq), F32) for _ in range(ATT_HEADS)]
        ),
        compiler_params=pltpu.CompilerParams(
            dimension_semantics=("arbitrary", "arbitrary"),
            vmem_limit_bytes=VMEM_LIMIT),
        name="diff_flash_attn",
    )(lam, q, q, k, vt, g_col)


def _pack_bf16_pairs(h):
    half = h.shape[1] // 2
    hb = h.astype(BF16).astype(F32)
    bits = lax.bitcast_convert_type(hb, jnp.uint32)
    return (bits[:, 0:half] >> 16) | (bits[:, half:] & jnp.uint32(0xFFFF0000))


def _unpack_bf16_pairs(w):
    lo = lax.bitcast_convert_type(w << 16, F32).astype(BF16)
    hi = lax.bitcast_convert_type(w & jnp.uint32(0xFFFF0000), F32).astype(BF16)
    return jnp.concatenate([lo, hi], axis=1)


def _first_index_of(mask, row_f, big):
    return jnp.min(jnp.where(mask, row_f, big), axis=0, keepdims=True)


def _mix_kernel(x_ref, o_ref, u_ref, wg_ref, bg_ref, woa_ref, woc_ref, wmo_ref, g1_ref, b1_ref,
                wr_ref, br_ref,
                h1_ref, h1p_ref, ri_ref, rf_ref, cnt_ref, xs0_ref, carry_ref):
    step = pl.program_id(0)
    xs0_ref[...] = jnp.zeros(xs0_ref.shape, xs0_ref.dtype)

    @pl.when(step == 0)
    def _():
        carry_ref[...] = jnp.zeros(carry_ref.shape, F32)

    x = x_ref[...]
    xb = x.astype(BF16)
    gates = jax.nn.sigmoid(_dot(xb, wg_ref[...]) + bg_ref[...])
    ya = _dot(o_ref[...], woa_ref[...])
    yc = _dot(u_ref[...], woc_ref[...])
    merged = (gates[:, 0:D_MODEL] * ya + gates[:, D_MODEL:] * yc).astype(BF16)
    z = DEEPNORM_ALPHA * x + _dot(merged, wmo_ref[...])

    carry = carry_ref[:, 0:1]
    carry = _mix_tail(z, carry, g1_ref, b1_ref, wr_ref, br_ref, h1_ref, h1p_ref, ri_ref, rf_ref)
    new_carry = jnp.broadcast_to(carry, carry_ref.shape)
    carry_ref[...] = new_carry
    cnt_ref[...] = new_carry.astype(jnp.int32)


def _mix_tail(z, carry, g1_ref, b1_ref, wr_ref, br_ref, h1_ref, h1p_ref, ri_ref, rf_ref):
    tm = z.shape[0]
    mu = jnp.mean(z, axis=-1, keepdims=True)
    zc = z - mu
    var = jnp.mean(zc * zc, axis=-1, keepdims=True)
    h1 = zc * lax.rsqrt(var + LN_EPS) * g1_ref[...] + b1_ref[...]
    h1_ref[...] = h1
    h1p_ref[...] = _pack_bf16_pairs(h1)

    logits = _dot(h1.astype(BF16), wr_ref[...]) + br_ref[...]
    lt = logits.T
    row8 = lax.broadcasted_iota(jnp.int32, (SUBLANES, tm), 0).astype(F32)
    gl = jnp.where(row8 < N_GROUPS, lt[0:SUBLANES, :], NEG_BIG)
    gmax = jnp.max(gl, axis=0, keepdims=True)
    gsel = _first_index_of(gl == gmax, row8, float(SUBLANES))
    gw = 1.0 / jnp.sum(jnp.exp(gl - gmax), axis=0, keepdims=True)
    el = lt[SUBLANES:2 * SUBLANES, :]
    for g in range(1, N_GROUPS):
        el = jnp.where(gsel == float(g), lt[(g + 1) * SUBLANES:(g + 2) * SUBLANES, :], el)
    v1 = jnp.max(el, axis=0, keepdims=True)
    i1 = _first_index_of(el == v1, row8, float(EPG))
    el2 = jnp.where(row8 == i1, -jnp.inf, el)
    v2 = jnp.max(el2, axis=0, keepdims=True)
    i2 = _first_index_of(el2 == v2, row8, float(EPG))
    t = jnp.exp(v2 - v1)
    p1 = gw / (1.0 + t)
    p2 = gw * t / (1.0 + t)
    e1 = gsel * float(EPG) + i1
    e2 = gsel * float(EPG) + i2

    rowe = lax.broadcasted_iota(jnp.int32, (N_EXPERTS, tm), 0).astype(F32)
    oh1 = (rowe == e1).astype(F32)
    oh2 = (rowe == e2).astype(F32)
    r_i = lax.broadcasted_iota(jnp.int32, (tm, tm), 0)
    c_i = lax.broadcasted_iota(jnp.int32, (tm, tm), 1)
    upper = jnp.where(r_i < c_i, 1.0, 0.0).astype(BF16)
    cum1 = _dot(oh1.astype(BF16), upper)
    cum2 = _dot(oh2.astype(BF16), upper)
    tot1 = jnp.sum(oh1, axis=1, keepdims=True)
    tot2 = jnp.sum(oh2, axis=1, keepdims=True)
    rank1 = jnp.sum(oh1 * (carry + cum1), axis=0, keepdims=True)
    rank2 = jnp.sum(oh2 * (carry + tot1 + cum2), axis=0, keepdims=True)

    ri_ref[4:8, :] = jnp.zeros((4, tm), jnp.int32)
    ri_ref[0:1, :] = e1.astype(jnp.int32)
    ri_ref[1:2, :] = e2.astype(jnp.int32)
    ri_ref[2:3, :] = rank1.astype(jnp.int32)
    ri_ref[3:4, :] = rank2.astype(jnp.int32)

    row128 = lax.broadcasted_iota(jnp.int32, (LANES, tm), 0)
    pw = jnp.where(row128 == 0, p1, jnp.where(row128 == 1, p2, 0.0))
    rf_ref[...] = pw.T
    return carry + tot1 + tot2


def _mix_call(x2, o_n, u, wg, bg, woa, woc, wmo, g1, b1, wr, br, n_slots):
    T, D = x2.shape
    tm = MIX_TM
    n = T // tm
    slots_per_step = n_slots // n
    assert slots_per_step * n == n_slots and slots_per_step % SUBLANES == 0
    const = lambda i: (0, 0)
    row = lambda i: (i, 0)
    return pl.pallas_call(
        _mix_kernel,
        grid=(n,),
        in_specs=[
            pl.BlockSpec((tm, D), row),
            pl.BlockSpec((tm, QK_W), row),
            pl.BlockSpec((tm, CONV_W), row),
            pl.BlockSpec(wg.shape, const),
            pl.BlockSpec(bg.shape, const),
            pl.BlockSpec(woa.shape, const),
            pl.BlockSpec(woc.shape, const),
            pl.BlockSpec(wmo.shape, const),
            pl.BlockSpec(g1.shape, const),
            pl.BlockSpec(b1.shape, const),
            pl.BlockSpec(wr.shape, const),
            pl.BlockSpec(br.shape, const),
        ],
        out_specs=[
            pl.BlockSpec((tm, D), row),
            pl.BlockSpec((tm, D // 2), row),
            pl.BlockSpec((SUBLANES, tm), lambda i: (0, i)),
            pl.BlockSpec((tm, LANES), row),
            pl.BlockSpec((N_EXPERTS, LANES), const),
            pl.BlockSpec((slots_per_step, D // 2), row),
        ],
        out_shape=[
            jax.ShapeDtypeStruct((T, D), F32),
            jax.ShapeDtypeStruct((T, D // 2), jnp.uint32),
            jax.ShapeDtypeStruct((SUBLANES, T), jnp.int32),
            jax.ShapeDtypeStruct((T, LANES), F32),
            jax.ShapeDtypeStruct((N_EXPERTS, LANES), jnp.int32),
            jax.ShapeDtypeStruct((n_slots, D // 2), jnp.uint32),
        ],
        scratch_shapes=[pltpu.VMEM((N_EXPERTS, LANES), F32)],
        compiler_params=pltpu.CompilerParams(
            dimension_semantics=("arbitrary",),
            vmem_limit_bytes=VMEM_LIMIT),
        name="mix_ln_router",
    )(x2, o_n, u, wg, bg, woa, woc, wmo, g1, b1, wr, br)


def _sc_scatter_rows(h1p, dest, xs_ref):
    T, W = h1p.shape
    n_workers = SC_CORES * SC_SUBCORES
    per = T // n_workers
    n_win = per // SC_WIN
    assert n_win * SC_WIN * n_workers == T
    mesh = plsc.VectorSubcoreMesh(core_axis_name="core", subcore_axis_name="subcore")

    @pl.kernel(out_type=(), mesh=mesh,
               scratch_types=[pltpu.VMEM((2, SC_WIN, W), h1p.dtype),
                              pltpu.VMEM((2, SC_WIN), jnp.int32), pltpu.VMEM((2, SC_WIN), jnp.int32),
                              pltpu.SemaphoreType.DMA((2,)), pltpu.SemaphoreType.DMA((2,))],
               name="moe_dispatch_sc")
    def scatter(x_hbm, d0_hbm, d1_hbm, o_hbm, xbuf, i0, i1, load_sem, scat_sem):
        wid = lax.axis_index("core") * SC_SUBCORES + lax.axis_index("subcore")

        def loads(j, s):
            base = wid * per + j * SC_WIN
            return [pltpu.make_async_copy(x_hbm.at[pl.ds(base, SC_WIN)], xbuf.at[s], load_sem.at[s]),
                    pltpu.make_async_copy(d0_hbm.at[pl.ds(base, SC_WIN)], i0.at[s], load_sem.at[s]),
                    pltpu.make_async_copy(d1_hbm.at[pl.ds(base, SC_WIN)], i1.at[s], load_sem.at[s])]

        def scatters(s):
            return [pltpu.make_async_copy(xbuf.at[s], o_hbm.at[i0.at[s]], scat_sem.at[s]),
                    pltpu.make_async_copy(xbuf.at[s], o_hbm.at[i1.at[s]], scat_sem.at[s])]

        for c in loads(0, 0):
            c.start()
        for j in range(n_win):
            s = j % 2
            for c in loads(j, s):
                c.wait()
            for c in scatters(s):
                c.start()
            if j >= 1:
                for c in scatters(1 - s):
                    c.wait()
            if j + 1 < n_win:
                for c in loads(j + 1, 1 - s):
                    c.start()
        for c in scatters((n_win - 1) % 2):
            c.wait()

    scatter(h1p, dest[0], dest[1], xs_ref)


def _expert_kernel(be_ref, nu_ref, xs_ref, wg_ref, wu_ref, wd_ref, ys_ref, wgb_ref, wub_ref, wdb_ref):
    i = pl.program_id(0)
    used = i < nu_ref[0]

    @pl.when(used & ((i == 0) | (be_ref[i] != be_ref[jnp.maximum(i - 1, 0)])))
    def _():
        wgb_ref[...] = wg_ref[...].astype(BF16)
        wub_ref[...] = wu_ref[...].astype(BF16)
        wdb_ref[...] = wd_ref[...].astype(BF16)

    @pl.when(used)
    def _():
        xb = _unpack_bf16_pairs(xs_ref[...])
        g = _dot(xb, wgb_ref[...])
        u = _dot(xb, wub_ref[...])
        h = (g * jax.nn.sigmoid(g) * u).astype(BF16)
        ys_ref[...] = _pack_bf16_pairs(_dot(h, wdb_ref[...]))

    @pl.when(i >= nu_ref[0])
    def _():
        ys_ref[...] = jnp.zeros(ys_ref.shape, ys_ref.dtype)


def _expert_call(block_expert, n_used, xs, wg, wu, wd):
    P = xs.shape[0]
    bm = MOE_BM
    nb = P // bm
    xs_map = lambda i, be, nu: (jnp.minimum(i, nu[0] - 1), 0)
    w_map = lambda i, be, nu: (be[i], 0, 0)
    return pl.pallas_call(
        _expert_kernel,
        grid_spec=pltpu.PrefetchScalarGridSpec(
            num_scalar_prefetch=2,
            grid=(nb,),
            in_specs=[
                pl.BlockSpec((bm, D_MODEL // 2), xs_map),
                pl.BlockSpec((None, D_MODEL, EXPERT_FF), w_map),
                pl.BlockSpec((None, D_MODEL, EXPERT_FF), w_map),
                pl.BlockSpec((None, EXPERT_FF, D_MODEL), w_map),
            ],
            out_specs=pl.BlockSpec((bm, D_MODEL // 2), lambda i, be, nu: (i, 0)),
            scratch_shapes=[
                pltpu.VMEM((D_MODEL, EXPERT_FF), BF16),
                pltpu.VMEM((D_MODEL, EXPERT_FF), BF16),
                pltpu.VMEM((EXPERT_FF, D_MODEL), BF16),
            ],
        ),
        out_shape=jax.ShapeDtypeStruct((P, D_MODEL // 2), jnp.uint32),
        compiler_params=pltpu.CompilerParams(
            dimension_semantics=("arbitrary",),
            vmem_limit_bytes=VMEM_LIMIT),
        name="moe_experts",
    )(block_expert, n_used, xs, wg, wu, wd)


def _sc_gather_rows(ys, dest):
    _, T = dest.shape
    W = ys.shape[1]
    n_workers = SC_CORES * SC_SUBCORES
    per = T // n_workers
    win = SC_WIN // 2
    n_win = per // win
    assert n_win * win * n_workers == T
    mesh = plsc.VectorSubcoreMesh(core_axis_name="core", subcore_axis_name="subcore")

    @pl.kernel(out_type=jax.ShapeDtypeStruct((2, T, W), ys.dtype), mesh=mesh,
               scratch_types=[pltpu.VMEM((2, 2, win, W), ys.dtype), pltpu.VMEM((2, 2, win), jnp.int32),
                              pltpu.SemaphoreType.DMA((2,)), pltpu.SemaphoreType.DMA((2,)),
                              pltpu.SemaphoreType.DMA((2,))],
               name="moe_combine_sc")
    def gather(y_hbm, d0_hbm, d1_hbm, o_hbm, buf, idx, idx_sem, gat_sem, out_sem):
        wid = lax.axis_index("core") * SC_SUBCORES + lax.axis_index("subcore")
        d_hbm = (d0_hbm, d1_hbm)

        def base(j):
            return wid * per + j * win

        def idx_loads(j, s):
            return [pltpu.make_async_copy(d_hbm[k].at[pl.ds(base(j), win)], idx.at[s, k], idx_sem.at[s])
                    for k in range(2)]

        def gathers(s):
            return [pltpu.make_async_copy(y_hbm.at[idx.at[s, k]], buf.at[s, k], gat_sem.at[s]) for k in range(2)]

        def stores(j, s):
            return [pltpu.make_async_copy(buf.at[s, k], o_hbm.at[k, pl.ds(base(j), win)], out_sem.at[s])
                    for k in range(2)]

        for c in idx_loads(0, 0):
            c.start()
        for j in range(n_win):
            s = j % 2
            for c in idx_loads(j, s):
                c.wait()
            if j >= 2:
                for c in stores(j - 2, s):
                    c.wait()
            for c in gathers(s):
                c.start()
            if j >= 1:
                for c in gathers(1 - s):
                    c.wait()
                for c in stores(j - 1, 1 - s):
                    c.start()
            if j + 1 < n_win:
                for c in idx_loads(j + 1, 1 - s):
                    c.start()
        s_last = (n_win - 1) % 2
        for c in gathers(s_last):
            c.wait()
        for c in stores(n_win - 1, s_last):
            c.start()
        if n_win >= 2:
            for c in stores(n_win - 2, 1 - s_last):
                c.wait()
        for c in stores(n_win - 1, s_last):
            c.wait()

    return gather(ys, dest[0], dest[1])


def _combine_kernel(y_ref, h1_ref, rf_ref, g2_ref, b2_ref, out_ref):
    rf = rf_ref[...]
    def unpack_f32(w):
        return jnp.concatenate([lax.bitcast_convert_type(w << 16, F32),
                                lax.bitcast_convert_type(w & jnp.uint32(0xFFFF0000), F32)], axis=1)

    ffn = rf[:, 0:1] * unpack_f32(y_ref[0]) + rf[:, 1:2] * unpack_f32(y_ref[1])
    z = DEEPNORM_ALPHA * h1_ref[...] + ffn
    mu = jnp.mean(z, axis=-1, keepdims=True)
    zc = z - mu
    var = jnp.mean(zc * zc, axis=-1, keepdims=True)
    out_ref[...] = zc * lax.rsqrt(var + LN_EPS) * g2_ref[...] + b2_ref[...]


def _combine_call(y12, h1, rf, g2, b2):
    T, D = h1.shape
    tm = COMB_TM
    const = lambda i: (0, 0)
    row = lambda i: (i, 0)
    return pl.pallas_call(
        _combine_kernel,
        grid=(T // tm,),
        in_specs=[
            pl.BlockSpec((2, tm, D // 2), lambda i: (0, i, 0)),
            pl.BlockSpec((tm, D), row),
            pl.BlockSpec((tm, LANES), row),
            pl.BlockSpec(g2.shape, const),
            pl.BlockSpec(b2.shape, const),
        ],
        out_specs=pl.BlockSpec((tm, D), row),
        out_shape=jax.ShapeDtypeStruct((T, D), F32),
        compiler_params=pltpu.CompilerParams(
            dimension_semantics=("arbitrary",),
            vmem_limit_bytes=VMEM_LIMIT),
        name="moe_combine_ln",
    )(y12, h1, rf, g2, b2)


def kernel(x, positions, w_in, b_gate, lambda_q1, lambda_k1, lambda_q2, lambda_k2, subln_g, w_o_att, conv_w, w_o_conv, w_mix_out, ln1_g, ln1_b, w_router_group, b_router_group, w_router_expert, b_router_expert, w_exp_gate, w_exp_up, w_exp_down, ln2_g, ln2_b):
    B, S, D = x.shape
    T = B * S
    l = 0
    lambda_init = 0.8 - 0.6 * math.exp(-0.3 * l)

    inv_freq = ROPE_THETA ** (-jnp.arange(0, HEAD_DIM, 2, dtype=F32) / HEAD_DIM)
    pos_f = positions.astype(F32)[:, None, :]
    inv_f = inv_freq.reshape(HEAD_DIM // 2, 1)

    n_main = 2 * QK_W + QK_W + 3 * CONV_W
    w0 = w_in[l]
    w_main = jnp.concatenate([w0[:, QK_W:2 * QK_W], w0[:, 3 * QK_W:n_main]], axis=1).astype(BF16)
    wq_t = w0[:, 0:QK_W].T.astype(BF16)
    wv_t = w0[:, 2 * QK_W:3 * QK_W].T.astype(BF16)
    w_gates = w0[:, n_main:].astype(BF16)

    q, k, vt, u = _proj_call(x, w_main, wq_t, wv_t, pos_f, inv_f, conv_w[l])

    lam = (jnp.exp(jnp.sum(lambda_q1[l].astype(F32) * lambda_k1[l].astype(F32)))
           - jnp.exp(jnp.sum(lambda_q2[l].astype(F32) * lambda_k2[l].astype(F32)))
           + lambda_init).reshape(1)
    o_n = _attn_call(lam, q, k, vt, subln_g[l].astype(F32).reshape(HEAD_W, 1), 1.0 - lambda_init)

    wr = jnp.zeros((D, LANES), F32)
    wr = wr.at[:, 0:N_GROUPS].set(w_router_group[l]).at[:, SUBLANES:SUBLANES + N_EXPERTS].set(w_router_expert[l])
    br = jnp.zeros((1, LANES), F32)
    br = br.at[0, 0:N_GROUPS].set(b_router_group[l]).at[0, SUBLANES:SUBLANES + N_EXPERTS].set(b_router_expert[l])

    bm = MOE_BM
    A = 2 * T
    P = A + N_EXPERTS * bm
    nb = P // bm
    h1, h1p, ri, rf, cnt, xs0 = _mix_call(
        x.reshape(T, D), o_n.reshape(T, QK_W), u.reshape(T, CONV_W),
        w_gates, b_gate[l].reshape(1, -1).astype(F32),
        w_o_att[l].astype(BF16), w_o_conv[l].astype(BF16), w_mix_out[l].astype(BF16),
        ln1_g[l].reshape(1, D).astype(F32), ln1_b[l].reshape(1, D).astype(F32),
        wr.astype(BF16), br, P)

    counts = cnt[:, 0]
    padded = (counts + bm - 1) // bm * bm
    pend = jnp.cumsum(padded)
    pstart = pend - padded
    onehot = (ri[0:2, :, None] == jnp.arange(N_EXPERTS, dtype=jnp.int32)).astype(jnp.int32)
    dest = jnp.sum(onehot * pstart, axis=-1) + ri[2:4]
    blk_start = jnp.arange(nb, dtype=jnp.int32) * bm
    block_expert = jnp.minimum(
        jnp.sum((pend[None, :] <= blk_start[:, None]).astype(jnp.int32), axis=1), N_EXPERTS - 1)
    n_used = (pend[-1] // bm).astype(jnp.int32).reshape(1)

    xs_ref = jax.new_ref(xs0)
    _sc_scatter_rows(h1p, dest, xs_ref)
    xs = xs_ref[...]
    ys = _expert_call(block_expert, n_used, xs, w_exp_gate[l], w_exp_up[l], w_exp_down[l])
    out = _combine_call(_sc_gather_rows(ys, dest), h1, rf,
                        ln2_g[l].reshape(1, D).astype(F32), ln2_b[l].reshape(1, D).astype(F32))
    return out.reshape(B, S, D)
```

```python
import functools
import math

import jax
import jax.numpy as jnp
from jax import lax
from jax.experimental import pallas as pl
from jax.experimental.pallas import tpu as pltpu
from jax.experimental.pallas import tpu_sc as plsc

D_MODEL = 1024
ATT_HEADS = 4
HEAD_DIM = 64
HEAD_W = 2 * HEAD_DIM
QK_W = ATT_HEADS * HEAD_W
CONV_W = D_MODEL // 2
CONV_K = 3
N_GROUPS = 4
EPG = 8
N_EXPERTS = N_GROUPS * EPG
EXPERT_FF = D_MODEL // 2
ROPE_THETA = 10000.0
SUBLN_EPS = 1e-5
LN_EPS = 1e-5
DEPTH = 1
DEEPNORM_ALPHA = (2.0 * DEPTH) ** 0.25

LANES = 128
SUBLANES = 8
VMEM_LIMIT = 48 * 1024 * 1024

PROJ_TM = 1024
ATT_TQ = 256
ATT_TK = 256
ATT_UNROLL = 4
MIX_TM = 1024
MOE_BM = 512
SC_CORES = 2
SC_SUBCORES = 16
SC_WIN = 64
COMB_TM = 512
NEG_BIG = -1e30

F32 = jnp.float32
BF16 = jnp.bfloat16


def _dot(a, b):
    return jnp.dot(a, b, preferred_element_type=F32)


def _dot_nt(a, b):
    return lax.dot_general(a, b, (((1,), (1,)), ((), ())), preferred_element_type=F32)


def _proj_kernel(x_ref, w_ref, wqt_ref, wvt_ref, pos_ref, invf_ref, cw_ref,
                 qt_ref, k_ref, vt_ref, u_ref, pbuf):
    tm = x_ref.shape[0]
    i = pl.program_id(1)

    @pl.when(i == 0)
    def _():
        pbuf[0:SUBLANES, :] = jnp.zeros((SUBLANES, CONV_W), F32)

    xb = x_ref[...].astype(BF16)
    half = HEAD_DIM // 2
    ang_t = invf_ref[...] * pos_ref[...]
    cos_t = jnp.cos(ang_t)
    sin_t = jnp.sin(ang_t)

    q_scale = HEAD_DIM ** -0.5 * math.log2(math.e)
    qt = _dot_nt(wqt_ref[...], xb)
    for m in range(QK_W // HEAD_DIM):
        lo = qt[m * HEAD_DIM:m * HEAD_DIM + half, :]
        hi = qt[m * HEAD_DIM + half:(m + 1) * HEAD_DIM, :]
        qt_ref[m * HEAD_DIM:m * HEAD_DIM + half, :] = ((lo * cos_t - hi * sin_t) * q_scale).astype(BF16)
        qt_ref[m * HEAD_DIM + half:(m + 1) * HEAD_DIM, :] = ((hi * cos_t + lo * sin_t) * q_scale).astype(BF16)

    lane = lax.broadcasted_iota(jnp.int32, (tm, LANES), 1)
    low_half = (lane % HEAD_DIM) < half
    cos = jnp.concatenate([cos_t, cos_t, cos_t, cos_t], axis=0).T
    sin = jnp.concatenate([-sin_t, sin_t, -sin_t, sin_t], axis=0).T

    def rope(t):
        rot = jnp.where(low_half, pltpu.roll(t, LANES - half, 1), pltpu.roll(t, half, 1))
        return t * cos + rot * sin

    k = _dot(xb, w_ref[:, 0:QK_W])
    for h in range(ATT_HEADS):
        sl = slice(h * HEAD_W, (h + 1) * HEAD_W)
        k_ref[:, sl] = rope(k[:, sl]).astype(BF16)
    vt_ref[...] = _dot_nt(wvt_ref[...], xb).astype(BF16)

    c0 = QK_W
    cb = _dot(xb, w_ref[:, c0:c0 + CONV_W])
    cc = _dot(xb, w_ref[:, c0 + CONV_W:c0 + 2 * CONV_W])
    cx = _dot(xb, w_ref[:, c0 + 2 * CONV_W:c0 + 3 * CONV_W])
    p = cc * cx
    pbuf[SUBLANES:SUBLANES + tm, :] = p
    y = (cw_ref[0:1, :] * pbuf[SUBLANES - 2:SUBLANES - 2 + tm, :]
         + cw_ref[1:2, :] * pbuf[SUBLANES - 1:SUBLANES - 1 + tm, :]
         + cw_ref[2:3, :] * p)
    u_ref[...] = (cb * y).astype(BF16)
    pbuf[0:SUBLANES, :] = pbuf[tm:tm + SUBLANES, :]


def _proj_call(x, w_main, wq_t, wv_t, pos_f, inv_f, conv_w):
    B, S, D = x.shape
    tm = PROJ_TM
    n_main = w_main.shape[1]
    return pl.pallas_call(
        _proj_kernel,
        grid=(B, S // tm),
        in_specs=[
            pl.BlockSpec((None, tm, D), lambda b, i: (b, i, 0)),
            pl.BlockSpec((D, n_main), lambda b, i: (0, 0)),
            pl.BlockSpec((QK_W, D), lambda b, i: (0, 0)),
            pl.BlockSpec((QK_W, D), lambda b, i: (0, 0)),
            pl.BlockSpec((None, 1, tm), lambda b, i: (b, 0, i)),
            pl.BlockSpec((HEAD_DIM // 2, 1), lambda b, i: (0, 0)),
            pl.BlockSpec((CONV_K, CONV_W), lambda b, i: (0, 0)),
        ],
        out_specs=[
            pl.BlockSpec((None, QK_W, tm), lambda b, i: (b, 0, i)),
            pl.BlockSpec((None, tm, QK_W), lambda b, i: (b, i, 0)),
            pl.BlockSpec((None, QK_W, tm), lambda b, i: (b, 0, i)),
            pl.BlockSpec((None, tm, CONV_W), lambda b, i: (b, i, 0)),
        ],
        out_shape=[
            jax.ShapeDtypeStruct((B, QK_W, S), BF16),
            jax.ShapeDtypeStruct((B, S, QK_W), BF16),
            jax.ShapeDtypeStruct((B, QK_W, S), BF16),
            jax.ShapeDtypeStruct((B, S, CONV_W), BF16),
        ],
        scratch_shapes=[pltpu.VMEM((tm + 2 * SUBLANES, CONV_W), F32)],
        compiler_params=pltpu.CompilerParams(
            dimension_semantics=("arbitrary", "arbitrary"),
            vmem_limit_bytes=VMEM_LIMIT),
        name="proj_rope_conv",
    )(x, w_main, wq_t, wv_t, pos_f, inv_f, conv_w)


ACC_ROWS = HEAD_W + 16


def _attn_kernel(lam_ref, q_ref, qn_ref, k_ref, vt_ref, g_ref, o_ref, qzt_ref, *scratch, out_scale):
    nh = ATT_HEADS
    s_refs, p_refs = scratch[0:nh], scratch[nh:2 * nh]
    m_refs, a_refs, acc_refs = scratch[2 * nh:3 * nh], scratch[3 * nh:4 * nh], scratch[4 * nh:5 * nh]
    tq = q_ref.shape[1]
    tk = ATT_TK
    nblk = 2 * tq // LANES
    i = pl.program_id(1)
    feat = lax.broadcasted_iota(jnp.int32, (HEAD_W, tq), 0)
    ones_rows = jnp.ones((ACC_ROWS - HEAD_W, tk), BF16)

    def load_q(src_ref):
        for h in range(nh):
            qf = src_ref[h * HEAD_W:(h + 1) * HEAD_W, :].astype(F32)
            qzt_ref[h, :, 0:tq] = jnp.where(feat < HEAD_DIM, qf, 0.0).astype(BF16)
            qzt_ref[h, :, tq:2 * tq] = jnp.where(feat >= HEAD_DIM, qf, 0.0).astype(BF16)

    def reset_max():
        for h in range(nh):
            m_refs[h][...] = jnp.full(m_refs[h].shape, NEG_BIG, F32)

    for h in range(nh):
        acc_refs[h][...] = jnp.zeros(acc_refs[h].shape, F32)

    def scores(h, k0):
        kj = k_ref[pl.ds(k0, tk), h * HEAD_W:(h + 1) * HEAD_W]
        s_refs[h][:, 0:2 * tq] = _dot(kj, qzt_ref[h])

    def softmax(h, diag=None):
        for c in range(nblk):
            cs = slice(c * LANES, (c + 1) * LANES)
            s = s_refs[h][:, cs]
            if diag is not None:
                key = lax.broadcasted_iota(jnp.int32, (tk, LANES), 0) + diag * tk
                qry = lax.broadcasted_iota(jnp.int32, (tk, LANES), 1) + (c * LANES) % tq
                s = jnp.where(key <= qry, s, NEG_BIG)
            m_old = m_refs[h][:, cs]
            m_new = jnp.maximum(m_old, jnp.max(s, axis=0, keepdims=True))
            a_refs[h][:, cs] = jnp.exp2(m_old - m_new)
            m_refs[h][:, cs] = m_new
            p_refs[h][:, cs] = jnp.exp2(s - m_new).astype(BF16)

    def accumulate(h, k0):
        vj = jnp.concatenate([vt_ref[h * HEAD_W:(h + 1) * HEAD_W, pl.ds(k0, tk)], ones_rows], axis=0)
        acc_refs[h][...] = a_refs[h][...] * acc_refs[h][...] + _dot(vj, p_refs[h][:, 0:2 * tq])

    n_diag = tq // tk
    n_full = n_diag * i

    def tile_start(j):
        return pl.multiple_of(j * tk, tk)

    @pl.when(i == 0)
    def _():
        load_q(q_ref)
        reset_max()
        for h in range(nh):
            scores(h, 0)

    def step(j):
        for h in range(nh):
            accumulate(h, tile_start(j - 1))
            softmax(h)
            scores(h, tile_start(j + 1))

    n_steps = jnp.maximum(n_full - 1, 0)
    n_trips = n_steps // ATT_UNROLL

    def body(t, carry):
        for u in range(ATT_UNROLL):
            step(ATT_UNROLL * t + 1 + u)
        return carry

    lax.fori_loop(0, n_trips, body, 0)

    def tail(j, carry):
        step(j)
        return carry

    lax.fori_loop(ATT_UNROLL * n_trips + 1, n_steps + 1, tail, 0)

    @pl.when(i > 0)
    def _():
        for h in range(nh):
            accumulate(h, tile_start(n_full - 1))

    for d in range(n_diag):
        for h in range(nh):
            softmax(h, diag=d)
            if d + 1 < n_diag:
                scores(h, tile_start(n_full + d + 1))
            accumulate(h, tile_start(n_full + d))

    load_q(qn_ref)
    reset_max()
    for h in range(nh):
        scores(h, 0)
    for h in range(nh):
        softmax(h)
        scores(h, tile_start(1))

    lam = lam_ref[0]
    for h in range(nh):
        acc = acc_refs[h][0:HEAD_W, :]
        l = acc_refs[h][HEAD_W:HEAD_W + 1, :]
        o = acc[:, 0:tq] / l[:, 0:tq] - lam * (acc[:, tq:2 * tq] / l[:, tq:2 * tq])
        ms = jnp.mean(o * o, axis=0, keepdims=True)
        o = o * lax.rsqrt(ms + SUBLN_EPS) * g_ref[...] * out_scale
        o_ref[h * HEAD_W:(h + 1) * HEAD_W, :] = o.astype(BF16)


def _attn_call(lam, q, k, vt, g_col, out_scale):
    B, S, _ = k.shape
    tq = ATT_TQ
    assert ATT_TQ % ATT_TK == 0
    kernel = functools.partial(_attn_kernel, out_scale=out_scale)
    return pl.pallas_call(
        kernel,
        grid=(B, S // tq),
        in_specs=[
            pl.BlockSpec(memory_space=pltpu.SMEM),
            pl.BlockSpec((None, QK_W, tq), lambda b, i: (b, 0, i)),
            pl.BlockSpec((None, QK_W, tq), lambda b, i: (b, 0, jnp.minimum(i + 1, S // tq - 1))),
            pl.BlockSpec((None, S, QK_W), lambda b, i: (b, 0, 0), pipeline_mode=pl.Buffered(1)),
            pl.BlockSpec((None, QK_W, S), lambda b, i: (b, 0, 0), pipeline_mode=pl.Buffered(1)),
            pl.BlockSpec((HEAD_W, 1), lambda b, i: (0, 0)),
        ],
        out_specs=pl.BlockSpec((None, QK_W, tq), lambda b, i: (b, 0, i)),
        out_shape=jax.ShapeDtypeStruct((B, QK_W, S), BF16),
        scratch_shapes=(
            [pltpu.VMEM((ATT_HEADS, HEAD_W, 2 * tq), BF16)]
            + [pltpu.VMEM((ATT_TK, 2 * tq + LANES), F32) for _ in range(ATT_HEADS)]
            + [pltpu.VMEM((ATT_TK, 2 * tq + LANES), BF16) for _ in range(ATT_HEADS)]
            + [pltpu.VMEM((1, 2 * tq), F32) for _ in range(ATT_HEADS)]
            + [pltpu.VMEM((1, 2 * tq), F32) for _ in range(ATT_HEADS)]
            + [pltpu.VMEM((ACC_ROWS, 2 * tq), F32) for _ in range(ATT_HEADS)]
        ),
        compiler_params=pltpu.CompilerParams(
            dimension_semantics=("arbitrary", "arbitrary"),
            vmem_limit_bytes=VMEM_LIMIT),
        name="diff_flash_attn",
    )(lam, q, q, k, vt, g_col)


def _pack_bf16_pairs(h):
    half = h.shape[1] // 2
    hb = h.astype(BF16).astype(F32)
    bits = lax.bitcast_convert_type(hb, jnp.uint32)
    return (bits[:, 0:half] >> 16) | (bits[:, half:] & jnp.uint32(0xFFFF0000))


def _unpack_bf16_pairs(w):
    lo = lax.bitcast_convert_type(w << 16, F32).astype(BF16)
    hi = lax.bitcast_convert_type(w & jnp.uint32(0xFFFF0000), F32).astype(BF16)
    return jnp.concatenate([lo, hi], axis=1)


def _first_index_of(mask, row_f, big):
    return jnp.min(jnp.where(mask, row_f, big), axis=0, keepdims=True)


def _mix_kernel(x_ref, o_ref, u_ref, wg_ref, bg_ref, woa_ref, woc_ref, wmo_ref, g1_ref, b1_ref,
                wr_ref, br_ref,
                h1_ref, h1p_ref, ri_ref, rf_ref, cnt_ref, xs0_ref, carry_ref):
    step = pl.program_id(0)
    xs0_ref[...] = jnp.zeros(xs0_ref.shape, xs0_ref.dtype)

    @pl.when(step == 0)
    def _():
        carry_ref[...] = jnp.zeros(carry_ref.shape, F32)

    x = x_ref[...]
    xb = x.astype(BF16)
    gates = jax.nn.sigmoid(_dot(xb, wg_ref[...]) + bg_ref[...])
    ya = lax.dot_general(o_ref[...], woa_ref[...], (((0,), (0,)), ((), ())), preferred_element_type=F32)
    yc = _dot(u_ref[...], woc_ref[...])
    merged = (gates[:, 0:D_MODEL] * ya + gates[:, D_MODEL:] * yc).astype(BF16)
    z = DEEPNORM_ALPHA * x + _dot(merged, wmo_ref[...])

    carry = carry_ref[:, 0:1]
    carry = _mix_tail(z, carry, g1_ref, b1_ref, wr_ref, br_ref, h1_ref, h1p_ref, ri_ref, rf_ref)
    new_carry = jnp.broadcast_to(carry, carry_ref.shape)
    carry_ref[...] = new_carry
    cnt_ref[...] = new_carry.astype(jnp.int32)


def _mix_tail(z, carry, g1_ref, b1_ref, wr_ref, br_ref, h1_ref, h1p_ref, ri_ref, rf_ref):
    tm = z.shape[0]
    mu = jnp.mean(z, axis=-1, keepdims=True)
    zc = z - mu
    var = jnp.mean(zc * zc, axis=-1, keepdims=True)
    h1 = zc * lax.rsqrt(var + LN_EPS) * g1_ref[...] + b1_ref[...]
    h1_ref[...] = h1
    h1p_ref[...] = _pack_bf16_pairs(h1)

    logits = _dot(h1.astype(BF16), wr_ref[...]) + br_ref[...]
    lt = logits.T
    row8 = lax.broadcasted_iota(jnp.int32, (SUBLANES, tm), 0).astype(F32)
    gl = jnp.where(row8 < N_GROUPS, lt[0:SUBLANES, :], NEG_BIG)
    gmax = jnp.max(gl, axis=0, keepdims=True)
    gsel = _first_index_of(gl == gmax, row8, float(SUBLANES))
    gw = 1.0 / jnp.sum(jnp.exp(gl - gmax), axis=0, keepdims=True)
    el = lt[SUBLANES:2 * SUBLANES, :]
    for g in range(1, N_GROUPS):
        el = jnp.where(gsel == float(g), lt[(g + 1) * SUBLANES:(g + 2) * SUBLANES, :], el)
    v1 = jnp.max(el, axis=0, keepdims=True)
    i1 = _first_index_of(el == v1, row8, float(EPG))
    el2 = jnp.where(row8 == i1, -jnp.inf, el)
    v2 = jnp.max(el2, axis=0, keepdims=True)
    i2 = _first_index_of(el2 == v2, row8, float(EPG))
    t = jnp.exp(v2 - v1)
    p1 = gw / (1.0 + t)
    p2 = gw * t / (1.0 + t)
    e1 = gsel * float(EPG) + i1
    e2 = gsel * float(EPG) + i2

    rowe = lax.broadcasted_iota(jnp.int32, (N_EXPERTS, tm), 0).astype(F32)
    oh1 = (rowe == e1).astype(F32)
    oh2 = (rowe == e2).astype(F32)
    r_i = lax.broadcasted_iota(jnp.int32, (tm, tm), 0)
    c_i = lax.broadcasted_iota(jnp.int32, (tm, tm), 1)
    upper = jnp.where(r_i < c_i, 1.0, 0.0).astype(BF16)
    cum1 = _dot(oh1.astype(BF16), upper)
    cum2 = _dot(oh2.astype(BF16), upper)
    tot1 = jnp.sum(oh1, axis=1, keepdims=True)
    tot2 = jnp.sum(oh2, axis=1, keepdims=True)
    rank1 = jnp.sum(oh1 * (carry + cum1), axis=0, keepdims=True)
    rank2 = jnp.sum(oh2 * (carry + tot1 + cum2), axis=0, keepdims=True)

    ri_ref[4:8, :] = jnp.zeros((4, tm), jnp.int32)
    ri_ref[0:1, :] = e1.astype(jnp.int32)
    ri_ref[1:2, :] = e2.astype(jnp.int32)
    ri_ref[2:3, :] = rank1.astype(jnp.int32)
    ri_ref[3:4, :] = rank2.astype(jnp.int32)

    row128 = lax.broadcasted_iota(jnp.int32, (LANES, tm), 0)
    pw = jnp.where(row128 == 0, p1, jnp.where(row128 == 1, p2, 0.0))
    rf_ref[...] = pw.T
    return carry + tot1 + tot2


def _mix_call(x2, o_n, u, wg, bg, woa, woc, wmo, g1, b1, wr, br, n_slots):
    T, D = x2.shape
    tm = MIX_TM
    n = T // tm
    slots_per_step = n_slots // n
    assert slots_per_step * n == n_slots and slots_per_step % SUBLANES == 0
    const = lambda i: (0, 0)
    row = lambda i: (i, 0)
    tiles_per_seq = o_n.shape[2] // tm
    assert tiles_per_seq * tm == o_n.shape[2]
    return pl.pallas_call(
        _mix_kernel,
        grid=(n,),
        in_specs=[
            pl.BlockSpec((tm, D), row),
            pl.BlockSpec((None, QK_W, tm), lambda i: (i // tiles_per_seq, 0, i % tiles_per_seq)),
            pl.BlockSpec((tm, CONV_W), row),
            pl.BlockSpec(wg.shape, const),
            pl.BlockSpec(bg.shape, const),
            pl.BlockSpec(woa.shape, const),
            pl.BlockSpec(woc.shape, const),
            pl.BlockSpec(wmo.shape, const),
            pl.BlockSpec(g1.shape, const),
            pl.BlockSpec(b1.shape, const),
            pl.BlockSpec(wr.shape, const),
            pl.BlockSpec(br.shape, const),
        ],
        out_specs=[
            pl.BlockSpec((tm, D), row),
            pl.BlockSpec((tm, D // 2), row),
            pl.BlockSpec((SUBLANES, tm), lambda i: (0, i)),
            pl.BlockSpec((tm, LANES), row),
            pl.BlockSpec((N_EXPERTS, LANES), const),
            pl.BlockSpec((slots_per_step, D // 2), row),
        ],
        out_shape=[
            jax.ShapeDtypeStruct((T, D), F32),
            jax.ShapeDtypeStruct((T, D // 2), jnp.uint32),
            jax.ShapeDtypeStruct((SUBLANES, T), jnp.int32),
            jax.ShapeDtypeStruct((T, LANES), F32),
            jax.ShapeDtypeStruct((N_EXPERTS, LANES), jnp.int32),
            jax.ShapeDtypeStruct((n_slots, D // 2), jnp.uint32),
        ],
        scratch_shapes=[pltpu.VMEM((N_EXPERTS, LANES), F32)],
        compiler_params=pltpu.CompilerParams(
            dimension_semantics=("arbitrary",),
            vmem_limit_bytes=VMEM_LIMIT),
        name="mix_ln_router",
    )(x2, o_n, u, wg, bg, woa, woc, wmo, g1, b1, wr, br)


def _sc_scatter_rows(h1p, dest, xs_ref):
    T, W = h1p.shape
    n_workers = SC_CORES * SC_SUBCORES
    per = T // n_workers
    n_win = per // SC_WIN
    assert n_win * SC_WIN * n_workers == T
    mesh = plsc.VectorSubcoreMesh(core_axis_name="core", subcore_axis_name="subcore")

    @pl.kernel(out_type=(), mesh=mesh,
               scratch_types=[pltpu.VMEM((2, SC_WIN, W), h1p.dtype),
                              pltpu.VMEM((2, SC_WIN), jnp.int32), pltpu.VMEM((2, SC_WIN), jnp.int32),
                              pltpu.SemaphoreType.DMA((2,)), pltpu.SemaphoreType.DMA((2,))],
               name="moe_dispatch_sc")
    def scatter(x_hbm, d0_hbm, d1_hbm, o_hbm, xbuf, i0, i1, load_sem, scat_sem):
        wid = lax.axis_index("core") * SC_SUBCORES + lax.axis_index("subcore")

        def loads(j, s):
            base = wid * per + j * SC_WIN
            return [pltpu.make_async_copy(x_hbm.at[pl.ds(base, SC_WIN)], xbuf.at[s], load_sem.at[s]),
                    pltpu.make_async_copy(d0_hbm.at[pl.ds(base, SC_WIN)], i0.at[s], load_sem.at[s]),
                    pltpu.make_async_copy(d1_hbm.at[pl.ds(base, SC_WIN)], i1.at[s], load_sem.at[s])]

        def scatters(s):
            return [pltpu.make_async_copy(xbuf.at[s], o_hbm.at[i0.at[s]], scat_sem.at[s]),
                    pltpu.make_async_copy(xbuf.at[s], o_hbm.at[i1.at[s]], scat_sem.at[s])]

        for c in loads(0, 0):
            c.start()
        for j in range(n_win):
            s = j % 2
            for c in loads(j, s):
                c.wait()
            for c in scatters(s):
                c.start()
            if j >= 1:
                for c in scatters(1 - s):
                    c.wait()
            if j + 1 < n_win:
                for c in loads(j + 1, 1 - s):
                    c.start()
        for c in scatters((n_win - 1) % 2):
            c.wait()

    scatter(h1p, dest[0], dest[1], xs_ref)


def _expert_kernel(be_ref, nu_ref, xs_ref, wg_ref, wu_ref, wd_ref, ys_ref, wgb_ref, wub_ref, wdb_ref):
    i = pl.program_id(0)
    used = i < nu_ref[0]

    @pl.when(used & ((i == 0) | (be_ref[i] != be_ref[jnp.maximum(i - 1, 0)])))
    def _():
        wgb_ref[...] = wg_ref[...].astype(BF16)
        wub_ref[...] = wu_ref[...].astype(BF16)
        wdb_ref[...] = wd_ref[...].astype(BF16)

    @pl.when(used)
    def _():
        xb = _unpack_bf16_pairs(xs_ref[...])
        g = _dot(xb, wgb_ref[...])
        u = _dot(xb, wub_ref[...])
        h = (g * jax.nn.sigmoid(g) * u).astype(BF16)
        ys_ref[...] = _pack_bf16_pairs(_dot(h, wdb_ref[...]))

    @pl.when(i >= nu_ref[0])
    def _():
        ys_ref[...] = jnp.zeros(ys_ref.shape, ys_ref.dtype)


def _expert_call(block_expert, n_used, xs, wg, wu, wd):
    P = xs.shape[0]
    bm = MOE_BM
    nb = P // bm
    xs_map = lambda i, be, nu: (jnp.minimum(i, nu[0] - 1), 0)
    w_map = lambda i, be, nu: (be[i], 0, 0)
    return pl.pallas_call(
        _expert_kernel,
        grid_spec=pltpu.PrefetchScalarGridSpec(
            num_scalar_prefetch=2,
            grid=(nb,),
            in_specs=[
                pl.BlockSpec((bm, D_MODEL // 2), xs_map),
                pl.BlockSpec((None, D_MODEL, EXPERT_FF), w_map),
                pl.BlockSpec((None, D_MODEL, EXPERT_FF), w_map),
                pl.BlockSpec((None, EXPERT_FF, D_MODEL), w_map),
            ],
            out_specs=pl.BlockSpec((bm, D_MODEL // 2), lambda i, be, nu: (i, 0)),
            scratch_shapes=[
                pltpu.VMEM((D_MODEL, EXPERT_FF), BF16),
                pltpu.VMEM((D_MODEL, EXPERT_FF), BF16),
                pltpu.VMEM((EXPERT_FF, D_MODEL), BF16),
            ],
        ),
        out_shape=jax.ShapeDtypeStruct((P, D_MODEL // 2), jnp.uint32),
        compiler_params=pltpu.CompilerParams(
            dimension_semantics=("arbitrary",),
            vmem_limit_bytes=VMEM_LIMIT),
        name="moe_experts",
    )(block_expert, n_used, xs, wg, wu, wd)


def _sc_gather_rows(ys, dest):
    _, T = dest.shape
    W = ys.shape[1]
    n_workers = SC_CORES * SC_SUBCORES
    per = T // n_workers
    win = SC_WIN // 2
    n_win = per // win
    assert n_win * win * n_workers == T
    mesh = plsc.VectorSubcoreMesh(core_axis_name="core", subcore_axis_name="subcore")

    @pl.kernel(out_type=jax.ShapeDtypeStruct((2, T, W), ys.dtype), mesh=mesh,
               scratch_types=[pltpu.VMEM((2, 2, win, W), ys.dtype), pltpu.VMEM((2, 2, win), jnp.int32),
                              pltpu.SemaphoreType.DMA((2,)), pltpu.SemaphoreType.DMA((2,)),
                              pltpu.SemaphoreType.DMA((2,))],
               name="moe_combine_sc")
    def gather(y_hbm, d0_hbm, d1_hbm, o_hbm, buf, idx, idx_sem, gat_sem, out_sem):
        wid = lax.axis_index("core") * SC_SUBCORES + lax.axis_index("subcore")
        d_hbm = (d0_hbm, d1_hbm)

        def base(j):
            return wid * per + j * win

        def idx_loads(j, s):
            return [pltpu.make_async_copy(d_hbm[k].at[pl.ds(base(j), win)], idx.at[s, k], idx_sem.at[s])
                    for k in range(2)]

        def gathers(s):
            return [pltpu.make_async_copy(y_hbm.at[idx.at[s, k]], buf.at[s, k], gat_sem.at[s]) for k in range(2)]

        def stores(j, s):
            return [pltpu.make_async_copy(buf.at[s, k], o_hbm.at[k, pl.ds(base(j), win)], out_sem.at[s])
                    for k in range(2)]

        for c in idx_loads(0, 0):
            c.start()
        for j in range(n_win):
            s = j % 2
            for c in idx_loads(j, s):
                c.wait()
            if j >= 2:
                for c in stores(j - 2, s):
                    c.wait()
            for c in gathers(s):
                c.start()
            if j >= 1:
                for c in gathers(1 - s):
                    c.wait()
                for c in stores(j - 1, 1 - s):
                    c.start()
            if j + 1 < n_win:
                for c in idx_loads(j + 1, 1 - s):
                    c.start()
        s_last = (n_win - 1) % 2
        for c in gathers(s_last):
            c.wait()
        for c in stores(n_win - 1, s_last):
            c.start()
        if n_win >= 2:
            for c in stores(n_win - 2, 1 - s_last):
                c.wait()
        for c in stores(n_win - 1, s_last):
            c.wait()

    return gather(ys, dest[0], dest[1])


def _combine_kernel(y_ref, h1_ref, rf_ref, g2_ref, b2_ref, out_ref):
    rf = rf_ref[...]
    def unpack_f32(w):
        return jnp.concatenate([lax.bitcast_convert_type(w << 16, F32),
                                lax.bitcast_convert_type(w & jnp.uint32(0xFFFF0000), F32)], axis=1)

    ffn = rf[:, 0:1] * unpack_f32(y_ref[0]) + rf[:, 1:2] * unpack_f32(y_ref[1])
    z = DEEPNORM_ALPHA * h1_ref[...] + ffn
    mu = jnp.mean(z, axis=-1, keepdims=True)
    zc = z - mu
    var = jnp.mean(zc * zc, axis=-1, keepdims=True)
    out_ref[...] = zc * lax.rsqrt(var + LN_EPS) * g2_ref[...] + b2_ref[...]


def _combine_call(y12, h1, rf, g2, b2):
    T, D = h1.shape
    tm = COMB_TM
    const = lambda i: (0, 0)
    row = lambda i: (i, 0)
    return pl.pallas_call(
        _combine_kernel,
        grid=(T // tm,),
        in_specs=[
            pl.BlockSpec((2, tm, D // 2), lambda i: (0, i, 0)),
            pl.BlockSpec((tm, D), row),
            pl.BlockSpec((tm, LANES), row),
            pl.BlockSpec(g2.shape, const),
            pl.BlockSpec(b2.shape, const),
        ],
        out_specs=pl.BlockSpec((tm, D), row),
        out_shape=jax.ShapeDtypeStruct((T, D), F32),
        compiler_params=pltpu.CompilerParams(
            dimension_semantics=("arbitrary",),
            vmem_limit_bytes=VMEM_LIMIT),
        name="moe_combine_ln",
    )(y12, h1, rf, g2, b2)


def kernel(x, positions, w_in, b_gate, lambda_q1, lambda_k1, lambda_q2, lambda_k2, subln_g, w_o_att, conv_w, w_o_conv, w_mix_out, ln1_g, ln1_b, w_router_group, b_router_group, w_router_expert, b_router_expert, w_exp_gate, w_exp_up, w_exp_down, ln2_g, ln2_b):
    B, S, D = x.shape
    T = B * S
    l = 0
    lambda_init = 0.8 - 0.6 * math.exp(-0.3 * l)

    inv_freq = ROPE_THETA ** (-jnp.arange(0, HEAD_DIM, 2, dtype=F32) / HEAD_DIM)
    pos_f = positions.astype(F32)[:, None, :]
    inv_f = inv_freq.reshape(HEAD_DIM // 2, 1)

    n_main = 2 * QK_W + QK_W + 3 * CONV_W
    w0 = w_in[l]
    w_main = jnp.concatenate([w0[:, QK_W:2 * QK_W], w0[:, 3 * QK_W:n_main]], axis=1).astype(BF16)
    wq_t = w0[:, 0:QK_W].T.astype(BF16)
    wv_t = w0[:, 2 * QK_W:3 * QK_W].T.astype(BF16)
    w_gates = w0[:, n_main:].astype(BF16)

    q, k, vt, u = _proj_call(x, w_main, wq_t, wv_t, pos_f, inv_f, conv_w[l])

    lam = (jnp.exp(jnp.sum(lambda_q1[l].astype(F32) * lambda_k1[l].astype(F32)))
           - jnp.exp(jnp.sum(lambda_q2[l].astype(F32) * lambda_k2[l].astype(F32)))
           + lambda_init).reshape(1)
    o_n = _attn_call(lam, q, k, vt, subln_g[l].astype(F32).reshape(HEAD_W, 1), 1.0 - lambda_init)

    wr = jnp.zeros((D, LANES), F32)
    wr = wr.at[:, 0:N_GROUPS].set(w_router_group[l]).at[:, SUBLANES:SUBLANES + N_EXPERTS].set(w_router_expert[l])
    br = jnp.zeros((1, LANES), F32)
    br = br.at[0, 0:N_GROUPS].set(b_router_group[l]).at[0, SUBLANES:SUBLANES + N_EXPERTS].set(b_router_expert[l])

    bm = MOE_BM
    A = 2 * T
    P = A + N_EXPERTS * bm
    nb = P // bm
    h1, h1p, ri, rf, cnt, xs0 = _mix_call(
        x.reshape(T, D), o_n, u.reshape(T, CONV_W),
        w_gates, b_gate[l].reshape(1, -1).astype(F32),
        w_o_att[l].astype(BF16), w_o_conv[l].astype(BF16), w_mix_out[l].astype(BF16),
        ln1_g[l].reshape(1, D).astype(F32), ln1_b[l].reshape(1, D).astype(F32),
        wr.astype(BF16), br, P)

    counts = cnt[:, 0]
    padded = (counts + bm - 1) // bm * bm
    pend = jnp.cumsum(padded)
    pstart = pend - padded
    onehot = (ri[0:2, :, None] == jnp.arange(N_EXPERTS, dtype=jnp.int32)).astype(jnp.int32)
    dest = jnp.sum(onehot * pstart, axis=-1) + ri[2:4]
    blk_start = jnp.arange(nb, dtype=jnp.int32) * bm
    block_expert = jnp.minimum(
        jnp.sum((pend[None, :] <= blk_start[:, None]).astype(jnp.int32), axis=1), N_EXPERTS - 1)
    n_used = (pend[-1] // bm).astype(jnp.int32).reshape(1)

    xs_ref = jax.new_ref(xs0)
    _sc_scatter_rows(h1p, dest, xs_ref)
    xs = xs_ref[...]
    ys = _expert_call(block_expert, n_used, xs, w_exp_gate[l], w_exp_up[l], w_exp_down[l])
    out = _combine_call(_sc_gather_rows(ys, dest), h1, rf,
                        ln2_g[l].reshape(1, D).astype(F32), ln2_b[l].reshape(1, D).astype(F32))
    return out.reshape(B, S, D)
```

```python
import functools
import math

import jax
import jax.numpy as jnp
from jax import lax
from jax.experimental import pallas as pl
from jax.experimental.pallas import tpu as pltpu
from jax.experimental.pallas import tpu_sc as plsc

D_MODEL = 1024
ATT_HEADS = 4
HEAD_DIM = 64
HEAD_W = 2 * HEAD_DIM
QK_W = ATT_HEADS * HEAD_W
CONV_W = D_MODEL // 2
CONV_K = 3
N_GROUPS = 4
EPG = 8
N_EXPERTS = N_GROUPS * EPG
EXPERT_FF = D_MODEL // 2
ROPE_THETA = 10000.0
SUBLN_EPS = 1e-5
LN_EPS = 1e-5
DEPTH = 1
DEEPNORM_ALPHA = (2.0 * DEPTH) ** 0.25

LANES = 128
SUBLANES = 8
VMEM_LIMIT = 48 * 1024 * 1024

PROJ_TM = 1024
ATT_TQ = 256
ATT_TK = 256
ATT_UNROLL = 4
MIX_TM = 1024
MOE_BM = 512
SC_CORES = 2
SC_SUBCORES = 16
SC_WIN = 64
COMB_TM = 512
NEG_BIG = -1e30

F32 = jnp.float32
BF16 = jnp.bfloat16


def _dot(a, b):
    return jnp.dot(a, b, preferred_element_type=F32)


def _dot_nt(a, b):
    return lax.dot_general(a, b, (((1,), (1,)), ((), ())), preferred_element_type=F32)


def _proj_kernel(x_ref, w_ref, wqt_ref, wvt_ref, pos_ref, invf_ref, cw_ref,
                 qt_ref, k_ref, vt_ref, u_ref, pbuf):
    tm = x_ref.shape[0]
    i = pl.program_id(1)

    @pl.when(i == 0)
    def _():
        pbuf[0:SUBLANES, :] = jnp.zeros((SUBLANES, CONV_W), F32)

    xb = x_ref[...].astype(BF16)
    half = HEAD_DIM // 2
    ang_t = invf_ref[...] * pos_ref[...]
    cos_t = jnp.cos(ang_t)
    sin_t = jnp.sin(ang_t)

    q_scale = HEAD_DIM ** -0.5 * math.log2(math.e)
    qt = _dot_nt(wqt_ref[...], xb)
    for m in range(QK_W // HEAD_DIM):
        lo = qt[m * HEAD_DIM:m * HEAD_DIM + half, :]
        hi = qt[m * HEAD_DIM + half:(m + 1) * HEAD_DIM, :]
        qt_ref[m * HEAD_DIM:m * HEAD_DIM + half, :] = ((lo * cos_t - hi * sin_t) * q_scale).astype(BF16)
        qt_ref[m * HEAD_DIM + half:(m + 1) * HEAD_DIM, :] = ((hi * cos_t + lo * sin_t) * q_scale).astype(BF16)

    lane = lax.broadcasted_iota(jnp.int32, (tm, LANES), 1)
    low_half = (lane % HEAD_DIM) < half
    cos = jnp.concatenate([cos_t, cos_t, cos_t, cos_t], axis=0).T
    sin = jnp.concatenate([-sin_t, sin_t, -sin_t, sin_t], axis=0).T

    def rope(t):
        rot = jnp.where(low_half, pltpu.roll(t, LANES - half, 1), pltpu.roll(t, half, 1))
        return t * cos + rot * sin

    k = _dot(xb, w_ref[:, 0:QK_W])
    for h in range(ATT_HEADS):
        sl = slice(h * HEAD_W, (h + 1) * HEAD_W)
        k_ref[:, sl] = rope(k[:, sl]).astype(BF16)
    vt_ref[...] = _dot_nt(wvt_ref[...], xb).astype(BF16)

    c0 = QK_W
    cb = _dot(xb, w_ref[:, c0:c0 + CONV_W])
    cc = _dot(xb, w_ref[:, c0 + CONV_W:c0 + 2 * CONV_W])
    cx = _dot(xb, w_ref[:, c0 + 2 * CONV_W:c0 + 3 * CONV_W])
    p = cc * cx
    pbuf[SUBLANES:SUBLANES + tm, :] = p
    y = (cw_ref[0:1, :] * pbuf[SUBLANES - 2:SUBLANES - 2 + tm, :]
         + cw_ref[1:2, :] * pbuf[SUBLANES - 1:SUBLANES - 1 + tm, :]
         + cw_ref[2:3, :] * p)
    u_ref[...] = (cb * y).astype(BF16)
    pbuf[0:SUBLANES, :] = pbuf[tm:tm + SUBLANES, :]


def _proj_call(x, w_main, wq_t, wv_t, pos_f, inv_f, conv_w):
    B, S, D = x.shape
    tm = PROJ_TM
    n_main = w_main.shape[1]
    return pl.pallas_call(
        _proj_kernel,
        grid=(B, S // tm),
        in_specs=[
            pl.BlockSpec((None, tm, D), lambda b, i: (b, i, 0)),
            pl.BlockSpec((D, n_main), lambda b, i: (0, 0)),
            pl.BlockSpec((QK_W, D), lambda b, i: (0, 0)),
            pl.BlockSpec((QK_W, D), lambda b, i: (0, 0)),
            pl.BlockSpec((None, 1, tm), lambda b, i: (b, 0, i)),
            pl.BlockSpec((HEAD_DIM // 2, 1), lambda b, i: (0, 0)),
            pl.BlockSpec((CONV_K, CONV_W), lambda b, i: (0, 0)),
        ],
        out_specs=[
            pl.BlockSpec((None, QK_W, tm), lambda b, i: (b, 0, i)),
            pl.BlockSpec((None, tm, QK_W), lambda b, i: (b, i, 0)),
            pl.BlockSpec((None, QK_W, tm), lambda b, i: (b, 0, i)),
            pl.BlockSpec((None, tm, CONV_W), lambda b, i: (b, i, 0)),
        ],
        out_shape=[
            jax.ShapeDtypeStruct((B, QK_W, S), BF16),
            jax.ShapeDtypeStruct((B, S, QK_W), BF16),
            jax.ShapeDtypeStruct((B, QK_W, S), BF16),
            jax.ShapeDtypeStruct((B, S, CONV_W), BF16),
        ],
        scratch_shapes=[pltpu.VMEM((tm + 2 * SUBLANES, CONV_W), F32)],
        compiler_params=pltpu.CompilerParams(
            dimension_semantics=("arbitrary", "arbitrary"),
            vmem_limit_bytes=VMEM_LIMIT),
        name="proj_rope_conv",
    )(x, w_main, wq_t, wv_t, pos_f, inv_f, conv_w)


ACC_ROWS = HEAD_W + 16


def _attn_kernel(lam_ref, q_ref, qn_ref, k_ref, vt_ref, g_ref, o_ref, qzt_ref, *scratch, out_scale):
    nh = ATT_HEADS
    s_refs, p_refs = scratch[0:nh], scratch[nh:2 * nh]
    m_refs, a_refs, acc_refs = scratch[2 * nh:3 * nh], scratch[3 * nh:4 * nh], scratch[4 * nh:5 * nh]
    tq = q_ref.shape[1]
    tk = ATT_TK
    nblk = 2 * tq // LANES
    i = pl.program_id(1)
    feat = lax.broadcasted_iota(jnp.int32, (HEAD_W, tq), 0)
    ones_rows = jnp.ones((ACC_ROWS - HEAD_W, tk), BF16)

    def load_q(src_ref):
        for h in range(nh):
            qf = src_ref[h * HEAD_W:(h + 1) * HEAD_W, :].astype(F32)
            qzt_ref[h, :, 0:tq] = jnp.where(feat < HEAD_DIM, qf, 0.0).astype(BF16)
            qzt_ref[h, :, tq:2 * tq] = jnp.where(feat >= HEAD_DIM, qf, 0.0).astype(BF16)

    def reset_max():
        for h in range(nh):
            m_refs[h][...] = jnp.full(m_refs[h].shape, NEG_BIG, F32)

    def reset_acc():
        for h in range(nh):
            acc_refs[h][...] = jnp.zeros(acc_refs[h].shape, F32)

    def scores(h, k0):
        kj = k_ref[pl.ds(k0, tk), h * HEAD_W:(h + 1) * HEAD_W]
        s_refs[h][:, 0:2 * tq] = _dot(kj, qzt_ref[h])

    def softmax(h, diag=None):
        for c in range(nblk):
            cs = slice(c * LANES, (c + 1) * LANES)
            s = s_refs[h][:, cs]
            if diag is not None:
                key = lax.broadcasted_iota(jnp.int32, (tk, LANES), 0) + diag * tk
                qry = lax.broadcasted_iota(jnp.int32, (tk, LANES), 1) + (c * LANES) % tq
                s = jnp.where(key <= qry, s, NEG_BIG)
            m_old = m_refs[h][:, cs]
            m_new = jnp.maximum(m_old, jnp.max(s, axis=0, keepdims=True))
            a_refs[h][:, cs] = jnp.exp2(m_old - m_new)
            m_refs[h][:, cs] = m_new
            p_refs[h][:, cs] = jnp.exp2(s - m_new).astype(BF16)

    def accumulate(h, k0):
        vj = jnp.concatenate([vt_ref[h * HEAD_W:(h + 1) * HEAD_W, pl.ds(k0, tk)], ones_rows], axis=0)
        acc_refs[h][...] = a_refs[h][...] * acc_refs[h][...] + _dot(vj, p_refs[h][:, 0:2 * tq])

    n_diag = tq // tk
    n_full = n_diag * i

    def tile_start(j):
        return pl.multiple_of(j * tk, tk)

    @pl.when(i == 0)
    def _():
        load_q(q_ref)
        reset_max()
        reset_acc()
        for h in range(nh):
            scores(h, 0)

    def step(j):
        for h in range(nh):
            accumulate(h, tile_start(j - 1))
            softmax(h)
            scores(h, tile_start(j + 1))

    n_steps = jnp.maximum(n_full - 1, 0)
    n_trips = n_steps // ATT_UNROLL

    def body(t, carry):
        for u in range(ATT_UNROLL):
            step(ATT_UNROLL * t + 1 + u)
        return carry

    lax.fori_loop(0, n_trips, body, 0)

    def tail(j, carry):
        step(j)
        return carry

    lax.fori_loop(ATT_UNROLL * n_trips + 1, n_steps + 1, tail, 0)

    @pl.when(i > 0)
    def _():
        for h in range(nh):
            accumulate(h, tile_start(n_full - 1))

    for d in range(n_diag):
        for h in range(nh):
            softmax(h, diag=d)
            if d + 1 < n_diag:
                scores(h, tile_start(n_full + d + 1))
            accumulate(h, tile_start(n_full + d))

    load_q(qn_ref)
    reset_max()
    for h in range(nh):
        scores(h, 0)
    for h in range(nh):
        softmax(h)
        scores(h, tile_start(1))

    lam = lam_ref[0]
    for h in range(nh):
        acc = acc_refs[h][0:HEAD_W, :]
        l = acc_refs[h][HEAD_W:HEAD_W + 1, :]
        o = acc[:, 0:tq] / l[:, 0:tq] - lam * (acc[:, tq:2 * tq] / l[:, tq:2 * tq])
        ms = jnp.mean(o * o, axis=0, keepdims=True)
        o = o * lax.rsqrt(ms + SUBLN_EPS) * g_ref[...] * out_scale
        o_ref[h * HEAD_W:(h + 1) * HEAD_W, :] = o.astype(BF16)
    reset_acc()


def _attn_call(lam, q, k, vt, g_col, out_scale):
    B, S, _ = k.shape
    tq = ATT_TQ
    assert ATT_TQ % ATT_TK == 0
    kernel = functools.partial(_attn_kernel, out_scale=out_scale)
    return pl.pallas_call(
        kernel,
        grid=(B, S // tq),
        in_specs=[
            pl.BlockSpec(memory_space=pltpu.SMEM),
            pl.BlockSpec((None, QK_W, tq), lambda b, i: (b, 0, i)),
            pl.BlockSpec((None, QK_W, tq), lambda b, i: (b, 0, jnp.minimum(i + 1, S // tq - 1))),
            pl.BlockSpec((None, S, QK_W), lambda b, i: (b, 0, 0), pipeline_mode=pl.Buffered(1)),
            pl.BlockSpec((None, QK_W, S), lambda b, i: (b, 0, 0), pipeline_mode=pl.Buffered(1)),
            pl.BlockSpec((HEAD_W, 1), lambda b, i: (0, 0)),
        ],
        out_specs=pl.BlockSpec((None, QK_W, tq), lambda b, i: (b, 0, i)),
        out_shape=jax.ShapeDtypeStruct((B, QK_W, S), BF16),
        scratch_shapes=(
            [pltpu.VMEM((ATT_HEADS, HEAD_W, 2 * tq), BF16)]
            + [pltpu.VMEM((ATT_TK, 2 * tq + LANES), F32) for _ in range(ATT_HEADS)]
            + [pltpu.VMEM((ATT_TK, 2 * tq + LANES), BF16) for _ in range(ATT_HEADS)]
            + [pltpu.VMEM((1, 2 * tq), F32) for _ in range(ATT_HEADS)]
            + [pltpu.VMEM((1, 2 * tq), F32) for _ in range(ATT_HEADS)]
            + [pltpu.VMEM((ACC_ROWS, 2 * tq), F32) for _ in range(ATT_HEADS)]
        ),
        compiler_params=pltpu.CompilerParams(
            dimension_semantics=("arbitrary", "arbitrary"),
            vmem_limit_bytes=VMEM_LIMIT),
        name="diff_flash_attn",
    )(lam, q, q, k, vt, g_col)


def _pack_bf16_pairs(h):
    half = h.shape[1] // 2
    hb = h.astype(BF16).astype(F32)
    bits = lax.bitcast_convert_type(hb, jnp.uint32)
    return (bits[:, 0:half] >> 16) | (bits[:, half:] & jnp.uint32(0xFFFF0000))


def _unpack_bf16_pairs(w):
    lo = lax.bitcast_convert_type(w << 16, F32).astype(BF16)
    hi = lax.bitcast_convert_type(w & jnp.uint32(0xFFFF0000), F32).astype(BF16)
    return jnp.concatenate([lo, hi], axis=1)


def _first_index_of(mask, row_f, big):
    return jnp.min(jnp.where(mask, row_f, big), axis=0, keepdims=True)


def _mix_kernel(x_ref, o_ref, u_ref, wg_ref, bg_ref, woa_ref, woc_ref, wmo_ref, g1_ref, b1_ref,
                wr_ref, br_ref,
                h1_ref, h1p_ref, ri_ref, rf_ref, cnt_ref, xs0_ref, carry_ref):
    step = pl.program_id(0)
    xs0_ref[...] = jnp.zeros(xs0_ref.shape, xs0_ref.dtype)

    @pl.when(step == 0)
    def _():
        carry_ref[...] = jnp.zeros(carry_ref.shape, F32)

    x = x_ref[...]
    xb = x.astype(BF16)
    gates = jax.nn.sigmoid(_dot(xb, wg_ref[...]) + bg_ref[...])
    ya = lax.dot_general(o_ref[...], woa_ref[...], (((0,), (0,)), ((), ())), preferred_element_type=F32)
    yc = _dot(u_ref[...], woc_ref[...])
    merged = (gates[:, 0:D_MODEL] * ya + gates[:, D_MODEL:] * yc).astype(BF16)
    z = DEEPNORM_ALPHA * x + _dot(merged, wmo_ref[...])

    carry = carry_ref[:, 0:1]
    carry = _mix_tail(z, carry, g1_ref, b1_ref, wr_ref, br_ref, h1_ref, h1p_ref, ri_ref, rf_ref)
    new_carry = jnp.broadcast_to(carry, carry_ref.shape)
    carry_ref[...] = new_carry
    cnt_ref[...] = new_carry.astype(jnp.int32)


def _mix_tail(z, carry, g1_ref, b1_ref, wr_ref, br_ref, h1_ref, h1p_ref, ri_ref, rf_ref):
    tm = z.shape[0]
    mu = jnp.mean(z, axis=-1, keepdims=True)
    zc = z - mu
    var = jnp.mean(zc * zc, axis=-1, keepdims=True)
    h1 = zc * lax.rsqrt(var + LN_EPS) * g1_ref[...] + b1_ref[...]
    h1_ref[...] = h1
    h1p_ref[...] = _pack_bf16_pairs(h1)

    logits = _dot(h1.astype(BF16), wr_ref[...]) + br_ref[...]
    lt = logits.T
    row8 = lax.broadcasted_iota(jnp.int32, (SUBLANES, tm), 0).astype(F32)
    gl = jnp.where(row8 < N_GROUPS, lt[0:SUBLANES, :], NEG_BIG)
    gmax = jnp.max(gl, axis=0, keepdims=True)
    gsel = _first_index_of(gl == gmax, row8, float(SUBLANES))
    gw = 1.0 / jnp.sum(jnp.exp(gl - gmax), axis=0, keepdims=True)
    el = lt[SUBLANES:2 * SUBLANES, :]
    for g in range(1, N_GROUPS):
        el = jnp.where(gsel == float(g), lt[(g + 1) * SUBLANES:(g + 2) * SUBLANES, :], el)
    v1 = jnp.max(el, axis=0, keepdims=True)
    i1 = _first_index_of(el == v1, row8, float(EPG))
    el2 = jnp.where(row8 == i1, -jnp.inf, el)
    v2 = jnp.max(el2, axis=0, keepdims=True)
    i2 = _first_index_of(el2 == v2, row8, float(EPG))
    t = jnp.exp(v2 - v1)
    p1 = gw / (1.0 + t)
    p2 = gw * t / (1.0 + t)
    e1 = gsel * float(EPG) + i1
    e2 = gsel * float(EPG) + i2

    rowe = lax.broadcasted_iota(jnp.int32, (N_EXPERTS, tm), 0).astype(F32)
    oh1 = (rowe == e1).astype(F32)
    oh2 = (rowe == e2).astype(F32)
    r_i = lax.broadcasted_iota(jnp.int32, (tm, tm), 0)
    c_i = lax.broadcasted_iota(jnp.int32, (tm, tm), 1)
    upper = jnp.where(r_i < c_i, 1.0, 0.0).astype(BF16)
    cum1 = _dot(oh1.astype(BF16), upper)
    cum2 = _dot(oh2.astype(BF16), upper)
    tot1 = jnp.sum(oh1, axis=1, keepdims=True)
    tot2 = jnp.sum(oh2, axis=1, keepdims=True)
    rank1 = jnp.sum(oh1 * (carry + cum1), axis=0, keepdims=True)
    rank2 = jnp.sum(oh2 * (carry + tot1 + cum2), axis=0, keepdims=True)

    ri_ref[4:8, :] = jnp.zeros((4, tm), jnp.int32)
    ri_ref[0:1, :] = e1.astype(jnp.int32)
    ri_ref[1:2, :] = e2.astype(jnp.int32)
    ri_ref[2:3, :] = rank1.astype(jnp.int32)
    ri_ref[3:4, :] = rank2.astype(jnp.int32)

    row128 = lax.broadcasted_iota(jnp.int32, (LANES, tm), 0)
    pw = jnp.where(row128 == 0, p1, jnp.where(row128 == 1, p2, 0.0))
    rf_ref[...] = pw.T
    return carry + tot1 + tot2


def _mix_call(x2, o_n, u, wg, bg, woa, woc, wmo, g1, b1, wr, br, n_slots):
    T, D = x2.shape
    tm = MIX_TM
    n = T // tm
    slots_per_step = n_slots // n
    assert slots_per_step * n == n_slots and slots_per_step % SUBLANES == 0
    const = lambda i: (0, 0)
    row = lambda i: (i, 0)
    tiles_per_seq = o_n.shape[2] // tm
    assert tiles_per_seq * tm == o_n.shape[2]
    return pl.pallas_call(
        _mix_kernel,
        grid=(n,),
        in_specs=[
            pl.BlockSpec((tm, D), row),
            pl.BlockSpec((None, QK_W, tm), lambda i: (i // tiles_per_seq, 0, i % tiles_per_seq)),
            pl.BlockSpec((tm, CONV_W), row),
            pl.BlockSpec(wg.shape, const),
            pl.BlockSpec(bg.shape, const),
            pl.BlockSpec(woa.shape, const),
            pl.BlockSpec(woc.shape, const),
            pl.BlockSpec(wmo.shape, const),
            pl.BlockSpec(g1.shape, const),
            pl.BlockSpec(b1.shape, const),
            pl.BlockSpec(wr.shape, const),
            pl.BlockSpec(br.shape, const),
        ],
        out_specs=[
            pl.BlockSpec((tm, D), row),
            pl.BlockSpec((tm, D // 2), row),
            pl.BlockSpec((SUBLANES, tm), lambda i: (0, i)),
            pl.BlockSpec((tm, LANES), row),
            pl.BlockSpec((N_EXPERTS, LANES), const),
            pl.BlockSpec((slots_per_step, D // 2), row),
        ],
        out_shape=[
            jax.ShapeDtypeStruct((T, D), F32),
            jax.ShapeDtypeStruct((T, D // 2), jnp.uint32),
            jax.ShapeDtypeStruct((SUBLANES, T), jnp.int32),
            jax.ShapeDtypeStruct((T, LANES), F32),
            jax.ShapeDtypeStruct((N_EXPERTS, LANES), jnp.int32),
            jax.ShapeDtypeStruct((n_slots, D // 2), jnp.uint32),
        ],
        scratch_shapes=[pltpu.VMEM((N_EXPERTS, LANES), F32)],
        compiler_params=pltpu.CompilerParams(
            dimension_semantics=("arbitrary",),
            vmem_limit_bytes=VMEM_LIMIT),
        name="mix_ln_router",
    )(x2, o_n, u, wg, bg, woa, woc, wmo, g1, b1, wr, br)


def _sc_scatter_rows(h1p, dest, xs_ref):
    T, W = h1p.shape
    n_workers = SC_CORES * SC_SUBCORES
    per = T // n_workers
    n_win = per // SC_WIN
    assert n_win * SC_WIN * n_workers == T
    mesh = plsc.VectorSubcoreMesh(core_axis_name="core", subcore_axis_name="subcore")

    @pl.kernel(out_type=(), mesh=mesh,
               scratch_types=[pltpu.VMEM((2, SC_WIN, W), h1p.dtype),
                              pltpu.VMEM((2, SC_WIN), jnp.int32), pltpu.VMEM((2, SC_WIN), jnp.int32),
                              pltpu.SemaphoreType.DMA((2,)), pltpu.SemaphoreType.DMA((2,))],
               name="moe_dispatch_sc")
    def scatter(x_hbm, d0_hbm, d1_hbm, o_hbm, xbuf, i0, i1, load_sem, scat_sem):
        wid = lax.axis_index("core") * SC_SUBCORES + lax.axis_index("subcore")

        def loads(j, s):
            base = wid * per + j * SC_WIN
            return [pltpu.make_async_copy(x_hbm.at[pl.ds(base, SC_WIN)], xbuf.at[s], load_sem.at[s]),
                    pltpu.make_async_copy(d0_hbm.at[pl.ds(base, SC_WIN)], i0.at[s], load_sem.at[s]),
                    pltpu.make_async_copy(d1_hbm.at[pl.ds(base, SC_WIN)], i1.at[s], load_sem.at[s])]

        def scatters(s):
            return [pltpu.make_async_copy(xbuf.at[s], o_hbm.at[i0.at[s]], scat_sem.at[s]),
                    pltpu.make_async_copy(xbuf.at[s], o_hbm.at[i1.at[s]], scat_sem.at[s])]

        for c in loads(0, 0):
            c.start()
        for j in range(n_win):
            s = j % 2
            for c in loads(j, s):
                c.wait()
            for c in scatters(s):
                c.start()
            if j >= 1:
                for c in scatters(1 - s):
                    c.wait()
            if j + 1 < n_win:
                for c in loads(j + 1, 1 - s):
                    c.start()
        for c in scatters((n_win - 1) % 2):
            c.wait()

    scatter(h1p, dest[0], dest[1], xs_ref)


def _expert_kernel(be_ref, nu_ref, xs_ref, wg_ref, wu_ref, wd_ref, ys_ref, wgb_ref, wub_ref, wdb_ref):
    i = pl.program_id(0)
    used = i < nu_ref[0]

    @pl.when(used & ((i == 0) | (be_ref[i] != be_ref[jnp.maximum(i - 1, 0)])))
    def _():
        wgb_ref[...] = wg_ref[...].astype(BF16)
        wub_ref[...] = wu_ref[...].astype(BF16)
        wdb_ref[...] = wd_ref[...].astype(BF16)

    @pl.when(used)
    def _():
        xb = _unpack_bf16_pairs(xs_ref[...])
        g = _dot(xb, wgb_ref[...])
        u = _dot(xb, wub_ref[...])
        h = (g * jax.nn.sigmoid(g) * u).astype(BF16)
        ys_ref[...] = _pack_bf16_pairs(_dot(h, wdb_ref[...]))

    @pl.when(i >= nu_ref[0])
    def _():
        ys_ref[...] = jnp.zeros(ys_ref.shape, ys_ref.dtype)


def _expert_call(block_expert, n_used, xs, wg, wu, wd):
    P = xs.shape[0]
    bm = MOE_BM
    nb = P // bm
    xs_map = lambda i, be, nu: (jnp.minimum(i, nu[0] - 1), 0)
    w_map = lambda i, be, nu: (be[i], 0, 0)
    return pl.pallas_call(
        _expert_kernel,
        grid_spec=pltpu.PrefetchScalarGridSpec(
            num_scalar_prefetch=2,
            grid=(nb,),
            in_specs=[
                pl.BlockSpec((bm, D_MODEL // 2), xs_map),
                pl.BlockSpec((None, D_MODEL, EXPERT_FF), w_map),
                pl.BlockSpec((None, D_MODEL, EXPERT_FF), w_map),
                pl.BlockSpec((None, EXPERT_FF, D_MODEL), w_map),
            ],
            out_specs=pl.BlockSpec((bm, D_MODEL // 2), lambda i, be, nu: (i, 0)),
            scratch_shapes=[
                pltpu.VMEM((D_MODEL, EXPERT_FF), BF16),
                pltpu.VMEM((D_MODEL, EXPERT_FF), BF16),
                pltpu.VMEM((EXPERT_FF, D_MODEL), BF16),
            ],
        ),
        out_shape=jax.ShapeDtypeStruct((P, D_MODEL // 2), jnp.uint32),
        compiler_params=pltpu.CompilerParams(
            dimension_semantics=("arbitrary",),
            vmem_limit_bytes=VMEM_LIMIT),
        name="moe_experts",
    )(block_expert, n_used, xs, wg, wu, wd)


def _sc_gather_rows(ys, dest):
    _, T = dest.shape
    W = ys.shape[1]
    n_workers = SC_CORES * SC_SUBCORES
    per = T // n_workers
    win = SC_WIN // 2
    n_win = per // win
    assert n_win * win * n_workers == T
    mesh = plsc.VectorSubcoreMesh(core_axis_name="core", subcore_axis_name="subcore")

    @pl.kernel(out_type=jax.ShapeDtypeStruct((2, T, W), ys.dtype), mesh=mesh,
               scratch_types=[pltpu.VMEM((2, 2, win, W), ys.dtype), pltpu.VMEM((2, 2, win), jnp.int32),
                              pltpu.SemaphoreType.DMA((2,)), pltpu.SemaphoreType.DMA((2,)),
                              pltpu.SemaphoreType.DMA((2,))],
               name="moe_combine_sc")
    def gather(y_hbm, d0_hbm, d1_hbm, o_hbm, buf, idx, idx_sem, gat_sem, out_sem):
        wid = lax.axis_index("core") * SC_SUBCORES + lax.axis_index("subcore")
        d_hbm = (d0_hbm, d1_hbm)

        def base(j):
            return wid * per + j * win

        def idx_loads(j, s):
            return [pltpu.make_async_copy(d_hbm[k].at[pl.ds(base(j), win)], idx.at[s, k], idx_sem.at[s])
                    for k in range(2)]

        def gathers(s):
            return [pltpu.make_async_copy(y_hbm.at[idx.at[s, k]], buf.at[s, k], gat_sem.at[s]) for k in range(2)]

        def stores(j, s):
            return [pltpu.make_async_copy(buf.at[s, k], o_hbm.at[k, pl.ds(base(j), win)], out_sem.at[s])
                    for k in range(2)]

        for c in idx_loads(0, 0):
            c.start()
        for j in range(n_win):
            s = j % 2
            for c in idx_loads(j, s):
                c.wait()
            if j >= 2:
                for c in stores(j - 2, s):
                    c.wait()
            for c in gathers(s):
                c.start()
            if j >= 1:
                for c in gathers(1 - s):
                    c.wait()
                for c in stores(j - 1, 1 - s):
                    c.start()
            if j + 1 < n_win:
                for c in idx_loads(j + 1, 1 - s):
                    c.start()
        s_last = (n_win - 1) % 2
        for c in gathers(s_last):
            c.wait()
        for c in stores(n_win - 1, s_last):
            c.start()
        if n_win >= 2:
            for c in stores(n_win - 2, 1 - s_last):
                c.wait()
        for c in stores(n_win - 1, s_last):
            c.wait()

    return gather(ys, dest[0], dest[1])


def _combine_kernel(y_ref, h1_ref, rf_ref, g2_ref, b2_ref, out_ref):
    rf = rf_ref[...]
    def unpack_f32(w):
        return jnp.concatenate([lax.bitcast_convert_type(w << 16, F32),
                                lax.bitcast_convert_type(w & jnp.uint32(0xFFFF0000), F32)], axis=1)

    ffn = rf[:, 0:1] * unpack_f32(y_ref[0]) + rf[:, 1:2] * unpack_f32(y_ref[1])
    z = DEEPNORM_ALPHA * h1_ref[...] + ffn
    mu = jnp.mean(z, axis=-1, keepdims=True)
    zc = z - mu
    var = jnp.mean(zc * zc, axis=-1, keepdims=True)
    out_ref[...] = zc * lax.rsqrt(var + LN_EPS) * g2_ref[...] + b2_ref[...]


def _combine_call(y12, h1, rf, g2, b2):
    T, D = h1.shape
    tm = COMB_TM
    const = lambda i: (0, 0)
    row = lambda i: (i, 0)
    return pl.pallas_call(
        _combine_kernel,
        grid=(T // tm,),
        in_specs=[
            pl.BlockSpec((2, tm, D // 2), lambda i: (0, i, 0)),
            pl.BlockSpec((tm, D), row),
            pl.BlockSpec((tm, LANES), row),
            pl.BlockSpec(g2.shape, const),
            pl.BlockSpec(b2.shape, const),
        ],
        out_specs=pl.BlockSpec((tm, D), row),
        out_shape=jax.ShapeDtypeStruct((T, D), F32),
        compiler_params=pltpu.CompilerParams(
            dimension_semantics=("arbitrary",),
            vmem_limit_bytes=VMEM_LIMIT),
        name="moe_combine_ln",
    )(y12, h1, rf, g2, b2)


def kernel(x, positions, w_in, b_gate, lambda_q1, lambda_k1, lambda_q2, lambda_k2, subln_g, w_o_att, conv_w, w_o_conv, w_mix_out, ln1_g, ln1_b, w_router_group, b_router_group, w_router_expert, b_router_expert, w_exp_gate, w_exp_up, w_exp_down, ln2_g, ln2_b):
    B, S, D = x.shape
    T = B * S
    l = 0
    lambda_init = 0.8 - 0.6 * math.exp(-0.3 * l)

    inv_freq = ROPE_THETA ** (-jnp.arange(0, HEAD_DIM, 2, dtype=F32) / HEAD_DIM)
    pos_f = positions.astype(F32)[:, None, :]
    inv_f = inv_freq.reshape(HEAD_DIM // 2, 1)

    n_main = 2 * QK_W + QK_W + 3 * CONV_W
    w0 = w_in[l]
    w_main = jnp.concatenate([w0[:, QK_W:2 * QK_W], w0[:, 3 * QK_W:n_main]], axis=1).astype(BF16)
    wq_t = w0[:, 0:QK_W].T.astype(BF16)
    wv_t = w0[:, 2 * QK_W:3 * QK_W].T.astype(BF16)
    w_gates = w0[:, n_main:].astype(BF16)

    q, k, vt, u = _proj_call(x, w_main, wq_t, wv_t, pos_f, inv_f, conv_w[l])

    lam = (jnp.exp(jnp.sum(lambda_q1[l].astype(F32) * lambda_k1[l].astype(F32)))
           - jnp.exp(jnp.sum(lambda_q2[l].astype(F32) * lambda_k2[l].astype(F32)))
           + lambda_init).reshape(1)
    o_n = _attn_call(lam, q, k, vt, subln_g[l].astype(F32).reshape(HEAD_W, 1), 1.0 - lambda_init)

    wr = jnp.zeros((D, LANES), F32)
    wr = wr.at[:, 0:N_GROUPS].set(w_router_group[l]).at[:, SUBLANES:SUBLANES + N_EXPERTS].set(w_router_expert[l])
    br = jnp.zeros((1, LANES), F32)
    br = br.at[0, 0:N_GROUPS].set(b_router_group[l]).at[0, SUBLANES:SUBLANES + N_EXPERTS].set(b_router_expert[l])

    bm = MOE_BM
    A = 2 * T
    P = A + N_EXPERTS * bm
    nb = P // bm
    h1, h1p, ri, rf, cnt, xs0 = _mix_call(
        x.reshape(T, D), o_n, u.reshape(T, CONV_W),
        w_gates, b_gate[l].reshape(1, -1).astype(F32),
        w_o_att[l].astype(BF16), w_o_conv[l].astype(BF16), w_mix_out[l].astype(BF16),
        ln1_g[l].reshape(1, D).astype(F32), ln1_b[l].reshape(1, D).astype(F32),
        wr.astype(BF16), br, P)

    counts = cnt[:, 0]
    padded = (counts + bm - 1) // bm * bm
    pend = jnp.cumsum(padded)
    pstart = pend - padded
    onehot = (ri[0:2, :, None] == jnp.arange(N_EXPERTS, dtype=jnp.int32)).astype(jnp.int32)
    dest = jnp.sum(onehot * pstart, axis=-1) + ri[2:4]
    blk_start = jnp.arange(nb, dtype=jnp.int32) * bm
    block_expert = jnp.minimum(
        jnp.sum((pend[None, :] <= blk_start[:, None]).astype(jnp.int32), axis=1), N_EXPERTS - 1)
    n_used = (pend[-1] // bm).astype(jnp.int32).reshape(1)

    xs_ref = jax.new_ref(xs0)
    _sc_scatter_rows(h1p, dest, xs_ref)
    xs = xs_ref[...]
    ys = _expert_call(block_expert, n_used, xs, w_exp_gate[l], w_exp_up[l], w_exp_down[l])
    out = _combine_call(_sc_gather_rows(ys, dest), h1, rf,
                        ln2_g[l].reshape(1, D).astype(F32), ln2_b[l].reshape(1, D).astype(F32))
    return out.reshape(B, S, D)
```
